```python
import functools
import jax, jax.numpy as jnp
from jax import lax
import numpy as np

D_MODEL = 1024
BATCH = 2
SEQ = 8192
DEPTH = 1
DEC_BATCH = 32
DEC_SEQ = 1
PAST_LEN = 8192
PAGE_SIZE = 128

HEAD_DIM = 64
ATT_WIDTH = D_MODEL // 2
RWKV_WIDTH = D_MODEL - ATT_WIDTH
ATT_HEADS = ATT_WIDTH // HEAD_DIM
RWKV_HEADS = RWKV_WIDTH // HEAD_DIM
DILATION_PAIRS = ((128, 1), (512, 4), (2048, 16))
MAX_WINDOW = 2048
WIN = min(MAX_WINDOW, PAST_LEN)
Q_BLOCK = 128
N_BUCKETS = 32
MAX_DISTANCE = MAX_WINDOW
D_DECAY_LORA = 32
D_ICLR_LORA = 32
D_GATE_LORA = 96
RWKV_PROJ = 3 * RWKV_WIDTH + D_DECAY_LORA + D_ICLR_LORA + D_GATE_LORA
PROJ = 3 * ATT_WIDTH + RWKV_PROJ
D_FF = ((8 * D_MODEL // 3 + 127) // 128) * 128
CONV_W = 3
NORM_EPS = 1e-6
GN_EPS = 64e-5
NEG = -1e30
ATT_SCALE = HEAD_DIM ** -0.5

kernel_name = "hymba_longnet_rwkv7_convffn_step"


def t5_bucket(dist):
    dist = np.asarray(dist, dtype=np.int64)
    exact = N_BUCKETS // 2
    scaled = np.log(np.maximum(dist, 1) / exact) / np.log(MAX_DISTANCE / exact)
    large = np.minimum(exact + (scaled * (N_BUCKETS - exact)).astype(np.int64), N_BUCKETS - 1)
    return np.where(dist < exact, dist, large).astype(np.int32)


def rmsnorm(x, g):
    xf = x.astype(jnp.float32)
    y = xf * lax.rsqrt(jnp.mean(xf * xf, axis=-1, keepdims=True) + NORM_EPS)
    return (y * g.astype(jnp.float32)).astype(x.dtype)


def last_rows(a, n):
    t = a.shape[1]
    if t >= n:
        return a[:, t - n:]
    pad = [(0, 0)] * a.ndim
    pad[1] = (n - t, 0)
    return jnp.pad(a, pad)


def dilated_branch_prompt(q, k, v, rel_bias, window, d):
    B, T, H, E = q.shape
    wd = window // d
    span = d * Q_BLOCK
    Tp = -(-T // span) * span
    padw = ((0, 0), (0, Tp - T), (0, 0), (0, 0))
    q, k, v = (jnp.pad(a, padw) for a in (q, k, v))
    L = Tp // d
    nb = L // Q_BLOCK

    def to_blocks(a):
        return a.reshape(B, L, d, H, E).transpose(0, 2, 1, 3, 4).reshape(B, d, nb, Q_BLOCK, H, E)

    def with_prev(a):
        prev = jnp.pad(a, ((0, 0), (0, 0), (1, 0), (0, 0), (0, 0), (0, 0)))[:, :, :nb]
        return jnp.concatenate([prev, a], axis=3)

    qb = to_blocks(q)
    kb = with_prev(to_blocks(k))
    vb = with_prev(to_blocks(v))

    i = np.arange(Q_BLOCK)[:, None]
    c = np.arange(2 * Q_BLOCK)[None, :]
    n = i + Q_BLOCK - c
    band_ok = (n >= 0) & (n <= wd)
    buckets = t5_bucket(d * np.clip(n, 0, wd))
    bias = jnp.transpose(rel_bias[buckets].astype(jnp.float32), (2, 0, 1))
    blk_ok = (np.arange(nb)[:, None] * Q_BLOCK - Q_BLOCK + np.arange(2 * Q_BLOCK)[None, :]) >= 0
    mask = band_ok[None, None, :, :] & blk_ok[:, None, None, :]

    logits = jnp.einsum('brnqhe,brnkhe->brnhqk', qb, kb) * ATT_SCALE + bias
    logits = jnp.where(mask, logits, NEG)
    m = jnp.max(logits, axis=-1)
    p = jnp.exp(logits - m[..., None])
    s = jnp.sum(p, axis=-1)
    num = jnp.einsum('brnhqk,brnkhe->brnqhe', p, vb)

    def from_blocks(a):
        rest = a.shape[4:]
        a = jnp.moveaxis(a.reshape((B, d, L) + rest), 1, 2)
        return a.reshape((B, Tp) + rest)[:, :T]

    return from_blocks(num), from_blocks(jnp.swapaxes(m, 3, 4)), from_blocks(jnp.swapaxes(s, 3, 4))


def dilated_branch_sample(q, k_all, v_all, rel_bias, window, d):
    S = q.shape[1]
    n_past = k_all.shape[1] - S
    wd = window // d
    steps = np.arange(wd + 1)
    idx = n_past + np.arange(S)[:, None] - d * steps[None, :]
    ok = idx >= 0
    idx = np.maximum(idx, 0)
    kg = k_all[:, idx]
    vg = v_all[:, idx]
    bias = rel_bias[t5_bucket(d * steps)].astype(jnp.float32).T
    logits = jnp.einsum('bshe,bskhe->bshk', q, kg) * ATT_SCALE + bias[None, None]
    logits = jnp.where(ok[None, :, None, :], logits, NEG)
    m = jnp.max(logits, axis=-1)
    p = jnp.exp(logits - m[..., None])
    s = jnp.sum(p, axis=-1)
    num = jnp.einsum('bshk,bskhe->bshe', p, vg)
    return num, m, s


def combine_branches(parts):
    ms = [pt[1] for pt in parts]
    m = functools.reduce(jnp.maximum, ms)
    num = 0.0
    den = 0.0
    for num_g, m_g, s_g in parts:
        w_g = jnp.exp(m_g - m)
        num = num + w_g[..., None] * num_g
        den = den + w_g * s_g
    return num / den[..., None]


def wkv_scan(r, decay, k, v, kk, a, S0):
    def step(S, xs):
        r_t, w_t, k_t, v_t, kk_t, a_t = xs
        sa = jnp.einsum('bhij,bhj->bhi', S, -kk_t)
        S = (S * w_t[:, :, None, :] + sa[..., None] * (kk_t * a_t)[:, :, None, :]
             + v_t[..., None] * k_t[:, :, None, :])
        return S, jnp.einsum('bhij,bhj->bhi', S, r_t)
    xs = tuple(jnp.moveaxis(t, 1, 0) for t in (r, decay, k, v, kk, a))
    S, ys = lax.scan(step, S0, xs)
    return jnp.moveaxis(ys, 0, 1), S


def trunk_layer(h, p, rel_bias, past):
    f32 = jnp.float32
    B, T, _ = h.shape
    xn = rmsnorm(h, p['g_mix'])
    proj = xn @ p['w_in']
    q, k, v, P = jnp.split(proj, [ATT_WIDTH, 2 * ATT_WIDTH, 3 * ATT_WIDTH], axis=-1)
    q, k, v = (t.reshape(B, T, ATT_HEADS, HEAD_DIM) for t in (q, k, v))
    qf = q.astype(f32)

    if past is None:
        kf, vf = k.astype(f32), v.astype(f32)
        parts = [dilated_branch_prompt(qf, kf, vf, rel_bias, w, d) for w, d in DILATION_PAIRS]
        new_k, new_v = last_rows(k, WIN), last_rows(v, WIN)
    else:
        k_all = jnp.concatenate([past['k'].astype(k.dtype), k], axis=1)
        v_all = jnp.concatenate([past['v'].astype(v.dtype), v], axis=1)
        kf, vf = k_all.astype(f32), v_all.astype(f32)
        parts = [dilated_branch_sample(qf, kf, vf, rel_bias, w, d) for w, d in DILATION_PAIRS]
        new_k, new_v = k_all[:, T:], v_all[:, T:]
    attn = combine_branches(parts).reshape(B, T, ATT_WIDTH)

    if past is None:
        prev = jnp.zeros((B, 1, RWKV_PROJ), P.dtype)
        S0 = jnp.zeros((B, RWKV_HEADS, HEAD_DIM, HEAD_DIM), f32)
    else:
        prev = past['shift'][:, None].astype(P.dtype)
        S0 = past['wkv'].astype(f32)
    P_prev = jnp.concatenate([prev, P[:, :-1]], axis=1)
    Pm = (P + p['tok_mu'] * (P_prev - P)).astype(f32)
    new_shift = P[:, -1]
    cuts = [RWKV_WIDTH, 2 * RWKV_WIDTH, 3 * RWKV_WIDTH,
            3 * RWKV_WIDTH + D_DECAY_LORA, 3 * RWKV_WIDTH + D_DECAY_LORA + D_ICLR_LORA]
    r, kr, vr, wdn, adn, gdn = jnp.split(Pm, cuts, axis=-1)
    w_log = -jax.nn.softplus(-(p['w0'] + jnp.tanh(wdn) @ p['w_decay_up'])) - 0.5
    decay = jnp.exp(-jnp.exp(w_log))
    a = jax.nn.sigmoid(p['a0'] + adn @ p['w_iclr_up'])
    g = jax.nn.sigmoid(gdn) @ p['w_gate_up']
    heads = lambda t: t.reshape(B, T, RWKV_HEADS, HEAD_DIM)
    kk = heads(kr * p['k_k'])
    kk = kk / jnp.maximum(jnp.sqrt(jnp.sum(kk * kk, axis=-1, keepdims=True)), 1e-12)
    kr = kr * (1.0 + (a - 1.0) * p['k_a'])
    r_h, k_h, v_h, a_h, w_h = heads(r), heads(kr), heads(vr), heads(a), heads(decay)
    y, S_new = wkv_scan(r_h, w_h, k_h, v_h, kk, a_h, S0)
    mu = jnp.mean(y, axis=-1, keepdims=True)
    var = jnp.mean(jnp.square(y - mu), axis=-1, keepdims=True)
    yn = (y - mu) * lax.rsqrt(var + GN_EPS)
    yn = yn * p['ln_x_w'].reshape(RWKV_HEADS, HEAD_DIM) + p['ln_x_b'].reshape(RWKV_HEADS, HEAD_DIM)
    bonus = jnp.sum(r_h * k_h * p['r_k'], axis=-1, keepdims=True) * v_h
    rw_out = (yn + bonus).reshape(B, T, RWKV_WIDTH) * g

    mixed = jnp.concatenate([attn.astype(h.dtype), rw_out.astype(h.dtype)], axis=-1) @ p['w_o']
    h = h + mixed

    xn2 = rmsnorm(h, p['g_ffn'])
    gate_pre, up = jnp.split(xn2 @ p['w_ffn_in'], [D_FF], axis=-1)
    if past is None:
        prev_rows = jnp.zeros((B, CONV_W - 1, D_FF), gate_pre.dtype)
    else:
        prev_rows = past['conv'].astype(gate_pre.dtype)
    full = jnp.concatenate([prev_rows, gate_pre], axis=1)
    conv = p['conv_b']
    for j in range(CONV_W):
        conv = conv + p['conv_w'][j] * full[:, j:j + T]
    new_conv = full[:, -(CONV_W - 1):]
    h = h + (jax.nn.silu(conv) * up) @ p['w_ffn_out']
    return h, (new_k, new_v, new_shift, S_new, new_conv)


def setup_inputs(seed: int = 0) -> dict:
    key = jax.random.key(seed)
    ks = jax.random.split(key, 32)
    f32 = jnp.float32
    nrm = lambda k, shape, scale: jax.random.normal(k, shape, f32) * scale
    return {
        'x_prompt': nrm(ks[0], (BATCH, SEQ, D_MODEL), 1.0),
        'x_sample': nrm(ks[1], (DEC_BATCH, DEC_SEQ, D_MODEL), 1.0),
        'cache_win_k': nrm(ks[2], (DEPTH, DEC_BATCH, WIN, ATT_HEADS, HEAD_DIM), 1.0),
        'cache_win_v': nrm(ks[3], (DEPTH, DEC_BATCH, WIN, ATT_HEADS, HEAD_DIM), 1.0),
        'state_shift': nrm(ks[4], (DEPTH, DEC_BATCH, RWKV_PROJ), 1.0),
        'state_wkv': nrm(ks[5], (DEPTH, DEC_BATCH, RWKV_HEADS, HEAD_DIM, HEAD_DIM), 0.3),
        'state_ffn_conv': nrm(ks[6], (DEPTH, DEC_BATCH, CONV_W - 1, D_FF), 0.5),
        'g_mix': 1.0 + nrm(ks[7], (DEPTH, D_MODEL), 0.02),
        'w_in': nrm(ks[8], (DEPTH, D_MODEL, PROJ), D_MODEL ** -0.5),
        'rel_bias': nrm(ks[9], (N_BUCKETS, ATT_HEADS), 0.3),
        'tok_mu': jax.random.uniform(ks[10], (DEPTH, RWKV_PROJ), f32),
        'w0': -1.0 + nrm(ks[11], (DEPTH, RWKV_WIDTH), 0.5),
        'w_decay_up': nrm(ks[12], (DEPTH, D_DECAY_LORA, RWKV_WIDTH), 0.1),
        'a0': nrm(ks[13], (DEPTH, RWKV_WIDTH), 0.1),
        'w_iclr_up': nrm(ks[14], (DEPTH, D_ICLR_LORA, RWKV_WIDTH), 0.1),
        'w_gate_up': nrm(ks[15], (DEPTH, D_GATE_LORA, RWKV_WIDTH), D_GATE_LORA ** -0.5),
        'k_k': 0.85 + nrm(ks[16], (DEPTH, RWKV_WIDTH), 0.05),
        'k_a': 1.0 + nrm(ks[17], (DEPTH, RWKV_WIDTH), 0.05),
        'r_k': nrm(ks[18], (DEPTH, RWKV_HEADS, HEAD_DIM), 0.1),
        'ln_x_w': 1.0 + nrm(ks[19], (DEPTH, RWKV_WIDTH), 0.02),
        'ln_x_b': nrm(ks[20], (DEPTH, RWKV_WIDTH), 0.02),
        'w_o': nrm(ks[21], (DEPTH, D_MODEL, D_MODEL), D_MODEL ** -0.5),
        'g_ffn': 1.0 + nrm(ks[22], (DEPTH, D_MODEL), 0.02),
        'w_ffn_in': nrm(ks[23], (DEPTH, D_MODEL, 2 * D_FF), D_MODEL ** -0.5),
        'conv_w': nrm(ks[24], (DEPTH, CONV_W, D_FF), CONV_W ** -0.5),
        'conv_b': nrm(ks[25], (DEPTH, D_FF), 0.02),
        'w_ffn_out': nrm(ks[26], (DEPTH, D_FF, D_MODEL), D_FF ** -0.5),
        'g_final': 1.0 + nrm(ks[27], (D_MODEL,), 0.02),
    }


def reference(x_prompt, x_sample, cache_win_k, cache_win_v, state_shift, state_wkv, state_ffn_conv,
              g_mix, w_in, rel_bias, tok_mu, w0, w_decay_up, a0, w_iclr_up, w_gate_up, k_k, k_a, r_k,
              ln_x_w, ln_x_b, w_o, g_ffn, w_ffn_in, conv_w, conv_b, w_ffn_out, g_final):
    hp, hs = x_prompt, x_sample
    sp = [[] for _ in range(5)]
    ss = [[] for _ in range(5)]
    for l in range(DEPTH):
        p = dict(g_mix=g_mix[l], w_in=w_in[l], tok_mu=tok_mu[l], w0=w0[l], w_decay_up=w_decay_up[l],
                 a0=a0[l], w_iclr_up=w_iclr_up[l], w_gate_up=w_gate_up[l], k_k=k_k[l], k_a=k_a[l],
                 r_k=r_k[l], ln_x_w=ln_x_w[l], ln_x_b=ln_x_b[l], w_o=w_o[l], g_ffn=g_ffn[l],
                 w_ffn_in=w_ffn_in[l], conv_w=conv_w[l], conv_b=conv_b[l], w_ffn_out=w_ffn_out[l])
        hp, st_p = trunk_layer(hp, p, rel_bias, None)
        past = dict(k=cache_win_k[l], v=cache_win_v[l], shift=state_shift[l], wkv=state_wkv[l],
                    conv=state_ffn_conv[l])
        hs, st_s = trunk_layer(hs, p, rel_bias, past)
        for j in range(5):
            sp[j].append(st_p[j])
            ss[j].append(st_s[j])
    y_prompt = rmsnorm(hp, g_final)
    y_sample = rmsnorm(hs, g_final)
    return (y_prompt, y_sample,
            jnp.stack(sp[0]), jnp.stack(sp[1]), jnp.stack(sp[2]), jnp.stack(sp[3]), jnp.stack(sp[4]),
            jnp.stack(ss[0]), jnp.stack(ss[1]), jnp.stack(ss[2]), jnp.stack(ss[3]), jnp.stack(ss[4]))
```

```python
import functools

import numpy as np
import jax
import jax.numpy as jnp
from jax import lax
from jax.experimental import pallas as pl
from jax.experimental.pallas import tpu as pltpu

F32 = jnp.float32
BF16 = jnp.bfloat16

D_MODEL = 1024
HEAD_DIM = 64
ATT_WIDTH = 512
RWKV_WIDTH = 512
N_HEADS = 8
QKV_W = 3 * ATT_WIDTH
RKV_W = 3 * RWKV_WIDTH
LORA_DECAY, LORA_ICLR, LORA_GATE = 32, 32, 96
LORA_W = LORA_DECAY + LORA_ICLR + LORA_GATE
LORA_PAD = 256
RWKV_PROJ = RKV_W + LORA_W
PROJ_PAD = QKV_W + RKV_W + LORA_PAD
D_FF = 2816
FF_CHUNK = 1408
CONV_W = 3
WIN = 2048
DILATIONS = (1, 4, 16)
Q_BLOCK = 128
SUPER = 2048
N_BUCKETS = 32
MAX_DISTANCE = 2048
NORM_EPS = 1e-6
GN_EPS = 64e-5
NEG = -1e30
ATT_SCALE = HEAD_DIM ** -0.5
CHUNK = 64
VMEM_LIMIT = 56 * 1024 * 1024

NT = (((1,), (1,)), ((), ()))
TN = (((0,), (0,)), ((), ()))


def _dot(a, b):
    return jnp.dot(a.astype(BF16), b.astype(BF16), preferred_element_type=F32)


def _dot_nt(a, b):
    return lax.dot_general(a.astype(BF16), b.astype(BF16), NT, preferred_element_type=F32)


def _dot_tn(a, b):
    return lax.dot_general(a.astype(BF16), b.astype(BF16), TN, preferred_element_type=F32)


def _split2(x):
    hi = x.astype(BF16)
    lo = (x - hi.astype(F32)).astype(BF16)
    return hi, lo


def _split3(x):
    hi = x.astype(BF16)
    r1 = x - hi.astype(F32)
    mid = r1.astype(BF16)
    lo = (r1 - mid.astype(F32)).astype(BF16)
    return hi, mid, lo


def _segsum(x, ones_bd):
    hi, lo = _split2(x)
    return (jnp.dot(hi, ones_bd, preferred_element_type=F32)
            + jnp.dot(lo, ones_bd, preferred_element_type=F32))


def _sigmoid(x):
    return 1.0 / (1.0 + jnp.exp(-x))


def _rmsnorm(x, g):
    return x * lax.rsqrt(jnp.mean(x * x, axis=-1, keepdims=True) + NORM_EPS) * g


def _params(sem):
    return pltpu.CompilerParams(dimension_semantics=sem, vmem_limit_bytes=VMEM_LIMIT)


def _t5_bucket(dist):
    dist = np.asarray(dist, dtype=np.int64)
    exact = N_BUCKETS // 2
    scaled = np.log(np.maximum(dist, 1) / exact) / np.log(MAX_DISTANCE / exact)
    large = np.minimum(exact + (scaled * (N_BUCKETS - exact)).astype(np.int64), N_BUCKETS - 1)
    return np.where(dist < exact, dist, large).astype(np.int32)


def _inproj_kernel(x_ref, g_ref, w_ref, qkv_ref, rkv_ref, lora_ref):
    xb = _rmsnorm(x_ref[...], g_ref[...]).astype(BF16)
    qkv_ref[...] = jnp.dot(xb, w_ref[:, 0:QKV_W], preferred_element_type=F32)
    rkv_ref[...] = jnp.dot(xb, w_ref[:, QKV_W:QKV_W + RKV_W], preferred_element_type=F32)
    lora_ref[...] = jnp.dot(xb, w_ref[:, QKV_W + RKV_W:PROJ_PAD], preferred_element_type=F32)


def _inproj(x2d, g, w, tm):
    m = x2d.shape[0]
    row = lambda i: (i, 0)
    fixed = lambda i: (0, 0)
    return pl.pallas_call(
        _inproj_kernel,
        grid=(m // tm,),
        in_specs=[pl.BlockSpec((tm, D_MODEL), row),
                  pl.BlockSpec((1, D_MODEL), fixed),
                  pl.BlockSpec((D_MODEL, PROJ_PAD), fixed)],
        out_specs=[pl.BlockSpec((tm, QKV_W), row),
                   pl.BlockSpec((tm, RKV_W), row),
                   pl.BlockSpec((tm, LORA_PAD), row)],
        out_shape=[jax.ShapeDtypeStruct((m, QKV_W), F32),
                   jax.ShapeDtypeStruct((m, RKV_W), F32),
                   jax.ShapeDtypeStruct((m, LORA_PAD), F32)],
        compiler_params=_params(("arbitrary",)),
        name="inproj",
    )(x2d, g, w)


def _prep_math(p, p_prev, l, l_prev, mu_p, mu_l, wcat, w0, a0, k_k, k_a, r_k, ones_bd):
    pm = p + mu_p * (p_prev - p)
    lm = l + mu_l * (l_prev - l)
    r = pm[:, 0:RWKV_WIDTH]
    kr = pm[:, RWKV_WIDTH:2 * RWKV_WIDTH]
    vr = pm[:, 2 * RWKV_WIDTH:3 * RWKV_WIDTH]
    lane = lax.broadcasted_iota(jnp.int32, lm.shape, 1)
    feat = jnp.where(lane < LORA_DECAY, jnp.tanh(lm),
                     jnp.where(lane < LORA_DECAY + LORA_ICLR, lm, _sigmoid(lm)))
    z = jnp.dot(feat.astype(BF16), wcat, preferred_element_type=F32)
    zd = -(w0 + z[:, 0:RWKV_WIDTH])
    softplus = jnp.maximum(zd, 0.0) + jnp.log(1.0 + jnp.exp(-jnp.abs(zd)))
    lw = -jnp.exp(-softplus - 0.5)
    a = _sigmoid(a0 + z[:, RWKV_WIDTH:2 * RWKV_WIDTH])
    gate = z[:, 2 * RWKV_WIDTH:3 * RWKV_WIDTH]
    kk = kr * k_k
    kk = kk / jnp.maximum(jnp.sqrt(_segsum(kk * kk, ones_bd)), 1e-12)
    k2 = kr * (1.0 + (a - 1.0) * k_a)
    bonus = _segsum(r * k2 * r_k, ones_bd) * vr
    return r, lw, k2, vr, kk, kk * a, gate, bonus


def _prep_prompt_kernel(p_ref, l_ref, mu_p_ref, mu_l_ref, wcat_ref, w0_ref, a0_ref, kk_ref, ka_ref,
                        rk_ref, ones_ref, *refs):
    outs, (cp_ref, cl_ref) = refs[:8], refs[8:]

    @pl.when(pl.program_id(1) == 0)
    def _():
        cp_ref[...] = jnp.zeros_like(cp_ref)
        cl_ref[...] = jnp.zeros_like(cl_ref)

    p = p_ref[...]
    l = l_ref[...]
    tm = p.shape[0]
    first_p = lax.broadcasted_iota(jnp.int32, p.shape, 0) == 0
    first_l = lax.broadcasted_iota(jnp.int32, l.shape, 0) == 0
    p_prev = jnp.where(first_p, cp_ref[...], pltpu.roll(p, 1, 0))
    l_prev = jnp.where(first_l, cl_ref[...], pltpu.roll(l, 1, 0))
    cp_ref[...] = p[tm - 1:tm, :]
    cl_ref[...] = l[tm - 1:tm, :]
    res = _prep_math(p, p_prev, l, l_prev, mu_p_ref[...], mu_l_ref[...], wcat_ref[...], w0_ref[...],
                     a0_ref[...], kk_ref[...], ka_ref[...], rk_ref[...], ones_ref[...])
    for o, v in zip(outs, res):
        o[...] = v


def _prep_sample_kernel(p_ref, pp_ref, l_ref, lp_ref, mu_p_ref, mu_l_ref, wcat_ref, w0_ref, a0_ref,
                        kk_ref, ka_ref, rk_ref, ones_ref, *outs):
    res = _prep_math(p_ref[...], pp_ref[...], l_ref[...], lp_ref[...], mu_p_ref[...], mu_l_ref[...],
                     wcat_ref[...], w0_ref[...], a0_ref[...], kk_ref[...], ka_ref[...], rk_ref[...],
                     ones_ref[...])
    for o, v in zip(outs, res):
        o[...] = v


def _prep_prompt(rkv, lora, pw, batch, seq, tm):
    nt = seq // tm
    row = lambda b, j: (b * nt + j, 0)
    fixed = lambda b, j: (0, 0)
    vec = lambda n: pl.BlockSpec((1, n), fixed)
    m = rkv.shape[0]
    return pl.pallas_call(
        _prep_prompt_kernel,
        grid=(batch, nt),
        in_specs=[pl.BlockSpec((tm, RKV_W), row), pl.BlockSpec((tm, LORA_PAD), row),
                  vec(RKV_W), vec(LORA_PAD), pl.BlockSpec((LORA_PAD, RKV_W), fixed),
                  vec(RWKV_WIDTH), vec(RWKV_WIDTH), vec(RWKV_WIDTH), vec(RWKV_WIDTH), vec(RWKV_WIDTH),
                  pl.BlockSpec((RWKV_WIDTH, RWKV_WIDTH), fixed)],
        out_specs=[pl.BlockSpec((tm, RWKV_WIDTH), row)] * 8,
        out_shape=[jax.ShapeDtypeStruct((m, RWKV_WIDTH), F32)] * 8,
        scratch_shapes=[pltpu.VMEM((1, RKV_W), F32), pltpu.VMEM((1, LORA_PAD), F32)],
        compiler_params=_params(("arbitrary", "arbitrary")),
        name="rwkv_prep_prompt",
    )(rkv, lora, *pw)


def _prep_sample(rkv, rkv_prev, lora, lora_prev, pw):
    m = rkv.shape[0]
    full = lambda a: pl.BlockSpec(a.shape, lambda i: (0,) * a.ndim)
    args = (rkv, rkv_prev, lora, lora_prev) + tuple(pw)
    return pl.pallas_call(
        _prep_sample_kernel,
        grid=(1,),
        in_specs=[full(a) for a in args],
        out_specs=[pl.BlockSpec((m, RWKV_WIDTH), lambda i: (0, 0))] * 8,
        out_shape=[jax.ShapeDtypeStruct((m, RWKV_WIDTH), F32)] * 8,
        compiler_params=_params(("arbitrary",)),
        name="rwkv_prep_sample",
    )(*args)


def _cumsum_rows(x, tri):
    tri_b = tri.astype(BF16)
    out = None
    for piece in _split3(x):
        t = jnp.dot(tri_b, piece, preferred_element_type=F32)
        out = t if out is None else out + t
    return out


def _wkv_kernel(r_ref, lw_ref, k_ref, v_ref, kk_ref, b_ref, y_ref, s_ref, st_ref):
    c = pl.program_id(1)

    @pl.when(c == 0)
    def _():
        st_ref[...] = jnp.zeros_like(st_ref)

    n = CHUNK
    row = lax.broadcasted_iota(jnp.int32, (n, n), 0)
    col = lax.broadcasted_iota(jnp.int32, (n, n), 1)
    lower_incl = row >= col
    lower_strict = row > col
    diag = row == col

    lw = lw_ref[...]
    lg = _cumsum_rows(lw, jnp.where(lower_incl, 1.0, 0.0))
    g = jnp.exp(lg)
    g_prev = jnp.exp(lg - lw)
    g_inv = jnp.exp(-lg)
    lg_end = lg[n - 1:n, :]
    g_to_end = jnp.exp(lg_end - lg)
    g_end = jnp.exp(lg_end)
    kk = kk_ref[...]
    b = b_ref[...]
    k = k_ref[...]
    a_t = -kk * g_prev
    b_t = b * g_inv
    k_t = k * g_inv
    r_t = r_ref[...] * g
    b_e = b * g_to_end
    k_e = k * g_to_end
    v = v_ref[...]

    for h in range(N_HEADS):
        sl = slice(h * HEAD_DIM, (h + 1) * HEAD_DIM)
        ah, bh, kh, rh, beh, keh, vh = (t[:, sl] for t in (a_t, b_t, k_t, r_t, b_e, k_e, v))
        l_ab = jnp.where(lower_strict, _dot_nt(ah, bh), 0.0)
        l_ak = jnp.where(lower_strict, _dot_nt(ah, kh), 0.0)
        m_rb = jnp.where(lower_incl, _dot_nt(rh, bh), 0.0)
        m_rk = jnp.where(lower_incl, _dot_nt(rh, kh), 0.0)
        tm_ = jnp.where(diag, 1.0, 0.0) + l_ab
        lp = l_ab
        span = 1
        while 2 * span < n:
            lp = _dot(lp, lp)
            tm_ = tm_ + _dot(tm_, lp)
            span *= 2
        a_hat = _dot(tm_, ah)
        u0 = _dot(tm_, _dot(l_ak, vh))
        s0 = st_ref[h]
        u = _dot_nt(a_hat, s0) + u0
        y = _dot_nt(rh, s0) + _dot(m_rb, u) + _dot(m_rk, vh)
        y_ref[:, sl] = y
        s_new = s0 * g_end[:, sl] + _dot_tn(u, beh) + _dot_tn(vh, keh)
        st_ref[h] = s_new

    @pl.when(c == pl.num_programs(1) - 1)
    def _():
        s_ref[0] = st_ref[...]


def _wkv_prompt(r, lw, k2, v, kk, b, batch, seq):
    nc = seq // CHUNK
    row = lambda bi, c: (bi * nc + c, 0)
    spec = pl.BlockSpec((CHUNK, RWKV_WIDTH), row)
    return pl.pallas_call(
        _wkv_kernel,
        grid=(batch, nc),
        in_specs=[spec] * 6,
        out_specs=[spec, pl.BlockSpec((1, N_HEADS, HEAD_DIM, HEAD_DIM), lambda bi, c: (bi, 0, 0, 0))],
        out_shape=[jax.ShapeDtypeStruct((batch * seq, RWKV_WIDTH), F32),
                   jax.ShapeDtypeStruct((batch, N_HEADS, HEAD_DIM, HEAD_DIM), F32)],
        scratch_shapes=[pltpu.VMEM((N_HEADS, HEAD_DIM, HEAD_DIM), F32)],
        compiler_params=_params(("arbitrary", "arbitrary")),
        name="wkv_prompt",
    )(r, lw, k2, v, kk, b)


def _wkv_step_kernel(s_ref, lw_ref, kk_ref, b_ref, k_ref, r_ref, v_ref, y_ref, so_ref):
    s = s_ref[...]
    sa = jnp.sum(s * (-kk_ref[...]), axis=-1, keepdims=True)
    s = s * jnp.exp(lw_ref[...]) + sa * b_ref[...] + v_ref[...] * k_ref[...]
    so_ref[...] = s
    y_ref[...] = jnp.sum(s * r_ref[...], axis=-1, keepdims=True)


def _wkv_sample(state, lw, kk, b, k2, r, v, bb):
    nb = state.shape[0]
    rowv = lambda t: t.reshape(nb, N_HEADS, 1, HEAD_DIM)
    colv = lambda t: t.reshape(nb, N_HEADS, HEAD_DIM, 1)
    idx = lambda i: (i, 0, 0, 0)
    s_spec = pl.BlockSpec((bb, N_HEADS, HEAD_DIM, HEAD_DIM), idx)
    r_spec = pl.BlockSpec((bb, N_HEADS, 1, HEAD_DIM), idx)
    c_spec = pl.BlockSpec((bb, N_HEADS, HEAD_DIM, 1), idx)
    y, s_new = pl.pallas_call(
        _wkv_step_kernel,
        grid=(nb // bb,),
        in_specs=[s_spec] + [r_spec] * 5 + [c_spec],
        out_specs=[c_spec, s_spec],
        out_shape=[jax.ShapeDtypeStruct((nb, N_HEADS, HEAD_DIM, 1), F32),
                   jax.ShapeDtypeStruct(state.shape, F32)],
        compiler_params=_params(("arbitrary",)),
        name="wkv_sample",
    )(state, rowv(lw), rowv(kk), rowv(b), rowv(k2), rowv(r), colv(v))
    return y.reshape(nb, RWKV_WIDTH), s_new


def _bias_tables():
    i = np.arange(Q_BLOCK)[:, None]
    c = np.arange(2 * Q_BLOCK)[None, :]
    n = i + Q_BLOCK - c
    ok = (n >= 0) & (n <= Q_BLOCK)
    prompt = np.stack([np.where(ok, _t5_bucket(d * np.clip(n, 0, Q_BLOCK)), -1) for d in DILATIONS])
    dist = Q_BLOCK - np.arange(Q_BLOCK)
    sample = np.stack([np.broadcast_to(_t5_bucket(d * dist)[:, None, None], (Q_BLOCK, N_HEADS, HEAD_DIM))
                       for d in DILATIONS])
    return prompt.astype(np.int32), sample.astype(np.int32)


def _bias_kernel(rb_ref, rbx_ref, bp_ref, bs_ref, op_ref, os_ref):
    bp = bp_ref[0]
    bs = bs_ref[0]
    acc_s = jnp.zeros(bs.shape, F32)
    for bk in range(N_BUCKETS):
        acc_s = jnp.where(bs == bk, rbx_ref[bk], acc_s)
    os_ref[0] = acc_s
    for h in range(N_HEADS):
        acc = jnp.full(bp.shape, NEG, F32)
        for bk in range(N_BUCKETS):
            acc = jnp.where(bp == bk, rb_ref[bk, h], acc)
        op_ref[0, h] = acc


def _bias_build(rel_bias):
    bp, bs = _bias_tables()
    rbx = jnp.broadcast_to(rel_bias[:, :, None], (N_BUCKETS, N_HEADS, HEAD_DIM))
    nb = len(DILATIONS)
    return pl.pallas_call(
        _bias_kernel,
        grid=(nb,),
        in_specs=[pl.BlockSpec(memory_space=pltpu.SMEM),
                  pl.BlockSpec((N_BUCKETS, N_HEADS, HEAD_DIM), lambda i: (0, 0, 0)),
                  pl.BlockSpec((1, Q_BLOCK, 2 * Q_BLOCK), lambda i: (i, 0, 0)),
                  pl.BlockSpec((1, Q_BLOCK, N_HEADS, HEAD_DIM), lambda i: (i, 0, 0, 0))],
        out_specs=[pl.BlockSpec((1, N_HEADS, Q_BLOCK, 2 * Q_BLOCK), lambda i: (i, 0, 0, 0)),
                   pl.BlockSpec((1, Q_BLOCK, N_HEADS, HEAD_DIM), lambda i: (i, 0, 0, 0))],
        out_shape=[jax.ShapeDtypeStruct((nb, N_HEADS, Q_BLOCK, 2 * Q_BLOCK), F32),
                   jax.ShapeDtypeStruct((nb, Q_BLOCK, N_HEADS, HEAD_DIM), F32)],
        compiler_params=_params(("arbitrary",)),
        name="bias_build",
    )(rel_bias, rbx, jnp.asarray(bp), jnp.asarray(bs)), rbx[0]


def _attn_unit(q, kp, kc, vp, vc, bias_a, bias_b, prev_ok):
    lane = lax.broadcasted_iota(jnp.int32, q.shape, 1)
    is_a = lane < HEAD_DIM
    qs = q * ATT_SCALE
    kpb, kcb, vpb, vcb = (t.astype(BF16) for t in (kp, kc, vp, vc))

    def head(qh, bias):
        lp = lax.dot_general(qh, kpb, NT, preferred_element_type=F32) + bias[:, :Q_BLOCK]
        lc = lax.dot_general(qh, kcb, NT, preferred_element_type=F32) + bias[:, Q_BLOCK:]
        if prev_ok is not None:
            lp = jnp.where(prev_ok, lp, NEG)
        m = jnp.maximum(jnp.max(lp, axis=-1, keepdims=True), jnp.max(lc, axis=-1, keepdims=True))
        pp = jnp.exp(lp - m)
        pc = jnp.exp(lc - m)
        s = jnp.sum(pp, axis=-1, keepdims=True) + jnp.sum(pc, axis=-1, keepdims=True)
        num = (jnp.dot(pp.astype(BF16), vpb, preferred_element_type=F32)
               + jnp.dot(pc.astype(BF16), vcb, preferred_element_type=F32))
        return num, m, s

    n_a, m_a, s_a = head(jnp.where(is_a, qs, 0.0).astype(BF16), bias_a)
    n_b, m_b, s_b = head(jnp.where(is_a, 0.0, qs).astype(BF16), bias_b)
    return jnp.where(is_a, n_a, n_b), jnp.where(is_a, m_a, m_b), jnp.where(is_a, s_a, s_b)


def _attn_kernel(q_ref, kc_ref, kp_ref, vc_ref, vp_ref, bias_ref, o_ref, num_ref, m_ref, s_ref):
    has_prev = pl.program_id(2) > 0
    for br, d in enumerate(DILATIONS):
        nblk = SUPER // (d * Q_BLOCK)
        for r in range(d):
            for blk in range(nblk):
                rows = lambda j: (pl.ds(r + d * Q_BLOCK * j, Q_BLOCK, stride=d) if d > 1
                                  else pl.ds(Q_BLOCK * j, Q_BLOCK))
                cur = rows(blk)
                if blk == 0:
                    prv = rows(nblk - 1)
                    kp, vp, ok = kp_ref[prv, :], vp_ref[prv, :], has_prev
                else:
                    prv = rows(blk - 1)
                    kp, vp, ok = kc_ref[prv, :], vc_ref[prv, :], None
                num, m, s = _attn_unit(q_ref[cur, :], kp, kc_ref[cur, :], vp, vc_ref[cur, :],
                                       bias_ref[br, 0], bias_ref[br, 1], ok)
                num_ref[br, cur, :] = num
                m_ref[br, cur, :] = m
                s_ref[br, cur, :] = s

    def merge(i, carry):
        rs = pl.ds(pl.multiple_of(i * Q_BLOCK, Q_BLOCK), Q_BLOCK)
        ms = [m_ref[br, rs, :] for br in range(len(DILATIONS))]
        m = functools.reduce(jnp.maximum, ms)
        num = 0.0
        den = 0.0
        for br in range(len(DILATIONS)):
            w = jnp.exp(ms[br] - m)
            num = num + w * num_ref[br, rs, :]
            den = den + w * s_ref[br, rs, :]
        o_ref[rs, :] = num / den
        return carry

    lax.fori_loop(0, SUPER // Q_BLOCK, merge, 0)


def _attn_prompt(qkv, bias_p, batch, seq):
    ns = seq // SUPER
    npair = ATT_WIDTH // 128
    blk = lambda col0, prev: pl.BlockSpec(
        (SUPER, 128),
        (lambda b, hp, sb: (b * ns + jnp.maximum(sb - 1, 0), col0 + hp)) if prev
        else (lambda b, hp, sb: (b * ns + sb, col0 + hp)))
    nb = len(DILATIONS)
    return pl.pallas_call(
        _attn_kernel,
        grid=(batch, npair, ns),
        in_specs=[blk(0, False), blk(npair, False), blk(npair, True), blk(2 * npair, False),
                  blk(2 * npair, True),
                  pl.BlockSpec((nb, 2, Q_BLOCK, 2 * Q_BLOCK), lambda b, hp, sb: (0, hp, 0, 0))],
        out_specs=pl.BlockSpec((SUPER, 128), lambda b, hp, sb: (b * ns + sb, hp)),
        out_shape=jax.ShapeDtypeStruct((batch * seq, ATT_WIDTH), F32),
        scratch_shapes=[pltpu.VMEM((nb, SUPER, 128), F32)] * 3,
        compiler_params=_params(("arbitrary", "arbitrary", "arbitrary")),
        name="attn_prompt",
    )(qkv, qkv, qkv, qkv, qkv, bias_p)


def _attn_sample_kernel(q_ref, kn_ref, vn_ref, k1_ref, k4_ref, k16_ref, v1_ref, v4_ref, v16_ref,
                        bias_ref, rb0_ref, o_ref):
    k_refs = (k1_ref, k4_ref, k16_ref)
    v_refs = (v1_ref, v4_ref, v16_ref)
    for bi in range(q_ref.shape[0]):
        qs = q_ref[bi] * ATT_SCALE
        v_new = vn_ref[bi]
        l_new = jnp.sum(kn_ref[bi] * qs, axis=-1, keepdims=True) + rb0_ref[...]
        parts = []
        for br in range(len(DILATIONS)):
            lg = jnp.sum(k_refs[br][bi, :, 0] * qs, axis=-1, keepdims=True) + bias_ref[br]
            m = jnp.maximum(jnp.max(lg, axis=0), l_new)
            p = jnp.exp(lg - m)
            p_new = jnp.exp(l_new - m)
            s = jnp.sum(p, axis=0) + p_new
            num = jnp.sum(p * v_refs[br][bi, :, 0], axis=0) + p_new * v_new
            parts.append((num, m, s))
        m = functools.reduce(jnp.maximum, [pt[1] for pt in parts])
        num = 0.0
        den = 0.0
        for num_g, m_g, s_g in parts:
            w = jnp.exp(m_g - m)
            num = num + w * num_g
            den = den + w * s_g
        o_ref[bi] = num / den


def _attn_sample(q, k_new, v_new, cache_k, cache_v, bias_s, rb0, bb):
    nb = q.shape[0]
    heads = lambda t: t.reshape(nb, N_HEADS, HEAD_DIM)
    one = pl.BlockSpec((bb, N_HEADS, HEAD_DIM), lambda i: (i, 0, 0))

    def views(cache):
        out = []
        for d in DILATIONS:
            arr = cache.reshape(nb, WIN // d, d, N_HEADS, HEAD_DIM)
            last = WIN // d // Q_BLOCK - 1
            out.append((arr, pl.BlockSpec((bb, Q_BLOCK, 1, N_HEADS, HEAD_DIM),
                                          lambda i, last=last: (i, last, 0, 0, 0))))
        return out

    kv = views(cache_k) + views(cache_v)
    fixed = lambda a: pl.BlockSpec(a.shape, lambda i: (0,) * a.ndim)
    out = pl.pallas_call(
        _attn_sample_kernel,
        grid=(nb // bb,),
        in_specs=[one, one, one] + [s for _, s in kv] + [fixed(bias_s), fixed(rb0)],
        out_specs=one,
        out_shape=jax.ShapeDtypeStruct((nb, N_HEADS, HEAD_DIM), F32),
        compiler_params=_params(("arbitrary",)),
        name="attn_sample",
    )(heads(q), heads(k_new), heads(v_new), *[a for a, _ in kv], bias_s, rb0)
    return out.reshape(nb, ATT_WIDTH)


SHIFT_SPLIT = 4


def _shift_copies(ck_ref, cv_ref, kn_ref, vn_ref, ok_ref, ov_ref, sem):
    nb = ck_ref.shape[0]
    per = nb // SHIFT_SPLIT
    copies = []
    for t, (c_ref, n_ref, o_ref) in enumerate(((ck_ref, kn_ref, ok_ref), (cv_ref, vn_ref, ov_ref))):
        for j in range(SHIFT_SPLIT):
            bs = pl.ds(j * per, per)
            copies.append(pltpu.make_async_copy(c_ref.at[bs, pl.ds(1, WIN - 1)],
                                                o_ref.at[bs, pl.ds(0, WIN - 1)],
                                                sem.at[t * (SHIFT_SPLIT + 1) + j]))
        copies.append(pltpu.make_async_copy(n_ref, o_ref.at[:, pl.ds(WIN - 1, 1)],
                                            sem.at[t * (SHIFT_SPLIT + 1) + SHIFT_SPLIT]))
    return copies


def _shift_kernel(ck_ref, cv_ref, kn_ref, vn_ref, ok_ref, ov_ref, sem):
    copies = _shift_copies(ck_ref, cv_ref, kn_ref, vn_ref, ok_ref, ov_ref, sem)
    for cp in copies:
        cp.start()
    for cp in copies:
        cp.wait()


def _window_shift(cache_k, cache_v, k_new, v_new):
    nb = cache_k.shape[0]
    any_spec = pl.BlockSpec(memory_space=pl.ANY)
    row = lambda t: t.reshape(nb, 1, N_HEADS, HEAD_DIM)
    return pl.pallas_call(
        _shift_kernel,
        in_specs=[any_spec] * 4,
        out_specs=[any_spec] * 2,
        out_shape=[jax.ShapeDtypeStruct(cache_k.shape, F32)] * 2,
        scratch_shapes=[pltpu.SemaphoreType.DMA((2 * (SHIFT_SPLIT + 1),))],
        name="window_shift",
    )(cache_k, cache_v, row(k_new), row(v_new))


def _outproj_kernel(x_ref, attn_ref, y_ref, gate_ref, bonus_ref, lnw_ref, lnb_ref, wo_ref, gffn_ref,
                    ones_ref, h_ref, xn_ref):
    ones_bd = ones_ref[...]
    y = y_ref[...]
    mu = _segsum(y, ones_bd) * (1.0 / HEAD_DIM)
    dy = y - mu
    var = _segsum(dy * dy, ones_bd) * (1.0 / HEAD_DIM)
    yn = dy * lax.rsqrt(var + GN_EPS) * lnw_ref[...] + lnb_ref[...]
    rw = (yn + bonus_ref[...]) * gate_ref[...]
    mixed = (jnp.dot(attn_ref[...].astype(BF16), wo_ref[0:ATT_WIDTH, :], preferred_element_type=F32)
             + jnp.dot(rw.astype(BF16), wo_ref[ATT_WIDTH:D_MODEL, :], preferred_element_type=F32))
    h = x_ref[...] + mixed
    h_ref[...] = h
    xn_ref[...] = _rmsnorm(h, gffn_ref[...]).astype(BF16)


def _outproj(x2d, attn, y, gate, bonus, ln_w, ln_b, w_o, g_ffn, ones_bd, tm):
    m = x2d.shape[0]
    row = lambda i: (i, 0)
    fixed = lambda i: (0, 0)
    half = pl.BlockSpec((tm, RWKV_WIDTH), row)
    full = pl.BlockSpec((tm, D_MODEL), row)
    return pl.pallas_call(
        _outproj_kernel,
        grid=(m // tm,),
        in_specs=[full, half, half, half, half,
                  pl.BlockSpec((1, RWKV_WIDTH), fixed), pl.BlockSpec((1, RWKV_WIDTH), fixed),
                  pl.BlockSpec((D_MODEL, D_MODEL), fixed), pl.BlockSpec((1, D_MODEL), fixed),
                  pl.BlockSpec((RWKV_WIDTH, RWKV_WIDTH), fixed)],
        out_specs=[full, full],
        out_shape=[jax.ShapeDtypeStruct((m, D_MODEL), F32), jax.ShapeDtypeStruct((m, D_MODEL), BF16)],
        compiler_params=_params(("arbitrary",)),
        name="outproj",
    )(x2d, attn, y, gate, bonus, ln_w, ln_b, w_o, g_ffn, ones_bd)


def _ffn_tail(xn, h, gp_m1, gp_m2, gp_of, wg_ref, wu_ref, cw_ref, cb_ref, wout_ref, gfin_ref, o_ref):
    acc = h
    for c in range(D_FF // FF_CHUNK):
        cs = slice(c * FF_CHUNK, (c + 1) * FF_CHUNK)
        gp = gp_of(c)
        up = jnp.dot(xn, wu_ref[:, cs], preferred_element_type=F32)
        conv = (cb_ref[:, cs] + cw_ref[0:1, cs] * gp_m2(c, gp) + cw_ref[1:2, cs] * gp_m1(c, gp)
                + cw_ref[2:3, cs] * gp)
        act = conv * _sigmoid(conv) * up
        acc = acc + jnp.dot(act.astype(BF16), wout_ref[cs, :], preferred_element_type=F32)
    o_ref[...] = _rmsnorm(acc, gfin_ref[...])


def _ffn_prompt_kernel(xn_ref, h_ref, wg_ref, wu_ref, cw_ref, cb_ref, wout_ref, gfin_ref,
                       o_ref, conv_ref, carry_ref):
    @pl.when(pl.program_id(1) == 0)
    def _():
        carry_ref[...] = jnp.zeros_like(carry_ref)

    xn = xn_ref[...]
    tm = xn.shape[0]
    rowi = lax.broadcasted_iota(jnp.int32, (tm, FF_CHUNK), 0)
    gps = {}

    def gp_of(c):
        cs = slice(c * FF_CHUNK, (c + 1) * FF_CHUNK)
        gps[c] = jnp.dot(xn, wg_ref[:, cs], preferred_element_type=F32)
        return gps[c]

    def gp_m1(c, gp):
        cs = slice(c * FF_CHUNK, (c + 1) * FF_CHUNK)
        return jnp.where(rowi == 0, carry_ref[1:2, cs], pltpu.roll(gp, 1, 0))

    def gp_m2(c, gp):
        cs = slice(c * FF_CHUNK, (c + 1) * FF_CHUNK)
        return jnp.where(rowi == 0, carry_ref[0:1, cs],
                         jnp.where(rowi == 1, carry_ref[1:2, cs], pltpu.roll(gp, 2, 0)))

    _ffn_tail(xn, h_ref[...], gp_m1, gp_m2, gp_of, wg_ref, wu_ref, cw_ref, cb_ref, wout_ref, gfin_ref, o_ref)
    for c, gp in gps.items():
        cs = slice(c * FF_CHUNK, (c + 1) * FF_CHUNK)
        carry_ref[:, cs] = gp[tm - 2:tm, :]
        conv_ref[0, :, cs] = gp[tm - 2:tm, :]


def _ffn_sample_kernel(xn_ref, h_ref, p1_ref, p2_ref, wg_ref, wu_ref, cw_ref, cb_ref, wout_ref, gfin_ref,
                       o_ref, gp_ref):
    xn = xn_ref[...]

    def gp_of(c):
        cs = slice(c * FF_CHUNK, (c + 1) * FF_CHUNK)
        gp = jnp.dot(xn, wg_ref[:, cs], preferred_element_type=F32)
        gp_ref[:, cs] = gp
        return gp

    gp_m1 = lambda c, gp: p1_ref[:, c * FF_CHUNK:(c + 1) * FF_CHUNK]
    gp_m2 = lambda c, gp: p2_ref[:, c * FF_CHUNK:(c + 1) * FF_CHUNK]
    _ffn_tail(xn, h_ref[...], gp_m1, gp_m2, gp_of, wg_ref, wu_ref, cw_ref, cb_ref, wout_ref, gfin_ref, o_ref)


def _ffn_weight_specs(fixed):
    once = pl.Buffered(1)
    return [pl.BlockSpec((D_MODEL, D_FF), fixed, pipeline_mode=once),
            pl.BlockSpec((D_MODEL, D_FF), fixed, pipeline_mode=once),
            pl.BlockSpec((CONV_W, D_FF), fixed), pl.BlockSpec((1, D_FF), fixed),
            pl.BlockSpec((D_FF, D_MODEL), fixed, pipeline_mode=once), pl.BlockSpec((1, D_MODEL), fixed)]


def _ffn_prompt(xn, h, fw, batch, seq, tm):
    nt = seq // tm
    row = lambda b, j: (b * nt + j, 0)
    fixed = lambda b, j: (0, 0)
    full = pl.BlockSpec((tm, D_MODEL), row)
    return pl.pallas_call(
        _ffn_prompt_kernel,
        grid=(batch, nt),
        in_specs=[full, full] + _ffn_weight_specs(fixed),
        out_specs=[full, pl.BlockSpec((1, CONV_W - 1, D_FF), lambda b, j: (b, 0, 0))],
        out_shape=[jax.ShapeDtypeStruct((batch * seq, D_MODEL), F32),
                   jax.ShapeDtypeStruct((batch, CONV_W - 1, D_FF), F32)],
        scratch_shapes=[pltpu.VMEM((CONV_W - 1, D_FF), F32)],
        compiler_params=_params(("arbitrary", "arbitrary")),
        name="ffn_prompt",
    )(xn, h, *fw)


def _ffn_sample(xn, h, prev1, prev2, fw):
    m = xn.shape[0]
    fixed = lambda i: (0, 0)
    full = pl.BlockSpec((m, D_MODEL), fixed)
    ffs = pl.BlockSpec((m, D_FF), fixed)
    return pl.pallas_call(
        _ffn_sample_kernel,
        grid=(1,),
        in_specs=[full, full, ffs, ffs] + _ffn_weight_specs(fixed),
        out_specs=[full, ffs],
        out_shape=[jax.ShapeDtypeStruct((m, D_MODEL), F32), jax.ShapeDtypeStruct((m, D_FF), F32)],
        compiler_params=_params(("arbitrary",)),
        name="ffn_sample",
    )(xn, h, prev1, prev2, *fw)


def _layer_weights(g_mix, w_in, tok_mu, w0, w_decay_up, a0, w_iclr_up, w_gate_up, k_k, k_a, r_k,
                   ln_x_w, ln_x_b, w_o, g_ffn, w_ffn_in, conv_w, conv_b, w_ffn_out, g_final):
    vec = lambda t: t.reshape(1, -1)
    pad = LORA_PAD - LORA_W
    w_cat = jnp.pad(w_in, ((0, 0), (0, pad))).astype(BF16)
    mu_p = vec(tok_mu[:RKV_W])
    mu_l = vec(jnp.pad(tok_mu[RKV_W:], (0, pad)))
    lora_up = jnp.zeros((LORA_PAD, RKV_W), F32)
    lora_up = lora_up.at[0:LORA_DECAY, 0:RWKV_WIDTH].set(w_decay_up)
    lora_up = lora_up.at[LORA_DECAY:LORA_DECAY + LORA_ICLR, RWKV_WIDTH:2 * RWKV_WIDTH].set(w_iclr_up)
    lora_up = lora_up.at[LORA_DECAY + LORA_ICLR:LORA_W, 2 * RWKV_WIDTH:].set(w_gate_up)
    seg = np.arange(RWKV_WIDTH) // HEAD_DIM
    ones_bd = jnp.asarray(seg[:, None] == seg[None, :], BF16)
    prep = (mu_p, mu_l, lora_up.astype(BF16), vec(w0), vec(a0), vec(k_k), vec(k_a), vec(r_k), ones_bd)
    ffn = (w_ffn_in[:, :D_FF].astype(BF16), w_ffn_in[:, D_FF:].astype(BF16), conv_w, vec(conv_b),
           w_ffn_out.astype(BF16), vec(g_final))
    return dict(g_mix=vec(g_mix), w_cat=w_cat, prep=prep, ones_bd=ones_bd, ln_w=vec(ln_x_w), ln_b=vec(ln_x_b),
                w_o=w_o.astype(BF16), g_ffn=vec(g_ffn), ffn=ffn)


def kernel(x_prompt, x_sample, cache_win_k, cache_win_v, state_shift, state_wkv, state_ffn_conv, g_mix, w_in, rel_bias, tok_mu, w0, w_decay_up, a0, w_iclr_up, w_gate_up, k_k, k_a, r_k, ln_x_w, ln_x_b, w_o, g_ffn, w_ffn_in, conv_w, conv_b, w_ffn_out, g_final):
    batch, seq, _ = x_prompt.shape
    nb = x_sample.shape[0]
    lw = _layer_weights(g_mix[0], w_in[0], tok_mu[0], w0[0], w_decay_up[0], a0[0], w_iclr_up[0],
                        w_gate_up[0], k_k[0], k_a[0], r_k[0].reshape(-1), ln_x_w[0], ln_x_b[0], w_o[0],
                        g_ffn[0], w_ffn_in[0], conv_w[0], conv_b[0], w_ffn_out[0], g_final)
    (bias_p, bias_s), rb0 = _bias_build(rel_bias)

    xp = x_prompt.reshape(batch * seq, D_MODEL)
    qkv, rkv, lora = _inproj(xp, lw['g_mix'], lw['w_cat'], 512)
    r, lgw, k2, v, kk, b, gate, bonus = _prep_prompt(rkv, lora, lw['prep'], batch, seq, 512)
    y, wkv_p = _wkv_prompt(r, lgw, k2, v, kk, b, batch, seq)
    attn = _attn_prompt(qkv, bias_p, batch, seq)
    h1, xn2 = _outproj(xp, attn, y, gate, bonus, lw['ln_w'], lw['ln_b'], lw['w_o'], lw['g_ffn'],
                       lw['ones_bd'], 512)
    y_p, conv_p = _ffn_prompt(xn2, h1, lw['ffn'], batch, seq, 512)
    qkv3 = qkv.reshape(batch, seq, QKV_W)
    win_k_p = qkv3[:, seq - WIN:, ATT_WIDTH:2 * ATT_WIDTH].reshape(1, batch, WIN, N_HEADS, HEAD_DIM)
    win_v_p = qkv3[:, seq - WIN:, 2 * ATT_WIDTH:].reshape(1, batch, WIN, N_HEADS, HEAD_DIM)
    shift_p = jnp.concatenate([rkv.reshape(batch, seq, RKV_W)[:, -1],
                               lora.reshape(batch, seq, LORA_PAD)[:, -1, :LORA_W]], axis=-1)[None]

    xs = x_sample.reshape(nb, D_MODEL)
    qkv_s, rkv_s, lora_s = _inproj(xs, lw['g_mix'], lw['w_cat'], nb)
    sh = state_shift[0]
    r, lgw, k2, v, kk, b, gate, bonus = _prep_sample(
        rkv_s, sh[:, :RKV_W], lora_s, jnp.pad(sh[:, RKV_W:], ((0, 0), (0, LORA_PAD - LORA_W))), lw['prep'])
    y_s, wkv_s = _wkv_sample(state_wkv[0], lgw, kk, b, k2, r, v, 8)
    q_s = qkv_s[:, :ATT_WIDTH]
    k_s = qkv_s[:, ATT_WIDTH:2 * ATT_WIDTH]
    v_s = qkv_s[:, 2 * ATT_WIDTH:]
    ck = cache_win_k[0]
    cv = cache_win_v[0]
    attn_s = _attn_sample(q_s, k_s, v_s, ck, cv, bias_s, rb0, 4)
    win_k_s, win_v_s = _window_shift(ck, cv, k_s, v_s)
    h1_s, xn2_s = _outproj(xs, attn_s, y_s, gate, bonus, lw['ln_w'], lw['ln_b'], lw['w_o'], lw['g_ffn'],
                           lw['ones_bd'], nb)
    conv_state = state_ffn_conv[0]
    y_smp, gp_s = _ffn_sample(xn2_s, h1_s, conv_state[:, 1], conv_state[:, 0], lw['ffn'])
    conv_s = jnp.stack([conv_state[:, 1], gp_s], axis=1)
    shift_s = jnp.concatenate([rkv_s, lora_s[:, :LORA_W]], axis=-1)[None]

    return (y_p.reshape(batch, seq, D_MODEL), y_smp.reshape(nb, 1, D_MODEL),
            win_k_p, win_v_p, shift_p, wkv_p[None], conv_p[None],
            win_k_s[None], win_v_s[None], shift_s, wkv_s[None], conv_s[None])
```

```python
import functools

import numpy as np
import jax
import jax.numpy as jnp
from jax import lax
from jax.experimental import pallas as pl
from jax.experimental.pallas import tpu as pltpu

F32 = jnp.float32
BF16 = jnp.bfloat16

D_MODEL = 1024
HEAD_DIM = 64
ATT_WIDTH = 512
RWKV_WIDTH = 512
N_HEADS = 8
QKV_W = 3 * ATT_WIDTH
RKV_W = 3 * RWKV_WIDTH
LORA_DECAY, LORA_ICLR, LORA_GATE = 32, 32, 96
LORA_W = LORA_DECAY + LORA_ICLR + LORA_GATE
LORA_PAD = 256
RWKV_PROJ = RKV_W + LORA_W
PROJ_PAD = QKV_W + RKV_W + LORA_PAD
D_FF = 2816
FF_CHUNK = 1408
CONV_W = 3
WIN = 2048
DILATIONS = (1, 4, 16)
Q_BLOCK = 128
SUPER = 2048
N_BUCKETS = 32
MAX_DISTANCE = 2048
NORM_EPS = 1e-6
GN_EPS = 64e-5
NEG = -1e30
ATT_SCALE = HEAD_DIM ** -0.5
CHUNK = 64
VMEM_LIMIT = 56 * 1024 * 1024

NT = (((1,), (1,)), ((), ()))
TN = (((0,), (0,)), ((), ()))


def _dot(a, b):
    return jnp.dot(a.astype(BF16), b.astype(BF16), preferred_element_type=F32)


def _dot_nt(a, b):
    return lax.dot_general(a.astype(BF16), b.astype(BF16), NT, preferred_element_type=F32)


def _dot_tn(a, b):
    return lax.dot_general(a.astype(BF16), b.astype(BF16), TN, preferred_element_type=F32)


def _split2(x):
    hi = x.astype(BF16)
    lo = (x - hi.astype(F32)).astype(BF16)
    return hi, lo


def _split3(x):
    hi = x.astype(BF16)
    r1 = x - hi.astype(F32)
    mid = r1.astype(BF16)
    lo = (r1 - mid.astype(F32)).astype(BF16)
    return hi, mid, lo


def _segsum(x, ones_bd):
    hi, lo = _split2(x)
    return (jnp.dot(hi, ones_bd, preferred_element_type=F32)
            + jnp.dot(lo, ones_bd, preferred_element_type=F32))


def _sigmoid(x):
    return 1.0 / (1.0 + jnp.exp(-x))


def _rmsnorm(x, g):
    return x * lax.rsqrt(jnp.mean(x * x, axis=-1, keepdims=True) + NORM_EPS) * g


def _params(sem):
    return pltpu.CompilerParams(dimension_semantics=sem, vmem_limit_bytes=VMEM_LIMIT)


def _t5_bucket(dist):
    dist = np.asarray(dist, dtype=np.int64)
    exact = N_BUCKETS // 2
    scaled = np.log(np.maximum(dist, 1) / exact) / np.log(MAX_DISTANCE / exact)
    large = np.minimum(exact + (scaled * (N_BUCKETS - exact)).astype(np.int64), N_BUCKETS - 1)
    return np.where(dist < exact, dist, large).astype(np.int32)


def _inproj_kernel(x_ref, g_ref, w_ref, qkv_ref, rkv_ref, lora_ref):
    xb = _rmsnorm(x_ref[...], g_ref[...]).astype(BF16)
    qkv_ref[...] = jnp.dot(xb, w_ref[:, 0:QKV_W], preferred_element_type=F32)
    rkv_ref[...] = jnp.dot(xb, w_ref[:, QKV_W:QKV_W + RKV_W], preferred_element_type=F32)
    lora_ref[...] = jnp.dot(xb, w_ref[:, QKV_W + RKV_W:PROJ_PAD], preferred_element_type=F32)


def _inproj(x2d, g, w, tm):
    m = x2d.shape[0]
    row = lambda i: (i, 0)
    fixed = lambda i: (0, 0)
    return pl.pallas_call(
        _inproj_kernel,
        grid=(m // tm,),
        in_specs=[pl.BlockSpec((tm, D_MODEL), row),
                  pl.BlockSpec((1, D_MODEL), fixed),
                  pl.BlockSpec((D_MODEL, PROJ_PAD), fixed)],
        out_specs=[pl.BlockSpec((tm, QKV_W), row),
                   pl.BlockSpec((tm, RKV_W), row),
                   pl.BlockSpec((tm, LORA_PAD), row)],
        out_shape=[jax.ShapeDtypeStruct((m, QKV_W), F32),
                   jax.ShapeDtypeStruct((m, RKV_W), F32),
                   jax.ShapeDtypeStruct((m, LORA_PAD), F32)],
        compiler_params=_params(("arbitrary",)),
        name="inproj",
    )(x2d, g, w)


def _prep_math(p, p_prev, l, l_prev, mu_p, mu_l, wcat, w0, a0, k_k, k_a, r_k, ones_bd):
    pm = p + mu_p * (p_prev - p)
    lm = l + mu_l * (l_prev - l)
    r = pm[:, 0:RWKV_WIDTH]
    kr = pm[:, RWKV_WIDTH:2 * RWKV_WIDTH]
    vr = pm[:, 2 * RWKV_WIDTH:3 * RWKV_WIDTH]
    lane = lax.broadcasted_iota(jnp.int32, lm.shape, 1)
    feat = jnp.where(lane < LORA_DECAY, jnp.tanh(lm),
                     jnp.where(lane < LORA_DECAY + LORA_ICLR, lm, _sigmoid(lm)))
    z = jnp.dot(feat.astype(BF16), wcat, preferred_element_type=F32)
    zd = -(w0 + z[:, 0:RWKV_WIDTH])
    softplus = jnp.maximum(zd, 0.0) + jnp.log(1.0 + jnp.exp(-jnp.abs(zd)))
    lw = -jnp.exp(-softplus - 0.5)
    a = _sigmoid(a0 + z[:, RWKV_WIDTH:2 * RWKV_WIDTH])
    gate = z[:, 2 * RWKV_WIDTH:3 * RWKV_WIDTH]
    kk = kr * k_k
    kk = kk / jnp.maximum(jnp.sqrt(_segsum(kk * kk, ones_bd)), 1e-12)
    k2 = kr * (1.0 + (a - 1.0) * k_a)
    bonus = _segsum(r * k2 * r_k, ones_bd) * vr
    return r, lw, k2, vr, kk, kk * a, gate, bonus


def _prep_prompt_kernel(p_ref, l_ref, mu_p_ref, mu_l_ref, wcat_ref, w0_ref, a0_ref, kk_ref, ka_ref,
                        rk_ref, ones_ref, *refs):
    outs, (cp_ref, cl_ref) = refs[:8], refs[8:]

    @pl.when(pl.program_id(1) == 0)
    def _():
        cp_ref[...] = jnp.zeros_like(cp_ref)
        cl_ref[...] = jnp.zeros_like(cl_ref)

    p = p_ref[...]
    l = l_ref[...]
    tm = p.shape[0]
    first_p = lax.broadcasted_iota(jnp.int32, p.shape, 0) == 0
    first_l = lax.broadcasted_iota(jnp.int32, l.shape, 0) == 0
    p_prev = jnp.where(first_p, cp_ref[...], pltpu.roll(p, 1, 0))
    l_prev = jnp.where(first_l, cl_ref[...], pltpu.roll(l, 1, 0))
    cp_ref[...] = p[tm - 1:tm, :]
    cl_ref[...] = l[tm - 1:tm, :]
    res = _prep_math(p, p_prev, l, l_prev, mu_p_ref[...], mu_l_ref[...], wcat_ref[...], w0_ref[...],
                     a0_ref[...], kk_ref[...], ka_ref[...], rk_ref[...], ones_ref[...])
    for o, v in zip(outs, res):
        o[...] = v


def _prep_sample_kernel(p_ref, pp_ref, l_ref, lp_ref, mu_p_ref, mu_l_ref, wcat_ref, w0_ref, a0_ref,
                        kk_ref, ka_ref, rk_ref, ones_ref, *outs):
    res = _prep_math(p_ref[...], pp_ref[...], l_ref[...], lp_ref[...], mu_p_ref[...], mu_l_ref[...],
                     wcat_ref[...], w0_ref[...], a0_ref[...], kk_ref[...], ka_ref[...], rk_ref[...],
                     ones_ref[...])
    for o, v in zip(outs, res):
        o[...] = v


def _prep_prompt(rkv, lora, pw, batch, seq, tm):
    nt = seq // tm
    row = lambda b, j: (b * nt + j, 0)
    fixed = lambda b, j: (0, 0)
    vec = lambda n: pl.BlockSpec((1, n), fixed)
    m = rkv.shape[0]
    return pl.pallas_call(
        _prep_prompt_kernel,
        grid=(batch, nt),
        in_specs=[pl.BlockSpec((tm, RKV_W), row), pl.BlockSpec((tm, LORA_PAD), row),
                  vec(RKV_W), vec(LORA_PAD), pl.BlockSpec((LORA_PAD, RKV_W), fixed),
                  vec(RWKV_WIDTH), vec(RWKV_WIDTH), vec(RWKV_WIDTH), vec(RWKV_WIDTH), vec(RWKV_WIDTH),
                  pl.BlockSpec((RWKV_WIDTH, RWKV_WIDTH), fixed)],
        out_specs=[pl.BlockSpec((tm, RWKV_WIDTH), row)] * 8,
        out_shape=[jax.ShapeDtypeStruct((m, RWKV_WIDTH), F32)] * 8,
        scratch_shapes=[pltpu.VMEM((1, RKV_W), F32), pltpu.VMEM((1, LORA_PAD), F32)],
        compiler_params=_params(("arbitrary", "arbitrary")),
        name="rwkv_prep_prompt",
    )(rkv, lora, *pw)


def _prep_sample(rkv, rkv_prev, lora, lora_prev, pw):
    m = rkv.shape[0]
    full = lambda a: pl.BlockSpec(a.shape, lambda i: (0,) * a.ndim)
    args = (rkv, rkv_prev, lora, lora_prev) + tuple(pw)
    return pl.pallas_call(
        _prep_sample_kernel,
        grid=(1,),
        in_specs=[full(a) for a in args],
        out_specs=[pl.BlockSpec((m, RWKV_WIDTH), lambda i: (0, 0))] * 8,
        out_shape=[jax.ShapeDtypeStruct((m, RWKV_WIDTH), F32)] * 8,
        compiler_params=_params(("arbitrary",)),
        name="rwkv_prep_sample",
    )(*args)


def _cumsum_rows(x, tri):
    tri_b = tri.astype(BF16)
    out = None
    for piece in _split3(x):
        t = jnp.dot(tri_b, piece, preferred_element_type=F32)
        out = t if out is None else out + t
    return out


def _wkv_kernel(r_ref, lw_ref, k_ref, v_ref, kk_ref, b_ref, y_ref, s_ref, st_ref):
    c = pl.program_id(1)

    @pl.when(c == 0)
    def _():
        st_ref[...] = jnp.zeros_like(st_ref)

    n = CHUNK
    row = lax.broadcasted_iota(jnp.int32, (n, n), 0)
    col = lax.broadcasted_iota(jnp.int32, (n, n), 1)
    lower_incl = row >= col
    lower_strict = row > col
    diag = row == col

    lw = lw_ref[...]
    lg = _cumsum_rows(lw, jnp.where(lower_incl, 1.0, 0.0))
    g = jnp.exp(lg)
    g_prev = jnp.exp(lg - lw)
    g_inv = jnp.exp(-lg)
    lg_end = lg[n - 1:n, :]
    g_to_end = jnp.exp(lg_end - lg)
    g_end = jnp.exp(lg_end)
    kk = kk_ref[...]
    b = b_ref[...]
    k = k_ref[...]
    a_t = -kk * g_prev
    b_t = b * g_inv
    k_t = k * g_inv
    r_t = r_ref[...] * g
    b_e = b * g_to_end
    k_e = k * g_to_end
    v = v_ref[...]

    for h in range(N_HEADS):
        sl = slice(h * HEAD_DIM, (h + 1) * HEAD_DIM)
        ah, bh, kh, rh, beh, keh, vh = (t[:, sl] for t in (a_t, b_t, k_t, r_t, b_e, k_e, v))
        l_ab = jnp.where(lower_strict, _dot_nt(ah, bh), 0.0)
        l_ak = jnp.where(lower_strict, _dot_nt(ah, kh), 0.0)
        m_rb = jnp.where(lower_incl, _dot_nt(rh, bh), 0.0)
        m_rk = jnp.where(lower_incl, _dot_nt(rh, kh), 0.0)
        tm_ = jnp.where(diag, 1.0, 0.0) + l_ab
        lp = l_ab
        span = 1
        while 2 * span < n:
            lp = _dot(lp, lp)
            tm_ = tm_ + _dot(tm_, lp)
            span *= 2
        a_hat = _dot(tm_, ah)
        u0 = _dot(tm_, _dot(l_ak, vh))
        s0 = st_ref[h]
        u = _dot_nt(a_hat, s0) + u0
        y = _dot_nt(rh, s0) + _dot(m_rb, u) + _dot(m_rk, vh)
        y_ref[:, sl] = y
        s_new = s0 * g_end[:, sl] + _dot_tn(u, beh) + _dot_tn(vh, keh)
        st_ref[h] = s_new

    @pl.when(c == pl.num_programs(1) - 1)
    def _():
        s_ref[0] = st_ref[...]


def _wkv_prompt(r, lw, k2, v, kk, b, batch, seq):
    nc = seq // CHUNK
    row = lambda bi, c: (bi * nc + c, 0)
    spec = pl.BlockSpec((CHUNK, RWKV_WIDTH), row)
    return pl.pallas_call(
        _wkv_kernel,
        grid=(batch, nc),
        in_specs=[spec] * 6,
        out_specs=[spec, pl.BlockSpec((1, N_HEADS, HEAD_DIM, HEAD_DIM), lambda bi, c: (bi, 0, 0, 0))],
        out_shape=[jax.ShapeDtypeStruct((batch * seq, RWKV_WIDTH), F32),
                   jax.ShapeDtypeStruct((batch, N_HEADS, HEAD_DIM, HEAD_DIM), F32)],
        scratch_shapes=[pltpu.VMEM((N_HEADS, HEAD_DIM, HEAD_DIM), F32)],
        compiler_params=_params(("arbitrary", "arbitrary")),
        name="wkv_prompt",
    )(r, lw, k2, v, kk, b)


def _wkv_step_kernel(s_ref, lw_ref, kk_ref, b_ref, k_ref, r_ref, v_ref, y_ref, so_ref):
    s = s_ref[...]
    sa = jnp.sum(s * (-kk_ref[...]), axis=-1, keepdims=True)
    s = s * jnp.exp(lw_ref[...]) + sa * b_ref[...] + v_ref[...] * k_ref[...]
    so_ref[...] = s
    y_ref[...] = jnp.sum(s * r_ref[...], axis=-1, keepdims=True)


def _wkv_sample(state, lw, kk, b, k2, r, v, bb):
    nb = state.shape[0]
    rowv = lambda t: t.reshape(nb, N_HEADS, 1, HEAD_DIM)
    colv = lambda t: t.reshape(nb, N_HEADS, HEAD_DIM, 1)
    idx = lambda i: (i, 0, 0, 0)
    s_spec = pl.BlockSpec((bb, N_HEADS, HEAD_DIM, HEAD_DIM), idx)
    r_spec = pl.BlockSpec((bb, N_HEADS, 1, HEAD_DIM), idx)
    c_spec = pl.BlockSpec((bb, N_HEADS, HEAD_DIM, 1), idx)
    y, s_new = pl.pallas_call(
        _wkv_step_kernel,
        grid=(nb // bb,),
        in_specs=[s_spec] + [r_spec] * 5 + [c_spec],
        out_specs=[c_spec, s_spec],
        out_shape=[jax.ShapeDtypeStruct((nb, N_HEADS, HEAD_DIM, 1), F32),
                   jax.ShapeDtypeStruct(state.shape, F32)],
        compiler_params=_params(("arbitrary",)),
        name="wkv_sample",
    )(state, rowv(lw), rowv(kk), rowv(b), rowv(k2), rowv(r), colv(v))
    return y.reshape(nb, RWKV_WIDTH), s_new


def _bias_tables():
    i = np.arange(Q_BLOCK)[:, None]
    c = np.arange(2 * Q_BLOCK)[None, :]
    n = i + Q_BLOCK - c
    ok = (n >= 0) & (n <= Q_BLOCK)
    prompt = np.stack([np.where(ok, _t5_bucket(d * np.clip(n, 0, Q_BLOCK)), -1) for d in DILATIONS])
    dist = WIN - np.arange(WIN)
    sample = np.stack([np.where((dist % d == 0) & (dist // d <= Q_BLOCK), _t5_bucket(dist), -1)
                       for d in DILATIONS])
    sample = np.broadcast_to(sample[:, None, :], (len(DILATIONS), N_HEADS, WIN))
    return prompt.astype(np.int32), sample.astype(np.int32)


def _bias_kernel(rb_ref, rbc_ref, bp_ref, bs_ref, op_ref, os_ref):
    bp = bp_ref[0]
    bs = bs_ref[0]
    acc_s = jnp.full(bs.shape, NEG, F32)
    for bk in range(N_BUCKETS):
        acc_s = jnp.where(bs == bk, rbc_ref[bk], acc_s)
    os_ref[0] = acc_s
    for h in range(N_HEADS):
        acc = jnp.full(bp.shape, NEG, F32)
        for bk in range(N_BUCKETS):
            acc = jnp.where(bp == bk, rb_ref[bk, h], acc)
        op_ref[0, h] = acc


def _bias_build(rel_bias):
    bp, bs = _bias_tables()
    rbc = rel_bias.reshape(N_BUCKETS, N_HEADS, 1)
    nb = len(DILATIONS)
    return pl.pallas_call(
        _bias_kernel,
        grid=(nb,),
        in_specs=[pl.BlockSpec(memory_space=pltpu.SMEM),
                  pl.BlockSpec((N_BUCKETS, N_HEADS, 1), lambda i: (0, 0, 0)),
                  pl.BlockSpec((1, Q_BLOCK, 2 * Q_BLOCK), lambda i: (i, 0, 0)),
                  pl.BlockSpec((1, N_HEADS, WIN), lambda i: (i, 0, 0))],
        out_specs=[pl.BlockSpec((1, N_HEADS, Q_BLOCK, 2 * Q_BLOCK), lambda i: (i, 0, 0, 0)),
                   pl.BlockSpec((1, N_HEADS, WIN), lambda i: (i, 0, 0))],
        out_shape=[jax.ShapeDtypeStruct((nb, N_HEADS, Q_BLOCK, 2 * Q_BLOCK), F32),
                   jax.ShapeDtypeStruct((nb, N_HEADS, WIN), F32)],
        compiler_params=_params(("arbitrary",)),
        name="bias_build",
    )(rel_bias, rbc, jnp.asarray(bp), jnp.asarray(bs)), rbc[0]


def _attn_unit(q, kp, kc, vp, vc, bias_a, bias_b, prev_ok):
    lane = lax.broadcasted_iota(jnp.int32, q.shape, 1)
    is_a = lane < HEAD_DIM
    qs = q * ATT_SCALE
    kpb, kcb, vpb, vcb = (t.astype(BF16) for t in (kp, kc, vp, vc))

    def head(qh, bias):
        lp = lax.dot_general(qh, kpb, NT, preferred_element_type=F32) + bias[:, :Q_BLOCK]
        lc = lax.dot_general(qh, kcb, NT, preferred_element_type=F32) + bias[:, Q_BLOCK:]
        if prev_ok is not None:
            lp = jnp.where(prev_ok, lp, NEG)
        m = jnp.maximum(jnp.max(lp, axis=-1, keepdims=True), jnp.max(lc, axis=-1, keepdims=True))
        pp = jnp.exp(lp - m)
        pc = jnp.exp(lc - m)
        s = jnp.sum(pp, axis=-1, keepdims=True) + jnp.sum(pc, axis=-1, keepdims=True)
        num = (jnp.dot(pp.astype(BF16), vpb, preferred_element_type=F32)
               + jnp.dot(pc.astype(BF16), vcb, preferred_element_type=F32))
        return num, m, s

    n_a, m_a, s_a = head(jnp.where(is_a, qs, 0.0).astype(BF16), bias_a)
    n_b, m_b, s_b = head(jnp.where(is_a, 0.0, qs).astype(BF16), bias_b)
    return jnp.where(is_a, n_a, n_b), jnp.where(is_a, m_a, m_b), jnp.where(is_a, s_a, s_b)


def _attn_kernel(q_ref, kc_ref, kp_ref, vc_ref, vp_ref, bias_ref, o_ref, num_ref, m_ref, s_ref):
    has_prev = pl.program_id(2) > 0
    for br, d in enumerate(DILATIONS):
        nblk = SUPER // (d * Q_BLOCK)
        for r in range(d):
            for blk in range(nblk):
                rows = lambda j: (pl.ds(r + d * Q_BLOCK * j, Q_BLOCK, stride=d) if d > 1
                                  else pl.ds(Q_BLOCK * j, Q_BLOCK))
                cur = rows(blk)
                if blk == 0:
                    prv = rows(nblk - 1)
                    kp, vp, ok = kp_ref[prv, :], vp_ref[prv, :], has_prev
                else:
                    prv = rows(blk - 1)
                    kp, vp, ok = kc_ref[prv, :], vc_ref[prv, :], None
                num, m, s = _attn_unit(q_ref[cur, :], kp, kc_ref[cur, :], vp, vc_ref[cur, :],
                                       bias_ref[br, 0], bias_ref[br, 1], ok)
                num_ref[br, cur, :] = num
                m_ref[br, cur, :] = m
                s_ref[br, cur, :] = s

    def merge(i, carry):
        rs = pl.ds(pl.multiple_of(i * Q_BLOCK, Q_BLOCK), Q_BLOCK)
        ms = [m_ref[br, rs, :] for br in range(len(DILATIONS))]
        m = functools.reduce(jnp.maximum, ms)
        num = 0.0
        den = 0.0
        for br in range(len(DILATIONS)):
            w = jnp.exp(ms[br] - m)
            num = num + w * num_ref[br, rs, :]
            den = den + w * s_ref[br, rs, :]
        o_ref[rs, :] = num / den
        return carry

    lax.fori_loop(0, SUPER // Q_BLOCK, merge, 0)


def _attn_prompt(qkv, bias_p, batch, seq):
    ns = seq // SUPER
    npair = ATT_WIDTH // 128
    blk = lambda col0, prev: pl.BlockSpec(
        (SUPER, 128),
        (lambda b, hp, sb: (b * ns + jnp.maximum(sb - 1, 0), col0 + hp)) if prev
        else (lambda b, hp, sb: (b * ns + sb, col0 + hp)))
    nb = len(DILATIONS)
    return pl.pallas_call(
        _attn_kernel,
        grid=(batch, npair, ns),
        in_specs=[blk(0, False), blk(npair, False), blk(npair, True), blk(2 * npair, False),
                  blk(2 * npair, True),
                  pl.BlockSpec((nb, 2, Q_BLOCK, 2 * Q_BLOCK), lambda b, hp, sb: (0, hp, 0, 0))],
        out_specs=pl.BlockSpec((SUPER, 128), lambda b, hp, sb: (b * ns + sb, hp)),
        out_shape=jax.ShapeDtypeStruct((batch * seq, ATT_WIDTH), F32),
        scratch_shapes=[pltpu.VMEM((nb, SUPER, 128), F32)] * 3,
        compiler_params=_params(("arbitrary", "arbitrary", "arbitrary")),
        name="attn_prompt",
    )(qkv, qkv, qkv, qkv, qkv, bias_p)


def _sample_window_kernel(q_ref, kn_ref, qc_ref, knc_ref, vnc_ref, ck_ref, cv_ref, bias_ref, rb0_ref,
                          o_ref, ok_ref, ov_ref, wt_ref):
    lane = lax.broadcasted_iota(jnp.int32, (HEAD_DIM, WIN), 1)
    last = lane == WIN - 1
    l_new = jnp.sum(kn_ref[0] * (q_ref[0] * ATT_SCALE), axis=-1, keepdims=True) + rb0_ref[...]
    rows = []
    for h in range(N_HEADS):
        kh = ck_ref[0, h]
        rows.append(jnp.sum(kh * (qc_ref[0, h] * ATT_SCALE), axis=0, keepdims=True))
        ok_ref[0, h] = jnp.where(last, knc_ref[0, h], pltpu.roll(kh, WIN - 1, 1))
    logits = jnp.concatenate(rows, axis=0)
    parts = []
    for br in range(len(DILATIONS)):
        lg = logits + bias_ref[br]
        m = jnp.maximum(jnp.max(lg, axis=-1, keepdims=True), l_new)
        p = jnp.exp(lg - m)
        p_new = jnp.exp(l_new - m)
        parts.append((p, p_new, m, jnp.sum(p, axis=-1, keepdims=True) + p_new))
    m = functools.reduce(jnp.maximum, [pt[2] for pt in parts])
    wt = 0.0
    w_new = 0.0
    den = 0.0
    for p, p_new, m_g, s_g in parts:
        w = jnp.exp(m_g - m)
        wt = wt + w * p
        w_new = w_new + w * p_new
        den = den + w * s_g
    wt_ref[...] = wt / den
    w_new = w_new / den
    for h in range(N_HEADS):
        vh = cv_ref[0, h]
        o_ref[0, h] = (jnp.sum(vh * wt_ref[h:h + 1, :], axis=1, keepdims=True)
                       + vnc_ref[0, h] * w_new[h:h + 1, :])
        ov_ref[0, h] = jnp.where(last, vnc_ref[0, h], pltpu.roll(vh, WIN - 1, 1))


def _sample_window(q, k_new, v_new, cache_k, cache_v, bias_s, rb0):
    nb = q.shape[0]
    rows = lambda t: t.reshape(nb, N_HEADS, HEAD_DIM)
    cols = lambda t: t.reshape(nb, N_HEADS, HEAD_DIM, 1)
    row_spec = pl.BlockSpec((1, N_HEADS, HEAD_DIM), lambda i: (i, 0, 0))
    col_spec = pl.BlockSpec((1, N_HEADS, HEAD_DIM, 1), lambda i: (i, 0, 0, 0))
    win_spec = pl.BlockSpec((1, N_HEADS, HEAD_DIM, WIN), lambda i: (i, 0, 0, 0))
    fixed = lambda a: pl.BlockSpec(a.shape, lambda i: (0,) * a.ndim)
    out, win_k, win_v = pl.pallas_call(
        _sample_window_kernel,
        grid=(nb,),
        in_specs=[row_spec, row_spec, col_spec, col_spec, col_spec, win_spec, win_spec,
                  fixed(bias_s), fixed(rb0)],
        out_specs=[col_spec, win_spec, win_spec],
        out_shape=[jax.ShapeDtypeStruct((nb, N_HEADS, HEAD_DIM, 1), F32),
                   jax.ShapeDtypeStruct(cache_k.shape, F32), jax.ShapeDtypeStruct(cache_v.shape, F32)],
        scratch_shapes=[pltpu.VMEM((N_HEADS, WIN), F32)],
        compiler_params=_params(("arbitrary",)),
        name="sample_window",
    )(rows(q), rows(k_new), cols(q), cols(k_new), cols(v_new), cache_k, cache_v, bias_s, rb0)
    return out.reshape(nb, ATT_WIDTH), win_k, win_v


def _outproj_kernel(x_ref, attn_ref, y_ref, gate_ref, bonus_ref, lnw_ref, lnb_ref, wo_ref, gffn_ref,
                    ones_ref, h_ref, xn_ref):
    ones_bd = ones_ref[...]
    y = y_ref[...]
    mu = _segsum(y, ones_bd) * (1.0 / HEAD_DIM)
    dy = y - mu
    var = _segsum(dy * dy, ones_bd) * (1.0 / HEAD_DIM)
    yn = dy * lax.rsqrt(var + GN_EPS) * lnw_ref[...] + lnb_ref[...]
    rw = (yn + bonus_ref[...]) * gate_ref[...]
    mixed = (jnp.dot(attn_ref[...].astype(BF16), wo_ref[0:ATT_WIDTH, :], preferred_element_type=F32)
             + jnp.dot(rw.astype(BF16), wo_ref[ATT_WIDTH:D_MODEL, :], preferred_element_type=F32))
    h = x_ref[...] + mixed
    h_ref[...] = h
    xn_ref[...] = _rmsnorm(h, gffn_ref[...]).astype(BF16)


def _outproj(x2d, attn, y, gate, bonus, ln_w, ln_b, w_o, g_ffn, ones_bd, tm):
    m = x2d.shape[0]
    row = lambda i: (i, 0)
    fixed = lambda i: (0, 0)
    half = pl.BlockSpec((tm, RWKV_WIDTH), row)
    full = pl.BlockSpec((tm, D_MODEL), row)
    return pl.pallas_call(
        _outproj_kernel,
        grid=(m // tm,),
        in_specs=[full, half, half, half, half,
                  pl.BlockSpec((1, RWKV_WIDTH), fixed), pl.BlockSpec((1, RWKV_WIDTH), fixed),
                  pl.BlockSpec((D_MODEL, D_MODEL), fixed), pl.BlockSpec((1, D_MODEL), fixed),
                  pl.BlockSpec((RWKV_WIDTH, RWKV_WIDTH), fixed)],
        out_specs=[full, full],
        out_shape=[jax.ShapeDtypeStruct((m, D_MODEL), F32), jax.ShapeDtypeStruct((m, D_MODEL), BF16)],
        compiler_params=_params(("arbitrary",)),
        name="outproj",
    )(x2d, attn, y, gate, bonus, ln_w, ln_b, w_o, g_ffn, ones_bd)


def _ffn_tail(xn, h, gp_m1, gp_m2, gp_of, wg_ref, wu_ref, cw_ref, cb_ref, wout_ref, gfin_ref, o_ref):
    acc = h
    for c in range(D_FF // FF_CHUNK):
        cs = slice(c * FF_CHUNK, (c + 1) * FF_CHUNK)
        gp = gp_of(c)
        up = jnp.dot(xn, wu_ref[:, cs], preferred_element_type=F32)
        conv = (cb_ref[:, cs] + cw_ref[0:1, cs] * gp_m2(c, gp) + cw_ref[1:2, cs] * gp_m1(c, gp)
                + cw_ref[2:3, cs] * gp)
        act = conv * _sigmoid(conv) * up
        acc = acc + jnp.dot(act.astype(BF16), wout_ref[cs, :], preferred_element_type=F32)
    o_ref[...] = _rmsnorm(acc, gfin_ref[...])


def _ffn_prompt_kernel(xn_ref, h_ref, wg_ref, wu_ref, cw_ref, cb_ref, wout_ref, gfin_ref,
                       o_ref, conv_ref, carry_ref):
    @pl.when(pl.program_id(1) == 0)
    def _():
        carry_ref[...] = jnp.zeros_like(carry_ref)

    xn = xn_ref[...]
    tm = xn.shape[0]
    rowi = lax.broadcasted_iota(jnp.int32, (tm, FF_CHUNK), 0)
    gps = {}

    def gp_of(c):
        cs = slice(c * FF_CHUNK, (c + 1) * FF_CHUNK)
        gps[c] = jnp.dot(xn, wg_ref[:, cs], preferred_element_type=F32)
        return gps[c]

    def gp_m1(c, gp):
        cs = slice(c * FF_CHUNK, (c + 1) * FF_CHUNK)
        return jnp.where(rowi == 0, carry_ref[1:2, cs], pltpu.roll(gp, 1, 0))

    def gp_m2(c, gp):
        cs = slice(c * FF_CHUNK, (c + 1) * FF_CHUNK)
        return jnp.where(rowi == 0, carry_ref[0:1, cs],
                         jnp.where(rowi == 1, carry_ref[1:2, cs], pltpu.roll(gp, 2, 0)))

    _ffn_tail(xn, h_ref[...], gp_m1, gp_m2, gp_of, wg_ref, wu_ref, cw_ref, cb_ref, wout_ref, gfin_ref, o_ref)
    for c, gp in gps.items():
        cs = slice(c * FF_CHUNK, (c + 1) * FF_CHUNK)
        carry_ref[:, cs] = gp[tm - 2:tm, :]
        conv_ref[0, :, cs] = gp[tm - 2:tm, :]


def _ffn_sample_kernel(xn_ref, h_ref, p1_ref, p2_ref, wg_ref, wu_ref, cw_ref, cb_ref, wout_ref, gfin_ref,
                       o_ref, gp_ref):
    xn = xn_ref[...]

    def gp_of(c):
        cs = slice(c * FF_CHUNK, (c + 1) * FF_CHUNK)
        gp = jnp.dot(xn, wg_ref[:, cs], preferred_element_type=F32)
        gp_ref[:, cs] = gp
        return gp

    gp_m1 = lambda c, gp: p1_ref[:, c * FF_CHUNK:(c + 1) * FF_CHUNK]
    gp_m2 = lambda c, gp: p2_ref[:, c * FF_CHUNK:(c + 1) * FF_CHUNK]
    _ffn_tail(xn, h_ref[...], gp_m1, gp_m2, gp_of, wg_ref, wu_ref, cw_ref, cb_ref, wout_ref, gfin_ref, o_ref)


def _ffn_weight_specs(fixed):
    once = pl.Buffered(1)
    return [pl.BlockSpec((D_MODEL, D_FF), fixed, pipeline_mode=once),
            pl.BlockSpec((D_MODEL, D_FF), fixed, pipeline_mode=once),
            pl.BlockSpec((CONV_W, D_FF), fixed), pl.BlockSpec((1, D_FF), fixed),
            pl.BlockSpec((D_FF, D_MODEL), fixed, pipeline_mode=once), pl.BlockSpec((1, D_MODEL), fixed)]


def _ffn_prompt(xn, h, fw, batch, seq, tm):
    nt = seq // tm
    row = lambda b, j: (b * nt + j, 0)
    fixed = lambda b, j: (0, 0)
    full = pl.BlockSpec((tm, D_MODEL), row)
    return pl.pallas_call(
        _ffn_prompt_kernel,
        grid=(batch, nt),
        in_specs=[full, full] + _ffn_weight_specs(fixed),
        out_specs=[full, pl.BlockSpec((1, CONV_W - 1, D_FF), lambda b, j: (b, 0, 0))],
        out_shape=[jax.ShapeDtypeStruct((batch * seq, D_MODEL), F32),
                   jax.ShapeDtypeStruct((batch, CONV_W - 1, D_FF), F32)],
        scratch_shapes=[pltpu.VMEM((CONV_W - 1, D_FF), F32)],
        compiler_params=_params(("arbitrary", "arbitrary")),
        name="ffn_prompt",
    )(xn, h, *fw)


def _ffn_sample(xn, h, prev1, prev2, fw):
    m = xn.shape[0]
    fixed = lambda i: (0, 0)
    full = pl.BlockSpec((m, D_MODEL), fixed)
    ffs = pl.BlockSpec((m, D_FF), fixed)
    return pl.pallas_call(
        _ffn_sample_kernel,
        grid=(1,),
        in_specs=[full, full, ffs, ffs] + _ffn_weight_specs(fixed),
        out_specs=[full, ffs],
        out_shape=[jax.ShapeDtypeStruct((m, D_MODEL), F32), jax.ShapeDtypeStruct((m, D_FF), F32)],
        compiler_params=_params(("arbitrary",)),
        name="ffn_sample",
    )(xn, h, prev1, prev2, *fw)


def _layer_weights(g_mix, w_in, tok_mu, w0, w_decay_up, a0, w_iclr_up, w_gate_up, k_k, k_a, r_k,
                   ln_x_w, ln_x_b, w_o, g_ffn, w_ffn_in, conv_w, conv_b, w_ffn_out, g_final):
    vec = lambda t: t.reshape(1, -1)
    pad = LORA_PAD - LORA_W
    w_cat = jnp.pad(w_in, ((0, 0), (0, pad))).astype(BF16)
    mu_p = vec(tok_mu[:RKV_W])
    mu_l = vec(jnp.pad(tok_mu[RKV_W:], (0, pad)))
    lora_up = jnp.zeros((LORA_PAD, RKV_W), F32)
    lora_up = lora_up.at[0:LORA_DECAY, 0:RWKV_WIDTH].set(w_decay_up)
    lora_up = lora_up.at[LORA_DECAY:LORA_DECAY + LORA_ICLR, RWKV_WIDTH:2 * RWKV_WIDTH].set(w_iclr_up)
    lora_up = lora_up.at[LORA_DECAY + LORA_ICLR:LORA_W, 2 * RWKV_WIDTH:].set(w_gate_up)
    seg = np.arange(RWKV_WIDTH) // HEAD_DIM
    ones_bd = jnp.asarray(seg[:, None] == seg[None, :], BF16)
    prep = (mu_p, mu_l, lora_up.astype(BF16), vec(w0), vec(a0), vec(k_k), vec(k_a), vec(r_k), ones_bd)
    ffn = (w_ffn_in[:, :D_FF].astype(BF16), w_ffn_in[:, D_FF:].astype(BF16), conv_w, vec(conv_b),
           w_ffn_out.astype(BF16), vec(g_final))
    return dict(g_mix=vec(g_mix), w_cat=w_cat, prep=prep, ones_bd=ones_bd, ln_w=vec(ln_x_w), ln_b=vec(ln_x_b),
                w_o=w_o.astype(BF16), g_ffn=vec(g_ffn), ffn=ffn)


def kernel(x_prompt, x_sample, cache_win_k, cache_win_v, state_shift, state_wkv, state_ffn_conv, g_mix, w_in, rel_bias, tok_mu, w0, w_decay_up, a0, w_iclr_up, w_gate_up, k_k, k_a, r_k, ln_x_w, ln_x_b, w_o, g_ffn, w_ffn_in, conv_w, conv_b, w_ffn_out, g_final):
    batch, seq, _ = x_prompt.shape
    nb = x_sample.shape[0]
    lw = _layer_weights(g_mix[0], w_in[0], tok_mu[0], w0[0], w_decay_up[0], a0[0], w_iclr_up[0],
                        w_gate_up[0], k_k[0], k_a[0], r_k[0].reshape(-1), ln_x_w[0], ln_x_b[0], w_o[0],
                        g_ffn[0], w_ffn_in[0], conv_w[0], conv_b[0], w_ffn_out[0], g_final)
    (bias_p, bias_s), rb0 = _bias_build(rel_bias)

    xp = x_prompt.reshape(batch * seq, D_MODEL)
    qkv, rkv, lora = _inproj(xp, lw['g_mix'], lw['w_cat'], 512)
    r, lgw, k2, v, kk, b, gate, bonus = _prep_prompt(rkv, lora, lw['prep'], batch, seq, 512)
    y, wkv_p = _wkv_prompt(r, lgw, k2, v, kk, b, batch, seq)
    attn = _attn_prompt(qkv, bias_p, batch, seq)
    h1, xn2 = _outproj(xp, attn, y, gate, bonus, lw['ln_w'], lw['ln_b'], lw['w_o'], lw['g_ffn'],
                       lw['ones_bd'], 512)
    y_p, conv_p = _ffn_prompt(xn2, h1, lw['ffn'], batch, seq, 512)
    qkv3 = qkv.reshape(batch, seq, QKV_W)
    win_k_p = qkv3[:, seq - WIN:, ATT_WIDTH:2 * ATT_WIDTH].reshape(1, batch, WIN, N_HEADS, HEAD_DIM)
    win_v_p = qkv3[:, seq - WIN:, 2 * ATT_WIDTH:].reshape(1, batch, WIN, N_HEADS, HEAD_DIM)
    shift_p = jnp.concatenate([rkv.reshape(batch, seq, RKV_W)[:, -1],
                               lora.reshape(batch, seq, LORA_PAD)[:, -1, :LORA_W]], axis=-1)[None]

    xs = x_sample.reshape(nb, D_MODEL)
    qkv_s, rkv_s, lora_s = _inproj(xs, lw['g_mix'], lw['w_cat'], nb)
    sh = state_shift[0]
    r, lgw, k2, v, kk, b, gate, bonus = _prep_sample(
        rkv_s, sh[:, :RKV_W], lora_s, jnp.pad(sh[:, RKV_W:], ((0, 0), (0, LORA_PAD - LORA_W))), lw['prep'])
    y_s, wkv_s = _wkv_sample(state_wkv[0], lgw, kk, b, k2, r, v, 8)
    q_s = qkv_s[:, :ATT_WIDTH]
    k_s = qkv_s[:, ATT_WIDTH:2 * ATT_WIDTH]
    v_s = qkv_s[:, 2 * ATT_WIDTH:]
    ck = jnp.transpose(cache_win_k[0], (0, 2, 3, 1))
    cv = jnp.transpose(cache_win_v[0], (0, 2, 3, 1))
    attn_s, win_k_s, win_v_s = _sample_window(q_s, k_s, v_s, ck, cv, bias_s, rb0)
    win_k_s = jnp.transpose(win_k_s, (0, 3, 1, 2))
    win_v_s = jnp.transpose(win_v_s, (0, 3, 1, 2))
    h1_s, xn2_s = _outproj(xs, attn_s, y_s, gate, bonus, lw['ln_w'], lw['ln_b'], lw['w_o'], lw['g_ffn'],
                           lw['ones_bd'], nb)
    conv_state = state_ffn_conv[0]
    y_smp, gp_s = _ffn_sample(xn2_s, h1_s, conv_state[:, 1], conv_state[:, 0], lw['ffn'])
    conv_s = jnp.stack([conv_state[:, 1], gp_s], axis=1)
    shift_s = jnp.concatenate([rkv_s, lora_s[:, :LORA_W]], axis=-1)[None]

    return (y_p.reshape(batch, seq, D_MODEL), y_smp.reshape(nb, 1, D_MODEL),
            win_k_p, win_v_p, shift_p, wkv_p[None], conv_p[None],
            win_k_s[None], win_v_s[None], shift_s, wkv_s[None], conv_s[None])
```

```python
import functools

import numpy as np
import jax
import jax.numpy as jnp
from jax import lax
from jax.experimental import pallas as pl
from jax.experimental.pallas import tpu as pltpu

F32 = jnp.float32
BF16 = jnp.bfloat16

D_MODEL = 1024
HEAD_DIM = 64
ATT_WIDTH = 512
RWKV_WIDTH = 512
N_HEADS = 8
QKV_W = 3 * ATT_WIDTH
RKV_W = 3 * RWKV_WIDTH
LORA_DECAY, LORA_ICLR, LORA_GATE = 32, 32, 96
LORA_W = LORA_DECAY + LORA_ICLR + LORA_GATE
LORA_PAD = 256
RWKV_PROJ = RKV_W + LORA_W
PROJ_PAD = QKV_W + RKV_W + LORA_PAD
D_FF = 2816
FF_CHUNK = 1408
CONV_W = 3
WIN = 2048
DILATIONS = (1, 4, 16)
Q_BLOCK = 128
SUPER = 2048
N_BUCKETS = 32
MAX_DISTANCE = 2048
NORM_EPS = 1e-6
GN_EPS = 64e-5
NEG = -1e30
ATT_SCALE = HEAD_DIM ** -0.5
CHUNK = 64
VMEM_LIMIT = 56 * 1024 * 1024

NT = (((1,), (1,)), ((), ()))
TN = (((0,), (0,)), ((), ()))


def _dot(a, b):
    return jnp.dot(a.astype(BF16), b.astype(BF16), preferred_element_type=F32)


def _dot_nt(a, b):
    return lax.dot_general(a.astype(BF16), b.astype(BF16), NT, preferred_element_type=F32)


def _dot_tn(a, b):
    return lax.dot_general(a.astype(BF16), b.astype(BF16), TN, preferred_element_type=F32)


def _split2(x):
    hi = x.astype(BF16)
    lo = (x - hi.astype(F32)).astype(BF16)
    return hi, lo


def _split3(x):
    hi = x.astype(BF16)
    r1 = x - hi.astype(F32)
    mid = r1.astype(BF16)
    lo = (r1 - mid.astype(F32)).astype(BF16)
    return hi, mid, lo


def _segsum(x, ones_bd):
    hi, lo = _split2(x)
    return (jnp.dot(hi, ones_bd, preferred_element_type=F32)
            + jnp.dot(lo, ones_bd, preferred_element_type=F32))


def _sigmoid(x):
    return 1.0 / (1.0 + jnp.exp(-x))


def _rmsnorm(x, g):
    return x * lax.rsqrt(jnp.mean(x * x, axis=-1, keepdims=True) + NORM_EPS) * g


def _params(sem):
    return pltpu.CompilerParams(dimension_semantics=sem, vmem_limit_bytes=VMEM_LIMIT)


def _t5_bucket(dist):
    dist = np.asarray(dist, dtype=np.int64)
    exact = N_BUCKETS // 2
    scaled = np.log(np.maximum(dist, 1) / exact) / np.log(MAX_DISTANCE / exact)
    large = np.minimum(exact + (scaled * (N_BUCKETS - exact)).astype(np.int64), N_BUCKETS - 1)
    return np.where(dist < exact, dist, large).astype(np.int32)


def _inproj_kernel(x_ref, g_ref, w_ref, qkv_ref, rkv_ref, lora_ref):
    xb = _rmsnorm(x_ref[...], g_ref[...]).astype(BF16)
    qkv_ref[...] = jnp.dot(xb, w_ref[:, 0:QKV_W], preferred_element_type=F32)
    rkv_ref[...] = jnp.dot(xb, w_ref[:, QKV_W:QKV_W + RKV_W], preferred_element_type=F32)
    lora_ref[...] = jnp.dot(xb, w_ref[:, QKV_W + RKV_W:PROJ_PAD], preferred_element_type=F32)


def _inproj(x2d, g, w, tm):
    m = x2d.shape[0]
    row = lambda i: (i, 0)
    fixed = lambda i: (0, 0)
    return pl.pallas_call(
        _inproj_kernel,
        grid=(m // tm,),
        in_specs=[pl.BlockSpec((tm, D_MODEL), row),
                  pl.BlockSpec((1, D_MODEL), fixed),
                  pl.BlockSpec((D_MODEL, PROJ_PAD), fixed)],
        out_specs=[pl.BlockSpec((tm, QKV_W), row),
                   pl.BlockSpec((tm, RKV_W), row),
                   pl.BlockSpec((tm, LORA_PAD), row)],
        out_shape=[jax.ShapeDtypeStruct((m, QKV_W), F32),
                   jax.ShapeDtypeStruct((m, RKV_W), F32),
                   jax.ShapeDtypeStruct((m, LORA_PAD), F32)],
        compiler_params=_params(("arbitrary",)),
        name="inproj",
    )(x2d, g, w)


def _prep_math(p, p_prev, l, l_prev, mu_p, mu_l, wcat, w0, a0, k_k, k_a, r_k, ones_bd):
    pm = p + mu_p * (p_prev - p)
    lm = l + mu_l * (l_prev - l)
    r = pm[:, 0:RWKV_WIDTH]
    kr = pm[:, RWKV_WIDTH:2 * RWKV_WIDTH]
    vr = pm[:, 2 * RWKV_WIDTH:3 * RWKV_WIDTH]
    lane = lax.broadcasted_iota(jnp.int32, lm.shape, 1)
    feat = jnp.where(lane < LORA_DECAY, jnp.tanh(lm),
                     jnp.where(lane < LORA_DECAY + LORA_ICLR, lm, _sigmoid(lm)))
    z = jnp.dot(feat.astype(BF16), wcat, preferred_element_type=F32)
    zd = -(w0 + z[:, 0:RWKV_WIDTH])
    softplus = jnp.maximum(zd, 0.0) + jnp.log(1.0 + jnp.exp(-jnp.abs(zd)))
    lw = -jnp.exp(-softplus - 0.5)
    a = _sigmoid(a0 + z[:, RWKV_WIDTH:2 * RWKV_WIDTH])
    gate = z[:, 2 * RWKV_WIDTH:3 * RWKV_WIDTH]
    kk = kr * k_k
    kk = kk / jnp.maximum(jnp.sqrt(_segsum(kk * kk, ones_bd)), 1e-12)
    k2 = kr * (1.0 + (a - 1.0) * k_a)
    bonus = _segsum(r * k2 * r_k, ones_bd) * vr
    return r, lw, k2, vr, kk, kk * a, gate, bonus


def _prep_prompt_kernel(p_ref, l_ref, mu_p_ref, mu_l_ref, wcat_ref, w0_ref, a0_ref, kk_ref, ka_ref,
                        rk_ref, ones_ref, *refs):
    outs, (cp_ref, cl_ref) = refs[:8], refs[8:]

    @pl.when(pl.program_id(1) == 0)
    def _():
        cp_ref[...] = jnp.zeros_like(cp_ref)
        cl_ref[...] = jnp.zeros_like(cl_ref)

    p = p_ref[...]
    l = l_ref[...]
    tm = p.shape[0]
    first_p = lax.broadcasted_iota(jnp.int32, p.shape, 0) == 0
    first_l = lax.broadcasted_iota(jnp.int32, l.shape, 0) == 0
    p_prev = jnp.where(first_p, cp_ref[...], pltpu.roll(p, 1, 0))
    l_prev = jnp.where(first_l, cl_ref[...], pltpu.roll(l, 1, 0))
    cp_ref[...] = p[tm - 1:tm, :]
    cl_ref[...] = l[tm - 1:tm, :]
    res = _prep_math(p, p_prev, l, l_prev, mu_p_ref[...], mu_l_ref[...], wcat_ref[...], w0_ref[...],
                     a0_ref[...], kk_ref[...], ka_ref[...], rk_ref[...], ones_ref[...])
    for o, v in zip(outs, res):
        o[...] = v


def _prep_sample_kernel(p_ref, pp_ref, l_ref, lp_ref, mu_p_ref, mu_l_ref, wcat_ref, w0_ref, a0_ref,
                        kk_ref, ka_ref, rk_ref, ones_ref, *outs):
    res = _prep_math(p_ref[...], pp_ref[...], l_ref[...], lp_ref[...], mu_p_ref[...], mu_l_ref[...],
                     wcat_ref[...], w0_ref[...], a0_ref[...], kk_ref[...], ka_ref[...], rk_ref[...],
                     ones_ref[...])
    for o, v in zip(outs, res):
        o[...] = v


def _prep_prompt(rkv, lora, pw, batch, seq, tm):
    nt = seq // tm
    row = lambda b, j: (b * nt + j, 0)
    fixed = lambda b, j: (0, 0)
    vec = lambda n: pl.BlockSpec((1, n), fixed)
    m = rkv.shape[0]
    return pl.pallas_call(
        _prep_prompt_kernel,
        grid=(batch, nt),
        in_specs=[pl.BlockSpec((tm, RKV_W), row), pl.BlockSpec((tm, LORA_PAD), row),
                  vec(RKV_W), vec(LORA_PAD), pl.BlockSpec((LORA_PAD, RKV_W), fixed),
                  vec(RWKV_WIDTH), vec(RWKV_WIDTH), vec(RWKV_WIDTH), vec(RWKV_WIDTH), vec(RWKV_WIDTH),
                  pl.BlockSpec((RWKV_WIDTH, RWKV_WIDTH), fixed)],
        out_specs=[pl.BlockSpec((tm, RWKV_WIDTH), row)] * 8,
        out_shape=[jax.ShapeDtypeStruct((m, RWKV_WIDTH), F32)] * 8,
        scratch_shapes=[pltpu.VMEM((1, RKV_W), F32), pltpu.VMEM((1, LORA_PAD), F32)],
        compiler_params=_params(("arbitrary", "arbitrary")),
        name="rwkv_prep_prompt",
    )(rkv, lora, *pw)


def _prep_sample(rkv, rkv_prev, lora, lora_prev, pw):
    m = rkv.shape[0]
    full = lambda a: pl.BlockSpec(a.shape, lambda i: (0,) * a.ndim)
    args = (rkv, rkv_prev, lora, lora_prev) + tuple(pw)
    return pl.pallas_call(
        _prep_sample_kernel,
        grid=(1,),
        in_specs=[full(a) for a in args],
        out_specs=[pl.BlockSpec((m, RWKV_WIDTH), lambda i: (0, 0))] * 8,
        out_shape=[jax.ShapeDtypeStruct((m, RWKV_WIDTH), F32)] * 8,
        compiler_params=_params(("arbitrary",)),
        name="rwkv_prep_sample",
    )(*args)


def _cumsum_rows(x, tri):
    tri_b = tri.astype(BF16)
    out = None
    for piece in _split3(x):
        t = jnp.dot(tri_b, piece, preferred_element_type=F32)
        out = t if out is None else out + t
    return out


def _wkv_kernel(r_ref, lw_ref, k_ref, v_ref, kk_ref, b_ref, y_ref, s_ref, st_ref):
    c = pl.program_id(0)

    @pl.when(c == 0)
    def _():
        st_ref[...] = jnp.zeros_like(st_ref)

    nb = r_ref.shape[0]
    n = CHUNK
    row = lax.broadcasted_iota(jnp.int32, (n, n), 0)
    col = lax.broadcasted_iota(jnp.int32, (n, n), 1)
    lower_incl = row >= col
    lower_strict = row > col
    eye = jnp.where(row == col, 1.0, 0.0)
    tri = jnp.where(lower_incl, 1.0, 0.0)

    ah, bh, kh, rh, beh, keh, vh, ge = ([] for _ in range(8))
    for bi in range(nb):
        lw = lw_ref[bi]
        lg = _cumsum_rows(lw, tri)
        g_inv = jnp.exp(-lg)
        lg_end = lg[n - 1:n, :]
        g_to_end = jnp.exp(lg_end - lg)
        g_end = jnp.exp(lg_end)
        kk = kk_ref[bi]
        b = b_ref[bi]
        k = k_ref[bi]
        full = (-kk * jnp.exp(lg - lw), b * g_inv, k * g_inv, r_ref[bi] * jnp.exp(lg),
                b * g_to_end, k * g_to_end, v_ref[bi], g_end)
        for h in range(N_HEADS):
            sl = slice(h * HEAD_DIM, (h + 1) * HEAD_DIM)
            for dst, t in zip((ah, bh, kh, rh, beh, keh, vh), full[:7]):
                dst.append(t[:, sl].astype(BF16))
            ge.append(g_end[:, sl])
    units = range(nb * N_HEADS)

    ar = [jnp.concatenate([ah[u], rh[u]], axis=0) for u in units]
    g_b = [_dot_nt(ar[u], bh[u]) for u in units]
    g_k = [_dot_nt(ar[u], kh[u]) for u in units]
    l_ab = [jnp.where(lower_strict, g_b[u][:n], 0.0) for u in units]
    m_rb = [jnp.where(lower_incl, g_b[u][n:], 0.0).astype(BF16) for u in units]
    l_ak = [jnp.where(lower_strict, g_k[u][:n], 0.0) for u in units]
    m_rk = [jnp.where(lower_incl, g_k[u][n:], 0.0).astype(BF16) for u in units]
    lv = [_dot(l_ak[u], vh[u]) for u in units]
    tm_ = [eye + l_ab[u] for u in units]
    lp = [x.astype(BF16) for x in l_ab]
    span = 1
    while 2 * span < n:
        lp = [_dot(x, x).astype(BF16) for x in lp]
        tm_ = [tm_[u] + _dot(tm_[u], lp[u]) for u in units]
        span *= 2
    tm_ = [t.astype(BF16) for t in tm_]
    a_hat = [_dot(tm_[u], ah[u]) for u in units]
    u0 = [_dot(tm_[u], lv[u]) for u in units]
    s0 = [st_ref[u] for u in units]
    s0b = [s.astype(BF16) for s in s0]
    uu = [(_dot_nt(a_hat[u], s0b[u]) + u0[u]).astype(BF16) for u in units]
    y = [_dot_nt(rh[u], s0b[u]) + _dot(m_rb[u], uu[u]) + _dot(m_rk[u], vh[u]) for u in units]
    for u in units:
        st_ref[u] = s0[u] * ge[u] + _dot_tn(uu[u], beh[u]) + _dot_tn(vh[u], keh[u])
    for bi in range(nb):
        y_ref[bi] = jnp.concatenate(y[bi * N_HEADS:(bi + 1) * N_HEADS], axis=1)

    @pl.when(c == pl.num_programs(0) - 1)
    def _():
        s_ref[...] = st_ref[...]


def _wkv_prompt(r, lw, k2, v, kk, b, batch, seq):
    nc = seq // CHUNK
    seqs = lambda t: t.reshape(batch, seq, RWKV_WIDTH)
    spec = pl.BlockSpec((batch, CHUNK, RWKV_WIDTH), lambda c: (0, c, 0))
    nu = batch * N_HEADS
    y, s = pl.pallas_call(
        _wkv_kernel,
        grid=(nc,),
        in_specs=[spec] * 6,
        out_specs=[spec, pl.BlockSpec((nu, HEAD_DIM, HEAD_DIM), lambda c: (0, 0, 0))],
        out_shape=[jax.ShapeDtypeStruct((batch, seq, RWKV_WIDTH), F32),
                   jax.ShapeDtypeStruct((nu, HEAD_DIM, HEAD_DIM), F32)],
        scratch_shapes=[pltpu.VMEM((nu, HEAD_DIM, HEAD_DIM), F32)],
        compiler_params=_params(("arbitrary",)),
        name="wkv_prompt",
    )(*(seqs(t) for t in (r, lw, k2, v, kk, b)))
    return y.reshape(batch * seq, RWKV_WIDTH), s.reshape(batch, N_HEADS, HEAD_DIM, HEAD_DIM)


def _wkv_step_kernel(s_ref, lw_ref, kk_ref, b_ref, k_ref, r_ref, v_ref, y_ref, so_ref):
    s = s_ref[...]
    sa = jnp.sum(s * (-kk_ref[...]), axis=-1, keepdims=True)
    s = s * jnp.exp(lw_ref[...]) + sa * b_ref[...] + v_ref[...] * k_ref[...]
    so_ref[...] = s
    y_ref[...] = jnp.sum(s * r_ref[...], axis=-1, keepdims=True)


def _wkv_sample(state, lw, kk, b, k2, r, v, bb):
    nb = state.shape[0]
    rowv = lambda t: t.reshape(nb, N_HEADS, 1, HEAD_DIM)
    colv = lambda t: t.reshape(nb, N_HEADS, HEAD_DIM, 1)
    idx = lambda i: (i, 0, 0, 0)
    s_spec = pl.BlockSpec((bb, N_HEADS, HEAD_DIM, HEAD_DIM), idx)
    r_spec = pl.BlockSpec((bb, N_HEADS, 1, HEAD_DIM), idx)
    c_spec = pl.BlockSpec((bb, N_HEADS, HEAD_DIM, 1), idx)
    y, s_new = pl.pallas_call(
        _wkv_step_kernel,
        grid=(nb // bb,),
        in_specs=[s_spec] + [r_spec] * 5 + [c_spec],
        out_specs=[c_spec, s_spec],
        out_shape=[jax.ShapeDtypeStruct((nb, N_HEADS, HEAD_DIM, 1), F32),
                   jax.ShapeDtypeStruct(state.shape, F32)],
        compiler_params=_params(("arbitrary",)),
        name="wkv_sample",
    )(state, rowv(lw), rowv(kk), rowv(b), rowv(k2), rowv(r), colv(v))
    return y.reshape(nb, RWKV_WIDTH), s_new


def _bias_tables():
    i = np.arange(Q_BLOCK)[:, None]
    c = np.arange(2 * Q_BLOCK)[None, :]
    n = i + Q_BLOCK - c
    ok = (n >= 0) & (n <= Q_BLOCK)
    prompt = np.stack([np.where(ok, _t5_bucket(d * np.clip(n, 0, Q_BLOCK)), -1) for d in DILATIONS])
    dist = WIN - np.arange(WIN)
    sample = np.stack([np.where((dist % d == 0) & (dist // d <= Q_BLOCK), _t5_bucket(dist), -1)
                       for d in DILATIONS])
    sample = np.broadcast_to(sample[:, None, :], (len(DILATIONS), N_HEADS, WIN))
    return prompt.astype(np.int32), sample.astype(np.int32)


def _bias_kernel(rb_ref, rbc_ref, bp_ref, bs_ref, op_ref, os_ref):
    bp = bp_ref[0]
    bs = bs_ref[0]
    acc_s = jnp.full(bs.shape, NEG, F32)
    for bk in range(N_BUCKETS):
        acc_s = jnp.where(bs == bk, rbc_ref[bk], acc_s)
    os_ref[0] = acc_s
    for h in range(N_HEADS):
        acc = jnp.full(bp.shape, NEG, F32)
        for bk in range(N_BUCKETS):
            acc = jnp.where(bp == bk, rb_ref[bk, h], acc)
        op_ref[0, h] = acc


def _bias_build(rel_bias):
    bp, bs = _bias_tables()
    rbc = rel_bias.reshape(N_BUCKETS, N_HEADS, 1)
    nb = len(DILATIONS)
    return pl.pallas_call(
        _bias_kernel,
        grid=(nb,),
        in_specs=[pl.BlockSpec(memory_space=pltpu.SMEM),
                  pl.BlockSpec((N_BUCKETS, N_HEADS, 1), lambda i: (0, 0, 0)),
                  pl.BlockSpec((1, Q_BLOCK, 2 * Q_BLOCK), lambda i: (i, 0, 0)),
                  pl.BlockSpec((1, N_HEADS, WIN), lambda i: (i, 0, 0))],
        out_specs=[pl.BlockSpec((1, N_HEADS, Q_BLOCK, 2 * Q_BLOCK), lambda i: (i, 0, 0, 0)),
                   pl.BlockSpec((1, N_HEADS, WIN), lambda i: (i, 0, 0))],
        out_shape=[jax.ShapeDtypeStruct((nb, N_HEADS, Q_BLOCK, 2 * Q_BLOCK), F32),
                   jax.ShapeDtypeStruct((nb, N_HEADS, WIN), F32)],
        compiler_params=_params(("arbitrary",)),
        name="bias_build",
    )(rel_bias, rbc, jnp.asarray(bp), jnp.asarray(bs)), rbc[0]


def _attn_unit(q, kp, kc, vp, vc, bias_a, bias_b, prev_ok):
    lane = lax.broadcasted_iota(jnp.int32, q.shape, 1)
    is_a = lane < HEAD_DIM
    qs = q * ATT_SCALE
    kpb, kcb, vpb, vcb = (t.astype(BF16) for t in (kp, kc, vp, vc))

    def head(qh, bias):
        lp = lax.dot_general(qh, kpb, NT, preferred_element_type=F32) + bias[:, :Q_BLOCK]
        lc = lax.dot_general(qh, kcb, NT, preferred_element_type=F32) + bias[:, Q_BLOCK:]
        if prev_ok is not None:
            lp = jnp.where(prev_ok, lp, NEG)
        m = jnp.maximum(jnp.max(lp, axis=-1, keepdims=True), jnp.max(lc, axis=-1, keepdims=True))
        pp = jnp.exp(lp - m)
        pc = jnp.exp(lc - m)
        s = jnp.sum(pp, axis=-1, keepdims=True) + jnp.sum(pc, axis=-1, keepdims=True)
        num = (jnp.dot(pp.astype(BF16), vpb, preferred_element_type=F32)
               + jnp.dot(pc.astype(BF16), vcb, preferred_element_type=F32))
        return num, m, s

    n_a, m_a, s_a = head(jnp.where(is_a, qs, 0.0).astype(BF16), bias_a)
    n_b, m_b, s_b = head(jnp.where(is_a, 0.0, qs).astype(BF16), bias_b)
    return jnp.where(is_a, n_a, n_b), jnp.where(is_a, m_a, m_b), jnp.where(is_a, s_a, s_b)


def _attn_kernel(q_ref, kc_ref, kp_ref, vc_ref, vp_ref, bias_ref, o_ref, num_ref, m_ref, s_ref):
    has_prev = pl.program_id(2) > 0
    for br, d in enumerate(DILATIONS):
        nblk = SUPER // (d * Q_BLOCK)
        for r in range(d):
            for blk in range(nblk):
                rows = lambda j: (pl.ds(r + d * Q_BLOCK * j, Q_BLOCK, stride=d) if d > 1
                                  else pl.ds(Q_BLOCK * j, Q_BLOCK))
                cur = rows(blk)
                if blk == 0:
                    prv = rows(nblk - 1)
                    kp, vp, ok = kp_ref[prv, :], vp_ref[prv, :], has_prev
                else:
                    prv = rows(blk - 1)
                    kp, vp, ok = kc_ref[prv, :], vc_ref[prv, :], None
                num, m, s = _attn_unit(q_ref[cur, :], kp, kc_ref[cur, :], vp, vc_ref[cur, :],
                                       bias_ref[br, 0], bias_ref[br, 1], ok)
                num_ref[br, cur, :] = num
                m_ref[br, cur, :] = m
                s_ref[br, cur, :] = s

    def merge(i, carry):
        rs = pl.ds(pl.multiple_of(i * Q_BLOCK, Q_BLOCK), Q_BLOCK)
        ms = [m_ref[br, rs, :] for br in range(len(DILATIONS))]
        m = functools.reduce(jnp.maximum, ms)
        num = 0.0
        den = 0.0
        for br in range(len(DILATIONS)):
            w = jnp.exp(ms[br] - m)
            num = num + w * num_ref[br, rs, :]
            den = den + w * s_ref[br, rs, :]
        o_ref[rs, :] = num / den
        return carry

    lax.fori_loop(0, SUPER // Q_BLOCK, merge, 0)


def _attn_prompt(qkv, bias_p, batch, seq):
    ns = seq // SUPER
    npair = ATT_WIDTH // 128
    blk = lambda col0, prev: pl.BlockSpec(
        (SUPER, 128),
        (lambda b, hp, sb: (b * ns + jnp.maximum(sb - 1, 0), col0 + hp)) if prev
        else (lambda b, hp, sb: (b * ns + sb, col0 + hp)))
    nb = len(DILATIONS)
    return pl.pallas_call(
        _attn_kernel,
        grid=(batch, npair, ns),
        in_specs=[blk(0, False), blk(npair, False), blk(npair, True), blk(2 * npair, False),
                  blk(2 * npair, True),
                  pl.BlockSpec((nb, 2, Q_BLOCK, 2 * Q_BLOCK), lambda b, hp, sb: (0, hp, 0, 0))],
        out_specs=pl.BlockSpec((SUPER, 128), lambda b, hp, sb: (b * ns + sb, hp)),
        out_shape=jax.ShapeDtypeStruct((batch * seq, ATT_WIDTH), F32),
        scratch_shapes=[pltpu.VMEM((nb, SUPER, 128), F32)] * 3,
        compiler_params=_params(("arbitrary", "arbitrary", "arbitrary")),
        name="attn_prompt",
    )(qkv, qkv, qkv, qkv, qkv, bias_p)


def _sample_window_kernel(q_ref, kn_ref, qc_ref, knc_ref, vnc_ref, ck_ref, cv_ref, bias_ref, rb0_ref,
                          o_ref, ok_ref, ov_ref, wt_ref):
    lane = lax.broadcasted_iota(jnp.int32, (HEAD_DIM, WIN), 1)
    last = lane == WIN - 1
    l_new = jnp.sum(kn_ref[0] * (q_ref[0] * ATT_SCALE), axis=-1, keepdims=True) + rb0_ref[...]
    rows = []
    for h in range(N_HEADS):
        kh = ck_ref[0, h]
        rows.append(jnp.sum(kh * (qc_ref[0, h] * ATT_SCALE), axis=0, keepdims=True))
        ok_ref[0, h] = jnp.where(last, knc_ref[0, h], pltpu.roll(kh, WIN - 1, 1))
    logits = jnp.concatenate(rows, axis=0)
    parts = []
    for br in range(len(DILATIONS)):
        lg = logits + bias_ref[br]
        m = jnp.maximum(jnp.max(lg, axis=-1, keepdims=True), l_new)
        p = jnp.exp(lg - m)
        p_new = jnp.exp(l_new - m)
        parts.append((p, p_new, m, jnp.sum(p, axis=-1, keepdims=True) + p_new))
    m = functools.reduce(jnp.maximum, [pt[2] for pt in parts])
    wt = 0.0
    w_new = 0.0
    den = 0.0
    for p, p_new, m_g, s_g in parts:
        w = jnp.exp(m_g - m)
        wt = wt + w * p
        w_new = w_new + w * p_new
        den = den + w * s_g
    wt_ref[...] = wt / den
    w_new = w_new / den
    for h in range(N_HEADS):
        vh = cv_ref[0, h]
        o_ref[0, h] = (jnp.sum(vh * wt_ref[h:h + 1, :], axis=1, keepdims=True)
                       + vnc_ref[0, h] * w_new[h:h + 1, :])
        ov_ref[0, h] = jnp.where(last, vnc_ref[0, h], pltpu.roll(vh, WIN - 1, 1))


def _sample_window(q, k_new, v_new, cache_k, cache_v, bias_s, rb0):
    nb = q.shape[0]
    rows = lambda t: t.reshape(nb, N_HEADS, HEAD_DIM)
    cols = lambda t: t.reshape(nb, N_HEADS, HEAD_DIM, 1)
    row_spec = pl.BlockSpec((1, N_HEADS, HEAD_DIM), lambda i: (i, 0, 0))
    col_spec = pl.BlockSpec((1, N_HEADS, HEAD_DIM, 1), lambda i: (i, 0, 0, 0))
    win_spec = pl.BlockSpec((1, N_HEADS, HEAD_DIM, WIN), lambda i: (i, 0, 0, 0))
    fixed = lambda a: pl.BlockSpec(a.shape, lambda i: (0,) * a.ndim)
    out, win_k, win_v = pl.pallas_call(
        _sample_window_kernel,
        grid=(nb,),
        in_specs=[row_spec, row_spec, col_spec, col_spec, col_spec, win_spec, win_spec,
                  fixed(bias_s), fixed(rb0)],
        out_specs=[col_spec, win_spec, win_spec],
        out_shape=[jax.ShapeDtypeStruct((nb, N_HEADS, HEAD_DIM, 1), F32),
                   jax.ShapeDtypeStruct(cache_k.shape, F32), jax.ShapeDtypeStruct(cache_v.shape, F32)],
        scratch_shapes=[pltpu.VMEM((N_HEADS, WIN), F32)],
        compiler_params=_params(("arbitrary",)),
        name="sample_window",
    )(rows(q), rows(k_new), cols(q), cols(k_new), cols(v_new), cache_k, cache_v, bias_s, rb0)
    return out.reshape(nb, ATT_WIDTH), win_k, win_v


def _outproj_kernel(x_ref, attn_ref, y_ref, gate_ref, bonus_ref, lnw_ref, lnb_ref, wo_ref, gffn_ref,
                    ones_ref, h_ref, xn_ref):
    ones_bd = ones_ref[...]
    y = y_ref[...]
    mu = _segsum(y, ones_bd) * (1.0 / HEAD_DIM)
    dy = y - mu
    var = _segsum(dy * dy, ones_bd) * (1.0 / HEAD_DIM)
    yn = dy * lax.rsqrt(var + GN_EPS) * lnw_ref[...] + lnb_ref[...]
    rw = (yn + bonus_ref[...]) * gate_ref[...]
    mixed = (jnp.dot(attn_ref[...].astype(BF16), wo_ref[0:ATT_WIDTH, :], preferred_element_type=F32)
             + jnp.dot(rw.astype(BF16), wo_ref[ATT_WIDTH:D_MODEL, :], preferred_element_type=F32))
    h = x_ref[...] + mixed
    h_ref[...] = h
    xn_ref[...] = _rmsnorm(h, gffn_ref[...]).astype(BF16)


def _outproj(x2d, attn, y, gate, bonus, ln_w, ln_b, w_o, g_ffn, ones_bd, tm):
    m = x2d.shape[0]
    row = lambda i: (i, 0)
    fixed = lambda i: (0, 0)
    half = pl.BlockSpec((tm, RWKV_WIDTH), row)
    full = pl.BlockSpec((tm, D_MODEL), row)
    return pl.pallas_call(
        _outproj_kernel,
        grid=(m // tm,),
        in_specs=[full, half, half, half, half,
                  pl.BlockSpec((1, RWKV_WIDTH), fixed), pl.BlockSpec((1, RWKV_WIDTH), fixed),
                  pl.BlockSpec((D_MODEL, D_MODEL), fixed), pl.BlockSpec((1, D_MODEL), fixed),
                  pl.BlockSpec((RWKV_WIDTH, RWKV_WIDTH), fixed)],
        out_specs=[full, full],
        out_shape=[jax.ShapeDtypeStruct((m, D_MODEL), F32), jax.ShapeDtypeStruct((m, D_MODEL), BF16)],
        compiler_params=_params(("arbitrary",)),
        name="outproj",
    )(x2d, attn, y, gate, bonus, ln_w, ln_b, w_o, g_ffn, ones_bd)


def _ffn_tail(xn, h, gp_m1, gp_m2, gp_of, wg_ref, wu_ref, cw_ref, cb_ref, wout_ref, gfin_ref, o_ref):
    acc = h
    for c in range(D_FF // FF_CHUNK):
        cs = slice(c * FF_CHUNK, (c + 1) * FF_CHUNK)
        gp = gp_of(c)
        up = jnp.dot(xn, wu_ref[:, cs], preferred_element_type=F32)
        conv = (cb_ref[:, cs] + cw_ref[0:1, cs] * gp_m2(c, gp) + cw_ref[1:2, cs] * gp_m1(c, gp)
                + cw_ref[2:3, cs] * gp)
        act = conv * _sigmoid(conv) * up
        acc = acc + jnp.dot(act.astype(BF16), wout_ref[cs, :], preferred_element_type=F32)
    o_ref[...] = _rmsnorm(acc, gfin_ref[...])


def _ffn_prompt_kernel(xn_ref, h_ref, wg_ref, wu_ref, cw_ref, cb_ref, wout_ref, gfin_ref,
                       o_ref, conv_ref, carry_ref):
    @pl.when(pl.program_id(1) == 0)
    def _():
        carry_ref[...] = jnp.zeros_like(carry_ref)

    xn = xn_ref[...]
    tm = xn.shape[0]
    rowi = lax.broadcasted_iota(jnp.int32, (tm, FF_CHUNK), 0)
    gps = {}

    def gp_of(c):
        cs = slice(c * FF_CHUNK, (c + 1) * FF_CHUNK)
        gps[c] = jnp.dot(xn, wg_ref[:, cs], preferred_element_type=F32)
        return gps[c]

    def gp_m1(c, gp):
        cs = slice(c * FF_CHUNK, (c + 1) * FF_CHUNK)
        return jnp.where(rowi == 0, carry_ref[1:2, cs], pltpu.roll(gp, 1, 0))

    def gp_m2(c, gp):
        cs = slice(c * FF_CHUNK, (c + 1) * FF_CHUNK)
        return jnp.where(rowi == 0, carry_ref[0:1, cs],
                         jnp.where(rowi == 1, carry_ref[1:2, cs], pltpu.roll(gp, 2, 0)))

    _ffn_tail(xn, h_ref[...], gp_m1, gp_m2, gp_of, wg_ref, wu_ref, cw_ref, cb_ref, wout_ref, gfin_ref, o_ref)
    for c, gp in gps.items():
        cs = slice(c * FF_CHUNK, (c + 1) * FF_CHUNK)
        carry_ref[:, cs] = gp[tm - 2:tm, :]
        conv_ref[0, :, cs] = gp[tm - 2:tm, :]


def _ffn_sample_kernel(xn_ref, h_ref, p1_ref, p2_ref, wg_ref, wu_ref, cw_ref, cb_ref, wout_ref, gfin_ref,
                       o_ref, gp_ref):
    xn = xn_ref[...]

    def gp_of(c):
        cs = slice(c * FF_CHUNK, (c + 1) * FF_CHUNK)
        gp = jnp.dot(xn, wg_ref[:, cs], preferred_element_type=F32)
        gp_ref[:, cs] = gp
        return gp

    gp_m1 = lambda c, gp: p1_ref[:, c * FF_CHUNK:(c + 1) * FF_CHUNK]
    gp_m2 = lambda c, gp: p2_ref[:, c * FF_CHUNK:(c + 1) * FF_CHUNK]
    _ffn_tail(xn, h_ref[...], gp_m1, gp_m2, gp_of, wg_ref, wu_ref, cw_ref, cb_ref, wout_ref, gfin_ref, o_ref)


def _ffn_weight_specs(fixed):
    once = pl.Buffered(1)
    return [pl.BlockSpec((D_MODEL, D_FF), fixed, pipeline_mode=once),
            pl.BlockSpec((D_MODEL, D_FF), fixed, pipeline_mode=once),
            pl.BlockSpec((CONV_W, D_FF), fixed), pl.BlockSpec((1, D_FF), fixed),
            pl.BlockSpec((D_FF, D_MODEL), fixed, pipeline_mode=once), pl.BlockSpec((1, D_MODEL), fixed)]


def _ffn_prompt(xn, h, fw, batch, seq, tm):
    nt = seq // tm
    row = lambda b, j: (b * nt + j, 0)
    fixed = lambda b, j: (0, 0)
    full = pl.BlockSpec((tm, D_MODEL), row)
    return pl.pallas_call(
        _ffn_prompt_kernel,
        grid=(batch, nt),
        in_specs=[full, full] + _ffn_weight_specs(fixed),
        out_specs=[full, pl.BlockSpec((1, CONV_W - 1, D_FF), lambda b, j: (b, 0, 0))],
        out_shape=[jax.ShapeDtypeStruct((batch * seq, D_MODEL), F32),
                   jax.ShapeDtypeStruct((batch, CONV_W - 1, D_FF), F32)],
        scratch_shapes=[pltpu.VMEM((CONV_W - 1, D_FF), F32)],
        compiler_params=_params(("arbitrary", "arbitrary")),
        name="ffn_prompt",
    )(xn, h, *fw)


def _ffn_sample(xn, h, prev1, prev2, fw):
    m = xn.shape[0]
    fixed = lambda i: (0, 0)
    full = pl.BlockSpec((m, D_MODEL), fixed)
    ffs = pl.BlockSpec((m, D_FF), fixed)
    return pl.pallas_call(
        _ffn_sample_kernel,
        grid=(1,),
        in_specs=[full, full, ffs, ffs] + _ffn_weight_specs(fixed),
        out_specs=[full, ffs],
        out_shape=[jax.ShapeDtypeStruct((m, D_MODEL), F32), jax.ShapeDtypeStruct((m, D_FF), F32)],
        compiler_params=_params(("arbitrary",)),
        name="ffn_sample",
    )(xn, h, prev1, prev2, *fw)


def _layer_weights(g_mix, w_in, tok_mu, w0, w_decay_up, a0, w_iclr_up, w_gate_up, k_k, k_a, r_k,
                   ln_x_w, ln_x_b, w_o, g_ffn, w_ffn_in, conv_w, conv_b, w_ffn_out, g_final):
    vec = lambda t: t.reshape(1, -1)
    pad = LORA_PAD - LORA_W
    w_cat = jnp.pad(w_in, ((0, 0), (0, pad))).astype(BF16)
    mu_p = vec(tok_mu[:RKV_W])
    mu_l = vec(jnp.pad(tok_mu[RKV_W:], (0, pad)))
    lora_up = jnp.zeros((LORA_PAD, RKV_W), F32)
    lora_up = lora_up.at[0:LORA_DECAY, 0:RWKV_WIDTH].set(w_decay_up)
    lora_up = lora_up.at[LORA_DECAY:LORA_DECAY + LORA_ICLR, RWKV_WIDTH:2 * RWKV_WIDTH].set(w_iclr_up)
    lora_up = lora_up.at[LORA_DECAY + LORA_ICLR:LORA_W, 2 * RWKV_WIDTH:].set(w_gate_up)
    seg = np.arange(RWKV_WIDTH) // HEAD_DIM
    ones_bd = jnp.asarray(seg[:, None] == seg[None, :], BF16)
    prep = (mu_p, mu_l, lora_up.astype(BF16), vec(w0), vec(a0), vec(k_k), vec(k_a), vec(r_k), ones_bd)
    ffn = (w_ffn_in[:, :D_FF].astype(BF16), w_ffn_in[:, D_FF:].astype(BF16), conv_w, vec(conv_b),
           w_ffn_out.astype(BF16), vec(g_final))
    return dict(g_mix=vec(g_mix), w_cat=w_cat, prep=prep, ones_bd=ones_bd, ln_w=vec(ln_x_w), ln_b=vec(ln_x_b),
                w_o=w_o.astype(BF16), g_ffn=vec(g_ffn), ffn=ffn)


def kernel(x_prompt, x_sample, cache_win_k, cache_win_v, state_shift, state_wkv, state_ffn_conv, g_mix, w_in, rel_bias, tok_mu, w0, w_decay_up, a0, w_iclr_up, w_gate_up, k_k, k_a, r_k, ln_x_w, ln_x_b, w_o, g_ffn, w_ffn_in, conv_w, conv_b, w_ffn_out, g_final):
    batch, seq, _ = x_prompt.shape
    nb = x_sample.shape[0]
    lw = _layer_weights(g_mix[0], w_in[0], tok_mu[0], w0[0], w_decay_up[0], a0[0], w_iclr_up[0],
                        w_gate_up[0], k_k[0], k_a[0], r_k[0].reshape(-1), ln_x_w[0], ln_x_b[0], w_o[0],
                        g_ffn[0], w_ffn_in[0], conv_w[0], conv_b[0], w_ffn_out[0], g_final)
    (bias_p, bias_s), rb0 = _bias_build(rel_bias)

    xp = x_prompt.reshape(batch * seq, D_MODEL)
    qkv, rkv, lora = _inproj(xp, lw['g_mix'], lw['w_cat'], 512)
    r, lgw, k2, v, kk, b, gate, bonus = _prep_prompt(rkv, lora, lw['prep'], batch, seq, 512)
    y, wkv_p = _wkv_prompt(r, lgw, k2, v, kk, b, batch, seq)
    attn = _attn_prompt(qkv, bias_p, batch, seq)
    h1, xn2 = _outproj(xp, attn, y, gate, bonus, lw['ln_w'], lw['ln_b'], lw['w_o'], lw['g_ffn'],
                       lw['ones_bd'], 512)
    y_p, conv_p = _ffn_prompt(xn2, h1, lw['ffn'], batch, seq, 512)
    qkv3 = qkv.reshape(batch, seq, QKV_W)
    win_k_p = qkv3[:, seq - WIN:, ATT_WIDTH:2 * ATT_WIDTH].reshape(1, batch, WIN, N_HEADS, HEAD_DIM)
    win_v_p = qkv3[:, seq - WIN:, 2 * ATT_WIDTH:].reshape(1, batch, WIN, N_HEADS, HEAD_DIM)
    shift_p = jnp.concatenate([rkv.reshape(batch, seq, RKV_W)[:, -1],
                               lora.reshape(batch, seq, LORA_PAD)[:, -1, :LORA_W]], axis=-1)[None]

    xs = x_sample.reshape(nb, D_MODEL)
    qkv_s, rkv_s, lora_s = _inproj(xs, lw['g_mix'], lw['w_cat'], nb)
    sh = state_shift[0]
    r, lgw, k2, v, kk, b, gate, bonus = _prep_sample(
        rkv_s, sh[:, :RKV_W], lora_s, jnp.pad(sh[:, RKV_W:], ((0, 0), (0, LORA_PAD - LORA_W))), lw['prep'])
    y_s, wkv_s = _wkv_sample(state_wkv[0], lgw, kk, b, k2, r, v, 8)
    q_s = qkv_s[:, :ATT_WIDTH]
    k_s = qkv_s[:, ATT_WIDTH:2 * ATT_WIDTH]
    v_s = qkv_s[:, 2 * ATT_WIDTH:]
    ck = jnp.transpose(cache_win_k[0], (0, 2, 3, 1))
    cv = jnp.transpose(cache_win_v[0], (0, 2, 3, 1))
    attn_s, win_k_s, win_v_s = _sample_window(q_s, k_s, v_s, ck, cv, bias_s, rb0)
    win_k_s = jnp.transpose(win_k_s, (0, 3, 1, 2))
    win_v_s = jnp.transpose(win_v_s, (0, 3, 1, 2))
    h1_s, xn2_s = _outproj(xs, attn_s, y_s, gate, bonus, lw['ln_w'], lw['ln_b'], lw['w_o'], lw['g_ffn'],
                           lw['ones_bd'], nb)
    conv_state = state_ffn_conv[0]
    y_smp, gp_s = _ffn_sample(xn2_s, h1_s, conv_state[:, 1], conv_state[:, 0], lw['ffn'])
    conv_s = jnp.stack([conv_state[:, 1], gp_s], axis=1)
    shift_s = jnp.concatenate([rkv_s, lora_s[:, :LORA_W]], axis=-1)[None]

    return (y_p.reshape(batch, seq, D_MODEL), y_smp.reshape(nb, 1, D_MODEL),
            win_k_p, win_v_p, shift_p, wkv_p[None], conv_p[None],
            win_k_s[None], win_v_s[None], shift_s, wkv_s[None], conv_s[None])
```

```python
import functools

import numpy as np
import jax
import jax.numpy as jnp
from jax import lax
from jax.experimental import pallas as pl
from jax.experimental.pallas import tpu as pltpu

F32 = jnp.float32
BF16 = jnp.bfloat16

D_MODEL = 1024
HEAD_DIM = 64
ATT_WIDTH = 512
RWKV_WIDTH = 512
N_HEADS = 8
QKV_W = 3 * ATT_WIDTH
RKV_W = 3 * RWKV_WIDTH
LORA_DECAY, LORA_ICLR, LORA_GATE = 32, 32, 96
LORA_W = LORA_DECAY + LORA_ICLR + LORA_GATE
LORA_PAD = 256
RWKV_PROJ = RKV_W + LORA_W
PROJ_PAD = QKV_W + RKV_W + LORA_PAD
D_FF = 2816
FF_CHUNK = 1408
CONV_W = 3
WIN = 2048
DILATIONS = (1, 4, 16)
Q_BLOCK = 128
SUPER = 2048
N_BUCKETS = 32
MAX_DISTANCE = 2048
NORM_EPS = 1e-6
GN_EPS = 64e-5
NEG = -1e30
ATT_SCALE = HEAD_DIM ** -0.5
CHUNK = 64
VMEM_LIMIT = 56 * 1024 * 1024

NT = (((1,), (1,)), ((), ()))
TN = (((0,), (0,)), ((), ()))


def _dot(a, b):
    return jnp.dot(a.astype(BF16), b.astype(BF16), preferred_element_type=F32)


def _dot_nt(a, b):
    return lax.dot_general(a.astype(BF16), b.astype(BF16), NT, preferred_element_type=F32)


def _dot_tn(a, b):
    return lax.dot_general(a.astype(BF16), b.astype(BF16), TN, preferred_element_type=F32)


def _split2(x):
    hi = x.astype(BF16)
    lo = (x - hi.astype(F32)).astype(BF16)
    return hi, lo


def _split3(x):
    hi = x.astype(BF16)
    r1 = x - hi.astype(F32)
    mid = r1.astype(BF16)
    lo = (r1 - mid.astype(F32)).astype(BF16)
    return hi, mid, lo


def _segsum(x, ones_bd):
    hi, lo = _split2(x)
    return (jnp.dot(hi, ones_bd, preferred_element_type=F32)
            + jnp.dot(lo, ones_bd, preferred_element_type=F32))


def _sigmoid(x):
    return 1.0 / (1.0 + jnp.exp(-x))


def _rmsnorm(x, g):
    return x * lax.rsqrt(jnp.mean(x * x, axis=-1, keepdims=True) + NORM_EPS) * g


def _params(sem):
    return pltpu.CompilerParams(dimension_semantics=sem, vmem_limit_bytes=VMEM_LIMIT)


def _t5_bucket(dist):
    dist = np.asarray(dist, dtype=np.int64)
    exact = N_BUCKETS // 2
    scaled = np.log(np.maximum(dist, 1) / exact) / np.log(MAX_DISTANCE / exact)
    large = np.minimum(exact + (scaled * (N_BUCKETS - exact)).astype(np.int64), N_BUCKETS - 1)
    return np.where(dist < exact, dist, large).astype(np.int32)


def _inproj_kernel(x_ref, g_ref, w_ref, qkv_ref, rkv_ref, lora_ref):
    xb = _rmsnorm(x_ref[...], g_ref[...]).astype(BF16)
    qkv_ref[...] = jnp.dot(xb, w_ref[:, 0:QKV_W], preferred_element_type=F32)
    rkv_ref[...] = jnp.dot(xb, w_ref[:, QKV_W:QKV_W + RKV_W], preferred_element_type=F32)
    lora_ref[...] = jnp.dot(xb, w_ref[:, QKV_W + RKV_W:PROJ_PAD], preferred_element_type=F32)


def _inproj(x2d, g, w, tm):
    m = x2d.shape[0]
    row = lambda i: (i, 0)
    fixed = lambda i: (0, 0)
    return pl.pallas_call(
        _inproj_kernel,
        grid=(m // tm,),
        in_specs=[pl.BlockSpec((tm, D_MODEL), row),
                  pl.BlockSpec((1, D_MODEL), fixed),
                  pl.BlockSpec((D_MODEL, PROJ_PAD), fixed)],
        out_specs=[pl.BlockSpec((tm, QKV_W), row),
                   pl.BlockSpec((tm, RKV_W), row),
                   pl.BlockSpec((tm, LORA_PAD), row)],
        out_shape=[jax.ShapeDtypeStruct((m, QKV_W), F32),
                   jax.ShapeDtypeStruct((m, RKV_W), F32),
                   jax.ShapeDtypeStruct((m, LORA_PAD), F32)],
        compiler_params=_params(("arbitrary",)),
        name="inproj",
    )(x2d, g, w)


def _prep_math(p, p_prev, l, l_prev, mu_p, mu_l, wcat, w0, a0, k_k, k_a, r_k, ones_bd):
    pm = p + mu_p * (p_prev - p)
    lm = l + mu_l * (l_prev - l)
    r = pm[:, 0:RWKV_WIDTH]
    kr = pm[:, RWKV_WIDTH:2 * RWKV_WIDTH]
    vr = pm[:, 2 * RWKV_WIDTH:3 * RWKV_WIDTH]
    lane = lax.broadcasted_iota(jnp.int32, lm.shape, 1)
    feat = jnp.where(lane < LORA_DECAY, jnp.tanh(lm),
                     jnp.where(lane < LORA_DECAY + LORA_ICLR, lm, _sigmoid(lm)))
    z = jnp.dot(feat.astype(BF16), wcat, preferred_element_type=F32)
    zd = -(w0 + z[:, 0:RWKV_WIDTH])
    softplus = jnp.maximum(zd, 0.0) + jnp.log(1.0 + jnp.exp(-jnp.abs(zd)))
    lw = -jnp.exp(-softplus - 0.5)
    a = _sigmoid(a0 + z[:, RWKV_WIDTH:2 * RWKV_WIDTH])
    gate = z[:, 2 * RWKV_WIDTH:3 * RWKV_WIDTH]
    kk = kr * k_k
    kk = kk / jnp.maximum(jnp.sqrt(_segsum(kk * kk, ones_bd)), 1e-12)
    k2 = kr * (1.0 + (a - 1.0) * k_a)
    bonus = _segsum(r * k2 * r_k, ones_bd) * vr
    return r, lw, k2, vr, kk, kk * a, gate, bonus


def _prep_prompt_kernel(p_ref, l_ref, mu_p_ref, mu_l_ref, wcat_ref, w0_ref, a0_ref, kk_ref, ka_ref,
                        rk_ref, ones_ref, *refs):
    outs, (cp_ref, cl_ref) = refs[:8], refs[8:]

    @pl.when(pl.program_id(1) == 0)
    def _():
        cp_ref[...] = jnp.zeros_like(cp_ref)
        cl_ref[...] = jnp.zeros_like(cl_ref)

    p = p_ref[...]
    l = l_ref[...]
    tm = p.shape[0]
    first_p = lax.broadcasted_iota(jnp.int32, p.shape, 0) == 0
    first_l = lax.broadcasted_iota(jnp.int32, l.shape, 0) == 0
    p_prev = jnp.where(first_p, cp_ref[...], pltpu.roll(p, 1, 0))
    l_prev = jnp.where(first_l, cl_ref[...], pltpu.roll(l, 1, 0))
    cp_ref[...] = p[tm - 1:tm, :]
    cl_ref[...] = l[tm - 1:tm, :]
    res = _prep_math(p, p_prev, l, l_prev, mu_p_ref[...], mu_l_ref[...], wcat_ref[...], w0_ref[...],
                     a0_ref[...], kk_ref[...], ka_ref[...], rk_ref[...], ones_ref[...])
    for o, v in zip(outs, res):
        o[...] = v


def _prep_sample_kernel(p_ref, pp_ref, l_ref, lp_ref, mu_p_ref, mu_l_ref, wcat_ref, w0_ref, a0_ref,
                        kk_ref, ka_ref, rk_ref, ones_ref, *outs):
    res = _prep_math(p_ref[...], pp_ref[...], l_ref[...], lp_ref[...], mu_p_ref[...], mu_l_ref[...],
                     wcat_ref[...], w0_ref[...], a0_ref[...], kk_ref[...], ka_ref[...], rk_ref[...],
                     ones_ref[...])
    for o, v in zip(outs, res):
        o[...] = v


def _prep_prompt(rkv, lora, pw, batch, seq, tm):
    nt = seq // tm
    row = lambda b, j: (b * nt + j, 0)
    fixed = lambda b, j: (0, 0)
    vec = lambda n: pl.BlockSpec((1, n), fixed)
    m = rkv.shape[0]
    return pl.pallas_call(
        _prep_prompt_kernel,
        grid=(batch, nt),
        in_specs=[pl.BlockSpec((tm, RKV_W), row), pl.BlockSpec((tm, LORA_PAD), row),
                  vec(RKV_W), vec(LORA_PAD), pl.BlockSpec((LORA_PAD, RKV_W), fixed),
                  vec(RWKV_WIDTH), vec(RWKV_WIDTH), vec(RWKV_WIDTH), vec(RWKV_WIDTH), vec(RWKV_WIDTH),
                  pl.BlockSpec((RWKV_WIDTH, RWKV_WIDTH), fixed)],
        out_specs=[pl.BlockSpec((tm, RWKV_WIDTH), row)] * 8,
        out_shape=[jax.ShapeDtypeStruct((m, RWKV_WIDTH), F32)] * 8,
        scratch_shapes=[pltpu.VMEM((1, RKV_W), F32), pltpu.VMEM((1, LORA_PAD), F32)],
        compiler_params=_params(("arbitrary", "arbitrary")),
        name="rwkv_prep_prompt",
    )(rkv, lora, *pw)


def _prep_sample(rkv, rkv_prev, lora, lora_prev, pw):
    m = rkv.shape[0]
    full = lambda a: pl.BlockSpec(a.shape, lambda i: (0,) * a.ndim)
    args = (rkv, rkv_prev, lora, lora_prev) + tuple(pw)
    return pl.pallas_call(
        _prep_sample_kernel,
        grid=(1,),
        in_specs=[full(a) for a in args],
        out_specs=[pl.BlockSpec((m, RWKV_WIDTH), lambda i: (0, 0))] * 8,
        out_shape=[jax.ShapeDtypeStruct((m, RWKV_WIDTH), F32)] * 8,
        compiler_params=_params(("arbitrary",)),
        name="rwkv_prep_sample",
    )(*args)


def _cumsum_rows(x, tri):
    tri_b = tri.astype(BF16)
    out = None
    for piece in _split3(x):
        t = jnp.dot(tri_b, piece, preferred_element_type=F32)
        out = t if out is None else out + t
    return out


def _wkv_kernel(r_ref, lw_ref, k_ref, v_ref, kk_ref, b_ref, y_ref, s_ref, st_ref):
    c = pl.program_id(0)

    @pl.when(c == 0)
    def _():
        st_ref[...] = jnp.zeros_like(st_ref)

    nb = r_ref.shape[0]
    n = CHUNK
    row = lax.broadcasted_iota(jnp.int32, (n, n), 0)
    col = lax.broadcasted_iota(jnp.int32, (n, n), 1)
    lower_incl = row >= col
    lower_strict = row > col
    eye = jnp.where(row == col, 1.0, 0.0)
    tri = jnp.where(lower_incl, 1.0, 0.0)

    ah, bh, kh, rh, beh, keh, vh, ge = ([] for _ in range(8))
    for bi in range(nb):
        lw = lw_ref[bi]
        lg = _cumsum_rows(lw, tri)
        g_inv = jnp.exp(-lg)
        lg_end = lg[n - 1:n, :]
        g_to_end = jnp.exp(lg_end - lg)
        g_end = jnp.exp(lg_end)
        kk = kk_ref[bi]
        b = b_ref[bi]
        k = k_ref[bi]
        full = (-kk * jnp.exp(lg - lw), b * g_inv, k * g_inv, r_ref[bi] * jnp.exp(lg),
                b * g_to_end, k * g_to_end, v_ref[bi], g_end)
        for h in range(N_HEADS):
            sl = slice(h * HEAD_DIM, (h + 1) * HEAD_DIM)
            for dst, t in zip((ah, bh, kh, rh, beh, keh, vh), full[:7]):
                dst.append(t[:, sl].astype(BF16))
            ge.append(g_end[:, sl])
    units = range(nb * N_HEADS)

    ar = [jnp.concatenate([ah[u], rh[u]], axis=0) for u in units]
    g_b = [_dot_nt(ar[u], bh[u]) for u in units]
    g_k = [_dot_nt(ar[u], kh[u]) for u in units]
    l_ab = [jnp.where(lower_strict, g_b[u][:n], 0.0) for u in units]
    m_rb = [jnp.where(lower_incl, g_b[u][n:], 0.0).astype(BF16) for u in units]
    l_ak = [jnp.where(lower_strict, g_k[u][:n], 0.0) for u in units]
    m_rk = [jnp.where(lower_incl, g_k[u][n:], 0.0).astype(BF16) for u in units]
    lv = [_dot(l_ak[u], vh[u]) for u in units]
    tm_ = [eye + l_ab[u] for u in units]
    lp = [x.astype(BF16) for x in l_ab]
    span = 1
    while 2 * span < n:
        lp = [_dot(x, x).astype(BF16) for x in lp]
        tm_ = [tm_[u] + _dot(tm_[u], lp[u]) for u in units]
        span *= 2
    tm_ = [t.astype(BF16) for t in tm_]
    a_hat = [_dot(tm_[u], ah[u]) for u in units]
    u0 = [_dot(tm_[u], lv[u]) for u in units]
    s0 = [st_ref[u] for u in units]
    s0b = [s.astype(BF16) for s in s0]
    uu = [(_dot_nt(a_hat[u], s0b[u]) + u0[u]).astype(BF16) for u in units]
    y = [_dot_nt(rh[u], s0b[u]) + _dot(m_rb[u], uu[u]) + _dot(m_rk[u], vh[u]) for u in units]
    for u in units:
        st_ref[u] = s0[u] * ge[u] + _dot_tn(uu[u], beh[u]) + _dot_tn(vh[u], keh[u])
    for bi in range(nb):
        y_ref[bi] = jnp.concatenate(y[bi * N_HEADS:(bi + 1) * N_HEADS], axis=1)

    @pl.when(c == pl.num_programs(0) - 1)
    def _():
        s_ref[...] = st_ref[...]


def _wkv_prompt(r, lw, k2, v, kk, b, batch, seq):
    nc = seq // CHUNK
    seqs = lambda t: t.reshape(batch, seq, RWKV_WIDTH)
    spec = pl.BlockSpec((batch, CHUNK, RWKV_WIDTH), lambda c: (0, c, 0))
    nu = batch * N_HEADS
    y, s = pl.pallas_call(
        _wkv_kernel,
        grid=(nc,),
        in_specs=[spec] * 6,
        out_specs=[spec, pl.BlockSpec((nu, HEAD_DIM, HEAD_DIM), lambda c: (0, 0, 0))],
        out_shape=[jax.ShapeDtypeStruct((batch, seq, RWKV_WIDTH), F32),
                   jax.ShapeDtypeStruct((nu, HEAD_DIM, HEAD_DIM), F32)],
        scratch_shapes=[pltpu.VMEM((nu, HEAD_DIM, HEAD_DIM), F32)],
        compiler_params=_params(("arbitrary",)),
        name="wkv_prompt",
    )(*(seqs(t) for t in (r, lw, k2, v, kk, b)))
    return y.reshape(batch * seq, RWKV_WIDTH), s.reshape(batch, N_HEADS, HEAD_DIM, HEAD_DIM)


def _wkv_step_kernel(s_ref, lw_ref, kk_ref, b_ref, k_ref, r_ref, v_ref, y_ref, so_ref):
    s = s_ref[...]
    sa = jnp.sum(s * (-kk_ref[...]), axis=-1, keepdims=True)
    s = s * jnp.exp(lw_ref[...]) + sa * b_ref[...] + v_ref[...] * k_ref[...]
    so_ref[...] = s
    y_ref[...] = jnp.sum(s * r_ref[...], axis=-1, keepdims=True)


def _wkv_sample(state, lw, kk, b, k2, r, v, bb):
    nb = state.shape[0]
    rowv = lambda t: t.reshape(nb, N_HEADS, 1, HEAD_DIM)
    colv = lambda t: t.reshape(nb, N_HEADS, HEAD_DIM, 1)
    idx = lambda i: (i, 0, 0, 0)
    s_spec = pl.BlockSpec((bb, N_HEADS, HEAD_DIM, HEAD_DIM), idx)
    r_spec = pl.BlockSpec((bb, N_HEADS, 1, HEAD_DIM), idx)
    c_spec = pl.BlockSpec((bb, N_HEADS, HEAD_DIM, 1), idx)
    y, s_new = pl.pallas_call(
        _wkv_step_kernel,
        grid=(nb // bb,),
        in_specs=[s_spec] + [r_spec] * 5 + [c_spec],
        out_specs=[c_spec, s_spec],
        out_shape=[jax.ShapeDtypeStruct((nb, N_HEADS, HEAD_DIM, 1), F32),
                   jax.ShapeDtypeStruct(state.shape, F32)],
        compiler_params=_params(("arbitrary",)),
        name="wkv_sample",
    )(state, rowv(lw), rowv(kk), rowv(b), rowv(k2), rowv(r), colv(v))
    return y.reshape(nb, RWKV_WIDTH), s_new


def _bias_tables():
    i = np.arange(Q_BLOCK)[:, None]
    c = np.arange(2 * Q_BLOCK)[None, :]
    n = i + Q_BLOCK - c
    ok = (n >= 0) & (n <= Q_BLOCK)
    prompt = np.stack([np.where(ok, _t5_bucket(d * np.clip(n, 0, Q_BLOCK)), -1) for d in DILATIONS])
    dist = WIN - np.arange(WIN)
    sample = np.stack([np.where((dist % d == 0) & (dist // d <= Q_BLOCK), _t5_bucket(dist), -1)
                       for d in DILATIONS])
    sample = np.broadcast_to(sample[:, None, :], (len(DILATIONS), N_HEADS, WIN))
    return prompt.astype(np.int32), sample.astype(np.int32)


def _bias_kernel(rb_ref, rbc_ref, bp_ref, bs_ref, op_ref, os_ref):
    bp = bp_ref[0]
    bs = bs_ref[0]
    acc_s = jnp.full(bs.shape, NEG, F32)
    for bk in range(N_BUCKETS):
        acc_s = jnp.where(bs == bk, rbc_ref[bk], acc_s)
    os_ref[0] = acc_s
    for h in range(N_HEADS):
        acc = jnp.full(bp.shape, NEG, F32)
        for bk in range(N_BUCKETS):
            acc = jnp.where(bp == bk, rb_ref[bk, h], acc)
        op_ref[0, h] = acc


def _bias_build(rel_bias):
    bp, bs = _bias_tables()
    rbc = rel_bias.reshape(N_BUCKETS, N_HEADS, 1)
    nb = len(DILATIONS)
    return pl.pallas_call(
        _bias_kernel,
        grid=(nb,),
        in_specs=[pl.BlockSpec(memory_space=pltpu.SMEM),
                  pl.BlockSpec((N_BUCKETS, N_HEADS, 1), lambda i: (0, 0, 0)),
                  pl.BlockSpec((1, Q_BLOCK, 2 * Q_BLOCK), lambda i: (i, 0, 0)),
                  pl.BlockSpec((1, N_HEADS, WIN), lambda i: (i, 0, 0))],
        out_specs=[pl.BlockSpec((1, N_HEADS, Q_BLOCK, 2 * Q_BLOCK), lambda i: (i, 0, 0, 0)),
                   pl.BlockSpec((1, N_HEADS, WIN), lambda i: (i, 0, 0))],
        out_shape=[jax.ShapeDtypeStruct((nb, N_HEADS, Q_BLOCK, 2 * Q_BLOCK), F32),
                   jax.ShapeDtypeStruct((nb, N_HEADS, WIN), F32)],
        compiler_params=_params(("arbitrary",)),
        name="bias_build",
    )(rel_bias, rbc, jnp.asarray(bp), jnp.asarray(bs)), rbc[0]


ATTN_GROUP = 4


def _attn_units():
    units = []
    for br, d in enumerate(DILATIONS):
        nblk = SUPER // (d * Q_BLOCK)
        for r in range(d):
            rows = lambda j, r=r, d=d: (pl.ds(r + d * Q_BLOCK * j, Q_BLOCK, stride=d) if d > 1
                                        else pl.ds(Q_BLOCK * j, Q_BLOCK))
            for blk in range(nblk):
                units.append((br, rows(blk), rows(blk - 1 if blk else nblk - 1), blk == 0))
    return units


def _attn_kernel(q_ref, kc_ref, kp_ref, vc_ref, vp_ref, bias_ref, o_ref, acc_ref, m_ref):
    has_prev = pl.program_id(2) > 0
    lane = lax.broadcasted_iota(jnp.int32, (Q_BLOCK, 128), 1)
    is_a = lane < HEAD_DIM
    units = _attn_units()
    for g0 in range(0, len(units), ATTN_GROUP):
        group = units[g0:g0 + ATTN_GROUP]
        ops = []
        for br, cur, prv, from_prev in group:
            qs = q_ref[cur, :] * ATT_SCALE
            kp = (kp_ref if from_prev else kc_ref)[prv, :].astype(BF16)
            vp = (vp_ref if from_prev else vc_ref)[prv, :]
            kc = kc_ref[cur, :].astype(BF16)
            vc = vc_ref[cur, :]
            for hd in range(2):
                mine = is_a if hd == 0 else jnp.logical_not(is_a)
                ops.append((br, hd, cur, from_prev,
                            jnp.where(mine, qs, 0.0).astype(BF16), kp, kc,
                            jnp.where(mine, vp, 1.0).astype(BF16), jnp.where(mine, vc, 1.0).astype(BF16)))
        logits = []
        for br, hd, cur, from_prev, qh, kp, kc, vp, vc in ops:
            lp = lax.dot_general(qh, kp, NT, preferred_element_type=F32) + bias_ref[br, hd, :, :Q_BLOCK]
            lc = lax.dot_general(qh, kc, NT, preferred_element_type=F32) + bias_ref[br, hd, :, Q_BLOCK:]
            if from_prev:
                lp = jnp.where(has_prev, lp, NEG)
            logits.append((lp, lc))
        ms = [jnp.max(jnp.maximum(lp, lc), axis=-1, keepdims=True) for lp, lc in logits]
        ps = [(jnp.exp(lp - m).astype(BF16), jnp.exp(lc - m).astype(BF16)) for (lp, lc), m in zip(logits, ms)]
        for (br, hd, cur, _, _, _, _, vp, vc), (pp, pc), m in zip(ops, ps, ms):
            acc_ref[br, hd, cur, :] = (jnp.dot(pp, vp, preferred_element_type=F32)
                                       + jnp.dot(pc, vc, preferred_element_type=F32))
            m_ref[br, hd, cur, :] = jnp.broadcast_to(m, (Q_BLOCK, 128))

    def merge(i, carry):
        rs = pl.ds(pl.multiple_of(i * Q_BLOCK, Q_BLOCK), Q_BLOCK)
        outs = []
        for hd in range(2):
            ms = [m_ref[br, hd, rs, :] for br in range(len(DILATIONS))]
            m = functools.reduce(jnp.maximum, ms)
            tot = 0.0
            for br in range(len(DILATIONS)):
                tot = tot + jnp.exp(ms[br] - m) * acc_ref[br, hd, rs, :]
            outs.append(tot / pltpu.roll(tot, HEAD_DIM, 1))
        o_ref[rs, :] = jnp.where(is_a, outs[0], outs[1])
        return carry

    lax.fori_loop(0, SUPER // Q_BLOCK, merge, 0)


def _attn_prompt(qkv, bias_p, batch, seq):
    ns = seq // SUPER
    npair = ATT_WIDTH // 128
    blk = lambda col0, prev: pl.BlockSpec(
        (SUPER, 128),
        (lambda b, hp, sb: (b * ns + jnp.maximum(sb - 1, 0), col0 + hp)) if prev
        else (lambda b, hp, sb: (b * ns + sb, col0 + hp)))
    nb = len(DILATIONS)
    return pl.pallas_call(
        _attn_kernel,
        grid=(batch, npair, ns),
        in_specs=[blk(0, False), blk(npair, False), blk(npair, True), blk(2 * npair, False),
                  blk(2 * npair, True),
                  pl.BlockSpec((nb, 2, Q_BLOCK, 2 * Q_BLOCK), lambda b, hp, sb: (0, hp, 0, 0))],
        out_specs=pl.BlockSpec((SUPER, 128), lambda b, hp, sb: (b * ns + sb, hp)),
        out_shape=jax.ShapeDtypeStruct((batch * seq, ATT_WIDTH), F32),
        scratch_shapes=[pltpu.VMEM((nb, 2, SUPER, 128), F32)] * 2,
        compiler_params=_params(("arbitrary", "arbitrary", "arbitrary")),
        name="attn_prompt",
    )(qkv, qkv, qkv, qkv, qkv, bias_p)


def _sample_window_kernel(q_ref, kn_ref, qc_ref, knc_ref, vnc_ref, ck_ref, cv_ref, bias_ref, rb0_ref,
                          o_ref, ok_ref, ov_ref, wt_ref):
    lane = lax.broadcasted_iota(jnp.int32, (HEAD_DIM, WIN), 1)
    last = lane == WIN - 1
    l_new = jnp.sum(kn_ref[0] * (q_ref[0] * ATT_SCALE), axis=-1, keepdims=True) + rb0_ref[...]
    rows = []
    for h in range(N_HEADS):
        kh = ck_ref[0, h]
        rows.append(jnp.sum(kh * (qc_ref[0, h] * ATT_SCALE), axis=0, keepdims=True))
        ok_ref[0, h] = jnp.where(last, knc_ref[0, h], pltpu.roll(kh, WIN - 1, 1))
    logits = jnp.concatenate(rows, axis=0)
    parts = []
    for br in range(len(DILATIONS)):
        lg = logits + bias_ref[br]
        m = jnp.maximum(jnp.max(lg, axis=-1, keepdims=True), l_new)
        p = jnp.exp(lg - m)
        p_new = jnp.exp(l_new - m)
        parts.append((p, p_new, m, jnp.sum(p, axis=-1, keepdims=True) + p_new))
    m = functools.reduce(jnp.maximum, [pt[2] for pt in parts])
    wt = 0.0
    w_new = 0.0
    den = 0.0
    for p, p_new, m_g, s_g in parts:
        w = jnp.exp(m_g - m)
        wt = wt + w * p
        w_new = w_new + w * p_new
        den = den + w * s_g
    wt_ref[...] = wt / den
    w_new = w_new / den
    for h in range(N_HEADS):
        vh = cv_ref[0, h]
        o_ref[0, h] = (jnp.sum(vh * wt_ref[h:h + 1, :], axis=1, keepdims=True)
                       + vnc_ref[0, h] * w_new[h:h + 1, :])
        ov_ref[0, h] = jnp.where(last, vnc_ref[0, h], pltpu.roll(vh, WIN - 1, 1))


def _sample_window(q, k_new, v_new, cache_k, cache_v, bias_s, rb0):
    nb = q.shape[0]
    rows = lambda t: t.reshape(nb, N_HEADS, HEAD_DIM)
    cols = lambda t: t.reshape(nb, N_HEADS, HEAD_DIM, 1)
    row_spec = pl.BlockSpec((1, N_HEADS, HEAD_DIM), lambda i: (i, 0, 0))
    col_spec = pl.BlockSpec((1, N_HEADS, HEAD_DIM, 1), lambda i: (i, 0, 0, 0))
    win_spec = pl.BlockSpec((1, N_HEADS, HEAD_DIM, WIN), lambda i: (i, 0, 0, 0))
    fixed = lambda a: pl.BlockSpec(a.shape, lambda i: (0,) * a.ndim)
    out, win_k, win_v = pl.pallas_call(
        _sample_window_kernel,
        grid=(nb,),
        in_specs=[row_spec, row_spec, col_spec, col_spec, col_spec, win_spec, win_spec,
                  fixed(bias_s), fixed(rb0)],
        out_specs=[col_spec, win_spec, win_spec],
        out_shape=[jax.ShapeDtypeStruct((nb, N_HEADS, HEAD_DIM, 1), F32),
                   jax.ShapeDtypeStruct(cache_k.shape, F32), jax.ShapeDtypeStruct(cache_v.shape, F32)],
        scratch_shapes=[pltpu.VMEM((N_HEADS, WIN), F32)],
        compiler_params=_params(("arbitrary",)),
        name="sample_window",
    )(rows(q), rows(k_new), cols(q), cols(k_new), cols(v_new), cache_k, cache_v, bias_s, rb0)
    return out.reshape(nb, ATT_WIDTH), win_k, win_v


def _outproj_kernel(x_ref, attn_ref, y_ref, gate_ref, bonus_ref, lnw_ref, lnb_ref, wo_ref, gffn_ref,
                    ones_ref, h_ref, xn_ref):
    ones_bd = ones_ref[...]
    y = y_ref[...]
    mu = _segsum(y, ones_bd) * (1.0 / HEAD_DIM)
    dy = y - mu
    var = _segsum(dy * dy, ones_bd) * (1.0 / HEAD_DIM)
    yn = dy * lax.rsqrt(var + GN_EPS) * lnw_ref[...] + lnb_ref[...]
    rw = (yn + bonus_ref[...]) * gate_ref[...]
    mixed = (jnp.dot(attn_ref[...].astype(BF16), wo_ref[0:ATT_WIDTH, :], preferred_element_type=F32)
             + jnp.dot(rw.astype(BF16), wo_ref[ATT_WIDTH:D_MODEL, :], preferred_element_type=F32))
    h = x_ref[...] + mixed
    h_ref[...] = h
    xn_ref[...] = _rmsnorm(h, gffn_ref[...]).astype(BF16)


def _outproj(x2d, attn, y, gate, bonus, ln_w, ln_b, w_o, g_ffn, ones_bd, tm):
    m = x2d.shape[0]
    row = lambda i: (i, 0)
    fixed = lambda i: (0, 0)
    half = pl.BlockSpec((tm, RWKV_WIDTH), row)
    full = pl.BlockSpec((tm, D_MODEL), row)
    return pl.pallas_call(
        _outproj_kernel,
        grid=(m // tm,),
        in_specs=[full, half, half, half, half,
                  pl.BlockSpec((1, RWKV_WIDTH), fixed), pl.BlockSpec((1, RWKV_WIDTH), fixed),
                  pl.BlockSpec((D_MODEL, D_MODEL), fixed), pl.BlockSpec((1, D_MODEL), fixed),
                  pl.BlockSpec((RWKV_WIDTH, RWKV_WIDTH), fixed)],
        out_specs=[full, full],
        out_shape=[jax.ShapeDtypeStruct((m, D_MODEL), F32), jax.ShapeDtypeStruct((m, D_MODEL), BF16)],
        compiler_params=_params(("arbitrary",)),
        name="outproj",
    )(x2d, attn, y, gate, bonus, ln_w, ln_b, w_o, g_ffn, ones_bd)


def _ffn_tail(xn, h, gp_m1, gp_m2, gp_of, wg_ref, wu_ref, cw_ref, cb_ref, wout_ref, gfin_ref, o_ref):
    acc = h
    for c in range(D_FF // FF_CHUNK):
        cs = slice(c * FF_CHUNK, (c + 1) * FF_CHUNK)
        gp = gp_of(c)
        up = jnp.dot(xn, wu_ref[:, cs], preferred_element_type=F32)
        conv = (cb_ref[:, cs] + cw_ref[0:1, cs] * gp_m2(c, gp) + cw_ref[1:2, cs] * gp_m1(c, gp)
                + cw_ref[2:3, cs] * gp)
        act = conv * _sigmoid(conv) * up
        acc = acc + jnp.dot(act.astype(BF16), wout_ref[cs, :], preferred_element_type=F32)
    o_ref[...] = _rmsnorm(acc, gfin_ref[...])


def _ffn_prompt_kernel(xn_ref, h_ref, wg_ref, wu_ref, cw_ref, cb_ref, wout_ref, gfin_ref,
                       o_ref, conv_ref, carry_ref):
    @pl.when(pl.program_id(1) == 0)
    def _():
        carry_ref[...] = jnp.zeros_like(carry_ref)

    xn = xn_ref[...]
    tm = xn.shape[0]
    rowi = lax.broadcasted_iota(jnp.int32, (tm, FF_CHUNK), 0)
    gps = {}

    def gp_of(c):
        cs = slice(c * FF_CHUNK, (c + 1) * FF_CHUNK)
        gps[c] = jnp.dot(xn, wg_ref[:, cs], preferred_element_type=F32)
        return gps[c]

    def gp_m1(c, gp):
        cs = slice(c * FF_CHUNK, (c + 1) * FF_CHUNK)
        return jnp.where(rowi == 0, carry_ref[1:2, cs], pltpu.roll(gp, 1, 0))

    def gp_m2(c, gp):
        cs = slice(c * FF_CHUNK, (c + 1) * FF_CHUNK)
        return jnp.where(rowi == 0, carry_ref[0:1, cs],
                         jnp.where(rowi == 1, carry_ref[1:2, cs], pltpu.roll(gp, 2, 0)))

    _ffn_tail(xn, h_ref[...], gp_m1, gp_m2, gp_of, wg_ref, wu_ref, cw_ref, cb_ref, wout_ref, gfin_ref, o_ref)
    for c, gp in gps.items():
        cs = slice(c * FF_CHUNK, (c + 1) * FF_CHUNK)
        carry_ref[:, cs] = gp[tm - 2:tm, :]
        conv_ref[0, :, cs] = gp[tm - 2:tm, :]


def _ffn_sample_kernel(xn_ref, h_ref, p1_ref, p2_ref, wg_ref, wu_ref, cw_ref, cb_ref, wout_ref, gfin_ref,
                       o_ref, gp_ref):
    xn = xn_ref[...]

    def gp_of(c):
        cs = slice(c * FF_CHUNK, (c + 1) * FF_CHUNK)
        gp = jnp.dot(xn, wg_ref[:, cs], preferred_element_type=F32)
        gp_ref[:, cs] = gp
        return gp

    gp_m1 = lambda c, gp: p1_ref[:, c * FF_CHUNK:(c + 1) * FF_CHUNK]
    gp_m2 = lambda c, gp: p2_ref[:, c * FF_CHUNK:(c + 1) * FF_CHUNK]
    _ffn_tail(xn, h_ref[...], gp_m1, gp_m2, gp_of, wg_ref, wu_ref, cw_ref, cb_ref, wout_ref, gfin_ref, o_ref)


def _ffn_weight_specs(fixed):
    once = pl.Buffered(1)
    return [pl.BlockSpec((D_MODEL, D_FF), fixed, pipeline_mode=once),
            pl.BlockSpec((D_MODEL, D_FF), fixed, pipeline_mode=once),
            pl.BlockSpec((CONV_W, D_FF), fixed), pl.BlockSpec((1, D_FF), fixed),
            pl.BlockSpec((D_FF, D_MODEL), fixed, pipeline_mode=once), pl.BlockSpec((1, D_MODEL), fixed)]


def _ffn_prompt(xn, h, fw, batch, seq, tm):
    nt = seq // tm
    row = lambda b, j: (b * nt + j, 0)
    fixed = lambda b, j: (0, 0)
    full = pl.BlockSpec((tm, D_MODEL), row)
    return pl.pallas_call(
        _ffn_prompt_kernel,
        grid=(batch, nt),
        in_specs=[full, full] + _ffn_weight_specs(fixed),
        out_specs=[full, pl.BlockSpec((1, CONV_W - 1, D_FF), lambda b, j: (b, 0, 0))],
        out_shape=[jax.ShapeDtypeStruct((batch * seq, D_MODEL), F32),
                   jax.ShapeDtypeStruct((batch, CONV_W - 1, D_FF), F32)],
        scratch_shapes=[pltpu.VMEM((CONV_W - 1, D_FF), F32)],
        compiler_params=_params(("arbitrary", "arbitrary")),
        name="ffn_prompt",
    )(xn, h, *fw)


def _ffn_sample(xn, h, prev1, prev2, fw):
    m = xn.shape[0]
    fixed = lambda i: (0, 0)
    full = pl.BlockSpec((m, D_MODEL), fixed)
    ffs = pl.BlockSpec((m, D_FF), fixed)
    return pl.pallas_call(
        _ffn_sample_kernel,
        grid=(1,),
        in_specs=[full, full, ffs, ffs] + _ffn_weight_specs(fixed),
        out_specs=[full, ffs],
        out_shape=[jax.ShapeDtypeStruct((m, D_MODEL), F32), jax.ShapeDtypeStruct((m, D_FF), F32)],
        compiler_params=_params(("arbitrary",)),
        name="ffn_sample",
    )(xn, h, prev1, prev2, *fw)


def _layer_weights(g_mix, w_in, tok_mu, w0, w_decay_up, a0, w_iclr_up, w_gate_up, k_k, k_a, r_k,
                   ln_x_w, ln_x_b, w_o, g_ffn, w_ffn_in, conv_w, conv_b, w_ffn_out, g_final):
    vec = lambda t: t.reshape(1, -1)
    pad = LORA_PAD - LORA_W
    w_cat = jnp.pad(w_in, ((0, 0), (0, pad))).astype(BF16)
    mu_p = vec(tok_mu[:RKV_W])
    mu_l = vec(jnp.pad(tok_mu[RKV_W:], (0, pad)))
    lora_up = jnp.zeros((LORA_PAD, RKV_W), F32)
    lora_up = lora_up.at[0:LORA_DECAY, 0:RWKV_WIDTH].set(w_decay_up)
    lora_up = lora_up.at[LORA_DECAY:LORA_DECAY + LORA_ICLR, RWKV_WIDTH:2 * RWKV_WIDTH].set(w_iclr_up)
    lora_up = lora_up.at[LORA_DECAY + LORA_ICLR:LORA_W, 2 * RWKV_WIDTH:].set(w_gate_up)
    seg = np.arange(RWKV_WIDTH) // HEAD_DIM
    ones_bd = jnp.asarray(seg[:, None] == seg[None, :], BF16)
    prep = (mu_p, mu_l, lora_up.astype(BF16), vec(w0), vec(a0), vec(k_k), vec(k_a), vec(r_k), ones_bd)
    ffn = (w_ffn_in[:, :D_FF].astype(BF16), w_ffn_in[:, D_FF:].astype(BF16), conv_w, vec(conv_b),
           w_ffn_out.astype(BF16), vec(g_final))
    return dict(g_mix=vec(g_mix), w_cat=w_cat, prep=prep, ones_bd=ones_bd, ln_w=vec(ln_x_w), ln_b=vec(ln_x_b),
                w_o=w_o.astype(BF16), g_ffn=vec(g_ffn), ffn=ffn)


def kernel(x_prompt, x_sample, cache_win_k, cache_win_v, state_shift, state_wkv, state_ffn_conv, g_mix, w_in, rel_bias, tok_mu, w0, w_decay_up, a0, w_iclr_up, w_gate_up, k_k, k_a, r_k, ln_x_w, ln_x_b, w_o, g_ffn, w_ffn_in, conv_w, conv_b, w_ffn_out, g_final):
    batch, seq, _ = x_prompt.shape
    nb = x_sample.shape[0]
    lw = _layer_weights(g_mix[0], w_in[0], tok_mu[0], w0[0], w_decay_up[0], a0[0], w_iclr_up[0],
                        w_gate_up[0], k_k[0], k_a[0], r_k[0].reshape(-1), ln_x_w[0], ln_x_b[0], w_o[0],
                        g_ffn[0], w_ffn_in[0], conv_w[0], conv_b[0], w_ffn_out[0], g_final)
    (bias_p, bias_s), rb0 = _bias_build(rel_bias)

    xp = x_prompt.reshape(batch * seq, D_MODEL)
    qkv, rkv, lora = _inproj(xp, lw['g_mix'], lw['w_cat'], 512)
    r, lgw, k2, v, kk, b, gate, bonus = _prep_prompt(rkv, lora, lw['prep'], batch, seq, 512)
    y, wkv_p = _wkv_prompt(r, lgw, k2, v, kk, b, batch, seq)
    attn = _attn_prompt(qkv, bias_p, batch, seq)
    h1, xn2 = _outproj(xp, attn, y, gate, bonus, lw['ln_w'], lw['ln_b'], lw['w_o'], lw['g_ffn'],
                       lw['ones_bd'], 512)
    y_p, conv_p = _ffn_prompt(xn2, h1, lw['ffn'], batch, seq, 512)
    qkv3 = qkv.reshape(batch, seq, QKV_W)
    win_k_p = qkv3[:, seq - WIN:, ATT_WIDTH:2 * ATT_WIDTH].reshape(1, batch, WIN, N_HEADS, HEAD_DIM)
    win_v_p = qkv3[:, seq - WIN:, 2 * ATT_WIDTH:].reshape(1, batch, WIN, N_HEADS, HEAD_DIM)
    shift_p = jnp.concatenate([rkv.reshape(batch, seq, RKV_W)[:, -1],
                               lora.reshape(batch, seq, LORA_PAD)[:, -1, :LORA_W]], axis=-1)[None]

    xs = x_sample.reshape(nb, D_MODEL)
    qkv_s, rkv_s, lora_s = _inproj(xs, lw['g_mix'], lw['w_cat'], nb)
    sh = state_shift[0]
    r, lgw, k2, v, kk, b, gate, bonus = _prep_sample(
        rkv_s, sh[:, :RKV_W], lora_s, jnp.pad(sh[:, RKV_W:], ((0, 0), (0, LORA_PAD - LORA_W))), lw['prep'])
    y_s, wkv_s = _wkv_sample(state_wkv[0], lgw, kk, b, k2, r, v, 8)
    q_s = qkv_s[:, :ATT_WIDTH]
    k_s = qkv_s[:, ATT_WIDTH:2 * ATT_WIDTH]
    v_s = qkv_s[:, 2 * ATT_WIDTH:]
    ck = jnp.transpose(cache_win_k[0], (0, 2, 3, 1))
    cv = jnp.transpose(cache_win_v[0], (0, 2, 3, 1))
    attn_s, win_k_s, win_v_s = _sample_window(q_s, k_s, v_s, ck, cv, bias_s, rb0)
    win_k_s = jnp.transpose(win_k_s, (0, 3, 1, 2))
    win_v_s = jnp.transpose(win_v_s, (0, 3, 1, 2))
    h1_s, xn2_s = _outproj(xs, attn_s, y_s, gate, bonus, lw['ln_w'], lw['ln_b'], lw['w_o'], lw['g_ffn'],
                           lw['ones_bd'], nb)
    conv_state = state_ffn_conv[0]
    y_smp, gp_s = _ffn_sample(xn2_s, h1_s, conv_state[:, 1], conv_state[:, 0], lw['ffn'])
    conv_s = jnp.stack([conv_state[:, 1], gp_s], axis=1)
    shift_s = jnp.concatenate([rkv_s, lora_s[:, :LORA_W]], axis=-1)[None]

    return (y_p.reshape(batch, seq, D_MODEL), y_smp.reshape(nb, 1, D_MODEL),
            win_k_p, win_v_p, shift_p, wkv_p[None], conv_p[None],
            win_k_s[None], win_v_s[None], shift_s, wkv_s[None], conv_s[None])
```

```python
import functools

import numpy as np
import jax
import jax.numpy as jnp
from jax import lax
from jax.experimental import pallas as pl
from jax.experimental.pallas import tpu as pltpu

F32 = jnp.float32
BF16 = jnp.bfloat16

D_MODEL = 1024
HEAD_DIM = 64
ATT_WIDTH = 512
RWKV_WIDTH = 512
N_HEADS = 8
QKV_W = 3 * ATT_WIDTH
RKV_W = 3 * RWKV_WIDTH
LORA_DECAY, LORA_ICLR, LORA_GATE = 32, 32, 96
LORA_W = LORA_DECAY + LORA_ICLR + LORA_GATE
LORA_PAD = 256
RWKV_PROJ = RKV_W + LORA_W
PROJ_PAD = QKV_W + RKV_W + LORA_PAD
D_FF = 2816
FF_CHUNK = 1408
CONV_W = 3
WIN = 2048
DILATIONS = (1, 4, 16)
Q_BLOCK = 128
SUPER = 2048
N_BUCKETS = 32
MAX_DISTANCE = 2048
NORM_EPS = 1e-6
GN_EPS = 64e-5
NEG = -1e30
ATT_SCALE = HEAD_DIM ** -0.5
LOG2E = float(np.log2(np.e))
CHUNK = 64
VMEM_LIMIT = 56 * 1024 * 1024

NT = (((1,), (1,)), ((), ()))
TN = (((0,), (0,)), ((), ()))


def _dot(a, b):
    return jnp.dot(a.astype(BF16), b.astype(BF16), preferred_element_type=F32)


def _dot_nt(a, b):
    return lax.dot_general(a.astype(BF16), b.astype(BF16), NT, preferred_element_type=F32)


def _dot_tn(a, b):
    return lax.dot_general(a.astype(BF16), b.astype(BF16), TN, preferred_element_type=F32)


def _split2(x):
    hi = x.astype(BF16)
    lo = (x - hi.astype(F32)).astype(BF16)
    return hi, lo


def _split3(x):
    hi = x.astype(BF16)
    r1 = x - hi.astype(F32)
    mid = r1.astype(BF16)
    lo = (r1 - mid.astype(F32)).astype(BF16)
    return hi, mid, lo


def _segsum(x, ones_bd):
    hi, lo = _split2(x)
    return (jnp.dot(hi, ones_bd, preferred_element_type=F32)
            + jnp.dot(lo, ones_bd, preferred_element_type=F32))


def _sigmoid(x):
    return 1.0 / (1.0 + jnp.exp(-x))


def _rmsnorm(x, g):
    return x * lax.rsqrt(jnp.mean(x * x, axis=-1, keepdims=True) + NORM_EPS) * g


def _params(sem):
    return pltpu.CompilerParams(dimension_semantics=sem, vmem_limit_bytes=VMEM_LIMIT)


def _t5_bucket(dist):
    dist = np.asarray(dist, dtype=np.int64)
    exact = N_BUCKETS // 2
    scaled = np.log(np.maximum(dist, 1) / exact) / np.log(MAX_DISTANCE / exact)
    large = np.minimum(exact + (scaled * (N_BUCKETS - exact)).astype(np.int64), N_BUCKETS - 1)
    return np.where(dist < exact, dist, large).astype(np.int32)


def _inproj_kernel(x_ref, g_ref, w_ref, qkv_ref, rkv_ref, lora_ref):
    xb = _rmsnorm(x_ref[...], g_ref[...]).astype(BF16)
    qkv_ref[...] = jnp.dot(xb, w_ref[:, 0:QKV_W], preferred_element_type=F32)
    rkv_ref[...] = jnp.dot(xb, w_ref[:, QKV_W:QKV_W + RKV_W], preferred_element_type=F32)
    lora_ref[...] = jnp.dot(xb, w_ref[:, QKV_W + RKV_W:PROJ_PAD], preferred_element_type=F32)


def _inproj(x2d, g, w, tm):
    m = x2d.shape[0]
    row = lambda i: (i, 0)
    fixed = lambda i: (0, 0)
    return pl.pallas_call(
        _inproj_kernel,
        grid=(m // tm,),
        in_specs=[pl.BlockSpec((tm, D_MODEL), row),
                  pl.BlockSpec((1, D_MODEL), fixed),
                  pl.BlockSpec((D_MODEL, PROJ_PAD), fixed)],
        out_specs=[pl.BlockSpec((tm, QKV_W), row),
                   pl.BlockSpec((tm, RKV_W), row),
                   pl.BlockSpec((tm, LORA_PAD), row)],
        out_shape=[jax.ShapeDtypeStruct((m, QKV_W), F32),
                   jax.ShapeDtypeStruct((m, RKV_W), F32),
                   jax.ShapeDtypeStruct((m, LORA_PAD), F32)],
        compiler_params=_params(("arbitrary",)),
        name="inproj",
    )(x2d, g, w)


def _prep_math(p, p_prev, l, l_prev, mu_p, mu_l, wcat, w0, a0, k_k, k_a, r_k, ones_bd):
    pm = p + mu_p * (p_prev - p)
    lm = l + mu_l * (l_prev - l)
    r = pm[:, 0:RWKV_WIDTH]
    kr = pm[:, RWKV_WIDTH:2 * RWKV_WIDTH]
    vr = pm[:, 2 * RWKV_WIDTH:3 * RWKV_WIDTH]
    lane = lax.broadcasted_iota(jnp.int32, lm.shape, 1)
    feat = jnp.where(lane < LORA_DECAY, jnp.tanh(lm),
                     jnp.where(lane < LORA_DECAY + LORA_ICLR, lm, _sigmoid(lm)))
    z = jnp.dot(feat.astype(BF16), wcat, preferred_element_type=F32)
    zd = -(w0 + z[:, 0:RWKV_WIDTH])
    softplus = jnp.maximum(zd, 0.0) + jnp.log(1.0 + jnp.exp(-jnp.abs(zd)))
    lw = -jnp.exp(-softplus - 0.5)
    a = _sigmoid(a0 + z[:, RWKV_WIDTH:2 * RWKV_WIDTH])
    gate = z[:, 2 * RWKV_WIDTH:3 * RWKV_WIDTH]
    kk = kr * k_k
    kk = kk / jnp.maximum(jnp.sqrt(_segsum(kk * kk, ones_bd)), 1e-12)
    k2 = kr * (1.0 + (a - 1.0) * k_a)
    bonus = _segsum(r * k2 * r_k, ones_bd) * vr
    return r, lw, k2, vr, kk, kk * a, gate, bonus


def _prep_prompt_kernel(p_ref, l_ref, mu_p_ref, mu_l_ref, wcat_ref, w0_ref, a0_ref, kk_ref, ka_ref,
                        rk_ref, ones_ref, *refs):
    outs, (cp_ref, cl_ref) = refs[:8], refs[8:]

    @pl.when(pl.program_id(1) == 0)
    def _():
        cp_ref[...] = jnp.zeros_like(cp_ref)
        cl_ref[...] = jnp.zeros_like(cl_ref)

    p = p_ref[...]
    l = l_ref[...]
    tm = p.shape[0]
    first_p = lax.broadcasted_iota(jnp.int32, p.shape, 0) == 0
    first_l = lax.broadcasted_iota(jnp.int32, l.shape, 0) == 0
    p_prev = jnp.where(first_p, cp_ref[...], pltpu.roll(p, 1, 0))
    l_prev = jnp.where(first_l, cl_ref[...], pltpu.roll(l, 1, 0))
    cp_ref[...] = p[tm - 1:tm, :]
    cl_ref[...] = l[tm - 1:tm, :]
    res = _prep_math(p, p_prev, l, l_prev, mu_p_ref[...], mu_l_ref[...], wcat_ref[...], w0_ref[...],
                     a0_ref[...], kk_ref[...], ka_ref[...], rk_ref[...], ones_ref[...])
    for o, v in zip(outs, res):
        o[...] = v


def _prep_sample_kernel(p_ref, pp_ref, l_ref, lp_ref, mu_p_ref, mu_l_ref, wcat_ref, w0_ref, a0_ref,
                        kk_ref, ka_ref, rk_ref, ones_ref, *outs):
    res = _prep_math(p_ref[...], pp_ref[...], l_ref[...], lp_ref[...], mu_p_ref[...], mu_l_ref[...],
                     wcat_ref[...], w0_ref[...], a0_ref[...], kk_ref[...], ka_ref[...], rk_ref[...],
                     ones_ref[...])
    for o, v in zip(outs, res):
        o[...] = v


def _prep_prompt(rkv, lora, pw, batch, seq, tm):
    nt = seq // tm
    row = lambda b, j: (b * nt + j, 0)
    fixed = lambda b, j: (0, 0)
    vec = lambda n: pl.BlockSpec((1, n), fixed)
    m = rkv.shape[0]
    return pl.pallas_call(
        _prep_prompt_kernel,
        grid=(batch, nt),
        in_specs=[pl.BlockSpec((tm, RKV_W), row), pl.BlockSpec((tm, LORA_PAD), row),
                  vec(RKV_W), vec(LORA_PAD), pl.BlockSpec((LORA_PAD, RKV_W), fixed),
                  vec(RWKV_WIDTH), vec(RWKV_WIDTH), vec(RWKV_WIDTH), vec(RWKV_WIDTH), vec(RWKV_WIDTH),
                  pl.BlockSpec((RWKV_WIDTH, RWKV_WIDTH), fixed)],
        out_specs=[pl.BlockSpec((tm, RWKV_WIDTH), row)] * 8,
        out_shape=[jax.ShapeDtypeStruct((m, RWKV_WIDTH), F32)] * 8,
        scratch_shapes=[pltpu.VMEM((1, RKV_W), F32), pltpu.VMEM((1, LORA_PAD), F32)],
        compiler_params=_params(("arbitrary", "arbitrary")),
        name="rwkv_prep_prompt",
    )(rkv, lora, *pw)


def _prep_sample(rkv, rkv_prev, lora, lora_prev, pw):
    m = rkv.shape[0]
    full = lambda a: pl.BlockSpec(a.shape, lambda i: (0,) * a.ndim)
    args = (rkv, rkv_prev, lora, lora_prev) + tuple(pw)
    return pl.pallas_call(
        _prep_sample_kernel,
        grid=(1,),
        in_specs=[full(a) for a in args],
        out_specs=[pl.BlockSpec((m, RWKV_WIDTH), lambda i: (0, 0))] * 8,
        out_shape=[jax.ShapeDtypeStruct((m, RWKV_WIDTH), F32)] * 8,
        compiler_params=_params(("arbitrary",)),
        name="rwkv_prep_sample",
    )(*args)


def _cumsum_rows(x, tri):
    tri_b = tri.astype(BF16)
    out = None
    for piece in _split3(x):
        t = jnp.dot(tri_b, piece, preferred_element_type=F32)
        out = t if out is None else out + t
    return out


def _wkv_kernel(r_ref, lw_ref, k_ref, v_ref, kk_ref, b_ref, y_ref, s_ref, st_ref):
    c = pl.program_id(0)

    @pl.when(c == 0)
    def _():
        st_ref[...] = jnp.zeros_like(st_ref)

    nb = r_ref.shape[0]
    n = CHUNK
    row = lax.broadcasted_iota(jnp.int32, (n, n), 0)
    col = lax.broadcasted_iota(jnp.int32, (n, n), 1)
    eye = jnp.where(row == col, 1.0, 0.0)
    tri = jnp.where(row >= col, 1.0, 0.0)
    row2 = lax.broadcasted_iota(jnp.int32, (2 * n, 2 * n), 0)
    col2 = lax.broadcasted_iota(jnp.int32, (2 * n, 2 * n), 1) % n
    gram_mask = jnp.logical_or(row2 % n > col2, jnp.logical_and(row2 >= n, row2 % n == col2))
    zeros = jnp.zeros((n, HEAD_DIM), BF16)

    ah, bh, kh, rh, beh, keh, vh, ge = ([] for _ in range(8))
    for bi in range(nb):
        lw = lw_ref[bi]
        lg = _cumsum_rows(lw, tri)
        g_inv = jnp.exp(-lg)
        lg_end = lg[n - 1:n, :]
        g_to_end = jnp.exp(lg_end - lg)
        g_end = jnp.exp(lg_end)
        kk = kk_ref[bi]
        b = b_ref[bi]
        k = k_ref[bi]
        full = (-kk * jnp.exp(lg - lw), b * g_inv, k * g_inv, r_ref[bi] * jnp.exp(lg),
                b * g_to_end, k * g_to_end, v_ref[bi], g_end)
        for h in range(N_HEADS):
            sl = slice(h * HEAD_DIM, (h + 1) * HEAD_DIM)
            for dst, t in zip((ah, bh, kh, rh, beh, keh, vh), full[:7]):
                dst.append(t[:, sl].astype(BF16))
            ge.append(g_end[:, sl])
    units = range(nb * N_HEADS)

    stack = lambda xs, ys: [jnp.concatenate([x, y], axis=0) for x, y in zip(xs, ys)]
    gram = [jnp.where(gram_mask, _dot_nt(ar, bk), 0.0) for ar, bk in zip(stack(ah, rh), stack(bh, kh))]
    top = [g[:n].astype(BF16) for g in gram]
    bot = [g[n:].astype(BF16) for g in gram]
    lv = [_dot(top[u], jnp.concatenate([zeros, vh[u]], axis=0)) for u in units]
    tm_ = [eye + gram[u][:n, :n] for u in units]
    lp = [_dot(t[:, :n], t[:, :n]).astype(BF16) for t in top]
    span = 2
    while 2 * span < n:
        prod = [_dot(x, p) for x, p in zip(stack(lp, [t.astype(BF16) for t in tm_]), lp)]
        tm_ = [tm_[u] + prod[u][n:] for u in units]
        lp = [p[:n].astype(BF16) for p in prod]
        span *= 2
    tm_ = [(tm_[u] + _dot(tm_[u], lp[u])).astype(BF16) for u in units]
    a_hat = [_dot(tm_[u], ah[u]).astype(BF16) for u in units]
    u0 = [_dot(tm_[u], lv[u]) for u in units]
    s0 = [st_ref[u] for u in units]
    on_s0 = [_dot_nt(x, s.astype(BF16)) for x, s in zip(stack(a_hat, rh), s0)]
    uv = stack([(on_s0[u][:n] + u0[u]).astype(BF16) for u in units], vh)
    y = [on_s0[u][n:] + _dot(bot[u], uv[u]) for u in units]
    for u, be_ke in zip(units, stack(beh, keh)):
        st_ref[u] = s0[u] * ge[u] + _dot_tn(uv[u], be_ke)
    for bi in range(nb):
        y_ref[bi] = jnp.concatenate(y[bi * N_HEADS:(bi + 1) * N_HEADS], axis=1)

    @pl.when(c == pl.num_programs(0) - 1)
    def _():
        s_ref[...] = st_ref[...]


def _wkv_prompt(r, lw, k2, v, kk, b, batch, seq):
    nc = seq // CHUNK
    seqs = lambda t: t.reshape(batch, seq, RWKV_WIDTH)
    spec = pl.BlockSpec((batch, CHUNK, RWKV_WIDTH), lambda c: (0, c, 0))
    nu = batch * N_HEADS
    y, s = pl.pallas_call(
        _wkv_kernel,
        grid=(nc,),
        in_specs=[spec] * 6,
        out_specs=[spec, pl.BlockSpec((nu, HEAD_DIM, HEAD_DIM), lambda c: (0, 0, 0))],
        out_shape=[jax.ShapeDtypeStruct((batch, seq, RWKV_WIDTH), F32),
                   jax.ShapeDtypeStruct((nu, HEAD_DIM, HEAD_DIM), F32)],
        scratch_shapes=[pltpu.VMEM((nu, HEAD_DIM, HEAD_DIM), F32)],
        compiler_params=_params(("arbitrary",)),
        name="wkv_prompt",
    )(*(seqs(t) for t in (r, lw, k2, v, kk, b)))
    return y.reshape(batch * seq, RWKV_WIDTH), s.reshape(batch, N_HEADS, HEAD_DIM, HEAD_DIM)


def _wkv_step_kernel(s_ref, lw_ref, kk_ref, b_ref, k_ref, r_ref, v_ref, y_ref, so_ref):
    s = s_ref[...]
    sa = jnp.sum(s * (-kk_ref[...]), axis=-1, keepdims=True)
    s = s * jnp.exp(lw_ref[...]) + sa * b_ref[...] + v_ref[...] * k_ref[...]
    so_ref[...] = s
    y_ref[...] = jnp.sum(s * r_ref[...], axis=-1, keepdims=True)


def _wkv_sample(state, lw, kk, b, k2, r, v, bb):
    nb = state.shape[0]
    rowv = lambda t: t.reshape(nb, N_HEADS, 1, HEAD_DIM)
    colv = lambda t: t.reshape(nb, N_HEADS, HEAD_DIM, 1)
    idx = lambda i: (i, 0, 0, 0)
    s_spec = pl.BlockSpec((bb, N_HEADS, HEAD_DIM, HEAD_DIM), idx)
    r_spec = pl.BlockSpec((bb, N_HEADS, 1, HEAD_DIM), idx)
    c_spec = pl.BlockSpec((bb, N_HEADS, HEAD_DIM, 1), idx)
    y, s_new = pl.pallas_call(
        _wkv_step_kernel,
        grid=(nb // bb,),
        in_specs=[s_spec] + [r_spec] * 5 + [c_spec],
        out_specs=[c_spec, s_spec],
        out_shape=[jax.ShapeDtypeStruct((nb, N_HEADS, HEAD_DIM, 1), F32),
                   jax.ShapeDtypeStruct(state.shape, F32)],
        compiler_params=_params(("arbitrary",)),
        name="wkv_sample",
    )(state, rowv(lw), rowv(kk), rowv(b), rowv(k2), rowv(r), colv(v))
    return y.reshape(nb, RWKV_WIDTH), s_new


def _bias_tables():
    i = np.arange(Q_BLOCK)[:, None]
    c = np.arange(2 * Q_BLOCK)[None, :]
    n = i + Q_BLOCK - c
    ok = (n >= 0) & (n <= Q_BLOCK)
    prompt = np.stack([np.where(ok, _t5_bucket(d * np.clip(n, 0, Q_BLOCK)), -1) for d in DILATIONS])
    dist = WIN - np.arange(WIN)
    sample = np.stack([np.where((dist % d == 0) & (dist // d <= Q_BLOCK), _t5_bucket(dist), -1)
                       for d in DILATIONS])
    sample = np.broadcast_to(sample[:, None, :], (len(DILATIONS), N_HEADS, WIN))
    return prompt.astype(np.int32), sample.astype(np.int32)


def _bias_kernel(rb_ref, rbc_ref, bp_ref, bs_ref, op_ref, os_ref):
    bp = bp_ref[0]
    bs = bs_ref[0]
    acc_s = jnp.full(bs.shape, NEG, F32)
    for bk in range(N_BUCKETS):
        acc_s = jnp.where(bs == bk, rbc_ref[bk], acc_s)
    os_ref[0] = acc_s
    for h in range(N_HEADS):
        acc = jnp.full(bp.shape, NEG, F32)
        for bk in range(N_BUCKETS):
            acc = jnp.where(bp == bk, rb_ref[bk, h], acc)
        op_ref[0, h] = acc * LOG2E


def _bias_build(rel_bias):
    bp, bs = _bias_tables()
    rbc = rel_bias.reshape(N_BUCKETS, N_HEADS, 1)
    nb = len(DILATIONS)
    return pl.pallas_call(
        _bias_kernel,
        grid=(nb,),
        in_specs=[pl.BlockSpec(memory_space=pltpu.SMEM),
                  pl.BlockSpec((N_BUCKETS, N_HEADS, 1), lambda i: (0, 0, 0)),
                  pl.BlockSpec((1, Q_BLOCK, 2 * Q_BLOCK), lambda i: (i, 0, 0)),
                  pl.BlockSpec((1, N_HEADS, WIN), lambda i: (i, 0, 0))],
        out_specs=[pl.BlockSpec((1, N_HEADS, Q_BLOCK, 2 * Q_BLOCK), lambda i: (i, 0, 0, 0)),
                   pl.BlockSpec((1, N_HEADS, WIN), lambda i: (i, 0, 0))],
        out_shape=[jax.ShapeDtypeStruct((nb, N_HEADS, Q_BLOCK, 2 * Q_BLOCK), F32),
                   jax.ShapeDtypeStruct((nb, N_HEADS, WIN), F32)],
        compiler_params=_params(("arbitrary",)),
        name="bias_build",
    )(rel_bias, rbc, jnp.asarray(bp), jnp.asarray(bs)), rbc[0]


ATTN_GROUP = 4


def _attn_units():
    units = []
    for br, d in enumerate(DILATIONS):
        nblk = SUPER // (d * Q_BLOCK)
        for r in range(d):
            rows = lambda j, r=r, d=d: (pl.ds(r + d * Q_BLOCK * j, Q_BLOCK, stride=d) if d > 1
                                        else pl.ds(Q_BLOCK * j, Q_BLOCK))
            for blk in range(nblk):
                units.append((br, rows(blk), rows(blk - 1 if blk else nblk - 1), blk == 0))
    return units


def _attn_kernel(q_ref, kc_ref, kp_ref, vc_ref, vp_ref, bias_ref, o_ref, acc_ref, m_ref):
    has_prev = pl.program_id(2) > 0
    is_a = lax.broadcasted_iota(jnp.int32, (Q_BLOCK, 128), 1) < HEAD_DIM
    is_a2 = lax.broadcasted_iota(jnp.int32, (2 * Q_BLOCK, 128), 1) < HEAD_DIM
    live = jnp.logical_or(has_prev,
                          lax.broadcasted_iota(jnp.int32, (2 * Q_BLOCK, 2 * Q_BLOCK), 1) >= Q_BLOCK)
    units = _attn_units()
    for g0 in range(0, len(units), ATTN_GROUP):
        group = units[g0:g0 + ATTN_GROUP]
        ops = []
        for br, cur, prv, from_prev in group:
            qs = q_ref[cur, :] * (ATT_SCALE * LOG2E)
            q2 =jnp.concatenate([jnp.where(is_a, qs, 0.0), jnp.where(is_a, 0.0, qs)], axis=0)
            k2 = jnp.concatenate([(kp_ref if from_prev else kc_ref)[prv, :], kc_ref[cur, :]], axis=0)
            v2 = jnp.concatenate([(vp_ref if from_prev else vc_ref)[prv, :], vc_ref[cur, :]], axis=0)
            ops.append((br, cur, from_prev, q2.astype(BF16), k2.astype(BF16),
                        jnp.where(is_a2, v2, 1.0).astype(BF16), jnp.where(is_a2, 1.0, v2).astype(BF16)))
        logits = []
        for br, cur, from_prev, q2, k2, va, vb in ops:
            lg = lax.dot_general(q2, k2, NT, preferred_element_type=F32) + bias_ref[br]
            logits.append(jnp.where(live, lg, NEG) if from_prev else lg)
        ms = [jnp.max(lg, axis=-1, keepdims=True) for lg in logits]
        ps = [jnp.exp2(lg - m).astype(BF16) for lg, m in zip(logits, ms)]
        for (br, cur, _, _, _, va, vb), p, m in zip(ops, ps, ms):
            acc_ref[br, 0, cur, :] = jnp.dot(p[:Q_BLOCK], va, preferred_element_type=F32)
            acc_ref[br, 1, cur, :] = jnp.dot(p[Q_BLOCK:], vb, preferred_element_type=F32)
            m_ref[br, 0, cur, :] = jnp.broadcast_to(m[:Q_BLOCK], (Q_BLOCK, 128))
            m_ref[br, 1, cur, :] = jnp.broadcast_to(m[Q_BLOCK:], (Q_BLOCK, 128))

    def merge(i, carry):
        rs = pl.ds(pl.multiple_of(i * Q_BLOCK, Q_BLOCK), Q_BLOCK)
        outs = []
        for hd in range(2):
            ms = [m_ref[br, hd, rs, :] for br in range(len(DILATIONS))]
            m = functools.reduce(jnp.maximum, ms)
            tot = 0.0
            for br in range(len(DILATIONS)):
                tot = tot + jnp.exp2(ms[br] - m) * acc_ref[br, hd, rs, :]
            outs.append(tot / pltpu.roll(tot, HEAD_DIM, 1))
        o_ref[rs, :] = jnp.where(is_a, outs[0], outs[1])
        return carry

    lax.fori_loop(0, SUPER // Q_BLOCK, merge, 0)


def _attn_prompt(qkv, bias_p, batch, seq):
    ns = seq // SUPER
    npair = ATT_WIDTH // 128
    blk = lambda col0, prev: pl.BlockSpec(
        (SUPER, 128),
        (lambda b, hp, sb: (b * ns + jnp.maximum(sb - 1, 0), col0 + hp)) if prev
        else (lambda b, hp, sb: (b * ns + sb, col0 + hp)))
    nb = len(DILATIONS)
    return pl.pallas_call(
        _attn_kernel,
        grid=(batch, npair, ns),
        in_specs=[blk(0, False), blk(npair, False), blk(npair, True), blk(2 * npair, False),
                  blk(2 * npair, True),
                  pl.BlockSpec((nb, None, 2 * Q_BLOCK, 2 * Q_BLOCK), lambda b, hp, sb: (0, hp, 0, 0))],
        out_specs=pl.BlockSpec((SUPER, 128), lambda b, hp, sb: (b * ns + sb, hp)),
        out_shape=jax.ShapeDtypeStruct((batch * seq, ATT_WIDTH), F32),
        scratch_shapes=[pltpu.VMEM((nb, 2, SUPER, 128), F32)] * 2,
        compiler_params=_params(("arbitrary", "arbitrary", "arbitrary")),
        name="attn_prompt",
    )(qkv, qkv, qkv, qkv, qkv, bias_p.reshape(nb, npair, 2 * Q_BLOCK, 2 * Q_BLOCK))


def _sample_window_kernel(q_ref, kn_ref, qc_ref, knc_ref, vnc_ref, ck_ref, cv_ref, bias_ref, rb0_ref,
                          o_ref, ok_ref, ov_ref, wt_ref):
    lane = lax.broadcasted_iota(jnp.int32, (HEAD_DIM, WIN), 1)
    last = lane == WIN - 1
    l_new = jnp.sum(kn_ref[0] * (q_ref[0] * ATT_SCALE), axis=-1, keepdims=True) + rb0_ref[...]
    rows = []
    for h in range(N_HEADS):
        kh = ck_ref[0, h]
        rows.append(jnp.sum(kh * (qc_ref[0, h] * ATT_SCALE), axis=0, keepdims=True))
        ok_ref[0, h] = jnp.where(last, knc_ref[0, h], pltpu.roll(kh, WIN - 1, 1))
    logits = jnp.concatenate(rows, axis=0)
    parts = []
    for br in range(len(DILATIONS)):
        lg = logits + bias_ref[br]
        m = jnp.maximum(jnp.max(lg, axis=-1, keepdims=True), l_new)
        p = jnp.exp(lg - m)
        p_new = jnp.exp(l_new - m)
        parts.append((p, p_new, m, jnp.sum(p, axis=-1, keepdims=True) + p_new))
    m = functools.reduce(jnp.maximum, [pt[2] for pt in parts])
    wt = 0.0
    w_new = 0.0
    den = 0.0
    for p, p_new, m_g, s_g in parts:
        w = jnp.exp(m_g - m)
        wt = wt + w * p
        w_new = w_new + w * p_new
        den = den + w * s_g
    wt_ref[...] = wt / den
    w_new = w_new / den
    for h in range(N_HEADS):
        vh = cv_ref[0, h]
        o_ref[0, h] = (jnp.sum(vh * wt_ref[h:h + 1, :], axis=1, keepdims=True)
                       + vnc_ref[0, h] * w_new[h:h + 1, :])
        ov_ref[0, h] = jnp.where(last, vnc_ref[0, h], pltpu.roll(vh, WIN - 1, 1))


def _sample_window(q, k_new, v_new, cache_k, cache_v, bias_s, rb0):
    nb = q.shape[0]
    rows = lambda t: t.reshape(nb, N_HEADS, HEAD_DIM)
    cols = lambda t: t.reshape(nb, N_HEADS, HEAD_DIM, 1)
    row_spec = pl.BlockSpec((1, N_HEADS, HEAD_DIM), lambda i: (i, 0, 0))
    col_spec = pl.BlockSpec((1, N_HEADS, HEAD_DIM, 1), lambda i: (i, 0, 0, 0))
    win_spec = pl.BlockSpec((1, N_HEADS, HEAD_DIM, WIN), lambda i: (i, 0, 0, 0))
    fixed = lambda a: pl.BlockSpec(a.shape, lambda i: (0,) * a.ndim)
    out, win_k, win_v = pl.pallas_call(
        _sample_window_kernel,
        grid=(nb,),
        in_specs=[row_spec, row_spec, col_spec, col_spec, col_spec, win_spec, win_spec,
                  fixed(bias_s), fixed(rb0)],
        out_specs=[col_spec, win_spec, win_spec],
        out_shape=[jax.ShapeDtypeStruct((nb, N_HEADS, HEAD_DIM, 1), F32),
                   jax.ShapeDtypeStruct(cache_k.shape, F32), jax.ShapeDtypeStruct(cache_v.shape, F32)],
        scratch_shapes=[pltpu.VMEM((N_HEADS, WIN), F32)],
        compiler_params=_params(("arbitrary",)),
        name="sample_window",
    )(rows(q), rows(k_new), cols(q), cols(k_new), cols(v_new), cache_k, cache_v, bias_s, rb0)
    return out.reshape(nb, ATT_WIDTH), win_k, win_v


def _outproj_kernel(x_ref, attn_ref, y_ref, gate_ref, bonus_ref, lnw_ref, lnb_ref, wo_ref, gffn_ref,
                    ones_ref, h_ref, xn_ref):
    ones_bd = ones_ref[...]
    y = y_ref[...]
    mu = _segsum(y, ones_bd) * (1.0 / HEAD_DIM)
    dy = y - mu
    var = _segsum(dy * dy, ones_bd) * (1.0 / HEAD_DIM)
    yn = dy * lax.rsqrt(var + GN_EPS) * lnw_ref[...] + lnb_ref[...]
    rw = (yn + bonus_ref[...]) * gate_ref[...]
    mixed = (jnp.dot(attn_ref[...].astype(BF16), wo_ref[0:ATT_WIDTH, :], preferred_element_type=F32)
             + jnp.dot(rw.astype(BF16), wo_ref[ATT_WIDTH:D_MODEL, :], preferred_element_type=F32))
    h = x_ref[...] + mixed
    h_ref[...] = h
    xn_ref[...] = _rmsnorm(h, gffn_ref[...]).astype(BF16)


def _outproj(x2d, attn, y, gate, bonus, ln_w, ln_b, w_o, g_ffn, ones_bd, tm):
    m = x2d.shape[0]
    row = lambda i: (i, 0)
    fixed = lambda i: (0, 0)
    half = pl.BlockSpec((tm, RWKV_WIDTH), row)
    full = pl.BlockSpec((tm, D_MODEL), row)
    return pl.pallas_call(
        _outproj_kernel,
        grid=(m // tm,),
        in_specs=[full, half, half, half, half,
                  pl.BlockSpec((1, RWKV_WIDTH), fixed), pl.BlockSpec((1, RWKV_WIDTH), fixed),
                  pl.BlockSpec((D_MODEL, D_MODEL), fixed), pl.BlockSpec((1, D_MODEL), fixed),
                  pl.BlockSpec((RWKV_WIDTH, RWKV_WIDTH), fixed)],
        out_specs=[full, full],
        out_shape=[jax.ShapeDtypeStruct((m, D_MODEL), F32), jax.ShapeDtypeStruct((m, D_MODEL), BF16)],
        compiler_params=_params(("arbitrary",)),
        name="outproj",
    )(x2d, attn, y, gate, bonus, ln_w, ln_b, w_o, g_ffn, ones_bd)


def _ffn_tail(xn, h, gp_m1, gp_m2, gp_of, wg_ref, wu_ref, cw_ref, cb_ref, wout_ref, gfin_ref, o_ref):
    acc = h
    for c in range(D_FF // FF_CHUNK):
        cs = slice(c * FF_CHUNK, (c + 1) * FF_CHUNK)
        gp = gp_of(c)
        up = jnp.dot(xn, wu_ref[:, cs], preferred_element_type=F32)
        conv = (cb_ref[:, cs] + cw_ref[0:1, cs] * gp_m2(c, gp) + cw_ref[1:2, cs] * gp_m1(c, gp)
                + cw_ref[2:3, cs] * gp)
        act = conv * _sigmoid(conv) * up
        acc = acc + jnp.dot(act.astype(BF16), wout_ref[cs, :], preferred_element_type=F32)
    o_ref[...] = _rmsnorm(acc, gfin_ref[...])


def _ffn_prompt_kernel(xn_ref, h_ref, wg_ref, wu_ref, cw_ref, cb_ref, wout_ref, gfin_ref,
                       o_ref, conv_ref, carry_ref):
    @pl.when(pl.program_id(1) == 0)
    def _():
        carry_ref[...] = jnp.zeros_like(carry_ref)

    xn = xn_ref[...]
    tm = xn.shape[0]
    rowi = lax.broadcasted_iota(jnp.int32, (tm, FF_CHUNK), 0)
    gps = {}

    def gp_of(c):
        cs = slice(c * FF_CHUNK, (c + 1) * FF_CHUNK)
        gps[c] = jnp.dot(xn, wg_ref[:, cs], preferred_element_type=F32)
        return gps[c]

    def gp_m1(c, gp):
        cs = slice(c * FF_CHUNK, (c + 1) * FF_CHUNK)
        return jnp.where(rowi == 0, carry_ref[1:2, cs], pltpu.roll(gp, 1, 0))

    def gp_m2(c, gp):
        cs = slice(c * FF_CHUNK, (c + 1) * FF_CHUNK)
        return jnp.where(rowi == 0, carry_ref[0:1, cs],
                         jnp.where(rowi == 1, carry_ref[1:2, cs], pltpu.roll(gp, 2, 0)))

    _ffn_tail(xn, h_ref[...], gp_m1, gp_m2, gp_of, wg_ref, wu_ref, cw_ref, cb_ref, wout_ref, gfin_ref, o_ref)
    for c, gp in gps.items():
        cs = slice(c * FF_CHUNK, (c + 1) * FF_CHUNK)
        carry_ref[:, cs] = gp[tm - 2:tm, :]
        conv_ref[0, :, cs] = gp[tm - 2:tm, :]


def _ffn_sample_kernel(xn_ref, h_ref, p1_ref, p2_ref, wg_ref, wu_ref, cw_ref, cb_ref, wout_ref, gfin_ref,
                       o_ref, gp_ref):
    xn = xn_ref[...]

    def gp_of(c):
        cs = slice(c * FF_CHUNK, (c + 1) * FF_CHUNK)
        gp = jnp.dot(xn, wg_ref[:, cs], preferred_element_type=F32)
        gp_ref[:, cs] = gp
        return gp

    gp_m1 = lambda c, gp: p1_ref[:, c * FF_CHUNK:(c + 1) * FF_CHUNK]
    gp_m2 = lambda c, gp: p2_ref[:, c * FF_CHUNK:(c + 1) * FF_CHUNK]
    _ffn_tail(xn, h_ref[...], gp_m1, gp_m2, gp_of, wg_ref, wu_ref, cw_ref, cb_ref, wout_ref, gfin_ref, o_ref)


def _ffn_weight_specs(fixed):
    once = pl.Buffered(1)
    return [pl.BlockSpec((D_MODEL, D_FF), fixed, pipeline_mode=once),
            pl.BlockSpec((D_MODEL, D_FF), fixed, pipeline_mode=once),
            pl.BlockSpec((CONV_W, D_FF), fixed), pl.BlockSpec((1, D_FF), fixed),
            pl.BlockSpec((D_FF, D_MODEL), fixed, pipeline_mode=once), pl.BlockSpec((1, D_MODEL), fixed)]


def _ffn_prompt(xn, h, fw, batch, seq, tm):
    nt = seq // tm
    row = lambda b, j: (b * nt + j, 0)
    fixed = lambda b, j: (0, 0)
    full = pl.BlockSpec((tm, D_MODEL), row)
    return pl.pallas_call(
        _ffn_prompt_kernel,
        grid=(batch, nt),
        in_specs=[full, full] + _ffn_weight_specs(fixed),
        out_specs=[full, pl.BlockSpec((1, CONV_W - 1, D_FF), lambda b, j: (b, 0, 0))],
        out_shape=[jax.ShapeDtypeStruct((batch * seq, D_MODEL), F32),
                   jax.ShapeDtypeStruct((batch, CONV_W - 1, D_FF), F32)],
        scratch_shapes=[pltpu.VMEM((CONV_W - 1, D_FF), F32)],
        compiler_params=_params(("arbitrary", "arbitrary")),
        name="ffn_prompt",
    )(xn, h, *fw)


def _ffn_sample(xn, h, prev1, prev2, fw):
    m = xn.shape[0]
    fixed = lambda i: (0, 0)
    full = pl.BlockSpec((m, D_MODEL), fixed)
    ffs = pl.BlockSpec((m, D_FF), fixed)
    return pl.pallas_call(
        _ffn_sample_kernel,
        grid=(1,),
        in_specs=[full, full, ffs, ffs] + _ffn_weight_specs(fixed),
        out_specs=[full, ffs],
        out_shape=[jax.ShapeDtypeStruct((m, D_MODEL), F32), jax.ShapeDtypeStruct((m, D_FF), F32)],
        compiler_params=_params(("arbitrary",)),
        name="ffn_sample",
    )(xn, h, prev1, prev2, *fw)


def _layer_weights(g_mix, w_in, tok_mu, w0, w_decay_up, a0, w_iclr_up, w_gate_up, k_k, k_a, r_k,
                   ln_x_w, ln_x_b, w_o, g_ffn, w_ffn_in, conv_w, conv_b, w_ffn_out, g_final):
    vec = lambda t: t.reshape(1, -1)
    pad = LORA_PAD - LORA_W
    w_cat = jnp.pad(w_in, ((0, 0), (0, pad))).astype(BF16)
    mu_p = vec(tok_mu[:RKV_W])
    mu_l = vec(jnp.pad(tok_mu[RKV_W:], (0, pad)))
    lora_up = jnp.zeros((LORA_PAD, RKV_W), F32)
    lora_up = lora_up.at[0:LORA_DECAY, 0:RWKV_WIDTH].set(w_decay_up)
    lora_up = lora_up.at[LORA_DECAY:LORA_DECAY + LORA_ICLR, RWKV_WIDTH:2 * RWKV_WIDTH].set(w_iclr_up)
    lora_up = lora_up.at[LORA_DECAY + LORA_ICLR:LORA_W, 2 * RWKV_WIDTH:].set(w_gate_up)
    seg = np.arange(RWKV_WIDTH) // HEAD_DIM
    ones_bd = jnp.asarray(seg[:, None] == seg[None, :], BF16)
    prep = (mu_p, mu_l, lora_up.astype(BF16), vec(w0), vec(a0), vec(k_k), vec(k_a), vec(r_k), ones_bd)
    ffn = (w_ffn_in[:, :D_FF].astype(BF16), w_ffn_in[:, D_FF:].astype(BF16), conv_w, vec(conv_b),
           w_ffn_out.astype(BF16), vec(g_final))
    return dict(g_mix=vec(g_mix), w_cat=w_cat, prep=prep, ones_bd=ones_bd, ln_w=vec(ln_x_w), ln_b=vec(ln_x_b),
                w_o=w_o.astype(BF16), g_ffn=vec(g_ffn), ffn=ffn)


def kernel(x_prompt, x_sample, cache_win_k, cache_win_v, state_shift, state_wkv, state_ffn_conv, g_mix, w_in, rel_bias, tok_mu, w0, w_decay_up, a0, w_iclr_up, w_gate_up, k_k, k_a, r_k, ln_x_w, ln_x_b, w_o, g_ffn, w_ffn_in, conv_w, conv_b, w_ffn_out, g_final):
    batch, seq, _ = x_prompt.shape
    nb = x_sample.shape[0]
    lw = _layer_weights(g_mix[0], w_in[0], tok_mu[0], w0[0], w_decay_up[0], a0[0], w_iclr_up[0],
                        w_gate_up[0], k_k[0], k_a[0], r_k[0].reshape(-1), ln_x_w[0], ln_x_b[0], w_o[0],
                        g_ffn[0], w_ffn_in[0], conv_w[0], conv_b[0], w_ffn_out[0], g_final)
    (bias_p, bias_s), rb0 = _bias_build(rel_bias)

    xp = x_prompt.reshape(batch * seq, D_MODEL)
    qkv, rkv, lora = _inproj(xp, lw['g_mix'], lw['w_cat'], 512)
    r, lgw, k2, v, kk, b, gate, bonus = _prep_prompt(rkv, lora, lw['prep'], batch, seq, 512)
    y, wkv_p = _wkv_prompt(r, lgw, k2, v, kk, b, batch, seq)
    attn = _attn_prompt(qkv, bias_p, batch, seq)
    h1, xn2 = _outproj(xp, attn, y, gate, bonus, lw['ln_w'], lw['ln_b'], lw['w_o'], lw['g_ffn'],
                       lw['ones_bd'], 512)
    y_p, conv_p = _ffn_prompt(xn2, h1, lw['ffn'], batch, seq, 512)
    qkv3 = qkv.reshape(batch, seq, QKV_W)
    win_k_p = qkv3[:, seq - WIN:, ATT_WIDTH:2 * ATT_WIDTH].reshape(1, batch, WIN, N_HEADS, HEAD_DIM)
    win_v_p = qkv3[:, seq - WIN:, 2 * ATT_WIDTH:].reshape(1, batch, WIN, N_HEADS, HEAD_DIM)
    shift_p = jnp.concatenate([rkv.reshape(batch, seq, RKV_W)[:, -1],
                               lora.reshape(batch, seq, LORA_PAD)[:, -1, :LORA_W]], axis=-1)[None]

    xs = x_sample.reshape(nb, D_MODEL)
    qkv_s, rkv_s, lora_s = _inproj(xs, lw['g_mix'], lw['w_cat'], nb)
    sh = state_shift[0]
    r, lgw, k2, v, kk, b, gate, bonus = _prep_sample(
        rkv_s, sh[:, :RKV_W], lora_s, jnp.pad(sh[:, RKV_W:], ((0, 0), (0, LORA_PAD - LORA_W))), lw['prep'])
    y_s, wkv_s = _wkv_sample(state_wkv[0], lgw, kk, b, k2, r, v, 8)
    q_s = qkv_s[:, :ATT_WIDTH]
    k_s = qkv_s[:, ATT_WIDTH:2 * ATT_WIDTH]
    v_s = qkv_s[:, 2 * ATT_WIDTH:]
    ck = jnp.transpose(cache_win_k[0], (0, 2, 3, 1))
    cv = jnp.transpose(cache_win_v[0], (0, 2, 3, 1))
    attn_s, win_k_s, win_v_s = _sample_window(q_s, k_s, v_s, ck, cv, bias_s, rb0)
    win_k_s = jnp.transpose(win_k_s, (0, 3, 1, 2))
    win_v_s = jnp.transpose(win_v_s, (0, 3, 1, 2))
    h1_s, xn2_s = _outproj(xs, attn_s, y_s, gate, bonus, lw['ln_w'], lw['ln_b'], lw['w_o'], lw['g_ffn'],
                           lw['ones_bd'], nb)
    conv_state = state_ffn_conv[0]
    y_smp, gp_s = _ffn_sample(xn2_s, h1_s, conv_state[:, 1], conv_state[:, 0], lw['ffn'])
    conv_s = jnp.stack([conv_state[:, 1], gp_s], axis=1)
    shift_s = jnp.concatenate([rkv_s, lora_s[:, :LORA_W]], axis=-1)[None]

    return (y_p.reshape(batch, seq, D_MODEL), y_smp.reshape(nb, 1, D_MODEL),
            win_k_p, win_v_p, shift_p, wkv_p[None], conv_p[None],
            win_k_s[None], win_v_s[None], shift_s, wkv_s[None], conv_s[None])
```

```python
import functools

import numpy as np
import jax
import jax.numpy as jnp
from jax import lax
from jax.experimental import pallas as pl
from jax.experimental.pallas import tpu as pltpu

F32 = jnp.float32
BF16 = jnp.bfloat16

D_MODEL = 1024
HEAD_DIM = 64
ATT_WIDTH = 512
RWKV_WIDTH = 512
N_HEADS = 8
QKV_W = 3 * ATT_WIDTH
RKV_W = 3 * RWKV_WIDTH
LORA_DECAY, LORA_ICLR, LORA_GATE = 32, 32, 96
LORA_W = LORA_DECAY + LORA_ICLR + LORA_GATE
LORA_PAD = 256
RWKV_PROJ = RKV_W + LORA_W
PROJ_PAD = QKV_W + RKV_W + LORA_PAD
D_FF = 2816
FF_CHUNK = 1408
CONV_W = 3
WIN = 2048
DILATIONS = (1, 4, 16)
Q_BLOCK = 128
SUPER = 2048
N_BUCKETS = 32
MAX_DISTANCE = 2048
NORM_EPS = 1e-6
GN_EPS = 64e-5
NEG = -1e30
ATT_SCALE = HEAD_DIM ** -0.5
LOG2E = float(np.log2(np.e))
CHUNK = 64
VMEM_LIMIT = 56 * 1024 * 1024

NT = (((1,), (1,)), ((), ()))
TN = (((0,), (0,)), ((), ()))


def _dot(a, b):
    return jnp.dot(a.astype(BF16), b.astype(BF16), preferred_element_type=F32)


def _dot_nt(a, b):
    return lax.dot_general(a.astype(BF16), b.astype(BF16), NT, preferred_element_type=F32)


def _dot_tn(a, b):
    return lax.dot_general(a.astype(BF16), b.astype(BF16), TN, preferred_element_type=F32)


def _split2(x):
    hi = x.astype(BF16)
    lo = (x - hi.astype(F32)).astype(BF16)
    return hi, lo


def _split3(x):
    hi = x.astype(BF16)
    r1 = x - hi.astype(F32)
    mid = r1.astype(BF16)
    lo = (r1 - mid.astype(F32)).astype(BF16)
    return hi, mid, lo


def _segsum(x, ones_bd):
    hi, lo = _split2(x)
    return (jnp.dot(hi, ones_bd, preferred_element_type=F32)
            + jnp.dot(lo, ones_bd, preferred_element_type=F32))


def _sigmoid(x):
    return 1.0 / (1.0 + jnp.exp(-x))


def _rmsnorm(x, g):
    return x * lax.rsqrt(jnp.mean(x * x, axis=-1, keepdims=True) + NORM_EPS) * g


def _params(sem):
    return pltpu.CompilerParams(dimension_semantics=sem, vmem_limit_bytes=VMEM_LIMIT)


def _t5_bucket(dist):
    dist = np.asarray(dist, dtype=np.int64)
    exact = N_BUCKETS // 2
    scaled = np.log(np.maximum(dist, 1) / exact) / np.log(MAX_DISTANCE / exact)
    large = np.minimum(exact + (scaled * (N_BUCKETS - exact)).astype(np.int64), N_BUCKETS - 1)
    return np.where(dist < exact, dist, large).astype(np.int32)


def _inproj_kernel(x_ref, g_ref, w_ref, qkv_ref, rkv_ref, lora_ref):
    xb = _rmsnorm(x_ref[...], g_ref[...]).astype(BF16)
    qkv_ref[...] = jnp.dot(xb, w_ref[:, 0:QKV_W], preferred_element_type=F32)
    rkv_ref[...] = jnp.dot(xb, w_ref[:, QKV_W:QKV_W + RKV_W], preferred_element_type=F32)
    lora_ref[...] = jnp.dot(xb, w_ref[:, QKV_W + RKV_W:PROJ_PAD], preferred_element_type=F32)


def _inproj(x2d, g, w, tm):
    m = x2d.shape[0]
    row = lambda i: (i, 0)
    fixed = lambda i: (0, 0)
    return pl.pallas_call(
        _inproj_kernel,
        grid=(m // tm,),
        in_specs=[pl.BlockSpec((tm, D_MODEL), row),
                  pl.BlockSpec((1, D_MODEL), fixed),
                  pl.BlockSpec((D_MODEL, PROJ_PAD), fixed)],
        out_specs=[pl.BlockSpec((tm, QKV_W), row),
                   pl.BlockSpec((tm, RKV_W), row),
                   pl.BlockSpec((tm, LORA_PAD), row)],
        out_shape=[jax.ShapeDtypeStruct((m, QKV_W), F32),
                   jax.ShapeDtypeStruct((m, RKV_W), F32),
                   jax.ShapeDtypeStruct((m, LORA_PAD), F32)],
        compiler_params=_params(("arbitrary",)),
        name="inproj",
    )(x2d, g, w)


def _prep_math(p, p_prev, l, l_prev, mu_p, mu_l, wcat, w0, a0, k_k, k_a, r_k, ones_bd):
    pm = p + mu_p * (p_prev - p)
    lm = l + mu_l * (l_prev - l)
    r = pm[:, 0:RWKV_WIDTH]
    kr = pm[:, RWKV_WIDTH:2 * RWKV_WIDTH]
    vr = pm[:, 2 * RWKV_WIDTH:3 * RWKV_WIDTH]
    lane = lax.broadcasted_iota(jnp.int32, lm.shape, 1)
    feat = jnp.where(lane < LORA_DECAY, jnp.tanh(lm),
                     jnp.where(lane < LORA_DECAY + LORA_ICLR, lm, _sigmoid(lm)))
    z = jnp.dot(feat.astype(BF16), wcat, preferred_element_type=F32)
    zd = -(w0 + z[:, 0:RWKV_WIDTH])
    softplus = jnp.maximum(zd, 0.0) + jnp.log(1.0 + jnp.exp(-jnp.abs(zd)))
    lw = -jnp.exp(-softplus - 0.5)
    a = _sigmoid(a0 + z[:, RWKV_WIDTH:2 * RWKV_WIDTH])
    gate = z[:, 2 * RWKV_WIDTH:3 * RWKV_WIDTH]
    kk = kr * k_k
    kk = kk / jnp.maximum(jnp.sqrt(_segsum(kk * kk, ones_bd)), 1e-12)
    k2 = kr * (1.0 + (a - 1.0) * k_a)
    bonus = _segsum(r * k2 * r_k, ones_bd) * vr
    return r, lw, k2, vr, kk, kk * a, gate, bonus


def _prep_prompt_kernel(p_ref, l_ref, mu_p_ref, mu_l_ref, wcat_ref, w0_ref, a0_ref, kk_ref, ka_ref,
                        rk_ref, ones_ref, *refs):
    outs, (cp_ref, cl_ref) = refs[:8], refs[8:]

    @pl.when(pl.program_id(1) == 0)
    def _():
        cp_ref[...] = jnp.zeros_like(cp_ref)
        cl_ref[...] = jnp.zeros_like(cl_ref)

    p = p_ref[...]
    l = l_ref[...]
    tm = p.shape[0]
    first_p = lax.broadcasted_iota(jnp.int32, p.shape, 0) == 0
    first_l = lax.broadcasted_iota(jnp.int32, l.shape, 0) == 0
    p_prev = jnp.where(first_p, cp_ref[...], pltpu.roll(p, 1, 0))
    l_prev = jnp.where(first_l, cl_ref[...], pltpu.roll(l, 1, 0))
    cp_ref[...] = p[tm - 1:tm, :]
    cl_ref[...] = l[tm - 1:tm, :]
    res = _prep_math(p, p_prev, l, l_prev, mu_p_ref[...], mu_l_ref[...], wcat_ref[...], w0_ref[...],
                     a0_ref[...], kk_ref[...], ka_ref[...], rk_ref[...], ones_ref[...])
    for o, v in zip(outs, res):
        o[...] = v


def _prep_sample_kernel(p_ref, pp_ref, l_ref, lp_ref, mu_p_ref, mu_l_ref, wcat_ref, w0_ref, a0_ref,
                        kk_ref, ka_ref, rk_ref, ones_ref, *outs):
    res = _prep_math(p_ref[...], pp_ref[...], l_ref[...], lp_ref[...], mu_p_ref[...], mu_l_ref[...],
                     wcat_ref[...], w0_ref[...], a0_ref[...], kk_ref[...], ka_ref[...], rk_ref[...],
                     ones_ref[...])
    for o, v in zip(outs, res):
        o[...] = v


def _prep_prompt(rkv, lora, pw, batch, seq, tm):
    nt = seq // tm
    row = lambda b, j: (b * nt + j, 0)
    fixed = lambda b, j: (0, 0)
    vec = lambda n: pl.BlockSpec((1, n), fixed)
    m = rkv.shape[0]
    return pl.pallas_call(
        _prep_prompt_kernel,
        grid=(batch, nt),
        in_specs=[pl.BlockSpec((tm, RKV_W), row), pl.BlockSpec((tm, LORA_PAD), row),
                  vec(RKV_W), vec(LORA_PAD), pl.BlockSpec((LORA_PAD, RKV_W), fixed),
                  vec(RWKV_WIDTH), vec(RWKV_WIDTH), vec(RWKV_WIDTH), vec(RWKV_WIDTH), vec(RWKV_WIDTH),
                  pl.BlockSpec((RWKV_WIDTH, RWKV_WIDTH), fixed)],
        out_specs=[pl.BlockSpec((tm, RWKV_WIDTH), row)] * 8,
        out_shape=[jax.ShapeDtypeStruct((m, RWKV_WIDTH), F32)] * 8,
        scratch_shapes=[pltpu.VMEM((1, RKV_W), F32), pltpu.VMEM((1, LORA_PAD), F32)],
        compiler_params=_params(("arbitrary", "arbitrary")),
        name="rwkv_prep_prompt",
    )(rkv, lora, *pw)


def _prep_sample(rkv, rkv_prev, lora, lora_prev, pw):
    m = rkv.shape[0]
    full = lambda a: pl.BlockSpec(a.shape, lambda i: (0,) * a.ndim)
    args = (rkv, rkv_prev, lora, lora_prev) + tuple(pw)
    return pl.pallas_call(
        _prep_sample_kernel,
        grid=(1,),
        in_specs=[full(a) for a in args],
        out_specs=[pl.BlockSpec((m, RWKV_WIDTH), lambda i: (0, 0))] * 8,
        out_shape=[jax.ShapeDtypeStruct((m, RWKV_WIDTH), F32)] * 8,
        compiler_params=_params(("arbitrary",)),
        name="rwkv_prep_sample",
    )(*args)


def _cumsum_rows(x, tri):
    tri_b = tri.astype(BF16)
    out = None
    for piece in _split3(x):
        t = jnp.dot(tri_b, piece, preferred_element_type=F32)
        out = t if out is None else out + t
    return out


def _wkv_kernel(r_ref, lw_ref, k_ref, v_ref, kk_ref, b_ref, y_ref, s_ref, st_ref):
    c = pl.program_id(0)

    @pl.when(c == 0)
    def _():
        st_ref[...] = jnp.zeros_like(st_ref)

    nb = r_ref.shape[0]
    n = CHUNK
    row = lax.broadcasted_iota(jnp.int32, (n, n), 0)
    col = lax.broadcasted_iota(jnp.int32, (n, n), 1)
    eye = jnp.where(row == col, 1.0, 0.0)
    tri = jnp.where(row >= col, 1.0, 0.0)
    row2 = lax.broadcasted_iota(jnp.int32, (2 * n, 2 * n), 0)
    col2 = lax.broadcasted_iota(jnp.int32, (2 * n, 2 * n), 1) % n
    gram_mask = jnp.logical_or(row2 % n > col2, jnp.logical_and(row2 >= n, row2 % n == col2))
    zeros = jnp.zeros((n, HEAD_DIM), BF16)

    ah, bh, kh, rh, beh, keh, vh, ge = ([] for _ in range(8))
    for bi in range(nb):
        lw = lw_ref[bi]
        lg = _cumsum_rows(lw, tri)
        g_inv = jnp.exp(-lg)
        lg_end = lg[n - 1:n, :]
        g_to_end = jnp.exp(lg_end - lg)
        g_end = jnp.exp(lg_end)
        kk = kk_ref[bi]
        b = b_ref[bi]
        k = k_ref[bi]
        full = (-kk * jnp.exp(lg - lw), b * g_inv, k * g_inv, r_ref[bi] * jnp.exp(lg),
                b * g_to_end, k * g_to_end, v_ref[bi], g_end)
        for h in range(N_HEADS):
            sl = slice(h * HEAD_DIM, (h + 1) * HEAD_DIM)
            for dst, t in zip((ah, bh, kh, rh, beh, keh, vh), full[:7]):
                dst.append(t[:, sl].astype(BF16))
            ge.append(g_end[:, sl])
    units = range(nb * N_HEADS)

    stack = lambda xs, ys: [jnp.concatenate([x, y], axis=0) for x, y in zip(xs, ys)]
    gram = [jnp.where(gram_mask, _dot_nt(ar, bk), 0.0) for ar, bk in zip(stack(ah, rh), stack(bh, kh))]
    top = [g[:n].astype(BF16) for g in gram]
    bot = [g[n:].astype(BF16) for g in gram]
    lv = [_dot(top[u], jnp.concatenate([zeros, vh[u]], axis=0)) for u in units]
    tm_ = [eye + gram[u][:n, :n] for u in units]
    lp = [_dot(t[:, :n], t[:, :n]).astype(BF16) for t in top]
    span = 2
    while 2 * span < n:
        prod = [_dot(x, p) for x, p in zip(stack(lp, [t.astype(BF16) for t in tm_]), lp)]
        tm_ = [tm_[u] + prod[u][n:] for u in units]
        lp = [p[:n].astype(BF16) for p in prod]
        span *= 2
    tm_ = [(tm_[u] + _dot(tm_[u], lp[u])).astype(BF16) for u in units]
    a_hat = [_dot(tm_[u], ah[u]).astype(BF16) for u in units]
    u0 = [_dot(tm_[u], lv[u]) for u in units]
    s0 = [st_ref[u] for u in units]
    on_s0 = [_dot_nt(x, s.astype(BF16)) for x, s in zip(stack(a_hat, rh), s0)]
    uv = stack([(on_s0[u][:n] + u0[u]).astype(BF16) for u in units], vh)
    y = [on_s0[u][n:] + _dot(bot[u], uv[u]) for u in units]
    for u, be_ke in zip(units, stack(beh, keh)):
        st_ref[u] = s0[u] * ge[u] + _dot_tn(uv[u], be_ke)
    for bi in range(nb):
        y_ref[bi] = jnp.concatenate(y[bi * N_HEADS:(bi + 1) * N_HEADS], axis=1)

    @pl.when(c == pl.num_programs(0) - 1)
    def _():
        s_ref[...] = st_ref[...]


def _wkv_prompt(r, lw, k2, v, kk, b, batch, seq):
    nc = seq // CHUNK
    seqs = lambda t: t.reshape(batch, seq, RWKV_WIDTH)
    spec = pl.BlockSpec((batch, CHUNK, RWKV_WIDTH), lambda c: (0, c, 0))
    nu = batch * N_HEADS
    y, s = pl.pallas_call(
        _wkv_kernel,
        grid=(nc,),
        in_specs=[spec] * 6,
        out_specs=[spec, pl.BlockSpec((nu, HEAD_DIM, HEAD_DIM), lambda c: (0, 0, 0))],
        out_shape=[jax.ShapeDtypeStruct((batch, seq, RWKV_WIDTH), F32),
                   jax.ShapeDtypeStruct((nu, HEAD_DIM, HEAD_DIM), F32)],
        scratch_shapes=[pltpu.VMEM((nu, HEAD_DIM, HEAD_DIM), F32)],
        compiler_params=_params(("arbitrary",)),
        name="wkv_prompt",
    )(*(seqs(t) for t in (r, lw, k2, v, kk, b)))
    return y.reshape(batch * seq, RWKV_WIDTH), s.reshape(batch, N_HEADS, HEAD_DIM, HEAD_DIM)


def _wkv_step_kernel(s_ref, lw_ref, kk_ref, b_ref, k_ref, r_ref, v_ref, y_ref, so_ref):
    s = s_ref[...]
    sa = jnp.sum(s * (-kk_ref[...]), axis=-1, keepdims=True)
    s = s * jnp.exp(lw_ref[...]) + sa * b_ref[...] + v_ref[...] * k_ref[...]
    so_ref[...] = s
    y_ref[...] = jnp.sum(s * r_ref[...], axis=-1, keepdims=True)


def _wkv_sample(state, lw, kk, b, k2, r, v, bb):
    nb = state.shape[0]
    rowv = lambda t: t.reshape(nb, N_HEADS, 1, HEAD_DIM)
    colv = lambda t: t.reshape(nb, N_HEADS, HEAD_DIM, 1)
    idx = lambda i: (i, 0, 0, 0)
    s_spec = pl.BlockSpec((bb, N_HEADS, HEAD_DIM, HEAD_DIM), idx)
    r_spec = pl.BlockSpec((bb, N_HEADS, 1, HEAD_DIM), idx)
    c_spec = pl.BlockSpec((bb, N_HEADS, HEAD_DIM, 1), idx)
    y, s_new = pl.pallas_call(
        _wkv_step_kernel,
        grid=(nb // bb,),
        in_specs=[s_spec] + [r_spec] * 5 + [c_spec],
        out_specs=[c_spec, s_spec],
        out_shape=[jax.ShapeDtypeStruct((nb, N_HEADS, HEAD_DIM, 1), F32),
                   jax.ShapeDtypeStruct(state.shape, F32)],
        compiler_params=_params(("arbitrary",)),
        name="wkv_sample",
    )(state, rowv(lw), rowv(kk), rowv(b), rowv(k2), rowv(r), colv(v))
    return y.reshape(nb, RWKV_WIDTH), s_new


def _bias_tables():
    i = np.arange(Q_BLOCK)[:, None]
    c = np.arange(2 * Q_BLOCK)[None, :]
    n = i + Q_BLOCK - c
    ok = (n >= 0) & (n <= Q_BLOCK)
    prompt = np.stack([np.where(ok, _t5_bucket(d * np.clip(n, 0, Q_BLOCK)), -1) for d in DILATIONS])
    dist = WIN - np.arange(WIN)
    sample = np.stack([np.where((dist % d == 0) & (dist // d <= Q_BLOCK), _t5_bucket(dist), -1)
                       for d in DILATIONS])
    sample = np.broadcast_to(sample[:, None, :], (len(DILATIONS), N_HEADS, WIN))
    return prompt.astype(np.int32), sample.astype(np.int32)


def _bias_kernel(rb_ref, rbc_ref, bp_ref, bs_ref, op_ref, os_ref):
    bp = bp_ref[0]
    bs = bs_ref[0]
    acc_s = jnp.full(bs.shape, NEG, F32)
    for bk in range(N_BUCKETS):
        acc_s = jnp.where(bs == bk, rbc_ref[bk], acc_s)
    os_ref[0] = acc_s
    for h in range(N_HEADS):
        acc = jnp.full(bp.shape, NEG, F32)
        for bk in range(N_BUCKETS):
            acc = jnp.where(bp == bk, rb_ref[bk, h], acc)
        op_ref[0, h] = acc * LOG2E


def _bias_build(rel_bias):
    bp, bs = _bias_tables()
    rbc = rel_bias.reshape(N_BUCKETS, N_HEADS, 1)
    nb = len(DILATIONS)
    return pl.pallas_call(
        _bias_kernel,
        grid=(nb,),
        in_specs=[pl.BlockSpec(memory_space=pltpu.SMEM),
                  pl.BlockSpec((N_BUCKETS, N_HEADS, 1), lambda i: (0, 0, 0)),
                  pl.BlockSpec((1, Q_BLOCK, 2 * Q_BLOCK), lambda i: (i, 0, 0)),
                  pl.BlockSpec((1, N_HEADS, WIN), lambda i: (i, 0, 0))],
        out_specs=[pl.BlockSpec((1, N_HEADS, Q_BLOCK, 2 * Q_BLOCK), lambda i: (i, 0, 0, 0)),
                   pl.BlockSpec((1, N_HEADS, WIN), lambda i: (i, 0, 0))],
        out_shape=[jax.ShapeDtypeStruct((nb, N_HEADS, Q_BLOCK, 2 * Q_BLOCK), F32),
                   jax.ShapeDtypeStruct((nb, N_HEADS, WIN), F32)],
        compiler_params=_params(("arbitrary",)),
        name="bias_build",
    )(rel_bias, rbc, jnp.asarray(bp), jnp.asarray(bs)), rbc[0]


ATTN_GROUP = 4


def _attn_units():
    units = []
    for br, d in enumerate(DILATIONS):
        nblk = SUPER // (d * Q_BLOCK)
        for r in range(d):
            rows = lambda j, r=r, d=d: (pl.ds(r + d * Q_BLOCK * j, Q_BLOCK, stride=d) if d > 1
                                        else pl.ds(Q_BLOCK * j, Q_BLOCK))
            for blk in range(nblk):
                units.append((br, rows(blk), rows(blk - 1 if blk else nblk - 1), blk == 0))
    return units


def _attn_kernel(q_ref, kc_ref, kp_ref, vc_ref, vp_ref, bias_ref, o_ref, acc_ref, m_ref):
    has_prev = pl.program_id(2) > 0
    is_a = lax.broadcasted_iota(jnp.int32, (Q_BLOCK, 128), 1) < HEAD_DIM
    is_a2 = lax.broadcasted_iota(jnp.int32, (2 * Q_BLOCK, 128), 1) < HEAD_DIM
    live = jnp.logical_or(has_prev,
                          lax.broadcasted_iota(jnp.int32, (2 * Q_BLOCK, 2 * Q_BLOCK), 1) >= Q_BLOCK)
    units = _attn_units()
    for g0 in range(0, len(units), ATTN_GROUP):
        group = units[g0:g0 + ATTN_GROUP]
        ops = []
        for br, cur, prv, from_prev in group:
            qs = q_ref[cur, :] * (ATT_SCALE * LOG2E)
            q2 =jnp.concatenate([jnp.where(is_a, qs, 0.0), jnp.where(is_a, 0.0, qs)], axis=0)
            k2 = jnp.concatenate([(kp_ref if from_prev else kc_ref)[prv, :], kc_ref[cur, :]], axis=0)
            v2 = jnp.concatenate([(vp_ref if from_prev else vc_ref)[prv, :], vc_ref[cur, :]], axis=0)
            ops.append((br, cur, from_prev, q2.astype(BF16), k2.astype(BF16),
                        jnp.where(is_a2, v2, 1.0).astype(BF16), jnp.where(is_a2, 1.0, v2).astype(BF16)))
        logits = []
        for br, cur, from_prev, q2, k2, va, vb in ops:
            lg = lax.dot_general(q2, k2, NT, preferred_element_type=F32) + bias_ref[br]
            logits.append(jnp.where(live, lg, NEG) if from_prev else lg)
        ms = [jnp.max(lg, axis=-1, keepdims=True) for lg in logits]
        ps = [jnp.exp2(lg - m).astype(BF16) for lg, m in zip(logits, ms)]
        for (br, cur, _, _, _, va, vb), p, m in zip(ops, ps, ms):
            acc_ref[br, 0, cur, :] = jnp.dot(p[:Q_BLOCK], va, preferred_element_type=F32)
            acc_ref[br, 1, cur, :] = jnp.dot(p[Q_BLOCK:], vb, preferred_element_type=F32)
            m_ref[br, 0, cur, :] = jnp.broadcast_to(m[:Q_BLOCK], (Q_BLOCK, 128))
            m_ref[br, 1, cur, :] = jnp.broadcast_to(m[Q_BLOCK:], (Q_BLOCK, 128))

    def merge(i, carry):
        rs = pl.ds(pl.multiple_of(i * Q_BLOCK, Q_BLOCK), Q_BLOCK)
        outs = []
        for hd in range(2):
            ms = [m_ref[br, hd, rs, :] for br in range(len(DILATIONS))]
            m = functools.reduce(jnp.maximum, ms)
            tot = 0.0
            for br in range(len(DILATIONS)):
                tot = tot + jnp.exp2(ms[br] - m) * acc_ref[br, hd, rs, :]
            outs.append(tot / pltpu.roll(tot, HEAD_DIM, 1))
        o_ref[rs, :] = jnp.where(is_a, outs[0], outs[1])
        return carry

    lax.fori_loop(0, SUPER // Q_BLOCK, merge, 0)


def _attn_prompt(qkv, bias_p, batch, seq):
    ns = seq // SUPER
    npair = ATT_WIDTH // 128
    blk = lambda col0, prev: pl.BlockSpec(
        (SUPER, 128),
        (lambda b, hp, sb: (b * ns + jnp.maximum(sb - 1, 0), col0 + hp)) if prev
        else (lambda b, hp, sb: (b * ns + sb, col0 + hp)))
    nb = len(DILATIONS)
    return pl.pallas_call(
        _attn_kernel,
        grid=(batch, npair, ns),
        in_specs=[blk(0, False), blk(npair, False), blk(npair, True), blk(2 * npair, False),
                  blk(2 * npair, True),
                  pl.BlockSpec((nb, None, 2 * Q_BLOCK, 2 * Q_BLOCK), lambda b, hp, sb: (0, hp, 0, 0))],
        out_specs=pl.BlockSpec((SUPER, 128), lambda b, hp, sb: (b * ns + sb, hp)),
        out_shape=jax.ShapeDtypeStruct((batch * seq, ATT_WIDTH), F32),
        scratch_shapes=[pltpu.VMEM((nb, 2, SUPER, 128), F32)] * 2,
        compiler_params=_params(("arbitrary", "arbitrary", "arbitrary")),
        name="attn_prompt",
    )(qkv, qkv, qkv, qkv, qkv, bias_p.reshape(nb, npair, 2 * Q_BLOCK, 2 * Q_BLOCK))


WINDOW_HEADS = 4


def _window_body(q_ref, kn_ref, qc_ref, knc_ref, vnc_ref, ck_ref, cv_ref, bias_ref, rb0_ref,
                 o_ref, ok_ref, ov_ref, wt_ref):
    nh = WINDOW_HEADS
    lane = lax.broadcasted_iota(jnp.int32, (HEAD_DIM, WIN), 1)
    last = lane == WIN - 1
    l_new = jnp.sum(kn_ref[...] * (q_ref[...] * ATT_SCALE), axis=-1, keepdims=True) + rb0_ref[...]
    rows = []
    for h in range(nh):
        kh = ck_ref[h]
        rows.append(jnp.sum(kh * (qc_ref[h] * ATT_SCALE), axis=0, keepdims=True))
        ok_ref[h] = jnp.where(last, knc_ref[h], pltpu.roll(kh, WIN - 1, 1))
    logits = jnp.concatenate(rows, axis=0)
    parts = []
    for br in range(len(DILATIONS)):
        lg = logits + bias_ref[br]
        m = jnp.maximum(jnp.max(lg, axis=-1, keepdims=True), l_new)
        p = jnp.exp(lg - m)
        p_new = jnp.exp(l_new - m)
        parts.append((p, p_new, m, jnp.sum(p, axis=-1, keepdims=True) + p_new))
    m = functools.reduce(jnp.maximum, [pt[2] for pt in parts])
    wt = 0.0
    w_new = 0.0
    den = 0.0
    for p, p_new, m_g, s_g in parts:
        w = jnp.exp(m_g - m)
        wt = wt + w * p
        w_new = w_new + w * p_new
        den = den + w * s_g
    wt_ref[0:nh, :] = wt / den
    w_new = w_new / den
    for h in range(nh):
        vh = cv_ref[h]
        o_ref[h] = (jnp.sum(vh * wt_ref[h:h + 1, :], axis=1, keepdims=True)
                    + vnc_ref[h] * w_new[h:h + 1, :])
        ov_ref[h] = jnp.where(last, vnc_ref[h], pltpu.roll(vh, WIN - 1, 1))


def _window_operands(q, k_new, v_new, cache_k, cache_v, bias_s, rb0, step_of):
    nb = q.shape[0]
    nh = WINDOW_HEADS
    groups = N_HEADS // nh
    seq_grp = lambda *g: (step_of(*g) // groups, step_of(*g) % groups)
    rows = lambda t: t.reshape(nb, groups, nh, HEAD_DIM)
    cols = lambda t: t.reshape(nb, groups, nh, HEAD_DIM, 1)
    wins = lambda t: t.reshape(nb, groups, nh, HEAD_DIM, WIN)
    row_spec = pl.BlockSpec((None, None, nh, HEAD_DIM), lambda *g: seq_grp(*g) + (0, 0))
    col_spec = pl.BlockSpec((None, None, nh, HEAD_DIM, 1), lambda *g: seq_grp(*g) + (0, 0, 0))
    win_spec = pl.BlockSpec((None, None, nh, HEAD_DIM, WIN), lambda *g: seq_grp(*g) + (0, 0, 0))
    nbr = len(DILATIONS)
    bias_spec = pl.BlockSpec((nbr, None, nh, WIN), lambda *g: (0, seq_grp(*g)[1], 0, 0))
    rb0_spec = pl.BlockSpec((None, nh, 1), lambda *g: (seq_grp(*g)[1], 0, 0))
    args = (rows(q), rows(k_new), cols(q), cols(k_new), cols(v_new), wins(cache_k), wins(cache_v),
            bias_s.reshape(nbr, groups, nh, WIN), rb0.reshape(groups, nh, 1))
    in_specs = [row_spec, row_spec, col_spec, col_spec, col_spec, win_spec, win_spec, bias_spec, rb0_spec]
    out_specs = [col_spec, win_spec, win_spec]
    out_shape = [jax.ShapeDtypeStruct((nb, groups, nh, HEAD_DIM, 1), F32),
                 jax.ShapeDtypeStruct((nb, groups, nh, HEAD_DIM, WIN), F32),
                 jax.ShapeDtypeStruct((nb, groups, nh, HEAD_DIM, WIN), F32)]
    return args, in_specs, out_specs, out_shape


def _outproj_kernel(x_ref, attn_ref, y_ref, gate_ref, bonus_ref, lnw_ref, lnb_ref, wo_ref, gffn_ref,
                    ones_ref, h_ref, xn_ref):
    ones_bd = ones_ref[...]
    y = y_ref[...]
    mu = _segsum(y, ones_bd) * (1.0 / HEAD_DIM)
    dy = y - mu
    var = _segsum(dy * dy, ones_bd) * (1.0 / HEAD_DIM)
    yn = dy * lax.rsqrt(var + GN_EPS) * lnw_ref[...] + lnb_ref[...]
    rw = (yn + bonus_ref[...]) * gate_ref[...]
    mixed = (jnp.dot(attn_ref[...].astype(BF16), wo_ref[0:ATT_WIDTH, :], preferred_element_type=F32)
             + jnp.dot(rw.astype(BF16), wo_ref[ATT_WIDTH:D_MODEL, :], preferred_element_type=F32))
    h = x_ref[...] + mixed
    h_ref[...] = h
    xn_ref[...] = _rmsnorm(h, gffn_ref[...]).astype(BF16)


def _outproj(x2d, attn, y, gate, bonus, ln_w, ln_b, w_o, g_ffn, ones_bd, tm):
    m = x2d.shape[0]
    row = lambda i: (i, 0)
    fixed = lambda i: (0, 0)
    half = pl.BlockSpec((tm, RWKV_WIDTH), row)
    full = pl.BlockSpec((tm, D_MODEL), row)
    return pl.pallas_call(
        _outproj_kernel,
        grid=(m // tm,),
        in_specs=[full, half, half, half, half,
                  pl.BlockSpec((1, RWKV_WIDTH), fixed), pl.BlockSpec((1, RWKV_WIDTH), fixed),
                  pl.BlockSpec((D_MODEL, D_MODEL), fixed), pl.BlockSpec((1, D_MODEL), fixed),
                  pl.BlockSpec((RWKV_WIDTH, RWKV_WIDTH), fixed)],
        out_specs=[full, full],
        out_shape=[jax.ShapeDtypeStruct((m, D_MODEL), F32), jax.ShapeDtypeStruct((m, D_MODEL), BF16)],
        compiler_params=_params(("arbitrary",)),
        name="outproj",
    )(x2d, attn, y, gate, bonus, ln_w, ln_b, w_o, g_ffn, ones_bd)


def _ffn_tail(xn, h, gp_m1, gp_m2, gp_of, wg_ref, wu_ref, cw_ref, cb_ref, wout_ref, gfin_ref, o_ref):
    acc = h
    for c in range(D_FF // FF_CHUNK):
        cs = slice(c * FF_CHUNK, (c + 1) * FF_CHUNK)
        gp = gp_of(c)
        up = jnp.dot(xn, wu_ref[:, cs], preferred_element_type=F32)
        conv = (cb_ref[:, cs] + cw_ref[0:1, cs] * gp_m2(c, gp) + cw_ref[1:2, cs] * gp_m1(c, gp)
                + cw_ref[2:3, cs] * gp)
        act = conv * _sigmoid(conv) * up
        acc = acc + jnp.dot(act.astype(BF16), wout_ref[cs, :], preferred_element_type=F32)
    o_ref[...] = _rmsnorm(acc, gfin_ref[...])


def _ffn_prompt_kernel(xn_ref, h_ref, wg_ref, wu_ref, cw_ref, cb_ref, wout_ref, gfin_ref, *refs):
    win_in, (o_ref, conv_ref), win_out, (carry_ref, wt_ref) = refs[:9], refs[9:11], refs[11:14], refs[14:]

    @pl.when(pl.program_id(1) == 0)
    def _():
        carry_ref[...] = jnp.zeros_like(carry_ref)

    _window_body(*win_in, *win_out, wt_ref)

    xn = xn_ref[...]
    tm = xn.shape[0]
    rowi = lax.broadcasted_iota(jnp.int32, (tm, FF_CHUNK), 0)
    gps = {}

    def gp_of(c):
        cs = slice(c * FF_CHUNK, (c + 1) * FF_CHUNK)
        gps[c] = jnp.dot(xn, wg_ref[:, cs], preferred_element_type=F32)
        return gps[c]

    def gp_m1(c, gp):
        cs = slice(c * FF_CHUNK, (c + 1) * FF_CHUNK)
        return jnp.where(rowi == 0, carry_ref[1:2, cs], pltpu.roll(gp, 1, 0))

    def gp_m2(c, gp):
        cs = slice(c * FF_CHUNK, (c + 1) * FF_CHUNK)
        return jnp.where(rowi == 0, carry_ref[0:1, cs],
                         jnp.where(rowi == 1, carry_ref[1:2, cs], pltpu.roll(gp, 2, 0)))

    _ffn_tail(xn, h_ref[...], gp_m1, gp_m2, gp_of, wg_ref, wu_ref, cw_ref, cb_ref, wout_ref, gfin_ref, o_ref)
    for c, gp in gps.items():
        cs = slice(c * FF_CHUNK, (c + 1) * FF_CHUNK)
        carry_ref[:, cs] = gp[tm - 2:tm, :]
        conv_ref[0, :, cs] = gp[tm - 2:tm, :]


def _ffn_sample_kernel(xn_ref, h_ref, p1_ref, p2_ref, wg_ref, wu_ref, cw_ref, cb_ref, wout_ref, gfin_ref,
                       o_ref, gp_ref):
    xn = xn_ref[...]

    def gp_of(c):
        cs = slice(c * FF_CHUNK, (c + 1) * FF_CHUNK)
        gp = jnp.dot(xn, wg_ref[:, cs], preferred_element_type=F32)
        gp_ref[:, cs] = gp
        return gp

    gp_m1 = lambda c, gp: p1_ref[:, c * FF_CHUNK:(c + 1) * FF_CHUNK]
    gp_m2 = lambda c, gp: p2_ref[:, c * FF_CHUNK:(c + 1) * FF_CHUNK]
    _ffn_tail(xn, h_ref[...], gp_m1, gp_m2, gp_of, wg_ref, wu_ref, cw_ref, cb_ref, wout_ref, gfin_ref, o_ref)


def _ffn_weight_specs(fixed):
    once = pl.Buffered(1)
    return [pl.BlockSpec((D_MODEL, D_FF), fixed, pipeline_mode=once),
            pl.BlockSpec((D_MODEL, D_FF), fixed, pipeline_mode=once),
            pl.BlockSpec((CONV_W, D_FF), fixed), pl.BlockSpec((1, D_FF), fixed),
            pl.BlockSpec((D_FF, D_MODEL), fixed, pipeline_mode=once), pl.BlockSpec((1, D_MODEL), fixed)]


def _ffn_prompt(xn, h, fw, batch, seq, window):
    nt = window[0].shape[0] * (N_HEADS // WINDOW_HEADS) // batch
    tm = seq // nt
    assert tm * nt == seq and tm % 8 == 0
    row = lambda b, j: (b * nt + j, 0)
    fixed = lambda b, j: (0, 0)
    full = pl.BlockSpec((tm, D_MODEL), row)
    w_args, w_in_specs, w_out_specs, w_out_shape = _window_operands(*window, lambda b, j: b * nt + j)
    y, conv, attn_s, win_k, win_v = pl.pallas_call(
        _ffn_prompt_kernel,
        grid=(batch, nt),
        in_specs=[full, full] + _ffn_weight_specs(fixed) + w_in_specs,
        out_specs=[full, pl.BlockSpec((1, CONV_W - 1, D_FF), lambda b, j: (b, 0, 0))] + w_out_specs,
        out_shape=[jax.ShapeDtypeStruct((batch * seq, D_MODEL), F32),
                   jax.ShapeDtypeStruct((batch, CONV_W - 1, D_FF), F32)] + w_out_shape,
        scratch_shapes=[pltpu.VMEM((CONV_W - 1, D_FF), F32), pltpu.VMEM((8, WIN), F32)],
        compiler_params=_params(("arbitrary", "arbitrary")),
        name="ffn_prompt",
    )(xn, h, *fw, *w_args)
    nb = window[0].shape[0]
    return (y, conv, attn_s.reshape(nb, ATT_WIDTH), win_k.reshape(nb, N_HEADS, HEAD_DIM, WIN),
            win_v.reshape(nb, N_HEADS, HEAD_DIM, WIN))


def _ffn_sample(xn, h, prev1, prev2, fw):
    m = xn.shape[0]
    fixed = lambda i: (0, 0)
    full = pl.BlockSpec((m, D_MODEL), fixed)
    ffs = pl.BlockSpec((m, D_FF), fixed)
    return pl.pallas_call(
        _ffn_sample_kernel,
        grid=(1,),
        in_specs=[full, full, ffs, ffs] + _ffn_weight_specs(fixed),
        out_specs=[full, ffs],
        out_shape=[jax.ShapeDtypeStruct((m, D_MODEL), F32), jax.ShapeDtypeStruct((m, D_FF), F32)],
        compiler_params=_params(("arbitrary",)),
        name="ffn_sample",
    )(xn, h, prev1, prev2, *fw)


def _layer_weights(g_mix, w_in, tok_mu, w0, w_decay_up, a0, w_iclr_up, w_gate_up, k_k, k_a, r_k,
                   ln_x_w, ln_x_b, w_o, g_ffn, w_ffn_in, conv_w, conv_b, w_ffn_out, g_final):
    vec = lambda t: t.reshape(1, -1)
    pad = LORA_PAD - LORA_W
    w_cat = jnp.pad(w_in, ((0, 0), (0, pad))).astype(BF16)
    mu_p = vec(tok_mu[:RKV_W])
    mu_l = vec(jnp.pad(tok_mu[RKV_W:], (0, pad)))
    lora_up = jnp.zeros((LORA_PAD, RKV_W), F32)
    lora_up = lora_up.at[0:LORA_DECAY, 0:RWKV_WIDTH].set(w_decay_up)
    lora_up = lora_up.at[LORA_DECAY:LORA_DECAY + LORA_ICLR, RWKV_WIDTH:2 * RWKV_WIDTH].set(w_iclr_up)
    lora_up = lora_up.at[LORA_DECAY + LORA_ICLR:LORA_W, 2 * RWKV_WIDTH:].set(w_gate_up)
    seg = np.arange(RWKV_WIDTH) // HEAD_DIM
    ones_bd = jnp.asarray(seg[:, None] == seg[None, :], BF16)
    prep = (mu_p, mu_l, lora_up.astype(BF16), vec(w0), vec(a0), vec(k_k), vec(k_a), vec(r_k), ones_bd)
    ffn = (w_ffn_in[:, :D_FF].astype(BF16), w_ffn_in[:, D_FF:].astype(BF16), conv_w, vec(conv_b),
           w_ffn_out.astype(BF16), vec(g_final))
    return dict(g_mix=vec(g_mix), w_cat=w_cat, prep=prep, ones_bd=ones_bd, ln_w=vec(ln_x_w), ln_b=vec(ln_x_b),
                w_o=w_o.astype(BF16), g_ffn=vec(g_ffn), ffn=ffn)


def kernel(x_prompt, x_sample, cache_win_k, cache_win_v, state_shift, state_wkv, state_ffn_conv, g_mix, w_in, rel_bias, tok_mu, w0, w_decay_up, a0, w_iclr_up, w_gate_up, k_k, k_a, r_k, ln_x_w, ln_x_b, w_o, g_ffn, w_ffn_in, conv_w, conv_b, w_ffn_out, g_final):
    batch, seq, _ = x_prompt.shape
    nb = x_sample.shape[0]
    lw = _layer_weights(g_mix[0], w_in[0], tok_mu[0], w0[0], w_decay_up[0], a0[0], w_iclr_up[0],
                        w_gate_up[0], k_k[0], k_a[0], r_k[0].reshape(-1), ln_x_w[0], ln_x_b[0], w_o[0],
                        g_ffn[0], w_ffn_in[0], conv_w[0], conv_b[0], w_ffn_out[0], g_final)
    (bias_p, bias_s), rb0 = _bias_build(rel_bias)

    xp = x_prompt.reshape(batch * seq, D_MODEL)
    qkv, rkv, lora = _inproj(xp, lw['g_mix'], lw['w_cat'], 512)
    r, lgw, k2, v, kk, b, gate, bonus = _prep_prompt(rkv, lora, lw['prep'], batch, seq, 512)
    y, wkv_p = _wkv_prompt(r, lgw, k2, v, kk, b, batch, seq)
    attn = _attn_prompt(qkv, bias_p, batch, seq)
    h1, xn2 = _outproj(xp, attn, y, gate, bonus, lw['ln_w'], lw['ln_b'], lw['w_o'], lw['g_ffn'],
                       lw['ones_bd'], 512)
    qkv3 = qkv.reshape(batch, seq, QKV_W)
    win_k_p = qkv3[:, seq - WIN:, ATT_WIDTH:2 * ATT_WIDTH].reshape(1, batch, WIN, N_HEADS, HEAD_DIM)
    win_v_p = qkv3[:, seq - WIN:, 2 * ATT_WIDTH:].reshape(1, batch, WIN, N_HEADS, HEAD_DIM)
    shift_p = jnp.concatenate([rkv.reshape(batch, seq, RKV_W)[:, -1],
                               lora.reshape(batch, seq, LORA_PAD)[:, -1, :LORA_W]], axis=-1)[None]

    xs = x_sample.reshape(nb, D_MODEL)
    qkv_s, rkv_s, lora_s = _inproj(xs, lw['g_mix'], lw['w_cat'], nb)
    sh = state_shift[0]
    r, lgw, k2, v, kk, b, gate_s, bonus_s = _prep_sample(
        rkv_s, sh[:, :RKV_W], lora_s, jnp.pad(sh[:, RKV_W:], ((0, 0), (0, LORA_PAD - LORA_W))), lw['prep'])
    y_s, wkv_s = _wkv_sample(state_wkv[0], lgw, kk, b, k2, r, v, 8)
    q_s = qkv_s[:, :ATT_WIDTH]
    k_s = qkv_s[:, ATT_WIDTH:2 * ATT_WIDTH]
    v_s = qkv_s[:, 2 * ATT_WIDTH:]
    ck = jnp.transpose(cache_win_k[0], (0, 2, 3, 1))
    cv = jnp.transpose(cache_win_v[0], (0, 2, 3, 1))

    y_p, conv_p, attn_s, win_k_s, win_v_s = _ffn_prompt(xn2, h1, lw['ffn'], batch, seq,
                                                        (q_s, k_s, v_s, ck, cv, bias_s, rb0))

    win_k_s = jnp.transpose(win_k_s, (0, 3, 1, 2))
    win_v_s = jnp.transpose(win_v_s, (0, 3, 1, 2))
    h1_s, xn2_s = _outproj(xs, attn_s, y_s, gate_s, bonus_s, lw['ln_w'], lw['ln_b'], lw['w_o'], lw['g_ffn'],
                           lw['ones_bd'], nb)
    conv_state = state_ffn_conv[0]
    y_smp, gp_s = _ffn_sample(xn2_s, h1_s, conv_state[:, 1], conv_state[:, 0], lw['ffn'])
    conv_s = jnp.stack([conv_state[:, 1], gp_s], axis=1)
    shift_s = jnp.concatenate([rkv_s, lora_s[:, :LORA_W]], axis=-1)[None]

    return (y_p.reshape(batch, seq, D_MODEL), y_smp.reshape(nb, 1, D_MODEL),
            win_k_p, win_v_p, shift_p, wkv_p[None], conv_p[None],
            win_k_s[None], win_v_s[None], shift_s, wkv_s[None], conv_s[None])
```

```python
import functools

import numpy as np
import jax
import jax.numpy as jnp
from jax import lax
from jax.experimental import pallas as pl
from jax.experimental.pallas import tpu as pltpu

F32 = jnp.float32
BF16 = jnp.bfloat16

D_MODEL = 1024
HEAD_DIM = 64
ATT_WIDTH = 512
RWKV_WIDTH = 512
N_HEADS = 8
QKV_W = 3 * ATT_WIDTH
RKV_W = 3 * RWKV_WIDTH
LORA_DECAY, LORA_ICLR, LORA_GATE = 32, 32, 96
LORA_W = LORA_DECAY + LORA_ICLR + LORA_GATE
LORA_PAD = 256
RWKV_PROJ = RKV_W + LORA_W
PROJ_PAD = QKV_W + RKV_W + LORA_PAD
D_FF = 2816
FF_CHUNK = 1408
CONV_W = 3
WIN = 2048
DILATIONS = (1, 4, 16)
Q_BLOCK = 128
SUPER = 2048
N_BUCKETS = 32
MAX_DISTANCE = 2048
NORM_EPS = 1e-6
GN_EPS = 64e-5
NEG = -1e30
ATT_SCALE = HEAD_DIM ** -0.5
LOG2E = float(np.log2(np.e))
CHUNK = 64
VMEM_LIMIT = 56 * 1024 * 1024

NT = (((1,), (1,)), ((), ()))
TN = (((0,), (0,)), ((), ()))


def _dot(a, b):
    return jnp.dot(a.astype(BF16), b.astype(BF16), preferred_element_type=F32)


def _dot_nt(a, b):
    return lax.dot_general(a.astype(BF16), b.astype(BF16), NT, preferred_element_type=F32)


def _dot_tn(a, b):
    return lax.dot_general(a.astype(BF16), b.astype(BF16), TN, preferred_element_type=F32)


def _split2(x):
    hi = x.astype(BF16)
    lo = (x - hi.astype(F32)).astype(BF16)
    return hi, lo


def _split3(x):
    hi = x.astype(BF16)
    r1 = x - hi.astype(F32)
    mid = r1.astype(BF16)
    lo = (r1 - mid.astype(F32)).astype(BF16)
    return hi, mid, lo


def _segsum(x, ones_bd):
    hi, lo = _split2(x)
    return (jnp.dot(hi, ones_bd, preferred_element_type=F32)
            + jnp.dot(lo, ones_bd, preferred_element_type=F32))


def _sigmoid(x):
    return 1.0 / (1.0 + jnp.exp(-x))


def _rmsnorm(x, g):
    return x * lax.rsqrt(jnp.mean(x * x, axis=-1, keepdims=True) + NORM_EPS) * g


def _params(sem):
    return pltpu.CompilerParams(dimension_semantics=sem, vmem_limit_bytes=VMEM_LIMIT)


def _t5_bucket(dist):
    dist = np.asarray(dist, dtype=np.int64)
    exact = N_BUCKETS // 2
    scaled = np.log(np.maximum(dist, 1) / exact) / np.log(MAX_DISTANCE / exact)
    large = np.minimum(exact + (scaled * (N_BUCKETS - exact)).astype(np.int64), N_BUCKETS - 1)
    return np.where(dist < exact, dist, large).astype(np.int32)


def _inproj_kernel(x_ref, g_ref, w_ref, qkv_ref, rkv_ref, lora_ref):
    xb = _rmsnorm(x_ref[...], g_ref[...]).astype(BF16)
    qkv_ref[...] = jnp.dot(xb, w_ref[:, 0:QKV_W], preferred_element_type=F32)
    rkv_ref[...] = jnp.dot(xb, w_ref[:, QKV_W:QKV_W + RKV_W], preferred_element_type=F32)
    lora_ref[...] = jnp.dot(xb, w_ref[:, QKV_W + RKV_W:PROJ_PAD], preferred_element_type=F32)


def _inproj(x2d, g, w, tm):
    m = x2d.shape[0]
    row = lambda i: (i, 0)
    fixed = lambda i: (0, 0)
    return pl.pallas_call(
        _inproj_kernel,
        grid=(m // tm,),
        in_specs=[pl.BlockSpec((tm, D_MODEL), row),
                  pl.BlockSpec((1, D_MODEL), fixed),
                  pl.BlockSpec((D_MODEL, PROJ_PAD), fixed)],
        out_specs=[pl.BlockSpec((tm, QKV_W), row),
                   pl.BlockSpec((tm, RKV_W), row),
                   pl.BlockSpec((tm, LORA_PAD), row)],
        out_shape=[jax.ShapeDtypeStruct((m, QKV_W), F32),
                   jax.ShapeDtypeStruct((m, RKV_W), F32),
                   jax.ShapeDtypeStruct((m, LORA_PAD), F32)],
        compiler_params=_params(("arbitrary",)),
        name="inproj",
    )(x2d, g, w)


def _prep_math(p, p_prev, l, l_prev, mu_p, mu_l, wcat, w0, a0, k_k, k_a, r_k, ones_bd):
    pm = p + mu_p * (p_prev - p)
    lm = l + mu_l * (l_prev - l)
    r = pm[:, 0:RWKV_WIDTH]
    kr = pm[:, RWKV_WIDTH:2 * RWKV_WIDTH]
    vr = pm[:, 2 * RWKV_WIDTH:3 * RWKV_WIDTH]
    lane = lax.broadcasted_iota(jnp.int32, lm.shape, 1)
    feat = jnp.where(lane < LORA_DECAY, jnp.tanh(lm),
                     jnp.where(lane < LORA_DECAY + LORA_ICLR, lm, _sigmoid(lm)))
    z = jnp.dot(feat.astype(BF16), wcat, preferred_element_type=F32)
    zd = -(w0 + z[:, 0:RWKV_WIDTH])
    softplus = jnp.maximum(zd, 0.0) + jnp.log(1.0 + jnp.exp(-jnp.abs(zd)))
    lw = -jnp.exp(-softplus - 0.5)
    a = _sigmoid(a0 + z[:, RWKV_WIDTH:2 * RWKV_WIDTH])
    gate = z[:, 2 * RWKV_WIDTH:3 * RWKV_WIDTH]
    kk = kr * k_k
    kk = kk / jnp.maximum(jnp.sqrt(_segsum(kk * kk, ones_bd)), 1e-12)
    k2 = kr * (1.0 + (a - 1.0) * k_a)
    bonus = _segsum(r * k2 * r_k, ones_bd) * vr
    return r, lw, k2, vr, kk, kk * a, gate, bonus


def _inproj_prep_kernel(x_ref, g_ref, w_ref, mu_p_ref, mu_l_ref, wcat_ref, w0_ref, a0_ref, kk_ref, ka_ref,
                        rk_ref, ones_ref, qkv_ref, *refs):
    outs, (sp_ref, sl_ref), (cp_ref, cl_ref) = refs[:8], refs[8:10], refs[10:]

    @pl.when(pl.program_id(1) == 0)
    def _():
        cp_ref[...] = jnp.zeros_like(cp_ref)
        cl_ref[...] = jnp.zeros_like(cl_ref)

    xb = _rmsnorm(x_ref[...], g_ref[...]).astype(BF16)
    p = jnp.dot(xb, w_ref[:, QKV_W:QKV_W + RKV_W], preferred_element_type=F32)
    l = jnp.dot(xb, w_ref[:, QKV_W + RKV_W:PROJ_PAD], preferred_element_type=F32)
    tm = p.shape[0]
    first_p = lax.broadcasted_iota(jnp.int32, p.shape, 0) == 0
    first_l = lax.broadcasted_iota(jnp.int32, l.shape, 0) == 0
    p_prev = jnp.where(first_p, cp_ref[...], pltpu.roll(p, 1, 0))
    l_prev = jnp.where(first_l, cl_ref[...], pltpu.roll(l, 1, 0))
    cp_ref[...] = p[tm - 1:tm, :]
    cl_ref[...] = l[tm - 1:tm, :]
    sp_ref[0] = p[tm - 1:tm, :]
    sl_ref[0] = l[tm - 1:tm, :]
    res = _prep_math(p, p_prev, l, l_prev, mu_p_ref[...], mu_l_ref[...], wcat_ref[...], w0_ref[...],
                     a0_ref[...], kk_ref[...], ka_ref[...], rk_ref[...], ones_ref[...])
    for o, v in zip(outs, res):
        o[...] = v
    qkv_ref[...] = jnp.dot(xb, w_ref[:, 0:QKV_W], preferred_element_type=F32)


def _prep_sample_kernel(p_ref, pp_ref, l_ref, lp_ref, mu_p_ref, mu_l_ref, wcat_ref, w0_ref, a0_ref,
                        kk_ref, ka_ref, rk_ref, ones_ref, *outs):
    res = _prep_math(p_ref[...], pp_ref[...], l_ref[...], lp_ref[...], mu_p_ref[...], mu_l_ref[...],
                     wcat_ref[...], w0_ref[...], a0_ref[...], kk_ref[...], ka_ref[...], rk_ref[...],
                     ones_ref[...])
    for o, v in zip(outs, res):
        o[...] = v


def _inproj_prep_prompt(x2d, g, w, pw, batch, seq, tm):
    nt = seq // tm
    row = lambda b, j: (b * nt + j, 0)
    fixed = lambda b, j: (0, 0)
    vec = lambda n: pl.BlockSpec((1, n), fixed)
    once = pl.Buffered(1)
    m = x2d.shape[0]
    last = lambda n: pl.BlockSpec((1, 1, n), lambda b, j: (b, 0, 0))
    return pl.pallas_call(
        _inproj_prep_kernel,
        grid=(batch, nt),
        in_specs=[pl.BlockSpec((tm, D_MODEL), row), vec(D_MODEL),
                  pl.BlockSpec((D_MODEL, PROJ_PAD), fixed, pipeline_mode=once),
                  vec(RKV_W), vec(LORA_PAD), pl.BlockSpec((LORA_PAD, RKV_W), fixed),
                  vec(RWKV_WIDTH), vec(RWKV_WIDTH), vec(RWKV_WIDTH), vec(RWKV_WIDTH), vec(RWKV_WIDTH),
                  pl.BlockSpec((RWKV_WIDTH, RWKV_WIDTH), fixed)],
        out_specs=[pl.BlockSpec((tm, QKV_W), row)] + [pl.BlockSpec((tm, RWKV_WIDTH), row)] * 8
                  + [last(RKV_W), last(LORA_PAD)],
        out_shape=[jax.ShapeDtypeStruct((m, QKV_W), F32)] + [jax.ShapeDtypeStruct((m, RWKV_WIDTH), F32)] * 8
                  + [jax.ShapeDtypeStruct((batch, 1, RKV_W), F32), jax.ShapeDtypeStruct((batch, 1, LORA_PAD), F32)],
        scratch_shapes=[pltpu.VMEM((1, RKV_W), F32), pltpu.VMEM((1, LORA_PAD), F32)],
        compiler_params=_params(("arbitrary", "arbitrary")),
        name="inproj_prep_prompt",
    )(x2d, g, w, *pw)


def _prep_sample(rkv, rkv_prev, lora, lora_prev, pw):
    m = rkv.shape[0]
    full = lambda a: pl.BlockSpec(a.shape, lambda i: (0,) * a.ndim)
    args = (rkv, rkv_prev, lora, lora_prev) + tuple(pw)
    return pl.pallas_call(
        _prep_sample_kernel,
        grid=(1,),
        in_specs=[full(a) for a in args],
        out_specs=[pl.BlockSpec((m, RWKV_WIDTH), lambda i: (0, 0))] * 8,
        out_shape=[jax.ShapeDtypeStruct((m, RWKV_WIDTH), F32)] * 8,
        compiler_params=_params(("arbitrary",)),
        name="rwkv_prep_sample",
    )(*args)


def _cumsum_rows(x, tri):
    tri_b = tri.astype(BF16)
    out = None
    for piece in _split3(x):
        t = jnp.dot(tri_b, piece, preferred_element_type=F32)
        out = t if out is None else out + t
    return out


def _wkv_kernel(r_ref, lw_ref, k_ref, v_ref, kk_ref, b_ref, gate_ref, bonus_ref, lnw_ref, lnb_ref,
                y_ref, s_ref, st_ref):
    c = pl.program_id(0)

    @pl.when(c == 0)
    def _():
        st_ref[...] = jnp.zeros_like(st_ref)

    nb = r_ref.shape[0]
    n = CHUNK
    row = lax.broadcasted_iota(jnp.int32, (n, n), 0)
    col = lax.broadcasted_iota(jnp.int32, (n, n), 1)
    eye = jnp.where(row == col, 1.0, 0.0)
    tri = jnp.where(row >= col, 1.0, 0.0)
    row2 = lax.broadcasted_iota(jnp.int32, (2 * n, 2 * n), 0)
    col2 = lax.broadcasted_iota(jnp.int32, (2 * n, 2 * n), 1) % n
    gram_mask = jnp.logical_or(row2 % n > col2, jnp.logical_and(row2 >= n, row2 % n == col2))
    zeros = jnp.zeros((n, HEAD_DIM), BF16)

    ah, bh, kh, rh, beh, keh, vh, ge = ([] for _ in range(8))
    for bi in range(nb):
        lw = lw_ref[bi]
        lg = _cumsum_rows(lw, tri)
        g_inv = jnp.exp(-lg)
        lg_end = lg[n - 1:n, :]
        g_to_end = jnp.exp(lg_end - lg)
        g_end = jnp.exp(lg_end)
        kk = kk_ref[bi]
        b = b_ref[bi]
        k = k_ref[bi]
        full = (-kk * jnp.exp(lg - lw), b * g_inv, k * g_inv, r_ref[bi] * jnp.exp(lg),
                b * g_to_end, k * g_to_end, v_ref[bi], g_end)
        for h in range(N_HEADS):
            sl = slice(h * HEAD_DIM, (h + 1) * HEAD_DIM)
            for dst, t in zip((ah, bh, kh, rh, beh, keh, vh), full[:7]):
                dst.append(t[:, sl].astype(BF16))
            ge.append(g_end[:, sl])
    units = range(nb * N_HEADS)

    stack = lambda xs, ys: [jnp.concatenate([x, y], axis=0) for x, y in zip(xs, ys)]
    gram = [jnp.where(gram_mask, _dot_nt(ar, bk), 0.0) for ar, bk in zip(stack(ah, rh), stack(bh, kh))]
    top = [g[:n].astype(BF16) for g in gram]
    bot = [g[n:].astype(BF16) for g in gram]
    lv = [_dot(top[u], jnp.concatenate([zeros, vh[u]], axis=0)) for u in units]
    tm_ = [eye + gram[u][:n, :n] for u in units]
    lp = [_dot(t[:, :n], t[:, :n]).astype(BF16) for t in top]
    span = 2
    while 2 * span < n:
        prod = [_dot(x, p) for x, p in zip(stack(lp, [t.astype(BF16) for t in tm_]), lp)]
        tm_ = [tm_[u] + prod[u][n:] for u in units]
        lp = [p[:n].astype(BF16) for p in prod]
        span *= 2
    tm_ = [(tm_[u] + _dot(tm_[u], lp[u])).astype(BF16) for u in units]
    a_hat = [_dot(tm_[u], ah[u]).astype(BF16) for u in units]
    u0 = [_dot(tm_[u], lv[u]) for u in units]
    s0 = [st_ref[u] for u in units]
    on_s0 = [_dot_nt(x, s.astype(BF16)) for x, s in zip(stack(a_hat, rh), s0)]
    uv = stack([(on_s0[u][:n] + u0[u]).astype(BF16) for u in units], vh)
    y = [on_s0[u][n:] + _dot(bot[u], uv[u]) for u in units]
    for u, be_ke in zip(units, stack(beh, keh)):
        st_ref[u] = s0[u] * ge[u] + _dot_tn(uv[u], be_ke)
    for bi in range(nb):
        mixed = []
        for h in range(N_HEADS):
            sl = slice(h * HEAD_DIM, (h + 1) * HEAD_DIM)
            yu = y[bi * N_HEADS + h]
            dy = yu - jnp.mean(yu, axis=-1, keepdims=True)
            yn = dy * lax.rsqrt(jnp.mean(dy * dy, axis=-1, keepdims=True) + GN_EPS)
            mixed.append((yn * lnw_ref[:, sl] + lnb_ref[:, sl] + bonus_ref[bi, :, sl]) * gate_ref[bi, :, sl])
        y_ref[bi] = jnp.concatenate(mixed, axis=1).astype(BF16)

    @pl.when(c == pl.num_programs(0) - 1)
    def _():
        s_ref[...] = st_ref[...]


def _wkv_prompt(r, lw, k2, v, kk, b, gate, bonus, ln_w, ln_b, batch, seq):
    nc = seq // CHUNK
    seqs = lambda t: t.reshape(batch, seq, RWKV_WIDTH)
    spec = pl.BlockSpec((batch, CHUNK, RWKV_WIDTH), lambda c: (0, c, 0))
    vec = pl.BlockSpec((1, RWKV_WIDTH), lambda c: (0, 0))
    nu = batch * N_HEADS
    y, s = pl.pallas_call(
        _wkv_kernel,
        grid=(nc,),
        in_specs=[spec] * 8 + [vec, vec],
        out_specs=[spec, pl.BlockSpec((nu, HEAD_DIM, HEAD_DIM), lambda c: (0, 0, 0))],
        out_shape=[jax.ShapeDtypeStruct((batch, seq, RWKV_WIDTH), BF16),
                   jax.ShapeDtypeStruct((nu, HEAD_DIM, HEAD_DIM), F32)],
        scratch_shapes=[pltpu.VMEM((nu, HEAD_DIM, HEAD_DIM), F32)],
        compiler_params=_params(("arbitrary",)),
        name="wkv_prompt",
    )(*(seqs(t) for t in (r, lw, k2, v, kk, b, gate, bonus)), ln_w, ln_b)
    return y.reshape(batch * seq, RWKV_WIDTH), s.reshape(batch, N_HEADS, HEAD_DIM, HEAD_DIM)


def _wkv_step_kernel(s_ref, lw_ref, kk_ref, b_ref, k_ref, r_ref, v_ref, y_ref, so_ref):
    s = s_ref[...]
    sa = jnp.sum(s * (-kk_ref[...]), axis=-1, keepdims=True)
    s = s * jnp.exp(lw_ref[...]) + sa * b_ref[...] + v_ref[...] * k_ref[...]
    so_ref[...] = s
    y_ref[...] = jnp.sum(s * r_ref[...], axis=-1, keepdims=True)


def _wkv_sample(state, lw, kk, b, k2, r, v, bb):
    nb = state.shape[0]
    rowv = lambda t: t.reshape(nb, N_HEADS, 1, HEAD_DIM)
    colv = lambda t: t.reshape(nb, N_HEADS, HEAD_DIM, 1)
    idx = lambda i: (i, 0, 0, 0)
    s_spec = pl.BlockSpec((bb, N_HEADS, HEAD_DIM, HEAD_DIM), idx)
    r_spec = pl.BlockSpec((bb, N_HEADS, 1, HEAD_DIM), idx)
    c_spec = pl.BlockSpec((bb, N_HEADS, HEAD_DIM, 1), idx)
    y, s_new = pl.pallas_call(
        _wkv_step_kernel,
        grid=(nb // bb,),
        in_specs=[s_spec] + [r_spec] * 5 + [c_spec],
        out_specs=[c_spec, s_spec],
        out_shape=[jax.ShapeDtypeStruct((nb, N_HEADS, HEAD_DIM, 1), F32),
                   jax.ShapeDtypeStruct(state.shape, F32)],
        compiler_params=_params(("arbitrary",)),
        name="wkv_sample",
    )(state, rowv(lw), rowv(kk), rowv(b), rowv(k2), rowv(r), colv(v))
    return y.reshape(nb, RWKV_WIDTH), s_new


def _bias_tables():
    i = np.arange(Q_BLOCK)[:, None]
    c = np.arange(2 * Q_BLOCK)[None, :]
    n = i + Q_BLOCK - c
    ok = (n >= 0) & (n <= Q_BLOCK)
    prompt = np.stack([np.where(ok, _t5_bucket(d * np.clip(n, 0, Q_BLOCK)), -1) for d in DILATIONS])
    dist = WIN - np.arange(WIN)
    sample = np.stack([np.where((dist % d == 0) & (dist // d <= Q_BLOCK), _t5_bucket(dist), -1)
                       for d in DILATIONS])
    sample = np.broadcast_to(sample[:, None, :], (len(DILATIONS), N_HEADS, WIN))
    return prompt.astype(np.int32), sample.astype(np.int32)


def _bias_kernel(rb_ref, rbc_ref, bp_ref, bs_ref, op_ref, os_ref):
    bp = bp_ref[0]
    bs = bs_ref[0]
    acc_s = jnp.full(bs.shape, NEG, F32)
    for bk in range(N_BUCKETS):
        acc_s = jnp.where(bs == bk, rbc_ref[bk], acc_s)
    os_ref[0] = acc_s
    for h in range(N_HEADS):
        acc = jnp.full(bp.shape, NEG, F32)
        for bk in range(N_BUCKETS):
            acc = jnp.where(bp == bk, rb_ref[bk, h], acc)
        op_ref[0, h] = acc * LOG2E


def _bias_build(rel_bias):
    bp, bs = _bias_tables()
    rbc = rel_bias.reshape(N_BUCKETS, N_HEADS, 1)
    nb = len(DILATIONS)
    return pl.pallas_call(
        _bias_kernel,
        grid=(nb,),
        in_specs=[pl.BlockSpec(memory_space=pltpu.SMEM),
                  pl.BlockSpec((N_BUCKETS, N_HEADS, 1), lambda i: (0, 0, 0)),
                  pl.BlockSpec((1, Q_BLOCK, 2 * Q_BLOCK), lambda i: (i, 0, 0)),
                  pl.BlockSpec((1, N_HEADS, WIN), lambda i: (i, 0, 0))],
        out_specs=[pl.BlockSpec((1, N_HEADS, Q_BLOCK, 2 * Q_BLOCK), lambda i: (i, 0, 0, 0)),
                   pl.BlockSpec((1, N_HEADS, WIN), lambda i: (i, 0, 0))],
        out_shape=[jax.ShapeDtypeStruct((nb, N_HEADS, Q_BLOCK, 2 * Q_BLOCK), F32),
                   jax.ShapeDtypeStruct((nb, N_HEADS, WIN), F32)],
        compiler_params=_params(("arbitrary",)),
        name="bias_build",
    )(rel_bias, rbc, jnp.asarray(bp), jnp.asarray(bs)), rbc[0]


ATTN_GROUP = 4


def _attn_units():
    units = []
    for br, d in enumerate(DILATIONS):
        nblk = SUPER // (d * Q_BLOCK)
        for r in range(d):
            rows = lambda j, r=r, d=d: (pl.ds(r + d * Q_BLOCK * j, Q_BLOCK, stride=d) if d > 1
                                        else pl.ds(Q_BLOCK * j, Q_BLOCK))
            for blk in range(nblk):
                units.append((br, rows(blk), rows(blk - 1 if blk else nblk - 1), blk == 0))
    return units


def _attn_kernel(q_ref, kc_ref, kp_ref, vc_ref, vp_ref, bias_ref, o_ref, acc_ref, m_ref):
    has_prev = pl.program_id(2) > 0
    is_a = lax.broadcasted_iota(jnp.int32, (Q_BLOCK, 128), 1) < HEAD_DIM
    is_a2 = lax.broadcasted_iota(jnp.int32, (2 * Q_BLOCK, 128), 1) < HEAD_DIM
    live = jnp.logical_or(has_prev,
                          lax.broadcasted_iota(jnp.int32, (2 * Q_BLOCK, 2 * Q_BLOCK), 1) >= Q_BLOCK)
    units = _attn_units()
    for g0 in range(0, len(units), ATTN_GROUP):
        group = units[g0:g0 + ATTN_GROUP]
        ops = []
        for br, cur, prv, from_prev in group:
            qs = q_ref[cur, :] * (ATT_SCALE * LOG2E)
            q2 =jnp.concatenate([jnp.where(is_a, qs, 0.0), jnp.where(is_a, 0.0, qs)], axis=0)
            k2 = jnp.concatenate([(kp_ref if from_prev else kc_ref)[prv, :], kc_ref[cur, :]], axis=0)
            v2 = jnp.concatenate([(vp_ref if from_prev else vc_ref)[prv, :], vc_ref[cur, :]], axis=0)
            ops.append((br, cur, from_prev, q2.astype(BF16), k2.astype(BF16),
                        jnp.where(is_a2, v2, 1.0).astype(BF16), jnp.where(is_a2, 1.0, v2).astype(BF16)))
        logits = []
        for br, cur, from_prev, q2, k2, va, vb in ops:
            lg = lax.dot_general(q2, k2, NT, preferred_element_type=F32) + bias_ref[br]
            logits.append(jnp.where(live, lg, NEG) if from_prev else lg)
        ms = [jnp.max(lg, axis=-1, keepdims=True) for lg in logits]
        ps = [jnp.exp2(lg - m).astype(BF16) for lg, m in zip(logits, ms)]
        for (br, cur, _, _, _, va, vb), p, m in zip(ops, ps, ms):
            acc_ref[br, 0, cur, :] = jnp.dot(p[:Q_BLOCK], va, preferred_element_type=F32)
            acc_ref[br, 1, cur, :] = jnp.dot(p[Q_BLOCK:], vb, preferred_element_type=F32)
            m_ref[br, 0, cur, :] = jnp.broadcast_to(m[:Q_BLOCK], (Q_BLOCK, 128))
            m_ref[br, 1, cur, :] = jnp.broadcast_to(m[Q_BLOCK:], (Q_BLOCK, 128))

    for i in range(SUPER // Q_BLOCK):
        rs = pl.ds(i * Q_BLOCK, Q_BLOCK)
        outs = []
        for hd in range(2):
            ms = [m_ref[br, hd, rs, :] for br in range(len(DILATIONS))]
            m = functools.reduce(jnp.maximum, ms)
            tot = 0.0
            for br in range(len(DILATIONS)):
                tot = tot + jnp.exp2(ms[br] - m) * acc_ref[br, hd, rs, :]
            outs.append(tot / pltpu.roll(tot, HEAD_DIM, 1))
        o_ref[rs, :] = jnp.where(is_a, outs[0], outs[1]).astype(o_ref.dtype)


def _attn_prompt(qkv, bias_p, batch, seq):
    ns = seq // SUPER
    npair = ATT_WIDTH // 128
    blk = lambda col0, prev: pl.BlockSpec(
        (SUPER, 128),
        (lambda b, hp, sb: (b * ns + jnp.maximum(sb - 1, 0), col0 + hp)) if prev
        else (lambda b, hp, sb: (b * ns + sb, col0 + hp)))
    nb = len(DILATIONS)
    return pl.pallas_call(
        _attn_kernel,
        grid=(batch, npair, ns),
        in_specs=[blk(0, False), blk(npair, False), blk(npair, True), blk(2 * npair, False),
                  blk(2 * npair, True),
                  pl.BlockSpec((nb, None, 2 * Q_BLOCK, 2 * Q_BLOCK), lambda b, hp, sb: (0, hp, 0, 0))],
        out_specs=pl.BlockSpec((SUPER, 128), lambda b, hp, sb: (b * ns + sb, hp)),
        out_shape=jax.ShapeDtypeStruct((batch * seq, ATT_WIDTH), BF16),
        scratch_shapes=[pltpu.VMEM((nb, 2, SUPER, 128), F32)] * 2,
        compiler_params=_params(("arbitrary", "arbitrary", "arbitrary")),
        name="attn_prompt",
    )(qkv, qkv, qkv, qkv, qkv, bias_p.reshape(nb, npair, 2 * Q_BLOCK, 2 * Q_BLOCK))


WINDOW_HEADS = 4


def _window_body(q_ref, kn_ref, qc_ref, knc_ref, vnc_ref, ck_ref, cv_ref, bias_ref, rb0_ref,
                 o_ref, ok_ref, ov_ref, wt_ref):
    nh = WINDOW_HEADS
    lane = lax.broadcasted_iota(jnp.int32, (HEAD_DIM, WIN), 1)
    last = lane == WIN - 1
    l_new = jnp.sum(kn_ref[...] * (q_ref[...] * ATT_SCALE), axis=-1, keepdims=True) + rb0_ref[...]
    rows = []
    for h in range(nh):
        kh = ck_ref[h]
        rows.append(jnp.sum(kh * (qc_ref[h] * ATT_SCALE), axis=0, keepdims=True))
        ok_ref[h] = jnp.where(last, knc_ref[h], pltpu.roll(kh, WIN - 1, 1))
    logits = jnp.concatenate(rows, axis=0)
    parts = []
    for br in range(len(DILATIONS)):
        lg = logits + bias_ref[br]
        m = jnp.maximum(jnp.max(lg, axis=-1, keepdims=True), l_new)
        p = jnp.exp(lg - m)
        p_new = jnp.exp(l_new - m)
        parts.append((p, p_new, m, jnp.sum(p, axis=-1, keepdims=True) + p_new))
    m = functools.reduce(jnp.maximum, [pt[2] for pt in parts])
    wt = 0.0
    w_new = 0.0
    den = 0.0
    for p, p_new, m_g, s_g in parts:
        w = jnp.exp(m_g - m)
        wt = wt + w * p
        w_new = w_new + w * p_new
        den = den + w * s_g
    wt_ref[0:nh, :] = wt / den
    w_new = w_new / den
    for h in range(nh):
        vh = cv_ref[h]
        o_ref[h] = (jnp.sum(vh * wt_ref[h:h + 1, :], axis=1, keepdims=True)
                    + vnc_ref[h] * w_new[h:h + 1, :])
        ov_ref[h] = jnp.where(last, vnc_ref[h], pltpu.roll(vh, WIN - 1, 1))


def _window_operands(q, k_new, v_new, cache_k, cache_v, bias_s, rb0, step_of):
    nb = q.shape[0]
    nh = WINDOW_HEADS
    groups = N_HEADS // nh
    seq_grp = lambda *g: (step_of(*g) // groups, step_of(*g) % groups)
    rows = lambda t: t.reshape(nb, groups, nh, HEAD_DIM)
    cols = lambda t: t.reshape(nb, groups, nh, HEAD_DIM, 1)
    wins = lambda t: t.reshape(nb, groups, nh, HEAD_DIM, WIN)
    row_spec = pl.BlockSpec((None, None, nh, HEAD_DIM), lambda *g: seq_grp(*g) + (0, 0))
    col_spec = pl.BlockSpec((None, None, nh, HEAD_DIM, 1), lambda *g: seq_grp(*g) + (0, 0, 0))
    win_spec = pl.BlockSpec((None, None, nh, HEAD_DIM, WIN), lambda *g: seq_grp(*g) + (0, 0, 0))
    nbr = len(DILATIONS)
    bias_spec = pl.BlockSpec((nbr, None, nh, WIN), lambda *g: (0, seq_grp(*g)[1], 0, 0))
    rb0_spec = pl.BlockSpec((None, nh, 1), lambda *g: (seq_grp(*g)[1], 0, 0))
    args = (rows(q), rows(k_new), cols(q), cols(k_new), cols(v_new), wins(cache_k), wins(cache_v),
            bias_s.reshape(nbr, groups, nh, WIN), rb0.reshape(groups, nh, 1))
    in_specs = [row_spec, row_spec, col_spec, col_spec, col_spec, win_spec, win_spec, bias_spec, rb0_spec]
    out_specs = [col_spec, win_spec, win_spec]
    out_shape = [jax.ShapeDtypeStruct((nb, groups, nh, HEAD_DIM, 1), F32),
                 jax.ShapeDtypeStruct((nb, groups, nh, HEAD_DIM, WIN), F32),
                 jax.ShapeDtypeStruct((nb, groups, nh, HEAD_DIM, WIN), F32)]
    return args, in_specs, out_specs, out_shape


def _outproj_kernel(x_ref, attn_ref, y_ref, gate_ref, bonus_ref, lnw_ref, lnb_ref, wo_ref, gffn_ref,
                    ones_ref, h_ref, xn_ref):
    ones_bd = ones_ref[...]
    y = y_ref[...]
    mu = _segsum(y, ones_bd) * (1.0 / HEAD_DIM)
    dy = y - mu
    var = _segsum(dy * dy, ones_bd) * (1.0 / HEAD_DIM)
    yn = dy * lax.rsqrt(var + GN_EPS) * lnw_ref[...] + lnb_ref[...]
    rw = (yn + bonus_ref[...]) * gate_ref[...]
    mixed = (jnp.dot(attn_ref[...].astype(BF16), wo_ref[0:ATT_WIDTH, :], preferred_element_type=F32)
             + jnp.dot(rw.astype(BF16), wo_ref[ATT_WIDTH:D_MODEL, :], preferred_element_type=F32))
    h = x_ref[...] + mixed
    h_ref[...] = h
    xn_ref[...] = _rmsnorm(h, gffn_ref[...]).astype(BF16)


def _outproj(x2d, attn, y, gate, bonus, ln_w, ln_b, w_o, g_ffn, ones_bd, tm):
    m = x2d.shape[0]
    row = lambda i: (i, 0)
    fixed = lambda i: (0, 0)
    half = pl.BlockSpec((tm, RWKV_WIDTH), row)
    full = pl.BlockSpec((tm, D_MODEL), row)
    return pl.pallas_call(
        _outproj_kernel,
        grid=(m // tm,),
        in_specs=[full, half, half, half, half,
                  pl.BlockSpec((1, RWKV_WIDTH), fixed), pl.BlockSpec((1, RWKV_WIDTH), fixed),
                  pl.BlockSpec((D_MODEL, D_MODEL), fixed), pl.BlockSpec((1, D_MODEL), fixed),
                  pl.BlockSpec((RWKV_WIDTH, RWKV_WIDTH), fixed)],
        out_specs=[full, full],
        out_shape=[jax.ShapeDtypeStruct((m, D_MODEL), F32), jax.ShapeDtypeStruct((m, D_MODEL), BF16)],
        compiler_params=_params(("arbitrary",)),
        name="outproj",
    )(x2d, attn, y, gate, bonus, ln_w, ln_b, w_o, g_ffn, ones_bd)


def _ffn_tail(xn, h, gp_m1, gp_m2, gp_of, wg_ref, wu_ref, cw_ref, cb_ref, wout_ref, gfin_ref, o_ref):
    acc = None
    for c in range(D_FF // FF_CHUNK):
        cs = slice(c * FF_CHUNK, (c + 1) * FF_CHUNK)
        gp = gp_of(c)
        up = jnp.dot(xn, wu_ref[:, cs], preferred_element_type=F32)
        conv = (cb_ref[:, cs] + cw_ref[0:1, cs] * gp_m2(c, gp) + cw_ref[1:2, cs] * gp_m1(c, gp)
                + cw_ref[2:3, cs] * gp)
        act = conv * _sigmoid(conv) * up
        part = jnp.dot(act.astype(BF16), wout_ref[cs, :], preferred_element_type=F32)
        acc = part if acc is None else acc + part
    o_ref[...] = _rmsnorm(h + acc, gfin_ref[...])


def _ffn_prompt_kernel(x_ref, attn_ref, rw_ref, wo_ref, gffn_ref, wg_ref, wu_ref, cw_ref, cb_ref, wout_ref,
                       gfin_ref, *refs):
    win_in, (o_ref, conv_ref), win_out, (carry_ref, wt_ref) = refs[:9], refs[9:11], refs[11:14], refs[14:]

    @pl.when(pl.program_id(1) == 0)
    def _():
        carry_ref[...] = jnp.zeros_like(carry_ref)

    _window_body(*win_in, *win_out, wt_ref)

    h = (x_ref[...]
         + jnp.dot(attn_ref[...], wo_ref[0:ATT_WIDTH, :], preferred_element_type=F32)
         + jnp.dot(rw_ref[...], wo_ref[ATT_WIDTH:D_MODEL, :], preferred_element_type=F32))
    xn = _rmsnorm(h, gffn_ref[...]).astype(BF16)
    tm = xn.shape[0]
    rowi = lax.broadcasted_iota(jnp.int32, (tm, FF_CHUNK), 0)
    gps = {}

    def gp_of(c):
        cs = slice(c * FF_CHUNK, (c + 1) * FF_CHUNK)
        gps[c] = jnp.dot(xn, wg_ref[:, cs], preferred_element_type=F32)
        return gps[c]

    def gp_m1(c, gp):
        cs = slice(c * FF_CHUNK, (c + 1) * FF_CHUNK)
        return jnp.where(rowi == 0, carry_ref[1:2, cs], pltpu.roll(gp, 1, 0))

    def gp_m2(c, gp):
        cs = slice(c * FF_CHUNK, (c + 1) * FF_CHUNK)
        return jnp.where(rowi == 0, carry_ref[0:1, cs],
                         jnp.where(rowi == 1, carry_ref[1:2, cs], pltpu.roll(gp, 2, 0)))

    _ffn_tail(xn, h, gp_m1, gp_m2, gp_of, wg_ref, wu_ref, cw_ref, cb_ref, wout_ref, gfin_ref, o_ref)
    for c, gp in gps.items():
        cs = slice(c * FF_CHUNK, (c + 1) * FF_CHUNK)
        carry_ref[:, cs] = gp[tm - 2:tm, :]
        conv_ref[0, :, cs] = gp[tm - 2:tm, :]


def _ffn_sample_kernel(xn_ref, h_ref, p1_ref, p2_ref, wg_ref, wu_ref, cw_ref, cb_ref, wout_ref, gfin_ref,
                       o_ref, gp_ref):
    xn = xn_ref[...]

    def gp_of(c):
        cs = slice(c * FF_CHUNK, (c + 1) * FF_CHUNK)
        gp = jnp.dot(xn, wg_ref[:, cs], preferred_element_type=F32)
        gp_ref[:, cs] = gp
        return gp

    gp_m1 = lambda c, gp: p1_ref[:, c * FF_CHUNK:(c + 1) * FF_CHUNK]
    gp_m2 = lambda c, gp: p2_ref[:, c * FF_CHUNK:(c + 1) * FF_CHUNK]
    _ffn_tail(xn, h_ref[...], gp_m1, gp_m2, gp_of, wg_ref, wu_ref, cw_ref, cb_ref, wout_ref, gfin_ref, o_ref)


def _ffn_weight_specs(fixed):
    once = pl.Buffered(1)
    return [pl.BlockSpec((D_MODEL, D_FF), fixed, pipeline_mode=once),
            pl.BlockSpec((D_MODEL, D_FF), fixed, pipeline_mode=once),
            pl.BlockSpec((CONV_W, D_FF), fixed), pl.BlockSpec((1, D_FF), fixed),
            pl.BlockSpec((D_FF, D_MODEL), fixed, pipeline_mode=once), pl.BlockSpec((1, D_MODEL), fixed)]


def _ffn_prompt(x2d, attn, rw, w_o, g_ffn, fw, batch, seq, window):
    nt = window[0].shape[0] * (N_HEADS // WINDOW_HEADS) // batch
    tm = seq // nt
    assert tm * nt == seq and tm % 16 == 0
    row = lambda b, j: (b * nt + j, 0)
    fixed = lambda b, j: (0, 0)
    full = pl.BlockSpec((tm, D_MODEL), row)
    half = pl.BlockSpec((tm, ATT_WIDTH), row)
    mix_specs = [full, half, half, pl.BlockSpec((D_MODEL, D_MODEL), fixed), pl.BlockSpec((1, D_MODEL), fixed)]
    w_args, w_in_specs, w_out_specs, w_out_shape = _window_operands(*window, lambda b, j: b * nt + j)
    y, conv, attn_s, win_k, win_v = pl.pallas_call(
        _ffn_prompt_kernel,
        grid=(batch, nt),
        in_specs=mix_specs + _ffn_weight_specs(fixed) + w_in_specs,
        out_specs=[full, pl.BlockSpec((1, CONV_W - 1, D_FF), lambda b, j: (b, 0, 0))] + w_out_specs,
        out_shape=[jax.ShapeDtypeStruct((batch * seq, D_MODEL), F32),
                   jax.ShapeDtypeStruct((batch, CONV_W - 1, D_FF), F32)] + w_out_shape,
        scratch_shapes=[pltpu.VMEM((CONV_W - 1, D_FF), F32), pltpu.VMEM((8, WIN), F32)],
        compiler_params=_params(("arbitrary", "arbitrary")),
        name="ffn_prompt",
    )(x2d, attn, rw, w_o, g_ffn, *fw, *w_args)
    nb = window[0].shape[0]
    return (y, conv, attn_s.reshape(nb, ATT_WIDTH), win_k.reshape(nb, N_HEADS, HEAD_DIM, WIN),
            win_v.reshape(nb, N_HEADS, HEAD_DIM, WIN))


def _ffn_sample(xn, h, prev1, prev2, fw):
    m = xn.shape[0]
    fixed = lambda i: (0, 0)
    full = pl.BlockSpec((m, D_MODEL), fixed)
    ffs = pl.BlockSpec((m, D_FF), fixed)
    return pl.pallas_call(
        _ffn_sample_kernel,
        grid=(1,),
        in_specs=[full, full, ffs, ffs] + _ffn_weight_specs(fixed),
        out_specs=[full, ffs],
        out_shape=[jax.ShapeDtypeStruct((m, D_MODEL), F32), jax.ShapeDtypeStruct((m, D_FF), F32)],
        compiler_params=_params(("arbitrary",)),
        name="ffn_sample",
    )(xn, h, prev1, prev2, *fw)


def _layer_weights(g_mix, w_in, tok_mu, w0, w_decay_up, a0, w_iclr_up, w_gate_up, k_k, k_a, r_k,
                   ln_x_w, ln_x_b, w_o, g_ffn, w_ffn_in, conv_w, conv_b, w_ffn_out, g_final):
    vec = lambda t: t.reshape(1, -1)
    pad = LORA_PAD - LORA_W
    w_cat = jnp.pad(w_in, ((0, 0), (0, pad))).astype(BF16)
    mu_p = vec(tok_mu[:RKV_W])
    mu_l = vec(jnp.pad(tok_mu[RKV_W:], (0, pad)))
    lora_up = jnp.zeros((LORA_PAD, RKV_W), F32)
    lora_up = lora_up.at[0:LORA_DECAY, 0:RWKV_WIDTH].set(w_decay_up)
    lora_up = lora_up.at[LORA_DECAY:LORA_DECAY + LORA_ICLR, RWKV_WIDTH:2 * RWKV_WIDTH].set(w_iclr_up)
    lora_up = lora_up.at[LORA_DECAY + LORA_ICLR:LORA_W, 2 * RWKV_WIDTH:].set(w_gate_up)
    seg = np.arange(RWKV_WIDTH) // HEAD_DIM
    ones_bd = jnp.asarray(seg[:, None] == seg[None, :], BF16)
    prep = (mu_p, mu_l, lora_up.astype(BF16), vec(w0), vec(a0), vec(k_k), vec(k_a), vec(r_k), ones_bd)
    ffn = (w_ffn_in[:, :D_FF].astype(BF16), w_ffn_in[:, D_FF:].astype(BF16), conv_w, vec(conv_b),
           w_ffn_out.astype(BF16), vec(g_final))
    return dict(g_mix=vec(g_mix), w_cat=w_cat, prep=prep, ones_bd=ones_bd, ln_w=vec(ln_x_w), ln_b=vec(ln_x_b),
                w_o=w_o.astype(BF16), g_ffn=vec(g_ffn), ffn=ffn)


def kernel(x_prompt, x_sample, cache_win_k, cache_win_v, state_shift, state_wkv, state_ffn_conv, g_mix, w_in, rel_bias, tok_mu, w0, w_decay_up, a0, w_iclr_up, w_gate_up, k_k, k_a, r_k, ln_x_w, ln_x_b, w_o, g_ffn, w_ffn_in, conv_w, conv_b, w_ffn_out, g_final):
    batch, seq, _ = x_prompt.shape
    nb = x_sample.shape[0]
    lw = _layer_weights(g_mix[0], w_in[0], tok_mu[0], w0[0], w_decay_up[0], a0[0], w_iclr_up[0],
                        w_gate_up[0], k_k[0], k_a[0], r_k[0].reshape(-1), ln_x_w[0], ln_x_b[0], w_o[0],
                        g_ffn[0], w_ffn_in[0], conv_w[0], conv_b[0], w_ffn_out[0], g_final)
    (bias_p, bias_s), rb0 = _bias_build(rel_bias)

    xp = x_prompt.reshape(batch * seq, D_MODEL)
    qkv, r, lgw, k2, v, kk, b, gate, bonus, sh_p, sh_l = _inproj_prep_prompt(
        xp, lw['g_mix'], lw['w_cat'], lw['prep'], batch, seq, 512)
    rw, wkv_p = _wkv_prompt(r, lgw, k2, v, kk, b, gate, bonus, lw['ln_w'], lw['ln_b'], batch, seq)
    attn = _attn_prompt(qkv, bias_p, batch, seq)
    qkv3 = qkv.reshape(batch, seq, QKV_W)
    win_k_p = qkv3[:, seq - WIN:, ATT_WIDTH:2 * ATT_WIDTH].reshape(1, batch, WIN, N_HEADS, HEAD_DIM)
    win_v_p = qkv3[:, seq - WIN:, 2 * ATT_WIDTH:].reshape(1, batch, WIN, N_HEADS, HEAD_DIM)
    shift_p = jnp.concatenate([sh_p[:, 0], sh_l[:, 0, :LORA_W]], axis=-1)[None]

    xs = x_sample.reshape(nb, D_MODEL)
    qkv_s, rkv_s, lora_s = _inproj(xs, lw['g_mix'], lw['w_cat'], nb)
    sh = state_shift[0]
    r, lgw, k2, v, kk, b, gate_s, bonus_s = _prep_sample(
        rkv_s, sh[:, :RKV_W], lora_s, jnp.pad(sh[:, RKV_W:], ((0, 0), (0, LORA_PAD - LORA_W))), lw['prep'])
    y_s, wkv_s = _wkv_sample(state_wkv[0], lgw, kk, b, k2, r, v, 8)
    q_s = qkv_s[:, :ATT_WIDTH]
    k_s = qkv_s[:, ATT_WIDTH:2 * ATT_WIDTH]
    v_s = qkv_s[:, 2 * ATT_WIDTH:]
    ck = jnp.transpose(cache_win_k[0], (0, 2, 3, 1))
    cv = jnp.transpose(cache_win_v[0], (0, 2, 3, 1))

    y_p, conv_p, attn_s, win_k_s, win_v_s = _ffn_prompt(xp, attn, rw, lw['w_o'], lw['g_ffn'], lw['ffn'],
                                                        batch, seq, (q_s, k_s, v_s, ck, cv, bias_s, rb0))

    win_k_s = jnp.transpose(win_k_s, (0, 3, 1, 2))
    win_v_s = jnp.transpose(win_v_s, (0, 3, 1, 2))
    h1_s, xn2_s = _outproj(xs, attn_s, y_s, gate_s, bonus_s, lw['ln_w'], lw['ln_b'], lw['w_o'], lw['g_ffn'],
                           lw['ones_bd'], nb)
    conv_state = state_ffn_conv[0]
    y_smp, gp_s = _ffn_sample(xn2_s, h1_s, conv_state[:, 1], conv_state[:, 0], lw['ffn'])
    conv_s = jnp.stack([conv_state[:, 1], gp_s], axis=1)
    shift_s = jnp.concatenate([rkv_s, lora_s[:, :LORA_W]], axis=-1)[None]

    return (y_p.reshape(batch, seq, D_MODEL), y_smp.reshape(nb, 1, D_MODEL),
            win_k_p, win_v_p, shift_p, wkv_p[None], conv_p[None],
            win_k_s[None], win_v_s[None], shift_s, wkv_s[None], conv_s[None])
```

```python
import functools

import numpy as np
import jax
import jax.numpy as jnp
from jax import lax
from jax.experimental import pallas as pl
from jax.experimental.pallas import tpu as pltpu

F32 = jnp.float32
BF16 = jnp.bfloat16

D_MODEL = 1024
HEAD_DIM = 64
ATT_WIDTH = 512
RWKV_WIDTH = 512
N_HEADS = 8
QKV_W = 3 * ATT_WIDTH
RKV_W = 3 * RWKV_WIDTH
LORA_DECAY, LORA_ICLR, LORA_GATE = 32, 32, 96
LORA_W = LORA_DECAY + LORA_ICLR + LORA_GATE
LORA_PAD = 256
RWKV_PROJ = RKV_W + LORA_W
PROJ_PAD = QKV_W + RKV_W + LORA_PAD
D_FF = 2816
FF_CHUNK = 1408
CONV_W = 3
WIN = 2048
DILATIONS = (1, 4, 16)
Q_BLOCK = 128
SUPER = 2048
N_BUCKETS = 32
MAX_DISTANCE = 2048
NORM_EPS = 1e-6
GN_EPS = 64e-5
NEG = -1e30
ATT_SCALE = HEAD_DIM ** -0.5
LOG2E = float(np.log2(np.e))
CHUNK = 64
VMEM_LIMIT = 56 * 1024 * 1024

NT = (((1,), (1,)), ((), ()))
TN = (((0,), (0,)), ((), ()))


def _dot(a, b):
    return jnp.dot(a.astype(BF16), b.astype(BF16), preferred_element_type=F32)


def _dot_nt(a, b):
    return lax.dot_general(a.astype(BF16), b.astype(BF16), NT, preferred_element_type=F32)


def _dot_tn(a, b):
    return lax.dot_general(a.astype(BF16), b.astype(BF16), TN, preferred_element_type=F32)


def _split2(x):
    hi = x.astype(BF16)
    lo = (x - hi.astype(F32)).astype(BF16)
    return hi, lo


def _split3(x):
    hi = x.astype(BF16)
    r1 = x - hi.astype(F32)
    mid = r1.astype(BF16)
    lo = (r1 - mid.astype(F32)).astype(BF16)
    return hi, mid, lo


def _segsum(x, ones_bd):
    hi, lo = _split2(x)
    return (jnp.dot(hi, ones_bd, preferred_element_type=F32)
            + jnp.dot(lo, ones_bd, preferred_element_type=F32))


def _sigmoid(x):
    return 1.0 / (1.0 + jnp.exp(-x))


def _eye(n):
    return jnp.where(lax.broadcasted_iota(jnp.int32, (n, n), 0) == lax.broadcasted_iota(jnp.int32, (n, n), 1),
                     1.0, 0.0)


def _rmsnorm(x, g):
    return x * lax.rsqrt(jnp.mean(x * x, axis=-1, keepdims=True) + NORM_EPS) * g


def _params(sem):
    return pltpu.CompilerParams(dimension_semantics=sem, vmem_limit_bytes=VMEM_LIMIT)


def _t5_bucket(dist):
    dist = np.asarray(dist, dtype=np.int64)
    exact = N_BUCKETS // 2
    scaled = np.log(np.maximum(dist, 1) / exact) / np.log(MAX_DISTANCE / exact)
    large = np.minimum(exact + (scaled * (N_BUCKETS - exact)).astype(np.int64), N_BUCKETS - 1)
    return np.where(dist < exact, dist, large).astype(np.int32)


def _inproj_kernel(x_ref, g_ref, w_ref, qkv_ref, rkv_ref, lora_ref):
    xb = _rmsnorm(x_ref[...], g_ref[...]).astype(BF16)
    qkv_ref[...] = jnp.dot(xb, w_ref[:, 0:QKV_W], preferred_element_type=F32)
    rkv_ref[...] = jnp.dot(xb, w_ref[:, QKV_W:QKV_W + RKV_W], preferred_element_type=F32)
    lora_ref[...] = jnp.dot(xb, w_ref[:, QKV_W + RKV_W:PROJ_PAD], preferred_element_type=F32)


def _inproj(x2d, g, w, tm):
    m = x2d.shape[0]
    row = lambda i: (i, 0)
    fixed = lambda i: (0, 0)
    return pl.pallas_call(
        _inproj_kernel,
        grid=(m // tm,),
        in_specs=[pl.BlockSpec((tm, D_MODEL), row),
                  pl.BlockSpec((1, D_MODEL), fixed),
                  pl.BlockSpec((D_MODEL, PROJ_PAD), fixed)],
        out_specs=[pl.BlockSpec((tm, QKV_W), row),
                   pl.BlockSpec((tm, RKV_W), row),
                   pl.BlockSpec((tm, LORA_PAD), row)],
        out_shape=[jax.ShapeDtypeStruct((m, QKV_W), F32),
                   jax.ShapeDtypeStruct((m, RKV_W), F32),
                   jax.ShapeDtypeStruct((m, LORA_PAD), F32)],
        compiler_params=_params(("arbitrary",)),
        name="inproj",
    )(x2d, g, w)


def _prep_math(p, p_prev, l, l_prev, mu_p, mu_l, wcat, w0, a0, k_k, k_a, r_k, ones_bd):
    pm = p + mu_p * (p_prev - p)
    lm = l + mu_l * (l_prev - l)
    r = pm[:, 0:RWKV_WIDTH]
    kr = pm[:, RWKV_WIDTH:2 * RWKV_WIDTH]
    vr = pm[:, 2 * RWKV_WIDTH:3 * RWKV_WIDTH]
    lane = lax.broadcasted_iota(jnp.int32, lm.shape, 1)
    feat = jnp.where(lane < LORA_DECAY, jnp.tanh(lm),
                     jnp.where(lane < LORA_DECAY + LORA_ICLR, lm, _sigmoid(lm)))
    z = jnp.dot(feat.astype(BF16), wcat, preferred_element_type=F32)
    zd = -(w0 + z[:, 0:RWKV_WIDTH])
    softplus = jnp.maximum(zd, 0.0) + jnp.log(1.0 + jnp.exp(-jnp.abs(zd)))
    lw = -jnp.exp(-softplus - 0.5)
    a = _sigmoid(a0 + z[:, RWKV_WIDTH:2 * RWKV_WIDTH])
    gate = z[:, 2 * RWKV_WIDTH:3 * RWKV_WIDTH]
    kk = kr * k_k
    kk = kk / jnp.maximum(jnp.sqrt(_segsum(kk * kk, ones_bd)), 1e-12)
    k2 = kr * (1.0 + (a - 1.0) * k_a)
    bonus = _segsum(r * k2 * r_k, ones_bd) * vr
    return r, lw, k2, vr, kk, kk * a, gate, bonus


def _inproj_prep_kernel(x_ref, g_ref, w_ref, mu_p_ref, mu_l_ref, wcat_ref, w0_ref, a0_ref, kk_ref, ka_ref,
                        rk_ref, ones_ref, qkv_ref, *refs):
    outs, (sp_ref, sl_ref), (cp_ref, cl_ref) = refs[:8], refs[8:10], refs[10:]

    @pl.when(pl.program_id(1) == 0)
    def _():
        cp_ref[...] = jnp.zeros_like(cp_ref)
        cl_ref[...] = jnp.zeros_like(cl_ref)

    xb = _rmsnorm(x_ref[...], g_ref[...]).astype(BF16)
    p = jnp.dot(xb, w_ref[:, QKV_W:QKV_W + RKV_W], preferred_element_type=F32)
    l = jnp.dot(xb, w_ref[:, QKV_W + RKV_W:PROJ_PAD], preferred_element_type=F32)
    tm = p.shape[0]
    first_p = lax.broadcasted_iota(jnp.int32, p.shape, 0) == 0
    first_l = lax.broadcasted_iota(jnp.int32, l.shape, 0) == 0
    p_prev = jnp.where(first_p, cp_ref[...], pltpu.roll(p, 1, 0))
    l_prev = jnp.where(first_l, cl_ref[...], pltpu.roll(l, 1, 0))
    cp_ref[...] = p[tm - 1:tm, :]
    cl_ref[...] = l[tm - 1:tm, :]
    sp_ref[0] = p[tm - 1:tm, :]
    sl_ref[0] = l[tm - 1:tm, :]
    res = _prep_math(p, p_prev, l, l_prev, mu_p_ref[...], mu_l_ref[...], wcat_ref[...], w0_ref[...],
                     a0_ref[...], kk_ref[...], ka_ref[...], rk_ref[...], ones_ref[...])
    for o, v in zip(outs, res):
        o[...] = v
    qkv_ref[...] = jnp.dot(xb, w_ref[:, 0:QKV_W], preferred_element_type=F32)


def _prep_sample_kernel(p_ref, pp_ref, l_ref, lp_ref, mu_p_ref, mu_l_ref, wcat_ref, w0_ref, a0_ref,
                        kk_ref, ka_ref, rk_ref, ones_ref, *outs):
    res = _prep_math(p_ref[...], pp_ref[...], l_ref[...], lp_ref[...], mu_p_ref[...], mu_l_ref[...],
                     wcat_ref[...], w0_ref[...], a0_ref[...], kk_ref[...], ka_ref[...], rk_ref[...],
                     ones_ref[...])
    for o, v in zip(outs, res):
        o[...] = v


def _inproj_prep_prompt(x2d, g, w, pw, batch, seq, tm):
    nt = seq // tm
    row = lambda b, j: (b * nt + j, 0)
    fixed = lambda b, j: (0, 0)
    vec = lambda n: pl.BlockSpec((1, n), fixed)
    once = pl.Buffered(1)
    m = x2d.shape[0]
    last = lambda n: pl.BlockSpec((1, 1, n), lambda b, j: (b, 0, 0))
    return pl.pallas_call(
        _inproj_prep_kernel,
        grid=(batch, nt),
        in_specs=[pl.BlockSpec((tm, D_MODEL), row), vec(D_MODEL),
                  pl.BlockSpec((D_MODEL, PROJ_PAD), fixed, pipeline_mode=once),
                  vec(RKV_W), vec(LORA_PAD), pl.BlockSpec((LORA_PAD, RKV_W), fixed),
                  vec(RWKV_WIDTH), vec(RWKV_WIDTH), vec(RWKV_WIDTH), vec(RWKV_WIDTH), vec(RWKV_WIDTH),
                  pl.BlockSpec((RWKV_WIDTH, RWKV_WIDTH), fixed)],
        out_specs=[pl.BlockSpec((tm, QKV_W), row)] + [pl.BlockSpec((tm, RWKV_WIDTH), row)] * 8
                  + [last(RKV_W), last(LORA_PAD)],
        out_shape=[jax.ShapeDtypeStruct((m, QKV_W), F32)] + [jax.ShapeDtypeStruct((m, RWKV_WIDTH), F32)] * 8
                  + [jax.ShapeDtypeStruct((batch, 1, RKV_W), F32), jax.ShapeDtypeStruct((batch, 1, LORA_PAD), F32)],
        scratch_shapes=[pltpu.VMEM((1, RKV_W), F32), pltpu.VMEM((1, LORA_PAD), F32)],
        compiler_params=_params(("arbitrary", "arbitrary")),
        name="inproj_prep_prompt",
    )(x2d, g, w, *pw)


def _prep_sample(rkv, rkv_prev, lora, lora_prev, pw):
    m = rkv.shape[0]
    full = lambda a: pl.BlockSpec(a.shape, lambda i: (0,) * a.ndim)
    args = (rkv, rkv_prev, lora, lora_prev) + tuple(pw)
    return pl.pallas_call(
        _prep_sample_kernel,
        grid=(1,),
        in_specs=[full(a) for a in args],
        out_specs=[pl.BlockSpec((m, RWKV_WIDTH), lambda i: (0, 0))] * 8,
        out_shape=[jax.ShapeDtypeStruct((m, RWKV_WIDTH), F32)] * 8,
        compiler_params=_params(("arbitrary",)),
        name="rwkv_prep_sample",
    )(*args)


def _cumsum_rows(x, tri):
    tri_b = tri.astype(BF16)
    out = None
    for piece in _split3(x):
        t = jnp.dot(tri_b, piece, preferred_element_type=F32)
        out = t if out is None else out + t
    return out


def _wkv_kernel(r_ref, lw_ref, k_ref, v_ref, kk_ref, b_ref, gate_ref, bonus_ref, lnw_ref, lnb_ref,
                y_ref, s_ref, st_ref):
    c = pl.program_id(0)

    @pl.when(c == 0)
    def _():
        st_ref[...] = jnp.zeros_like(st_ref)

    nb = r_ref.shape[0]
    n = CHUNK
    row = lax.broadcasted_iota(jnp.int32, (n, n), 0)
    col = lax.broadcasted_iota(jnp.int32, (n, n), 1)
    eye = jnp.where(row == col, 1.0, 0.0)
    tri = jnp.where(row >= col, 1.0, 0.0)
    row2 = lax.broadcasted_iota(jnp.int32, (2 * n, 2 * n), 0)
    col2 = lax.broadcasted_iota(jnp.int32, (2 * n, 2 * n), 1) % n
    gram_mask = jnp.logical_or(row2 % n > col2, jnp.logical_and(row2 >= n, row2 % n == col2))
    zeros = jnp.zeros((n, HEAD_DIM), BF16)

    ah, bh, kh, rh, beh, keh, vh, ge = ([] for _ in range(8))
    for bi in range(nb):
        lw = lw_ref[bi]
        lg = _cumsum_rows(lw, tri)
        g_inv = jnp.exp(-lg)
        lg_end = lg[n - 1:n, :]
        g_to_end = jnp.exp(lg_end - lg)
        g_end = jnp.exp(lg_end)
        kk = kk_ref[bi]
        b = b_ref[bi]
        k = k_ref[bi]
        full = (-kk * jnp.exp(lg - lw), b * g_inv, k * g_inv, r_ref[bi] * jnp.exp(lg),
                b * g_to_end, k * g_to_end, v_ref[bi], g_end)
        for h in range(N_HEADS):
            sl = slice(h * HEAD_DIM, (h + 1) * HEAD_DIM)
            for dst, t in zip((ah, bh, kh, rh, beh, keh, vh), full[:7]):
                dst.append(t[:, sl].astype(BF16))
            ge.append(g_end[:, sl])
    units = range(nb * N_HEADS)

    stack = lambda xs, ys: [jnp.concatenate([x, y], axis=0) for x, y in zip(xs, ys)]
    gram = [jnp.where(gram_mask, _dot_nt(ar, bk), 0.0) for ar, bk in zip(stack(ah, rh), stack(bh, kh))]
    top = [g[:n].astype(BF16) for g in gram]
    bot = [g[n:].astype(BF16) for g in gram]
    lv = [_dot(top[u], jnp.concatenate([zeros, vh[u]], axis=0)) for u in units]
    tm_ = [eye + gram[u][:n, :n] for u in units]
    lp = [_dot(t[:, :n], t[:, :n]).astype(BF16) for t in top]
    span = 2
    while 2 * span < n:
        prod = [_dot(x, p) for x, p in zip(stack(lp, [t.astype(BF16) for t in tm_]), lp)]
        tm_ = [tm_[u] + prod[u][n:] for u in units]
        lp = [p[:n].astype(BF16) for p in prod]
        span *= 2
    tm_ = [(tm_[u] + _dot(tm_[u], lp[u])).astype(BF16) for u in units]
    a_hat = [_dot(tm_[u], ah[u]).astype(BF16) for u in units]
    u0 = [_dot(tm_[u], lv[u]) for u in units]
    s0 = [st_ref[u] for u in units]
    on_s0 = [_dot_nt(x, s.astype(BF16)) for x, s in zip(stack(a_hat, rh), s0)]
    uv = stack([(on_s0[u][:n] + u0[u]).astype(BF16) for u in units], vh)
    y = [on_s0[u][n:] + _dot(bot[u], uv[u]) for u in units]
    for u, be_ke in zip(units, stack(beh, keh)):
        st_ref[u] = s0[u] * ge[u] + _dot_tn(uv[u], be_ke)
    for bi in range(nb):
        mixed = []
        for h in range(N_HEADS):
            sl = slice(h * HEAD_DIM, (h + 1) * HEAD_DIM)
            yu = y[bi * N_HEADS + h]
            dy = yu - jnp.mean(yu, axis=-1, keepdims=True)
            yn = dy * lax.rsqrt(jnp.mean(dy * dy, axis=-1, keepdims=True) + GN_EPS)
            mixed.append((yn * lnw_ref[:, sl] + lnb_ref[:, sl] + bonus_ref[bi, :, sl]) * gate_ref[bi, :, sl])
        y_ref[bi] = jnp.concatenate(mixed, axis=1).astype(BF16)

    @pl.when(c == pl.num_programs(0) - 1)
    def _():
        s_ref[...] = st_ref[...]


def _wkv_prompt(r, lw, k2, v, kk, b, gate, bonus, ln_w, ln_b, batch, seq):
    nc = seq // CHUNK
    seqs = lambda t: t.reshape(batch, seq, RWKV_WIDTH)
    spec = pl.BlockSpec((batch, CHUNK, RWKV_WIDTH), lambda c: (0, c, 0))
    vec = pl.BlockSpec((1, RWKV_WIDTH), lambda c: (0, 0))
    nu = batch * N_HEADS
    y, s = pl.pallas_call(
        _wkv_kernel,
        grid=(nc,),
        in_specs=[spec] * 8 + [vec, vec],
        out_specs=[spec, pl.BlockSpec((nu, HEAD_DIM, HEAD_DIM), lambda c: (0, 0, 0))],
        out_shape=[jax.ShapeDtypeStruct((batch, seq, RWKV_WIDTH), BF16),
                   jax.ShapeDtypeStruct((nu, HEAD_DIM, HEAD_DIM), F32)],
        scratch_shapes=[pltpu.VMEM((nu, HEAD_DIM, HEAD_DIM), F32)],
        compiler_params=_params(("arbitrary",)),
        name="wkv_prompt",
    )(*(seqs(t) for t in (r, lw, k2, v, kk, b, gate, bonus)), ln_w, ln_b)
    return y.reshape(batch * seq, RWKV_WIDTH), s.reshape(batch, N_HEADS, HEAD_DIM, HEAD_DIM)


def _wkv_step_kernel(s_ref, lw_ref, kk_ref, b_ref, k_ref, r_ref, v_ref, y_ref, so_ref):
    s = s_ref[...]
    eye = _eye(HEAD_DIM)
    v_col = jnp.sum(eye * v_ref[...], axis=-1, keepdims=True)
    sa = jnp.sum(s * (-kk_ref[...]), axis=-1, keepdims=True)
    s = s * jnp.exp(lw_ref[...]) + sa * b_ref[...] + v_col * k_ref[...]
    so_ref[...] = s
    y_col = jnp.sum(s * r_ref[...], axis=-1, keepdims=True)
    y_ref[...] = jnp.sum(eye * y_col, axis=-2, keepdims=True)


def _wkv_sample(state, lw, kk, b, k2, r, v, bb):
    nb = state.shape[0]
    rowv = lambda t: t.reshape(nb, N_HEADS, 1, HEAD_DIM)
    idx = lambda i: (i, 0, 0, 0)
    s_spec = pl.BlockSpec((bb, N_HEADS, HEAD_DIM, HEAD_DIM), idx)
    r_spec = pl.BlockSpec((bb, N_HEADS, 1, HEAD_DIM), idx)
    y, s_new = pl.pallas_call(
        _wkv_step_kernel,
        grid=(nb // bb,),
        in_specs=[s_spec] + [r_spec] * 6,
        out_specs=[r_spec, s_spec],
        out_shape=[jax.ShapeDtypeStruct((nb, N_HEADS, 1, HEAD_DIM), F32),
                   jax.ShapeDtypeStruct(state.shape, F32)],
        compiler_params=_params(("arbitrary",)),
        name="wkv_sample",
    )(state, rowv(lw), rowv(kk), rowv(b), rowv(k2), rowv(r), rowv(v))
    return y.reshape(nb, RWKV_WIDTH), s_new


def _bias_tables():
    i = np.arange(Q_BLOCK)[:, None]
    c = np.arange(2 * Q_BLOCK)[None, :]
    n = i + Q_BLOCK - c
    ok = (n >= 0) & (n <= Q_BLOCK)
    prompt = np.stack([np.where(ok, _t5_bucket(d * np.clip(n, 0, Q_BLOCK)), -1) for d in DILATIONS])
    dist = WIN - np.arange(WIN)
    sample = np.stack([np.where((dist % d == 0) & (dist // d <= Q_BLOCK), _t5_bucket(dist), -1)
                       for d in DILATIONS])
    sample = np.broadcast_to(sample[:, None, :], (len(DILATIONS), N_HEADS, WIN))
    return prompt.astype(np.int32), sample.astype(np.int32)


def _bias_kernel(rb_ref, rbc_ref, bp_ref, bs_ref, op_ref, os_ref):
    bp = bp_ref[0]
    bs = bs_ref[0]
    acc_s = jnp.full(bs.shape, NEG, F32)
    for bk in range(N_BUCKETS):
        acc_s = jnp.where(bs == bk, rbc_ref[bk], acc_s)
    os_ref[0] = acc_s
    for h in range(N_HEADS):
        acc = jnp.full(bp.shape, NEG, F32)
        for bk in range(N_BUCKETS):
            acc = jnp.where(bp == bk, rb_ref[bk, h], acc)
        op_ref[0, h] = acc * LOG2E


def _bias_build(rel_bias):
    bp, bs = _bias_tables()
    rbc = rel_bias.reshape(N_BUCKETS, N_HEADS, 1)
    nb = len(DILATIONS)
    return pl.pallas_call(
        _bias_kernel,
        grid=(nb,),
        in_specs=[pl.BlockSpec(memory_space=pltpu.SMEM),
                  pl.BlockSpec((N_BUCKETS, N_HEADS, 1), lambda i: (0, 0, 0)),
                  pl.BlockSpec((1, Q_BLOCK, 2 * Q_BLOCK), lambda i: (i, 0, 0)),
                  pl.BlockSpec((1, N_HEADS, WIN), lambda i: (i, 0, 0))],
        out_specs=[pl.BlockSpec((1, N_HEADS, Q_BLOCK, 2 * Q_BLOCK), lambda i: (i, 0, 0, 0)),
                   pl.BlockSpec((1, N_HEADS, WIN), lambda i: (i, 0, 0))],
        out_shape=[jax.ShapeDtypeStruct((nb, N_HEADS, Q_BLOCK, 2 * Q_BLOCK), F32),
                   jax.ShapeDtypeStruct((nb, N_HEADS, WIN), F32)],
        compiler_params=_params(("arbitrary",)),
        name="bias_build",
    )(rel_bias, rbc, jnp.asarray(bp), jnp.asarray(bs)), rbc[0]


ATTN_GROUP = 4


def _attn_units():
    units = []
    for br, d in enumerate(DILATIONS):
        nblk = SUPER // (d * Q_BLOCK)
        for r in range(d):
            rows = lambda j, r=r, d=d: (pl.ds(r + d * Q_BLOCK * j, Q_BLOCK, stride=d) if d > 1
                                        else pl.ds(Q_BLOCK * j, Q_BLOCK))
            for blk in range(nblk):
                units.append((br, rows(blk), rows(blk - 1 if blk else nblk - 1), blk == 0))
    return units


def _attn_kernel(q_ref, kc_ref, kp_ref, vc_ref, vp_ref, bias_ref, o_ref, acc_ref, m_ref):
    has_prev = pl.program_id(2) > 0
    is_a = lax.broadcasted_iota(jnp.int32, (Q_BLOCK, 128), 1) < HEAD_DIM
    is_a2 = lax.broadcasted_iota(jnp.int32, (2 * Q_BLOCK, 128), 1) < HEAD_DIM
    live = jnp.logical_or(has_prev,
                          lax.broadcasted_iota(jnp.int32, (2 * Q_BLOCK, 2 * Q_BLOCK), 1) >= Q_BLOCK)
    units = _attn_units()
    for g0 in range(0, len(units), ATTN_GROUP):
        group = units[g0:g0 + ATTN_GROUP]
        ops = []
        for br, cur, prv, from_prev in group:
            qs = q_ref[cur, :] * (ATT_SCALE * LOG2E)
            q2 =jnp.concatenate([jnp.where(is_a, qs, 0.0), jnp.where(is_a, 0.0, qs)], axis=0)
            k2 = jnp.concatenate([(kp_ref if from_prev else kc_ref)[prv, :], kc_ref[cur, :]], axis=0)
            v2 = jnp.concatenate([(vp_ref if from_prev else vc_ref)[prv, :], vc_ref[cur, :]], axis=0)
            ops.append((br, cur, from_prev, q2.astype(BF16), k2.astype(BF16),
                        jnp.where(is_a2, v2, 1.0).astype(BF16), jnp.where(is_a2, 1.0, v2).astype(BF16)))
        logits = []
        for br, cur, from_prev, q2, k2, va, vb in ops:
            lg = lax.dot_general(q2, k2, NT, preferred_element_type=F32) + bias_ref[br]
            logits.append(jnp.where(live, lg, NEG) if from_prev else lg)
        ms = [jnp.max(lg, axis=-1, keepdims=True) for lg in logits]
        ps = [jnp.exp2(lg - m).astype(BF16) for lg, m in zip(logits, ms)]
        for (br, cur, _, _, _, va, vb), p, m in zip(ops, ps, ms):
            acc_ref[br, 0, cur, :] = jnp.dot(p[:Q_BLOCK], va, preferred_element_type=F32)
            acc_ref[br, 1, cur, :] = jnp.dot(p[Q_BLOCK:], vb, preferred_element_type=F32)
            m_ref[br, 0, cur, :] = jnp.broadcast_to(m[:Q_BLOCK], (Q_BLOCK, 128))
            m_ref[br, 1, cur, :] = jnp.broadcast_to(m[Q_BLOCK:], (Q_BLOCK, 128))

    for i in range(SUPER // Q_BLOCK):
        rs = pl.ds(i * Q_BLOCK, Q_BLOCK)
        outs = []
        for hd in range(2):
            ms = [m_ref[br, hd, rs, :] for br in range(len(DILATIONS))]
            m = functools.reduce(jnp.maximum, ms)
            tot = 0.0
            for br in range(len(DILATIONS)):
                tot = tot + jnp.exp2(ms[br] - m) * acc_ref[br, hd, rs, :]
            outs.append(tot / pltpu.roll(tot, HEAD_DIM, 1))
        o_ref[rs, :] = jnp.where(is_a, outs[0], outs[1]).astype(o_ref.dtype)


def _attn_prompt(qkv, bias_p, batch, seq):
    ns = seq // SUPER
    npair = ATT_WIDTH // 128
    blk = lambda col0, prev: pl.BlockSpec(
        (SUPER, 128),
        (lambda b, hp, sb: (b * ns + jnp.maximum(sb - 1, 0), col0 + hp)) if prev
        else (lambda b, hp, sb: (b * ns + sb, col0 + hp)))
    nb = len(DILATIONS)
    return pl.pallas_call(
        _attn_kernel,
        grid=(batch, npair, ns),
        in_specs=[blk(0, False), blk(npair, False), blk(npair, True), blk(2 * npair, False),
                  blk(2 * npair, True),
                  pl.BlockSpec((nb, None, 2 * Q_BLOCK, 2 * Q_BLOCK), lambda b, hp, sb: (0, hp, 0, 0))],
        out_specs=pl.BlockSpec((SUPER, 128), lambda b, hp, sb: (b * ns + sb, hp)),
        out_shape=jax.ShapeDtypeStruct((batch * seq, ATT_WIDTH), BF16),
        scratch_shapes=[pltpu.VMEM((nb, 2, SUPER, 128), F32)] * 2,
        compiler_params=_params(("arbitrary", "arbitrary", "arbitrary")),
        name="attn_prompt",
    )(qkv, qkv, qkv, qkv, qkv, bias_p.reshape(nb, npair, 2 * Q_BLOCK, 2 * Q_BLOCK))


WINDOW_HEADS = 4


def _window_body(q_ref, kn_ref, vn_ref, ck_ref, cv_ref, bias_ref, rb0_ref, o_ref, ok_ref, ov_ref, wt_ref):
    nh = WINDOW_HEADS
    lane = lax.broadcasted_iota(jnp.int32, (HEAD_DIM, WIN), 1)
    last = lane == WIN - 1
    eye = _eye(HEAD_DIM)
    col = lambda ref, h: jnp.sum(eye * ref[h:h + 1, :], axis=-1, keepdims=True)
    l_new = jnp.sum(kn_ref[...] * (q_ref[...] * ATT_SCALE), axis=-1, keepdims=True) + rb0_ref[...]
    rows = []
    for h in range(nh):
        kh = ck_ref[h]
        rows.append(jnp.sum(kh * (col(q_ref, h) * ATT_SCALE), axis=0, keepdims=True))
        ok_ref[h] = jnp.where(last, col(kn_ref, h), pltpu.roll(kh, WIN - 1, 1))
    logits = jnp.concatenate(rows, axis=0)
    parts = []
    for br in range(len(DILATIONS)):
        lg = logits + bias_ref[br]
        m = jnp.maximum(jnp.max(lg, axis=-1, keepdims=True), l_new)
        p = jnp.exp(lg - m)
        p_new = jnp.exp(l_new - m)
        parts.append((p, p_new, m, jnp.sum(p, axis=-1, keepdims=True) + p_new))
    m = functools.reduce(jnp.maximum, [pt[2] for pt in parts])
    wt = 0.0
    w_new = 0.0
    den = 0.0
    for p, p_new, m_g, s_g in parts:
        w = jnp.exp(m_g - m)
        wt = wt + w * p
        w_new = w_new + w * p_new
        den = den + w * s_g
    wt_ref[0:nh, :] = wt / den
    w_new = w_new / den
    for h in range(nh):
        vh = cv_ref[h]
        v_new = col(vn_ref, h)
        o_col = jnp.sum(vh * wt_ref[h:h + 1, :], axis=1, keepdims=True) + v_new * w_new[h:h + 1, :]
        o_ref[h:h + 1, :] = jnp.sum(eye * o_col, axis=0, keepdims=True)
        ov_ref[h] = jnp.where(last, v_new, pltpu.roll(vh, WIN - 1, 1))


N_WINDOW_IN = 7


def _window_operands(q, k_new, v_new, cache_k, cache_v, bias_s, rb0, step_of):
    nb = q.shape[0]
    nh = WINDOW_HEADS
    groups = N_HEADS // nh
    seq_grp = lambda *g: (step_of(*g) // groups, step_of(*g) % groups)
    rows = lambda t: t.reshape(nb, groups, nh, HEAD_DIM)
    wins = lambda t: t.reshape(nb, groups, nh, HEAD_DIM, WIN)
    row_spec = pl.BlockSpec((None, None, nh, HEAD_DIM), lambda *g: seq_grp(*g) + (0, 0))
    win_spec = pl.BlockSpec((None, None, nh, HEAD_DIM, WIN), lambda *g: seq_grp(*g) + (0, 0, 0))
    nbr = len(DILATIONS)
    bias_spec = pl.BlockSpec((nbr, None, nh, WIN), lambda *g: (0, seq_grp(*g)[1], 0, 0))
    rb0_spec = pl.BlockSpec((None, nh, 1), lambda *g: (seq_grp(*g)[1], 0, 0))
    args = (rows(q), rows(k_new), rows(v_new), wins(cache_k), wins(cache_v),
            bias_s.reshape(nbr, groups, nh, WIN), rb0.reshape(groups, nh, 1))
    in_specs = [row_spec, row_spec, row_spec, win_spec, win_spec, bias_spec, rb0_spec]
    assert len(args) == len(in_specs) == N_WINDOW_IN
    out_specs = [row_spec, win_spec, win_spec]
    out_shape = [jax.ShapeDtypeStruct((nb, groups, nh, HEAD_DIM), F32),
                 jax.ShapeDtypeStruct((nb, groups, nh, HEAD_DIM, WIN), F32),
                 jax.ShapeDtypeStruct((nb, groups, nh, HEAD_DIM, WIN), F32)]
    return args, in_specs, out_specs, out_shape


def _outproj_kernel(x_ref, attn_ref, y_ref, gate_ref, bonus_ref, lnw_ref, lnb_ref, wo_ref, gffn_ref,
                    ones_ref, h_ref, xn_ref):
    ones_bd = ones_ref[...]
    y = y_ref[...]
    mu = _segsum(y, ones_bd) * (1.0 / HEAD_DIM)
    dy = y - mu
    var = _segsum(dy * dy, ones_bd) * (1.0 / HEAD_DIM)
    yn = dy * lax.rsqrt(var + GN_EPS) * lnw_ref[...] + lnb_ref[...]
    rw = (yn + bonus_ref[...]) * gate_ref[...]
    mixed = (jnp.dot(attn_ref[...].astype(BF16), wo_ref[0:ATT_WIDTH, :], preferred_element_type=F32)
             + jnp.dot(rw.astype(BF16), wo_ref[ATT_WIDTH:D_MODEL, :], preferred_element_type=F32))
    h = x_ref[...] + mixed
    h_ref[...] = h
    xn_ref[...] = _rmsnorm(h, gffn_ref[...]).astype(BF16)


def _outproj(x2d, attn, y, gate, bonus, ln_w, ln_b, w_o, g_ffn, ones_bd, tm):
    m = x2d.shape[0]
    row = lambda i: (i, 0)
    fixed = lambda i: (0, 0)
    half = pl.BlockSpec((tm, RWKV_WIDTH), row)
    full = pl.BlockSpec((tm, D_MODEL), row)
    return pl.pallas_call(
        _outproj_kernel,
        grid=(m // tm,),
        in_specs=[full, half, half, half, half,
                  pl.BlockSpec((1, RWKV_WIDTH), fixed), pl.BlockSpec((1, RWKV_WIDTH), fixed),
                  pl.BlockSpec((D_MODEL, D_MODEL), fixed), pl.BlockSpec((1, D_MODEL), fixed),
                  pl.BlockSpec((RWKV_WIDTH, RWKV_WIDTH), fixed)],
        out_specs=[full, full],
        out_shape=[jax.ShapeDtypeStruct((m, D_MODEL), F32), jax.ShapeDtypeStruct((m, D_MODEL), BF16)],
        compiler_params=_params(("arbitrary",)),
        name="outproj",
    )(x2d, attn, y, gate, bonus, ln_w, ln_b, w_o, g_ffn, ones_bd)


def _ffn_tail(xn, h, gp_m1, gp_m2, gp_of, wi_ref, cw_ref, cb_ref, wout_ref, gfin_ref, o_ref):
    acc = None
    for c in range(D_FF // FF_CHUNK):
        cs = slice(c * FF_CHUNK, (c + 1) * FF_CHUNK)
        gp = gp_of(c)
        up = jnp.dot(xn, wi_ref[:, D_FF + c * FF_CHUNK:D_FF + (c + 1) * FF_CHUNK], preferred_element_type=F32)
        conv = (cb_ref[:, cs] + cw_ref[0:1, cs] * gp_m2(c, gp) + cw_ref[1:2, cs] * gp_m1(c, gp)
                + cw_ref[2:3, cs] * gp)
        act = conv * _sigmoid(conv) * up
        part = jnp.dot(act.astype(BF16), wout_ref[cs, :], preferred_element_type=F32)
        acc = part if acc is None else acc + part
    o_ref[...] = _rmsnorm(h + acc, gfin_ref[...])


def _ffn_prompt_kernel(x_ref, attn_ref, rw_ref, wo_ref, gffn_ref, wi_ref, cw_ref, cb_ref, wout_ref,
                       gfin_ref, *refs):
    n = N_WINDOW_IN
    win_in, (o_ref, conv_ref), win_out, (carry_ref, wt_ref) = refs[:n], refs[n:n + 2], refs[n + 2:n + 5], refs[n + 5:]

    @pl.when(pl.program_id(1) == 0)
    def _():
        carry_ref[...] = jnp.zeros_like(carry_ref)

    _window_body(*win_in, *win_out, wt_ref)

    h = (x_ref[...]
         + jnp.dot(attn_ref[...], wo_ref[0:ATT_WIDTH, :], preferred_element_type=F32)
         + jnp.dot(rw_ref[...], wo_ref[ATT_WIDTH:D_MODEL, :], preferred_element_type=F32))
    xn = _rmsnorm(h, gffn_ref[...]).astype(BF16)
    tm = xn.shape[0]
    rowi = lax.broadcasted_iota(jnp.int32, (tm, FF_CHUNK), 0)
    gps = {}

    def gp_of(c):
        cs = slice(c * FF_CHUNK, (c + 1) * FF_CHUNK)
        gps[c] = jnp.dot(xn, wi_ref[:, cs], preferred_element_type=F32)
        return gps[c]

    def gp_m1(c, gp):
        cs = slice(c * FF_CHUNK, (c + 1) * FF_CHUNK)
        return jnp.where(rowi == 0, carry_ref[1:2, cs], pltpu.roll(gp, 1, 0))

    def gp_m2(c, gp):
        cs = slice(c * FF_CHUNK, (c + 1) * FF_CHUNK)
        return jnp.where(rowi == 0, carry_ref[0:1, cs],
                         jnp.where(rowi == 1, carry_ref[1:2, cs], pltpu.roll(gp, 2, 0)))

    _ffn_tail(xn, h, gp_m1, gp_m2, gp_of, wi_ref, cw_ref, cb_ref, wout_ref, gfin_ref, o_ref)
    for c, gp in gps.items():
        cs = slice(c * FF_CHUNK, (c + 1) * FF_CHUNK)
        carry_ref[:, cs] = gp[tm - 2:tm, :]
        conv_ref[0, :, cs] = gp[tm - 2:tm, :]


def _ffn_sample_kernel(xn_ref, h_ref, p1_ref, p2_ref, wi_ref, cw_ref, cb_ref, wout_ref, gfin_ref,
                       o_ref, gp_ref):
    xn = xn_ref[...]

    def gp_of(c):
        cs = slice(c * FF_CHUNK, (c + 1) * FF_CHUNK)
        gp = jnp.dot(xn, wi_ref[:, cs], preferred_element_type=F32)
        gp_ref[:, cs] = gp
        return gp

    gp_m1 = lambda c, gp: p1_ref[:, c * FF_CHUNK:(c + 1) * FF_CHUNK]
    gp_m2 = lambda c, gp: p2_ref[:, c * FF_CHUNK:(c + 1) * FF_CHUNK]
    _ffn_tail(xn, h_ref[...], gp_m1, gp_m2, gp_of, wi_ref, cw_ref, cb_ref, wout_ref, gfin_ref, o_ref)


def _ffn_weight_specs(fixed):
    once = pl.Buffered(1)
    return [pl.BlockSpec((D_MODEL, 2 * D_FF), fixed, pipeline_mode=once),
            pl.BlockSpec((CONV_W, D_FF), fixed), pl.BlockSpec((1, D_FF), fixed),
            pl.BlockSpec((D_FF, D_MODEL), fixed, pipeline_mode=once), pl.BlockSpec((1, D_MODEL), fixed)]


def _ffn_prompt(x2d, attn, rw, w_o, g_ffn, fw, batch, seq, window):
    nt = window[0].shape[0] * (N_HEADS // WINDOW_HEADS) // batch
    tm = seq // nt
    assert tm * nt == seq and tm % 16 == 0
    row = lambda b, j: (b * nt + j, 0)
    fixed = lambda b, j: (0, 0)
    full = pl.BlockSpec((tm, D_MODEL), row)
    half = pl.BlockSpec((tm, ATT_WIDTH), row)
    mix_specs = [full, half, half, pl.BlockSpec((D_MODEL, D_MODEL), fixed), pl.BlockSpec((1, D_MODEL), fixed)]
    w_args, w_in_specs, w_out_specs, w_out_shape = _window_operands(*window, lambda b, j: b * nt + j)
    y, conv, attn_s, win_k, win_v = pl.pallas_call(
        _ffn_prompt_kernel,
        grid=(batch, nt),
        in_specs=mix_specs + _ffn_weight_specs(fixed) + w_in_specs,
        out_specs=[full, pl.BlockSpec((1, CONV_W - 1, D_FF), lambda b, j: (b, 0, 0))] + w_out_specs,
        out_shape=[jax.ShapeDtypeStruct((batch * seq, D_MODEL), F32),
                   jax.ShapeDtypeStruct((batch, CONV_W - 1, D_FF), F32)] + w_out_shape,
        scratch_shapes=[pltpu.VMEM((CONV_W - 1, D_FF), F32), pltpu.VMEM((8, WIN), F32)],
        compiler_params=_params(("arbitrary", "arbitrary")),
        name="ffn_prompt",
    )(x2d, attn, rw, w_o, g_ffn, *fw, *w_args)
    nb = window[0].shape[0]
    return (y, conv, attn_s.reshape(nb, ATT_WIDTH), win_k.reshape(nb, N_HEADS, HEAD_DIM, WIN),
            win_v.reshape(nb, N_HEADS, HEAD_DIM, WIN))


def _ffn_sample(xn, h, prev1, prev2, fw):
    m = xn.shape[0]
    fixed = lambda i: (0, 0)
    full = pl.BlockSpec((m, D_MODEL), fixed)
    ffs = pl.BlockSpec((m, D_FF), fixed)
    return pl.pallas_call(
        _ffn_sample_kernel,
        grid=(1,),
        in_specs=[full, full, ffs, ffs] + _ffn_weight_specs(fixed),
        out_specs=[full, ffs],
        out_shape=[jax.ShapeDtypeStruct((m, D_MODEL), F32), jax.ShapeDtypeStruct((m, D_FF), F32)],
        compiler_params=_params(("arbitrary",)),
        name="ffn_sample",
    )(xn, h, prev1, prev2, *fw)


def _layer_weights(g_mix, w_in, tok_mu, w0, w_decay_up, a0, w_iclr_up, w_gate_up, k_k, k_a, r_k,
                   ln_x_w, ln_x_b, w_o, g_ffn, w_ffn_in, conv_w, conv_b, w_ffn_out, g_final):
    vec = lambda t: t.reshape(1, -1)
    pad = LORA_PAD - LORA_W
    w_cat = jnp.pad(w_in, ((0, 0), (0, pad))).astype(BF16)
    mu_p = vec(tok_mu[:RKV_W])
    mu_l = vec(jnp.pad(tok_mu[RKV_W:], (0, pad)))
    lora_up = jnp.zeros((LORA_PAD, RKV_W), F32)
    lora_up = lora_up.at[0:LORA_DECAY, 0:RWKV_WIDTH].set(w_decay_up)
    lora_up = lora_up.at[LORA_DECAY:LORA_DECAY + LORA_ICLR, RWKV_WIDTH:2 * RWKV_WIDTH].set(w_iclr_up)
    lora_up = lora_up.at[LORA_DECAY + LORA_ICLR:LORA_W, 2 * RWKV_WIDTH:].set(w_gate_up)
    seg = np.arange(RWKV_WIDTH) // HEAD_DIM
    ones_bd = jnp.asarray(seg[:, None] == seg[None, :], BF16)
    prep = (mu_p, mu_l, lora_up.astype(BF16), vec(w0), vec(a0), vec(k_k), vec(k_a), vec(r_k), ones_bd)
    ffn = (w_ffn_in.astype(BF16), conv_w, vec(conv_b),
           w_ffn_out.astype(BF16), vec(g_final))
    return dict(g_mix=vec(g_mix), w_cat=w_cat, prep=prep, ones_bd=ones_bd, ln_w=vec(ln_x_w), ln_b=vec(ln_x_b),
                w_o=w_o.astype(BF16), g_ffn=vec(g_ffn), ffn=ffn)


def kernel(x_prompt, x_sample, cache_win_k, cache_win_v, state_shift, state_wkv, state_ffn_conv, g_mix, w_in, rel_bias, tok_mu, w0, w_decay_up, a0, w_iclr_up, w_gate_up, k_k, k_a, r_k, ln_x_w, ln_x_b, w_o, g_ffn, w_ffn_in, conv_w, conv_b, w_ffn_out, g_final):
    batch, seq, _ = x_prompt.shape
    nb = x_sample.shape[0]
    lw = _layer_weights(g_mix[0], w_in[0], tok_mu[0], w0[0], w_decay_up[0], a0[0], w_iclr_up[0],
                        w_gate_up[0], k_k[0], k_a[0], r_k[0].reshape(-1), ln_x_w[0], ln_x_b[0], w_o[0],
                        g_ffn[0], w_ffn_in[0], conv_w[0], conv_b[0], w_ffn_out[0], g_final)
    (bias_p, bias_s), rb0 = _bias_build(rel_bias)

    xp = x_prompt.reshape(batch * seq, D_MODEL)
    qkv, r, lgw, k2, v, kk, b, gate, bonus, sh_p, sh_l = _inproj_prep_prompt(
        xp, lw['g_mix'], lw['w_cat'], lw['prep'], batch, seq, 512)
    rw, wkv_p = _wkv_prompt(r, lgw, k2, v, kk, b, gate, bonus, lw['ln_w'], lw['ln_b'], batch, seq)
    attn = _attn_prompt(qkv, bias_p, batch, seq)
    qkv3 = qkv.reshape(batch, seq, QKV_W)
    win_k_p = qkv3[:, seq - WIN:, ATT_WIDTH:2 * ATT_WIDTH].reshape(1, batch, WIN, N_HEADS, HEAD_DIM)
    win_v_p = qkv3[:, seq - WIN:, 2 * ATT_WIDTH:].reshape(1, batch, WIN, N_HEADS, HEAD_DIM)
    shift_p = jnp.concatenate([sh_p[:, 0], sh_l[:, 0, :LORA_W]], axis=-1)[None]

    xs = x_sample.reshape(nb, D_MODEL)
    qkv_s, rkv_s, lora_s = _inproj(xs, lw['g_mix'], lw['w_cat'], nb)
    sh = state_shift[0]
    r, lgw, k2, v, kk, b, gate_s, bonus_s = _prep_sample(
        rkv_s, sh[:, :RKV_W], lora_s, jnp.pad(sh[:, RKV_W:], ((0, 0), (0, LORA_PAD - LORA_W))), lw['prep'])
    y_s, wkv_s = _wkv_sample(state_wkv[0], lgw, kk, b, k2, r, v, 8)
    q_s = qkv_s[:, :ATT_WIDTH]
    k_s = qkv_s[:, ATT_WIDTH:2 * ATT_WIDTH]
    v_s = qkv_s[:, 2 * ATT_WIDTH:]
    ck = jnp.transpose(cache_win_k[0], (0, 2, 3, 1))
    cv = jnp.transpose(cache_win_v[0], (0, 2, 3, 1))

    y_p, conv_p, attn_s, win_k_s, win_v_s = _ffn_prompt(xp, attn, rw, lw['w_o'], lw['g_ffn'], lw['ffn'],
                                                        batch, seq, (q_s, k_s, v_s, ck, cv, bias_s, rb0))

    win_k_s = jnp.transpose(win_k_s, (0, 3, 1, 2))
    win_v_s = jnp.transpose(win_v_s, (0, 3, 1, 2))
    h1_s, xn2_s = _outproj(xs, attn_s, y_s, gate_s, bonus_s, lw['ln_w'], lw['ln_b'], lw['w_o'], lw['g_ffn'],
                           lw['ones_bd'], nb)
    conv_state = state_ffn_conv[0]
    y_smp, gp_s = _ffn_sample(xn2_s, h1_s, conv_state[:, 1], conv_state[:, 0], lw['ffn'])
    conv_s = jnp.stack([conv_state[:, 1], gp_s], axis=1)
    shift_s = jnp.concatenate([rkv_s, lora_s[:, :LORA_W]], axis=-1)[None]

    return (y_p.reshape(batch, seq, D_MODEL), y_smp.reshape(nb, 1, D_MODEL),
            win_k_p, win_v_p, shift_p, wkv_p[None], conv_p[None],
            win_k_s[None], win_v_s[None], shift_s, wkv_s[None], conv_s[None])
```

```python
import functools

import numpy as np
import jax
import jax.numpy as jnp
from jax import lax
from jax.experimental import pallas as pl
from jax.experimental.pallas import tpu as pltpu

F32 = jnp.float32
BF16 = jnp.bfloat16

D_MODEL = 1024
HEAD_DIM = 64
ATT_WIDTH = 512
RWKV_WIDTH = 512
N_HEADS = 8
QKV_W = 3 * ATT_WIDTH
RKV_W = 3 * RWKV_WIDTH
LORA_DECAY, LORA_ICLR, LORA_GATE = 32, 32, 96
LORA_W = LORA_DECAY + LORA_ICLR + LORA_GATE
LORA_PAD = 256
RWKV_PROJ = RKV_W + LORA_W
PROJ_PAD = QKV_W + RKV_W + LORA_PAD
D_FF = 2816
FF_CHUNK = 1408
CONV_W = 3
WIN = 2048
DILATIONS = (1, 4, 16)
Q_BLOCK = 128
SUPER = 2048
N_BUCKETS = 32
MAX_DISTANCE = 2048
NORM_EPS = 1e-6
GN_EPS = 64e-5
NEG = -1e30
ATT_SCALE = HEAD_DIM ** -0.5
LOG2E = float(np.log2(np.e))
CHUNK = 64
WKV_CHUNKS_PER_STEP = 2
VMEM_LIMIT = 56 * 1024 * 1024

NT = (((1,), (1,)), ((), ()))
TN = (((0,), (0,)), ((), ()))


def _dot(a, b):
    return jnp.dot(a.astype(BF16), b.astype(BF16), preferred_element_type=F32)


def _dot_nt(a, b):
    return lax.dot_general(a.astype(BF16), b.astype(BF16), NT, preferred_element_type=F32)


def _dot_tn(a, b):
    return lax.dot_general(a.astype(BF16), b.astype(BF16), TN, preferred_element_type=F32)


def _split2(x):
    hi = x.astype(BF16)
    lo = (x - hi.astype(F32)).astype(BF16)
    return hi, lo


def _split3(x):
    hi = x.astype(BF16)
    r1 = x - hi.astype(F32)
    mid = r1.astype(BF16)
    lo = (r1 - mid.astype(F32)).astype(BF16)
    return hi, mid, lo


def _segsum(x, ones_bd):
    hi, lo = _split2(x)
    return (jnp.dot(hi, ones_bd, preferred_element_type=F32)
            + jnp.dot(lo, ones_bd, preferred_element_type=F32))


def _sigmoid(x):
    return 1.0 / (1.0 + jnp.exp(-x))


def _eye(n):
    return jnp.where(lax.broadcasted_iota(jnp.int32, (n, n), 0) == lax.broadcasted_iota(jnp.int32, (n, n), 1),
                     1.0, 0.0)


def _rmsnorm(x, g):
    return x * lax.rsqrt(jnp.mean(x * x, axis=-1, keepdims=True) + NORM_EPS) * g


def _params(sem):
    return pltpu.CompilerParams(dimension_semantics=sem, vmem_limit_bytes=VMEM_LIMIT)


def _t5_bucket(dist):
    dist = np.asarray(dist, dtype=np.int64)
    exact = N_BUCKETS // 2
    scaled = np.log(np.maximum(dist, 1) / exact) / np.log(MAX_DISTANCE / exact)
    large = np.minimum(exact + (scaled * (N_BUCKETS - exact)).astype(np.int64), N_BUCKETS - 1)
    return np.where(dist < exact, dist, large).astype(np.int32)


def _inproj_kernel(x_ref, g_ref, w_ref, qkv_ref, rkv_ref, lora_ref):
    xb = _rmsnorm(x_ref[...], g_ref[...]).astype(BF16)
    qkv_ref[...] = jnp.dot(xb, w_ref[:, 0:QKV_W], preferred_element_type=F32)
    rkv_ref[...] = jnp.dot(xb, w_ref[:, QKV_W:QKV_W + RKV_W], preferred_element_type=F32)
    lora_ref[...] = jnp.dot(xb, w_ref[:, QKV_W + RKV_W:PROJ_PAD], preferred_element_type=F32)


def _inproj(x2d, g, w, tm):
    m = x2d.shape[0]
    row = lambda i: (i, 0)
    fixed = lambda i: (0, 0)
    return pl.pallas_call(
        _inproj_kernel,
        grid=(m // tm,),
        in_specs=[pl.BlockSpec((tm, D_MODEL), row),
                  pl.BlockSpec((1, D_MODEL), fixed),
                  pl.BlockSpec((D_MODEL, PROJ_PAD), fixed)],
        out_specs=[pl.BlockSpec((tm, QKV_W), row),
                   pl.BlockSpec((tm, RKV_W), row),
                   pl.BlockSpec((tm, LORA_PAD), row)],
        out_shape=[jax.ShapeDtypeStruct((m, QKV_W), F32),
                   jax.ShapeDtypeStruct((m, RKV_W), F32),
                   jax.ShapeDtypeStruct((m, LORA_PAD), F32)],
        compiler_params=_params(("arbitrary",)),
        name="inproj",
    )(x2d, g, w)


def _prep_math(p, p_prev, l, l_prev, mu_p, mu_l, wcat, w0, a0, k_k, k_a, r_k, ones_bd):
    pm = p + mu_p * (p_prev - p)
    lm = l + mu_l * (l_prev - l)
    r = pm[:, 0:RWKV_WIDTH]
    kr = pm[:, RWKV_WIDTH:2 * RWKV_WIDTH]
    vr = pm[:, 2 * RWKV_WIDTH:3 * RWKV_WIDTH]
    lane = lax.broadcasted_iota(jnp.int32, lm.shape, 1)
    feat = jnp.where(lane < LORA_DECAY, jnp.tanh(lm),
                     jnp.where(lane < LORA_DECAY + LORA_ICLR, lm, _sigmoid(lm)))
    z = jnp.dot(feat.astype(BF16), wcat, preferred_element_type=F32)
    zd = -(w0 + z[:, 0:RWKV_WIDTH])
    softplus = jnp.maximum(zd, 0.0) + jnp.log(1.0 + jnp.exp(-jnp.abs(zd)))
    lw = -jnp.exp(-softplus - 0.5)
    a = _sigmoid(a0 + z[:, RWKV_WIDTH:2 * RWKV_WIDTH])
    gate = z[:, 2 * RWKV_WIDTH:3 * RWKV_WIDTH]
    kk = kr * k_k
    kk = kk / jnp.maximum(jnp.sqrt(_segsum(kk * kk, ones_bd)), 1e-12)
    k2 = kr * (1.0 + (a - 1.0) * k_a)
    bonus = _segsum(r * k2 * r_k, ones_bd) * vr
    return r, lw, k2, vr, kk, kk * a, gate, bonus


def _inproj_prep_kernel(x_ref, g_ref, w_ref, mu_p_ref, mu_l_ref, wcat_ref, w0_ref, a0_ref, kk_ref, ka_ref,
                        rk_ref, ones_ref, qkv_ref, *refs):
    outs, (sp_ref, sl_ref), (cp_ref, cl_ref) = refs[:8], refs[8:10], refs[10:]

    @pl.when(pl.program_id(1) == 0)
    def _():
        cp_ref[...] = jnp.zeros_like(cp_ref)
        cl_ref[...] = jnp.zeros_like(cl_ref)

    xb = _rmsnorm(x_ref[...], g_ref[...]).astype(BF16)
    p = jnp.dot(xb, w_ref[:, QKV_W:QKV_W + RKV_W], preferred_element_type=F32)
    l = jnp.dot(xb, w_ref[:, QKV_W + RKV_W:PROJ_PAD], preferred_element_type=F32)
    tm = p.shape[0]
    first_p = lax.broadcasted_iota(jnp.int32, p.shape, 0) == 0
    first_l = lax.broadcasted_iota(jnp.int32, l.shape, 0) == 0
    p_prev = jnp.where(first_p, cp_ref[...], pltpu.roll(p, 1, 0))
    l_prev = jnp.where(first_l, cl_ref[...], pltpu.roll(l, 1, 0))
    cp_ref[...] = p[tm - 1:tm, :]
    cl_ref[...] = l[tm - 1:tm, :]
    sp_ref[0] = p[tm - 1:tm, :]
    sl_ref[0] = l[tm - 1:tm, :]
    res = _prep_math(p, p_prev, l, l_prev, mu_p_ref[...], mu_l_ref[...], wcat_ref[...], w0_ref[...],
                     a0_ref[...], kk_ref[...], ka_ref[...], rk_ref[...], ones_ref[...])
    for o, v in zip(outs, res):
        o[...] = v
    qkv_ref[...] = jnp.dot(xb, w_ref[:, 0:QKV_W], preferred_element_type=F32)


def _prep_sample_kernel(p_ref, pp_ref, l_ref, lp_ref, mu_p_ref, mu_l_ref, wcat_ref, w0_ref, a0_ref,
                        kk_ref, ka_ref, rk_ref, ones_ref, *outs):
    res = _prep_math(p_ref[...], pp_ref[...], l_ref[...], lp_ref[...], mu_p_ref[...], mu_l_ref[...],
                     wcat_ref[...], w0_ref[...], a0_ref[...], kk_ref[...], ka_ref[...], rk_ref[...],
                     ones_ref[...])
    for o, v in zip(outs, res):
        o[...] = v


def _inproj_prep_prompt(x2d, g, w, pw, batch, seq, tm):
    nt = seq // tm
    row = lambda b, j: (b * nt + j, 0)
    fixed = lambda b, j: (0, 0)
    vec = lambda n: pl.BlockSpec((1, n), fixed)
    once = pl.Buffered(1)
    m = x2d.shape[0]
    last = lambda n: pl.BlockSpec((1, 1, n), lambda b, j: (b, 0, 0))
    return pl.pallas_call(
        _inproj_prep_kernel,
        grid=(batch, nt),
        in_specs=[pl.BlockSpec((tm, D_MODEL), row), vec(D_MODEL),
                  pl.BlockSpec((D_MODEL, PROJ_PAD), fixed, pipeline_mode=once),
                  vec(RKV_W), vec(LORA_PAD), pl.BlockSpec((LORA_PAD, RKV_W), fixed),
                  vec(RWKV_WIDTH), vec(RWKV_WIDTH), vec(RWKV_WIDTH), vec(RWKV_WIDTH), vec(RWKV_WIDTH),
                  pl.BlockSpec((RWKV_WIDTH, RWKV_WIDTH), fixed)],
        out_specs=[pl.BlockSpec((tm, QKV_W), row)] + [pl.BlockSpec((tm, RWKV_WIDTH), row)] * 8
                  + [last(RKV_W), last(LORA_PAD)],
        out_shape=[jax.ShapeDtypeStruct((m, QKV_W), F32)] + [jax.ShapeDtypeStruct((m, RWKV_WIDTH), F32)] * 8
                  + [jax.ShapeDtypeStruct((batch, 1, RKV_W), F32), jax.ShapeDtypeStruct((batch, 1, LORA_PAD), F32)],
        scratch_shapes=[pltpu.VMEM((1, RKV_W), F32), pltpu.VMEM((1, LORA_PAD), F32)],
        compiler_params=_params(("arbitrary", "arbitrary")),
        name="inproj_prep_prompt",
    )(x2d, g, w, *pw)


def _prep_sample(rkv, rkv_prev, lora, lora_prev, pw):
    m = rkv.shape[0]
    full = lambda a: pl.BlockSpec(a.shape, lambda i: (0,) * a.ndim)
    args = (rkv, rkv_prev, lora, lora_prev) + tuple(pw)
    return pl.pallas_call(
        _prep_sample_kernel,
        grid=(1,),
        in_specs=[full(a) for a in args],
        out_specs=[pl.BlockSpec((m, RWKV_WIDTH), lambda i: (0, 0))] * 8,
        out_shape=[jax.ShapeDtypeStruct((m, RWKV_WIDTH), F32)] * 8,
        compiler_params=_params(("arbitrary",)),
        name="rwkv_prep_sample",
    )(*args)


def _cumsum_rows(x, tri):
    tri_b = tri.astype(BF16)
    out = None
    for piece in _split3(x):
        t = jnp.dot(tri_b, piece, preferred_element_type=F32)
        out = t if out is None else out + t
    return out


def _wkv_kernel(r_ref, lw_ref, k_ref, v_ref, kk_ref, b_ref, gate_ref, bonus_ref, lnw_ref, lnb_ref,
                y_ref, s_ref, st_ref):
    c = pl.program_id(0)

    @pl.when(c == 0)
    def _():
        st_ref[...] = jnp.zeros_like(st_ref)

    nb = r_ref.shape[0]
    n = CHUNK
    row = lax.broadcasted_iota(jnp.int32, (n, n), 0)
    col = lax.broadcasted_iota(jnp.int32, (n, n), 1)
    eye = jnp.where(row == col, 1.0, 0.0)
    tri = jnp.where(row >= col, 1.0, 0.0)
    row2 = lax.broadcasted_iota(jnp.int32, (2 * n, 2 * n), 0)
    col2 = lax.broadcasted_iota(jnp.int32, (2 * n, 2 * n), 1) % n
    gram_mask = jnp.logical_or(row2 % n > col2, jnp.logical_and(row2 >= n, row2 % n == col2))
    zeros = jnp.zeros((n, HEAD_DIM), BF16)

    units = range(nb * N_HEADS)
    stack = lambda xs, ys: [jnp.concatenate([x, y], axis=0) for x, y in zip(xs, ys)]

    def prelude(rows):
        ah, bh, kh, rh, beh, keh, vh, ge = ([] for _ in range(8))
        for bi in range(nb):
            lw = lw_ref[bi, rows, :]
            lg = _cumsum_rows(lw, tri)
            g_inv = jnp.exp(-lg)
            lg_end = lg[n - 1:n, :]
            g_to_end = jnp.exp(lg_end - lg)
            g_end = jnp.exp(lg_end)
            kk = kk_ref[bi, rows, :]
            b = b_ref[bi, rows, :]
            k = k_ref[bi, rows, :]
            full = (-kk * jnp.exp(lg - lw), b * g_inv, k * g_inv, r_ref[bi, rows, :] * jnp.exp(lg),
                    b * g_to_end, k * g_to_end, v_ref[bi, rows, :], g_end)
            for h in range(N_HEADS):
                sl = slice(h * HEAD_DIM, (h + 1) * HEAD_DIM)
                for dst, t in zip((ah, bh, kh, rh, beh, keh, vh), full[:7]):
                    dst.append(t[:, sl].astype(BF16))
                ge.append(g_end[:, sl])

        gram = [jnp.where(gram_mask, _dot_nt(ar, bk), 0.0) for ar, bk in zip(stack(ah, rh), stack(bh, kh))]
        top = [g[:n].astype(BF16) for g in gram]
        bot = [g[n:].astype(BF16) for g in gram]
        lv = [_dot(top[u], jnp.concatenate([zeros, vh[u]], axis=0)) for u in units]
        tm_ = [eye + gram[u][:n, :n] for u in units]
        lp = [_dot(t[:, :n], t[:, :n]).astype(BF16) for t in top]
        span = 2
        while 2 * span < n:
            prod = [_dot(x, p) for x, p in zip(stack(lp, [t.astype(BF16) for t in tm_]), lp)]
            tm_ = [tm_[u] + prod[u][n:] for u in units]
            lp = [p[:n].astype(BF16) for p in prod]
            span *= 2
        tm_ = [(tm_[u] + _dot(tm_[u], lp[u])).astype(BF16) for u in units]
        a_hat = [_dot(tm_[u], ah[u]).astype(BF16) for u in units]
        u0 = [_dot(tm_[u], lv[u]) for u in units]
        return rows, stack(a_hat, rh), u0, vh, bot, stack(beh, keh), ge

    def finish(rows, a_hat_r, u0, vh, bot, be_ke, ge):
        s0 = [st_ref[u] for u in units]
        on_s0 = [_dot_nt(x, s.astype(BF16)) for x, s in zip(a_hat_r, s0)]
        uv = stack([(on_s0[u][:n] + u0[u]).astype(BF16) for u in units], vh)
        y = [on_s0[u][n:] + _dot(bot[u], uv[u]) for u in units]
        for u in units:
            st_ref[u] = s0[u] * ge[u] + _dot_tn(uv[u], be_ke[u])
        for bi in range(nb):
            normed = []
            for yu in y[bi * N_HEADS:(bi + 1) * N_HEADS]:
                dy = yu - jnp.mean(yu, axis=-1, keepdims=True)
                normed.append(dy * lax.rsqrt(jnp.mean(dy * dy, axis=-1, keepdims=True) + GN_EPS))
            yn = jnp.concatenate(normed, axis=1)
            y_ref[bi, rows, :] = ((yn * lnw_ref[...] + lnb_ref[...] + bonus_ref[bi, rows, :])
                                  * gate_ref[bi, rows, :]).astype(BF16)

    for j in range(r_ref.shape[1] // n):
        finish(*prelude(pl.ds(j * n, n)))

    @pl.when(c == pl.num_programs(0) - 1)
    def _():
        s_ref[...] = st_ref[...]


def _wkv_prompt(r, lw, k2, v, kk, b, gate, bonus, ln_w, ln_b, batch, seq):
    rows = WKV_CHUNKS_PER_STEP * CHUNK
    nc = seq // rows
    assert nc * rows == seq
    seqs = lambda t: t.reshape(batch, seq, RWKV_WIDTH)
    spec = pl.BlockSpec((batch, rows, RWKV_WIDTH), lambda c: (0, c, 0))
    vec = pl.BlockSpec((1, RWKV_WIDTH), lambda c: (0, 0))
    nu = batch * N_HEADS
    y, s = pl.pallas_call(
        _wkv_kernel,
        grid=(nc,),
        in_specs=[spec] * 8 + [vec, vec],
        out_specs=[spec, pl.BlockSpec((nu, HEAD_DIM, HEAD_DIM), lambda c: (0, 0, 0))],
        out_shape=[jax.ShapeDtypeStruct((batch, seq, RWKV_WIDTH), BF16),
                   jax.ShapeDtypeStruct((nu, HEAD_DIM, HEAD_DIM), F32)],
        scratch_shapes=[pltpu.VMEM((nu, HEAD_DIM, HEAD_DIM), F32)],
        compiler_params=_params(("arbitrary",)),
        name="wkv_prompt",
    )(*(seqs(t) for t in (r, lw, k2, v, kk, b, gate, bonus)), ln_w, ln_b)
    return y.reshape(batch * seq, RWKV_WIDTH), s.reshape(batch, N_HEADS, HEAD_DIM, HEAD_DIM)


def _wkv_step_kernel(s_ref, lw_ref, kk_ref, b_ref, k_ref, r_ref, v_ref, y_ref, so_ref):
    s = s_ref[...]
    eye = _eye(HEAD_DIM)
    v_col = jnp.sum(eye * v_ref[...], axis=-1, keepdims=True)
    sa = jnp.sum(s * (-kk_ref[...]), axis=-1, keepdims=True)
    s = s * jnp.exp(lw_ref[...]) + sa * b_ref[...] + v_col * k_ref[...]
    so_ref[...] = s
    y_col = jnp.sum(s * r_ref[...], axis=-1, keepdims=True)
    y_ref[...] = jnp.sum(eye * y_col, axis=-2, keepdims=True)


def _wkv_sample(state, lw, kk, b, k2, r, v, bb):
    nb = state.shape[0]
    rowv = lambda t: t.reshape(nb, N_HEADS, 1, HEAD_DIM)
    idx = lambda i: (i, 0, 0, 0)
    s_spec = pl.BlockSpec((bb, N_HEADS, HEAD_DIM, HEAD_DIM), idx)
    r_spec = pl.BlockSpec((bb, N_HEADS, 1, HEAD_DIM), idx)
    y, s_new = pl.pallas_call(
        _wkv_step_kernel,
        grid=(nb // bb,),
        in_specs=[s_spec] + [r_spec] * 6,
        out_specs=[r_spec, s_spec],
        out_shape=[jax.ShapeDtypeStruct((nb, N_HEADS, 1, HEAD_DIM), F32),
                   jax.ShapeDtypeStruct(state.shape, F32)],
        compiler_params=_params(("arbitrary",)),
        name="wkv_sample",
    )(state, rowv(lw), rowv(kk), rowv(b), rowv(k2), rowv(r), rowv(v))
    return y.reshape(nb, RWKV_WIDTH), s_new


def _bias_tables():
    i = np.arange(Q_BLOCK)[:, None]
    c = np.arange(2 * Q_BLOCK)[None, :]
    n = i + Q_BLOCK - c
    ok = (n >= 0) & (n <= Q_BLOCK)
    prompt = np.stack([np.where(ok, _t5_bucket(d * np.clip(n, 0, Q_BLOCK)), -1) for d in DILATIONS])
    dist = WIN - np.arange(WIN)
    sample = np.stack([np.where((dist % d == 0) & (dist // d <= Q_BLOCK), _t5_bucket(dist), -1)
                       for d in DILATIONS])
    sample = np.broadcast_to(sample[:, None, :], (len(DILATIONS), N_HEADS, WIN))
    return prompt.astype(np.int32), sample.astype(np.int32)


def _bias_kernel(rb_ref, rbc_ref, bp_ref, bs_ref, op_ref, os_ref):
    bp = bp_ref[0]
    bs = bs_ref[0]
    acc_s = jnp.full(bs.shape, NEG, F32)
    for bk in range(N_BUCKETS):
        acc_s = jnp.where(bs == bk, rbc_ref[bk], acc_s)
    os_ref[0] = acc_s
    for h in range(N_HEADS):
        acc = jnp.full(bp.shape, NEG, F32)
        for bk in range(N_BUCKETS):
            acc = jnp.where(bp == bk, rb_ref[bk, h], acc)
        op_ref[0, h] = acc * LOG2E


def _bias_build(rel_bias):
    bp, bs = _bias_tables()
    rbc = rel_bias.reshape(N_BUCKETS, N_HEADS, 1)
    nb = len(DILATIONS)
    return pl.pallas_call(
        _bias_kernel,
        grid=(nb,),
        in_specs=[pl.BlockSpec(memory_space=pltpu.SMEM),
                  pl.BlockSpec((N_BUCKETS, N_HEADS, 1), lambda i: (0, 0, 0)),
                  pl.BlockSpec((1, Q_BLOCK, 2 * Q_BLOCK), lambda i: (i, 0, 0)),
                  pl.BlockSpec((1, N_HEADS, WIN), lambda i: (i, 0, 0))],
        out_specs=[pl.BlockSpec((1, N_HEADS, Q_BLOCK, 2 * Q_BLOCK), lambda i: (i, 0, 0, 0)),
                   pl.BlockSpec((1, N_HEADS, WIN), lambda i: (i, 0, 0))],
        out_shape=[jax.ShapeDtypeStruct((nb, N_HEADS, Q_BLOCK, 2 * Q_BLOCK), F32),
                   jax.ShapeDtypeStruct((nb, N_HEADS, WIN), F32)],
        compiler_params=_params(("arbitrary",)),
        name="bias_build",
    )(rel_bias, rbc, jnp.asarray(bp), jnp.asarray(bs)), rbc[0]


ATTN_GROUP = 4


def _attn_units():
    units = []
    for br, d in enumerate(DILATIONS):
        nblk = SUPER // (d * Q_BLOCK)
        for r in range(d):
            rows = lambda j, r=r, d=d: (pl.ds(r + d * Q_BLOCK * j, Q_BLOCK, stride=d) if d > 1
                                        else pl.ds(Q_BLOCK * j, Q_BLOCK))
            for blk in range(nblk):
                units.append((br, rows(blk), rows(blk - 1 if blk else nblk - 1), blk == 0))
    return units


def _attn_kernel(q_ref, kc_ref, kp_ref, vc_ref, vp_ref, bias_ref, o_ref, acc_ref, m_ref):
    has_prev = pl.program_id(2) > 0
    is_a = lax.broadcasted_iota(jnp.int32, (Q_BLOCK, 128), 1) < HEAD_DIM
    is_a2 = lax.broadcasted_iota(jnp.int32, (2 * Q_BLOCK, 128), 1) < HEAD_DIM
    live = jnp.logical_or(has_prev,
                          lax.broadcasted_iota(jnp.int32, (2 * Q_BLOCK, 2 * Q_BLOCK), 1) >= Q_BLOCK)
    units = _attn_units()
    for g0 in range(0, len(units), ATTN_GROUP):
        group = units[g0:g0 + ATTN_GROUP]
        ops = []
        for br, cur, prv, from_prev in group:
            qs = q_ref[cur, :] * (ATT_SCALE * LOG2E)
            q2 =jnp.concatenate([jnp.where(is_a, qs, 0.0), jnp.where(is_a, 0.0, qs)], axis=0)
            k2 = jnp.concatenate([(kp_ref if from_prev else kc_ref)[prv, :], kc_ref[cur, :]], axis=0)
            v2 = jnp.concatenate([(vp_ref if from_prev else vc_ref)[prv, :], vc_ref[cur, :]], axis=0)
            ops.append((br, cur, from_prev, q2.astype(BF16), k2.astype(BF16),
                        jnp.where(is_a2, v2, 1.0).astype(BF16), jnp.where(is_a2, 1.0, v2).astype(BF16)))
        logits = []
        for br, cur, from_prev, q2, k2, va, vb in ops:
            lg = lax.dot_general(q2, k2, NT, preferred_element_type=F32) + bias_ref[br]
            logits.append(jnp.where(live, lg, NEG) if from_prev else lg)
        ms = [jnp.max(lg, axis=-1, keepdims=True) for lg in logits]
        ps = [jnp.exp2(lg - m).astype(BF16) for lg, m in zip(logits, ms)]
        for (br, cur, _, _, _, va, vb), p, m in zip(ops, ps, ms):
            acc_ref[br, 0, cur, :] = jnp.dot(p[:Q_BLOCK], va, preferred_element_type=F32)
            acc_ref[br, 1, cur, :] = jnp.dot(p[Q_BLOCK:], vb, preferred_element_type=F32)
            m_ref[br, 0, cur, :] = jnp.broadcast_to(m[:Q_BLOCK], (Q_BLOCK, 128))
            m_ref[br, 1, cur, :] = jnp.broadcast_to(m[Q_BLOCK:], (Q_BLOCK, 128))

    for i in range(SUPER // Q_BLOCK):
        rs = pl.ds(i * Q_BLOCK, Q_BLOCK)
        outs = []
        for hd in range(2):
            ms = [m_ref[br, hd, rs, :] for br in range(len(DILATIONS))]
            m = functools.reduce(jnp.maximum, ms)
            tot = 0.0
            for br in range(len(DILATIONS)):
                tot = tot + jnp.exp2(ms[br] - m) * acc_ref[br, hd, rs, :]
            outs.append(tot / pltpu.roll(tot, HEAD_DIM, 1))
        o_ref[rs, :] = jnp.where(is_a, outs[0], outs[1]).astype(o_ref.dtype)


def _attn_prompt(qkv, bias_p, batch, seq):
    ns = seq // SUPER
    npair = ATT_WIDTH // 128
    blk = lambda col0, prev: pl.BlockSpec(
        (SUPER, 128),
        (lambda b, hp, sb: (b * ns + jnp.maximum(sb - 1, 0), col0 + hp)) if prev
        else (lambda b, hp, sb: (b * ns + sb, col0 + hp)))
    nb = len(DILATIONS)
    return pl.pallas_call(
        _attn_kernel,
        grid=(batch, npair, ns),
        in_specs=[blk(0, False), blk(npair, False), blk(npair, True), blk(2 * npair, False),
                  blk(2 * npair, True),
                  pl.BlockSpec((nb, None, 2 * Q_BLOCK, 2 * Q_BLOCK), lambda b, hp, sb: (0, hp, 0, 0))],
        out_specs=pl.BlockSpec((SUPER, 128), lambda b, hp, sb: (b * ns + sb, hp)),
        out_shape=jax.ShapeDtypeStruct((batch * seq, ATT_WIDTH), BF16),
        scratch_shapes=[pltpu.VMEM((nb, 2, SUPER, 128), F32)] * 2,
        compiler_params=_params(("arbitrary", "arbitrary", "arbitrary")),
        name="attn_prompt",
    )(qkv, qkv, qkv, qkv, qkv, bias_p.reshape(nb, npair, 2 * Q_BLOCK, 2 * Q_BLOCK))


WINDOW_HEADS = 4


def _window_body(q_ref, kn_ref, vn_ref, ck_ref, cv_ref, bias_ref, rb0_ref, o_ref, ok_ref, ov_ref, wt_ref):
    nh = WINDOW_HEADS
    lane = lax.broadcasted_iota(jnp.int32, (HEAD_DIM, WIN), 1)
    last = lane == WIN - 1
    eye = _eye(HEAD_DIM)
    col = lambda ref, h: jnp.sum(eye * ref[h:h + 1, :], axis=-1, keepdims=True)
    l_new = jnp.sum(kn_ref[...] * (q_ref[...] * ATT_SCALE), axis=-1, keepdims=True) + rb0_ref[...]
    rows = []
    for h in range(nh):
        kh = ck_ref[h]
        rows.append(jnp.sum(kh * (col(q_ref, h) * ATT_SCALE), axis=0, keepdims=True))
        ok_ref[h] = jnp.where(last, col(kn_ref, h), pltpu.roll(kh, WIN - 1, 1))
    logits = jnp.concatenate(rows, axis=0)
    parts = []
    for br in range(len(DILATIONS)):
        lg = logits + bias_ref[br]
        m = jnp.maximum(jnp.max(lg, axis=-1, keepdims=True), l_new)
        p = jnp.exp(lg - m)
        p_new = jnp.exp(l_new - m)
        parts.append((p, p_new, m, jnp.sum(p, axis=-1, keepdims=True) + p_new))
    m = functools.reduce(jnp.maximum, [pt[2] for pt in parts])
    wt = 0.0
    w_new = 0.0
    den = 0.0
    for p, p_new, m_g, s_g in parts:
        w = jnp.exp(m_g - m)
        wt = wt + w * p
        w_new = w_new + w * p_new
        den = den + w * s_g
    wt_ref[0:nh, :] = wt / den
    w_new = w_new / den
    for h in range(nh):
        vh = cv_ref[h]
        v_new = col(vn_ref, h)
        o_col = jnp.sum(vh * wt_ref[h:h + 1, :], axis=1, keepdims=True) + v_new * w_new[h:h + 1, :]
        o_ref[h:h + 1, :] = jnp.sum(eye * o_col, axis=0, keepdims=True)
        ov_ref[h] = jnp.where(last, v_new, pltpu.roll(vh, WIN - 1, 1))


N_WINDOW_IN = 7


def _window_operands(q, k_new, v_new, cache_k, cache_v, bias_s, rb0, step_of):
    nb = q.shape[0]
    nh = WINDOW_HEADS
    groups = N_HEADS // nh
    seq_grp = lambda *g: (step_of(*g) // groups, step_of(*g) % groups)
    rows = lambda t: t.reshape(nb, groups, nh, HEAD_DIM)
    wins = lambda t: t.reshape(nb, groups, nh, HEAD_DIM, WIN)
    row_spec = pl.BlockSpec((None, None, nh, HEAD_DIM), lambda *g: seq_grp(*g) + (0, 0))
    win_spec = pl.BlockSpec((None, None, nh, HEAD_DIM, WIN), lambda *g: seq_grp(*g) + (0, 0, 0))
    nbr = len(DILATIONS)
    bias_spec = pl.BlockSpec((nbr, None, nh, WIN), lambda *g: (0, seq_grp(*g)[1], 0, 0))
    rb0_spec = pl.BlockSpec((None, nh, 1), lambda *g: (seq_grp(*g)[1], 0, 0))
    args = (rows(q), rows(k_new), rows(v_new), wins(cache_k), wins(cache_v),
            bias_s.reshape(nbr, groups, nh, WIN), rb0.reshape(groups, nh, 1))
    in_specs = [row_spec, row_spec, row_spec, win_spec, win_spec, bias_spec, rb0_spec]
    assert len(args) == len(in_specs) == N_WINDOW_IN
    out_specs = [row_spec, win_spec, win_spec]
    out_shape = [jax.ShapeDtypeStruct((nb, groups, nh, HEAD_DIM), F32),
                 jax.ShapeDtypeStruct((nb, groups, nh, HEAD_DIM, WIN), F32),
                 jax.ShapeDtypeStruct((nb, groups, nh, HEAD_DIM, WIN), F32)]
    return args, in_specs, out_specs, out_shape


def _outproj_kernel(x_ref, attn_ref, y_ref, gate_ref, bonus_ref, lnw_ref, lnb_ref, wo_ref, gffn_ref,
                    ones_ref, h_ref, xn_ref):
    ones_bd = ones_ref[...]
    y = y_ref[...]
    mu = _segsum(y, ones_bd) * (1.0 / HEAD_DIM)
    dy = y - mu
    var = _segsum(dy * dy, ones_bd) * (1.0 / HEAD_DIM)
    yn = dy * lax.rsqrt(var + GN_EPS) * lnw_ref[...] + lnb_ref[...]
    rw = (yn + bonus_ref[...]) * gate_ref[...]
    mixed = (jnp.dot(attn_ref[...].astype(BF16), wo_ref[0:ATT_WIDTH, :], preferred_element_type=F32)
             + jnp.dot(rw.astype(BF16), wo_ref[ATT_WIDTH:D_MODEL, :], preferred_element_type=F32))
    h = x_ref[...] + mixed
    h_ref[...] = h
    xn_ref[...] = _rmsnorm(h, gffn_ref[...]).astype(BF16)


def _outproj(x2d, attn, y, gate, bonus, ln_w, ln_b, w_o, g_ffn, ones_bd, tm):
    m = x2d.shape[0]
    row = lambda i: (i, 0)
    fixed = lambda i: (0, 0)
    half = pl.BlockSpec((tm, RWKV_WIDTH), row)
    full = pl.BlockSpec((tm, D_MODEL), row)
    return pl.pallas_call(
        _outproj_kernel,
        grid=(m // tm,),
        in_specs=[full, half, half, half, half,
                  pl.BlockSpec((1, RWKV_WIDTH), fixed), pl.BlockSpec((1, RWKV_WIDTH), fixed),
                  pl.BlockSpec((D_MODEL, D_MODEL), fixed), pl.BlockSpec((1, D_MODEL), fixed),
                  pl.BlockSpec((RWKV_WIDTH, RWKV_WIDTH), fixed)],
        out_specs=[full, full],
        out_shape=[jax.ShapeDtypeStruct((m, D_MODEL), F32), jax.ShapeDtypeStruct((m, D_MODEL), BF16)],
        compiler_params=_params(("arbitrary",)),
        name="outproj",
    )(x2d, attn, y, gate, bonus, ln_w, ln_b, w_o, g_ffn, ones_bd)


def _ffn_tail(xn, h, gp_m1, gp_m2, gp_of, wi_ref, cw_ref, cb_ref, wout_ref, gfin_ref, o_ref):
    acc = None
    for c in range(D_FF // FF_CHUNK):
        cs = slice(c * FF_CHUNK, (c + 1) * FF_CHUNK)
        gp = gp_of(c)
        up = jnp.dot(xn, wi_ref[:, D_FF + c * FF_CHUNK:D_FF + (c + 1) * FF_CHUNK], preferred_element_type=F32)
        conv = (cb_ref[:, cs] + cw_ref[0:1, cs] * gp_m2(c, gp) + cw_ref[1:2, cs] * gp_m1(c, gp)
                + cw_ref[2:3, cs] * gp)
        act = conv * _sigmoid(conv) * up
        part = jnp.dot(act.astype(BF16), wout_ref[cs, :], preferred_element_type=F32)
        acc = part if acc is None else acc + part
    o_ref[...] = _rmsnorm(h + acc, gfin_ref[...])


def _ffn_prompt_kernel(x_ref, attn_ref, rw_ref, wo_ref, gffn_ref, wi_ref, cw_ref, cb_ref, wout_ref,
                       gfin_ref, *refs):
    n = N_WINDOW_IN
    win_in, (o_ref, conv_ref), win_out, (carry_ref, wt_ref) = refs[:n], refs[n:n + 2], refs[n + 2:n + 5], refs[n + 5:]

    @pl.when(pl.program_id(1) == 0)
    def _():
        carry_ref[...] = jnp.zeros_like(carry_ref)

    _window_body(*win_in, *win_out, wt_ref)

    h = (x_ref[...]
         + jnp.dot(attn_ref[...], wo_ref[0:ATT_WIDTH, :], preferred_element_type=F32)
         + jnp.dot(rw_ref[...], wo_ref[ATT_WIDTH:D_MODEL, :], preferred_element_type=F32))
    xn = _rmsnorm(h, gffn_ref[...]).astype(BF16)
    tm = xn.shape[0]
    rowi = lax.broadcasted_iota(jnp.int32, (tm, FF_CHUNK), 0)
    gps = {}

    def gp_of(c):
        cs = slice(c * FF_CHUNK, (c + 1) * FF_CHUNK)
        gps[c] = jnp.dot(xn, wi_ref[:, cs], preferred_element_type=F32)
        return gps[c]

    def gp_m1(c, gp):
        cs = slice(c * FF_CHUNK, (c + 1) * FF_CHUNK)
        return jnp.where(rowi == 0, carry_ref[1:2, cs], pltpu.roll(gp, 1, 0))

    def gp_m2(c, gp):
        cs = slice(c * FF_CHUNK, (c + 1) * FF_CHUNK)
        return jnp.where(rowi == 0, carry_ref[0:1, cs],
                         jnp.where(rowi == 1, carry_ref[1:2, cs], pltpu.roll(gp, 2, 0)))

    _ffn_tail(xn, h, gp_m1, gp_m2, gp_of, wi_ref, cw_ref, cb_ref, wout_ref, gfin_ref, o_ref)
    for c, gp in gps.items():
        cs = slice(c * FF_CHUNK, (c + 1) * FF_CHUNK)
        carry_ref[:, cs] = gp[tm - 2:tm, :]
        conv_ref[0, :, cs] = gp[tm - 2:tm, :]


def _ffn_sample_kernel(xn_ref, h_ref, p1_ref, p2_ref, wi_ref, cw_ref, cb_ref, wout_ref, gfin_ref,
                       o_ref, gp_ref):
    xn = xn_ref[...]

    def gp_of(c):
        cs = slice(c * FF_CHUNK, (c + 1) * FF_CHUNK)
        gp = jnp.dot(xn, wi_ref[:, cs], preferred_element_type=F32)
        gp_ref[:, cs] = gp
        return gp

    gp_m1 = lambda c, gp: p1_ref[:, c * FF_CHUNK:(c + 1) * FF_CHUNK]
    gp_m2 = lambda c, gp: p2_ref[:, c * FF_CHUNK:(c + 1) * FF_CHUNK]
    _ffn_tail(xn, h_ref[...], gp_m1, gp_m2, gp_of, wi_ref, cw_ref, cb_ref, wout_ref, gfin_ref, o_ref)


def _ffn_weight_specs(fixed):
    once = pl.Buffered(1)
    return [pl.BlockSpec((D_MODEL, 2 * D_FF), fixed, pipeline_mode=once),
            pl.BlockSpec((CONV_W, D_FF), fixed), pl.BlockSpec((1, D_FF), fixed),
            pl.BlockSpec((D_FF, D_MODEL), fixed, pipeline_mode=once), pl.BlockSpec((1, D_MODEL), fixed)]


def _ffn_prompt(x2d, attn, rw, w_o, g_ffn, fw, batch, seq, window):
    nt = window[0].shape[0] * (N_HEADS // WINDOW_HEADS) // batch
    tm = seq // nt
    assert tm * nt == seq and tm % 16 == 0
    row = lambda b, j: (b * nt + j, 0)
    fixed = lambda b, j: (0, 0)
    full = pl.BlockSpec((tm, D_MODEL), row)
    half = pl.BlockSpec((tm, ATT_WIDTH), row)
    mix_specs = [full, half, half, pl.BlockSpec((D_MODEL, D_MODEL), fixed), pl.BlockSpec((1, D_MODEL), fixed)]
    w_args, w_in_specs, w_out_specs, w_out_shape = _window_operands(*window, lambda b, j: b * nt + j)
    y, conv, attn_s, win_k, win_v = pl.pallas_call(
        _ffn_prompt_kernel,
        grid=(batch, nt),
        in_specs=mix_specs + _ffn_weight_specs(fixed) + w_in_specs,
        out_specs=[full, pl.BlockSpec((1, CONV_W - 1, D_FF), lambda b, j: (b, 0, 0))] + w_out_specs,
        out_shape=[jax.ShapeDtypeStruct((batch * seq, D_MODEL), F32),
                   jax.ShapeDtypeStruct((batch, CONV_W - 1, D_FF), F32)] + w_out_shape,
        scratch_shapes=[pltpu.VMEM((CONV_W - 1, D_FF), F32), pltpu.VMEM((8, WIN), F32)],
        compiler_params=_params(("arbitrary", "arbitrary")),
        name="ffn_prompt",
    )(x2d, attn, rw, w_o, g_ffn, *fw, *w_args)
    nb = window[0].shape[0]
    return (y, conv, attn_s.reshape(nb, ATT_WIDTH), win_k.reshape(nb, N_HEADS, HEAD_DIM, WIN),
            win_v.reshape(nb, N_HEADS, HEAD_DIM, WIN))


def _ffn_sample(xn, h, prev1, prev2, fw):
    m = xn.shape[0]
    fixed = lambda i: (0, 0)
    full = pl.BlockSpec((m, D_MODEL), fixed)
    ffs = pl.BlockSpec((m, D_FF), fixed)
    return pl.pallas_call(
        _ffn_sample_kernel,
        grid=(1,),
        in_specs=[full, full, ffs, ffs] + _ffn_weight_specs(fixed),
        out_specs=[full, ffs],
        out_shape=[jax.ShapeDtypeStruct((m, D_MODEL), F32), jax.ShapeDtypeStruct((m, D_FF), F32)],
        compiler_params=_params(("arbitrary",)),
        name="ffn_sample",
    )(xn, h, prev1, prev2, *fw)


def _layer_weights(g_mix, w_in, tok_mu, w0, w_decay_up, a0, w_iclr_up, w_gate_up, k_k, k_a, r_k,
                   ln_x_w, ln_x_b, w_o, g_ffn, w_ffn_in, conv_w, conv_b, w_ffn_out, g_final):
    vec = lambda t: t.reshape(1, -1)
    pad = LORA_PAD - LORA_W
    w_cat = jnp.pad(w_in, ((0, 0), (0, pad))).astype(BF16)
    mu_p = vec(tok_mu[:RKV_W])
    mu_l = vec(jnp.pad(tok_mu[RKV_W:], (0, pad)))
    lora_up = jnp.zeros((LORA_PAD, RKV_W), F32)
    lora_up = lora_up.at[0:LORA_DECAY, 0:RWKV_WIDTH].set(w_decay_up)
    lora_up = lora_up.at[LORA_DECAY:LORA_DECAY + LORA_ICLR, RWKV_WIDTH:2 * RWKV_WIDTH].set(w_iclr_up)
    lora_up = lora_up.at[LORA_DECAY + LORA_ICLR:LORA_W, 2 * RWKV_WIDTH:].set(w_gate_up)
    seg = np.arange(RWKV_WIDTH) // HEAD_DIM
    ones_bd = jnp.asarray(seg[:, None] == seg[None, :], BF16)
    prep = (mu_p, mu_l, lora_up.astype(BF16), vec(w0), vec(a0), vec(k_k), vec(k_a), vec(r_k), ones_bd)
    ffn = (w_ffn_in.astype(BF16), conv_w, vec(conv_b),
           w_ffn_out.astype(BF16), vec(g_final))
    return dict(g_mix=vec(g_mix), w_cat=w_cat, prep=prep, ones_bd=ones_bd, ln_w=vec(ln_x_w), ln_b=vec(ln_x_b),
                w_o=w_o.astype(BF16), g_ffn=vec(g_ffn), ffn=ffn)


def kernel(x_prompt, x_sample, cache_win_k, cache_win_v, state_shift, state_wkv, state_ffn_conv, g_mix, w_in, rel_bias, tok_mu, w0, w_decay_up, a0, w_iclr_up, w_gate_up, k_k, k_a, r_k, ln_x_w, ln_x_b, w_o, g_ffn, w_ffn_in, conv_w, conv_b, w_ffn_out, g_final):
    batch, seq, _ = x_prompt.shape
    nb = x_sample.shape[0]
    lw = _layer_weights(g_mix[0], w_in[0], tok_mu[0], w0[0], w_decay_up[0], a0[0], w_iclr_up[0],
                        w_gate_up[0], k_k[0], k_a[0], r_k[0].reshape(-1), ln_x_w[0], ln_x_b[0], w_o[0],
                        g_ffn[0], w_ffn_in[0], conv_w[0], conv_b[0], w_ffn_out[0], g_final)
    (bias_p, bias_s), rb0 = _bias_build(rel_bias)

    xp = x_prompt.reshape(batch * seq, D_MODEL)
    qkv, r, lgw, k2, v, kk, b, gate, bonus, sh_p, sh_l = _inproj_prep_prompt(
        xp, lw['g_mix'], lw['w_cat'], lw['prep'], batch, seq, 512)
    rw, wkv_p = _wkv_prompt(r, lgw, k2, v, kk, b, gate, bonus, lw['ln_w'], lw['ln_b'], batch, seq)
    attn = _attn_prompt(qkv, bias_p, batch, seq)
    qkv3 = qkv.reshape(batch, seq, QKV_W)
    win_k_p = qkv3[:, seq - WIN:, ATT_WIDTH:2 * ATT_WIDTH].reshape(1, batch, WIN, N_HEADS, HEAD_DIM)
    win_v_p = qkv3[:, seq - WIN:, 2 * ATT_WIDTH:].reshape(1, batch, WIN, N_HEADS, HEAD_DIM)
    shift_p = jnp.concatenate([sh_p[:, 0], sh_l[:, 0, :LORA_W]], axis=-1)[None]

    xs = x_sample.reshape(nb, D_MODEL)
    qkv_s, rkv_s, lora_s = _inproj(xs, lw['g_mix'], lw['w_cat'], nb)
    sh = state_shift[0]
    r, lgw, k2, v, kk, b, gate_s, bonus_s = _prep_sample(
        rkv_s, sh[:, :RKV_W], lora_s, jnp.pad(sh[:, RKV_W:], ((0, 0), (0, LORA_PAD - LORA_W))), lw['prep'])
    y_s, wkv_s = _wkv_sample(state_wkv[0], lgw, kk, b, k2, r, v, 8)
    q_s = qkv_s[:, :ATT_WIDTH]
    k_s = qkv_s[:, ATT_WIDTH:2 * ATT_WIDTH]
    v_s = qkv_s[:, 2 * ATT_WIDTH:]
    ck = jnp.transpose(cache_win_k[0], (0, 2, 3, 1))
    cv = jnp.transpose(cache_win_v[0], (0, 2, 3, 1))

    y_p, conv_p, attn_s, win_k_s, win_v_s = _ffn_prompt(xp, attn, rw, lw['w_o'], lw['g_ffn'], lw['ffn'],
                                                        batch, seq, (q_s, k_s, v_s, ck, cv, bias_s, rb0))

    win_k_s = jnp.transpose(win_k_s, (0, 3, 1, 2))
    win_v_s = jnp.transpose(win_v_s, (0, 3, 1, 2))
    h1_s, xn2_s = _outproj(xs, attn_s, y_s, gate_s, bonus_s, lw['ln_w'], lw['ln_b'], lw['w_o'], lw['g_ffn'],
                           lw['ones_bd'], nb)
    conv_state = state_ffn_conv[0]
    y_smp, gp_s = _ffn_sample(xn2_s, h1_s, conv_state[:, 1], conv_state[:, 0], lw['ffn'])
    conv_s = jnp.stack([conv_state[:, 1], gp_s], axis=1)
    shift_s = jnp.concatenate([rkv_s, lora_s[:, :LORA_W]], axis=-1)[None]

    return (y_p.reshape(batch, seq, D_MODEL), y_smp.reshape(nb, 1, D_MODEL),
            win_k_p, win_v_p, shift_p, wkv_p[None], conv_p[None],
            win_k_s[None], win_v_s[None], shift_s, wkv_s[None], conv_s[None])
```

```python
import functools

import numpy as np
import jax
import jax.numpy as jnp
from jax import lax
from jax.experimental import pallas as pl
from jax.experimental.pallas import tpu as pltpu

F32 = jnp.float32
BF16 = jnp.bfloat16

D_MODEL = 1024
HEAD_DIM = 64
ATT_WIDTH = 512
RWKV_WIDTH = 512
N_HEADS = 8
QKV_W = 3 * ATT_WIDTH
RKV_W = 3 * RWKV_WIDTH
LORA_DECAY, LORA_ICLR, LORA_GATE = 32, 32, 96
LORA_W = LORA_DECAY + LORA_ICLR + LORA_GATE
LORA_PAD = 256
RWKV_PROJ = RKV_W + LORA_W
PROJ_PAD = QKV_W + RKV_W + LORA_PAD
D_FF = 2816
FF_CHUNK = 2816
CONV_W = 3
WIN = 2048
DILATIONS = (1, 4, 16)
Q_BLOCK = 128
SUPER = 2048
N_BUCKETS = 32
MAX_DISTANCE = 2048
NORM_EPS = 1e-6
GN_EPS = 64e-5
NEG = -1e30
ATT_SCALE = HEAD_DIM ** -0.5
LOG2E = float(np.log2(np.e))
CHUNK = 64
WKV_CHUNKS_PER_STEP = 2
VMEM_LIMIT = 56 * 1024 * 1024

NT = (((1,), (1,)), ((), ()))
TN = (((0,), (0,)), ((), ()))


def _dot(a, b):
    return jnp.dot(a.astype(BF16), b.astype(BF16), preferred_element_type=F32)


def _dot_nt(a, b):
    return lax.dot_general(a.astype(BF16), b.astype(BF16), NT, preferred_element_type=F32)


def _dot_tn(a, b):
    return lax.dot_general(a.astype(BF16), b.astype(BF16), TN, preferred_element_type=F32)


def _split2(x):
    hi = x.astype(BF16)
    lo = (x - hi.astype(F32)).astype(BF16)
    return hi, lo


def _split3(x):
    hi = x.astype(BF16)
    r1 = x - hi.astype(F32)
    mid = r1.astype(BF16)
    lo = (r1 - mid.astype(F32)).astype(BF16)
    return hi, mid, lo


def _segsum(x, ones_bd):
    hi, lo = _split2(x)
    return (jnp.dot(hi, ones_bd, preferred_element_type=F32)
            + jnp.dot(lo, ones_bd, preferred_element_type=F32))


def _sigmoid(x):
    return 1.0 / (1.0 + jnp.exp(-x))


def _eye(n):
    return jnp.where(lax.broadcasted_iota(jnp.int32, (n, n), 0) == lax.broadcasted_iota(jnp.int32, (n, n), 1),
                     1.0, 0.0)


def _rmsnorm(x, g):
    return x * lax.rsqrt(jnp.mean(x * x, axis=-1, keepdims=True) + NORM_EPS) * g


def _params(sem):
    return pltpu.CompilerParams(dimension_semantics=sem, vmem_limit_bytes=VMEM_LIMIT)


def _t5_bucket(dist):
    dist = np.asarray(dist, dtype=np.int64)
    exact = N_BUCKETS // 2
    scaled = np.log(np.maximum(dist, 1) / exact) / np.log(MAX_DISTANCE / exact)
    large = np.minimum(exact + (scaled * (N_BUCKETS - exact)).astype(np.int64), N_BUCKETS - 1)
    return np.where(dist < exact, dist, large).astype(np.int32)


PROJ = QKV_W + RWKV_PROJ


def _inproj_kernel(x_ref, g_ref, w_ref, qkv_ref, rkv_ref, lora_ref):
    xb = _rmsnorm(x_ref[...], g_ref[...]).astype(BF16)
    cols = lambda a, b: w_ref[:, a:b].astype(BF16)
    qkv_ref[...] = jnp.dot(xb, cols(0, QKV_W), preferred_element_type=F32)
    rkv_ref[...] = jnp.dot(xb, cols(QKV_W, QKV_W + RKV_W), preferred_element_type=F32)
    lora_ref[:, 0:LORA_W] = jnp.dot(xb, cols(QKV_W + RKV_W, PROJ), preferred_element_type=F32)
    lora_ref[:, LORA_W:LORA_PAD] = jnp.zeros((x_ref.shape[0], LORA_PAD - LORA_W), F32)


def _inproj(x2d, g, w, tm):
    m = x2d.shape[0]
    row = lambda i: (i, 0)
    fixed = lambda i: (0, 0)
    return pl.pallas_call(
        _inproj_kernel,
        grid=(m // tm,),
        in_specs=[pl.BlockSpec((tm, D_MODEL), row),
                  pl.BlockSpec((1, D_MODEL), fixed),
                  pl.BlockSpec((D_MODEL, PROJ), fixed)],
        out_specs=[pl.BlockSpec((tm, QKV_W), row),
                   pl.BlockSpec((tm, RKV_W), row),
                   pl.BlockSpec((tm, LORA_PAD), row)],
        out_shape=[jax.ShapeDtypeStruct((m, QKV_W), F32),
                   jax.ShapeDtypeStruct((m, RKV_W), F32),
                   jax.ShapeDtypeStruct((m, LORA_PAD), F32)],
        compiler_params=_params(("arbitrary",)),
        name="inproj",
    )(x2d, g, w)


def _prep_math(p, p_prev, l, l_prev, mu_p, mu_l, wcat, w0, a0, k_k, k_a, r_k, ones_bd):
    pm = p + mu_p * (p_prev - p)
    lm = l + mu_l * (l_prev - l)
    r = pm[:, 0:RWKV_WIDTH]
    kr = pm[:, RWKV_WIDTH:2 * RWKV_WIDTH]
    vr = pm[:, 2 * RWKV_WIDTH:3 * RWKV_WIDTH]
    lane = lax.broadcasted_iota(jnp.int32, lm.shape, 1)
    feat = jnp.where(lane < LORA_DECAY, jnp.tanh(lm),
                     jnp.where(lane < LORA_DECAY + LORA_ICLR, lm, _sigmoid(lm)))
    z = jnp.dot(feat.astype(BF16), wcat, preferred_element_type=F32)
    zd = -(w0 + z[:, 0:RWKV_WIDTH])
    softplus = jnp.maximum(zd, 0.0) + jnp.log(1.0 + jnp.exp(-jnp.abs(zd)))
    lw = -jnp.exp(-softplus - 0.5)
    a = _sigmoid(a0 + z[:, RWKV_WIDTH:2 * RWKV_WIDTH])
    gate = z[:, 2 * RWKV_WIDTH:3 * RWKV_WIDTH]
    kk = kr * k_k
    kk = kk / jnp.maximum(jnp.sqrt(_segsum(kk * kk, ones_bd)), 1e-12)
    k2 = kr * (1.0 + (a - 1.0) * k_a)
    bonus = _segsum(r * k2 * r_k, ones_bd) * vr
    return r, lw, k2, vr, kk, kk * a, gate, bonus


def _inproj_prep_kernel(x_ref, g_ref, w_ref, mu_p_ref, mu_l_ref, wcat_ref, w0_ref, a0_ref, kk_ref, ka_ref,
                        rk_ref, ones_ref, qkv_ref, *refs):
    outs, (sp_ref, sl_ref), (cp_ref, cl_ref, wb_ref) = refs[:8], refs[8:10], refs[10:]

    @pl.when(jnp.logical_and(pl.program_id(0) == 0, pl.program_id(1) == 0))
    def _():
        for c0 in range(0, PROJ, RWKV_WIDTH):
            c1 = min(c0 + RWKV_WIDTH, PROJ)
            wb_ref[:, c0:c1] = w_ref[:, c0:c1].astype(BF16)
        wb_ref[:, PROJ:PROJ_PAD] = jnp.zeros((D_MODEL, PROJ_PAD - PROJ), BF16)

    @pl.when(pl.program_id(1) == 0)
    def _():
        cp_ref[...] = jnp.zeros_like(cp_ref)
        cl_ref[...] = jnp.zeros_like(cl_ref)

    xb = _rmsnorm(x_ref[...], g_ref[...]).astype(BF16)
    p = jnp.dot(xb, wb_ref[:, QKV_W:QKV_W + RKV_W], preferred_element_type=F32)
    l = jnp.dot(xb, wb_ref[:, QKV_W + RKV_W:PROJ_PAD], preferred_element_type=F32)
    tm = p.shape[0]
    first_p = lax.broadcasted_iota(jnp.int32, p.shape, 0) == 0
    first_l = lax.broadcasted_iota(jnp.int32, l.shape, 0) == 0
    p_prev = jnp.where(first_p, cp_ref[...], pltpu.roll(p, 1, 0))
    l_prev = jnp.where(first_l, cl_ref[...], pltpu.roll(l, 1, 0))
    cp_ref[...] = p[tm - 1:tm, :]
    cl_ref[...] = l[tm - 1:tm, :]
    sp_ref[0] = p[tm - 1:tm, :]
    sl_ref[0] = l[tm - 1:tm, :]
    res = _prep_math(p, p_prev, l, l_prev, mu_p_ref[...], mu_l_ref[...], wcat_ref[...], w0_ref[...],
                     a0_ref[...], kk_ref[...], ka_ref[...], rk_ref[...], ones_ref[...])
    for o, v in zip(outs, res):
        o[...] = v
    qkv_ref[...] = jnp.dot(xb, wb_ref[:, 0:QKV_W], preferred_element_type=F32)


def _prep_sample_kernel(p_ref, pp_ref, l_ref, lp_ref, mu_p_ref, mu_l_ref, wcat_ref, w0_ref, a0_ref,
                        kk_ref, ka_ref, rk_ref, ones_ref, *outs):
    res = _prep_math(p_ref[...], pp_ref[...], l_ref[...], lp_ref[...], mu_p_ref[...], mu_l_ref[...],
                     wcat_ref[...], w0_ref[...], a0_ref[...], kk_ref[...], ka_ref[...], rk_ref[...],
                     ones_ref[...])
    for o, v in zip(outs, res):
        o[...] = v


def _inproj_prep_prompt(x2d, g, w, pw, batch, seq, tm):
    nt = seq // tm
    row = lambda b, j: (b * nt + j, 0)
    fixed = lambda b, j: (0, 0)
    vec = lambda n: pl.BlockSpec((1, n), fixed)
    once = pl.Buffered(1)
    m = x2d.shape[0]
    last = lambda n: pl.BlockSpec((1, 1, n), lambda b, j: (b, 0, 0))
    return pl.pallas_call(
        _inproj_prep_kernel,
        grid=(batch, nt),
        in_specs=[pl.BlockSpec((tm, D_MODEL), row), vec(D_MODEL),
                  pl.BlockSpec((D_MODEL, PROJ), fixed, pipeline_mode=once),
                  vec(RKV_W), vec(LORA_PAD), pl.BlockSpec((LORA_PAD, RKV_W), fixed),
                  vec(RWKV_WIDTH), vec(RWKV_WIDTH), vec(RWKV_WIDTH), vec(RWKV_WIDTH), vec(RWKV_WIDTH),
                  pl.BlockSpec((RWKV_WIDTH, RWKV_WIDTH), fixed)],
        out_specs=[pl.BlockSpec((tm, QKV_W), row)] + [pl.BlockSpec((tm, RWKV_WIDTH), row)] * 8
                  + [last(RKV_W), last(LORA_PAD)],
        out_shape=[jax.ShapeDtypeStruct((m, QKV_W), F32)] + [jax.ShapeDtypeStruct((m, RWKV_WIDTH), F32)] * 8
                  + [jax.ShapeDtypeStruct((batch, 1, RKV_W), F32), jax.ShapeDtypeStruct((batch, 1, LORA_PAD), F32)],
        scratch_shapes=[pltpu.VMEM((1, RKV_W), F32), pltpu.VMEM((1, LORA_PAD), F32),
                        pltpu.VMEM((D_MODEL, PROJ_PAD), BF16)],
        compiler_params=_params(("arbitrary", "arbitrary")),
        name="inproj_prep_prompt",
    )(x2d, g, w, *pw)


def _prep_sample(rkv, rkv_prev, lora, lora_prev, pw):
    m = rkv.shape[0]
    full = lambda a: pl.BlockSpec(a.shape, lambda i: (0,) * a.ndim)
    args = (rkv, rkv_prev, lora, lora_prev) + tuple(pw)
    return pl.pallas_call(
        _prep_sample_kernel,
        grid=(1,),
        in_specs=[full(a) for a in args],
        out_specs=[pl.BlockSpec((m, RWKV_WIDTH), lambda i: (0, 0))] * 8,
        out_shape=[jax.ShapeDtypeStruct((m, RWKV_WIDTH), F32)] * 8,
        compiler_params=_params(("arbitrary",)),
        name="rwkv_prep_sample",
    )(*args)


def _cumsum_rows(x, tri):
    tri_b = tri.astype(BF16)
    out = None
    for piece in _split3(x):
        t = jnp.dot(tri_b, piece, preferred_element_type=F32)
        out = t if out is None else out + t
    return out


def _wkv_kernel(r_ref, lw_ref, k_ref, v_ref, kk_ref, b_ref, gate_ref, bonus_ref, lnw_ref, lnb_ref,
                y_ref, s_ref, st_ref):
    c = pl.program_id(0)

    @pl.when(c == 0)
    def _():
        st_ref[...] = jnp.zeros_like(st_ref)

    nb = r_ref.shape[0]
    n = CHUNK
    row = lax.broadcasted_iota(jnp.int32, (n, n), 0)
    col = lax.broadcasted_iota(jnp.int32, (n, n), 1)
    eye = jnp.where(row == col, 1.0, 0.0)
    tri = jnp.where(row >= col, 1.0, 0.0)
    row2 = lax.broadcasted_iota(jnp.int32, (2 * n, 2 * n), 0)
    col2 = lax.broadcasted_iota(jnp.int32, (2 * n, 2 * n), 1) % n
    gram_mask = jnp.logical_or(row2 % n > col2, jnp.logical_and(row2 >= n, row2 % n == col2))
    zeros = jnp.zeros((n, HEAD_DIM), BF16)

    units = range(nb * N_HEADS)
    stack = lambda xs, ys: [jnp.concatenate([x, y], axis=0) for x, y in zip(xs, ys)]

    def prelude(rows):
        ah, bh, kh, rh, beh, keh, vh, ge = ([] for _ in range(8))
        for bi in range(nb):
            lw = lw_ref[bi, rows, :]
            lg = _cumsum_rows(lw, tri)
            g_inv = jnp.exp(-lg)
            lg_end = lg[n - 1:n, :]
            g_to_end = jnp.exp(lg_end - lg)
            g_end = jnp.exp(lg_end)
            kk = kk_ref[bi, rows, :]
            b = b_ref[bi, rows, :]
            k = k_ref[bi, rows, :]
            full = (-kk * jnp.exp(lg - lw), b * g_inv, k * g_inv, r_ref[bi, rows, :] * jnp.exp(lg),
                    b * g_to_end, k * g_to_end, v_ref[bi, rows, :], g_end)
            for h in range(N_HEADS):
                sl = slice(h * HEAD_DIM, (h + 1) * HEAD_DIM)
                for dst, t in zip((ah, bh, kh, rh, beh, keh, vh), full[:7]):
                    dst.append(t[:, sl].astype(BF16))
                ge.append(g_end[:, sl])

        gram = [jnp.where(gram_mask, _dot_nt(ar, bk), 0.0) for ar, bk in zip(stack(ah, rh), stack(bh, kh))]
        top = [g[:n].astype(BF16) for g in gram]
        bot = [g[n:].astype(BF16) for g in gram]
        lv = [_dot(top[u], jnp.concatenate([zeros, vh[u]], axis=0)) for u in units]
        tm_ = [eye + gram[u][:n, :n] for u in units]
        lp = [_dot(t[:, :n], t[:, :n]).astype(BF16) for t in top]
        span = 2
        while 2 * span < n:
            prod = [_dot(x, p) for x, p in zip(stack(lp, [t.astype(BF16) for t in tm_]), lp)]
            tm_ = [tm_[u] + prod[u][n:] for u in units]
            lp = [p[:n].astype(BF16) for p in prod]
            span *= 2
        tm_ = [(tm_[u] + _dot(tm_[u], lp[u])).astype(BF16) for u in units]
        a_hat = [_dot(tm_[u], ah[u]).astype(BF16) for u in units]
        u0 = [_dot(tm_[u], lv[u]) for u in units]
        return rows, stack(a_hat, rh), u0, vh, bot, stack(beh, keh), ge

    def finish(rows, a_hat_r, u0, vh, bot, be_ke, ge):
        s0 = [st_ref[u] for u in units]
        on_s0 = [_dot_nt(x, s.astype(BF16)) for x, s in zip(a_hat_r, s0)]
        uv = stack([(on_s0[u][:n] + u0[u]).astype(BF16) for u in units], vh)
        y = [on_s0[u][n:] + _dot(bot[u], uv[u]) for u in units]
        for u in units:
            st_ref[u] = s0[u] * ge[u] + _dot_tn(uv[u], be_ke[u])
        for bi in range(nb):
            normed = []
            for yu in y[bi * N_HEADS:(bi + 1) * N_HEADS]:
                dy = yu - jnp.mean(yu, axis=-1, keepdims=True)
                normed.append(dy * lax.rsqrt(jnp.mean(dy * dy, axis=-1, keepdims=True) + GN_EPS))
            yn = jnp.concatenate(normed, axis=1)
            y_ref[bi, rows, :] = ((yn * lnw_ref[...] + lnb_ref[...] + bonus_ref[bi, rows, :])
                                  * gate_ref[bi, rows, :]).astype(BF16)

    for j in range(r_ref.shape[1] // n):
        finish(*prelude(pl.ds(j * n, n)))

    @pl.when(c == pl.num_programs(0) - 1)
    def _():
        s_ref[...] = st_ref[...]


def _wkv_prompt(r, lw, k2, v, kk, b, gate, bonus, ln_w, ln_b, batch, seq):
    rows = WKV_CHUNKS_PER_STEP * CHUNK
    nc = seq // rows
    assert nc * rows == seq
    seqs = lambda t: t.reshape(batch, seq, RWKV_WIDTH)
    spec = pl.BlockSpec((batch, rows, RWKV_WIDTH), lambda c: (0, c, 0))
    vec = pl.BlockSpec((1, RWKV_WIDTH), lambda c: (0, 0))
    nu = batch * N_HEADS
    y, s = pl.pallas_call(
        _wkv_kernel,
        grid=(nc,),
        in_specs=[spec] * 8 + [vec, vec],
        out_specs=[spec, pl.BlockSpec((nu, HEAD_DIM, HEAD_DIM), lambda c: (0, 0, 0))],
        out_shape=[jax.ShapeDtypeStruct((batch, seq, RWKV_WIDTH), BF16),
                   jax.ShapeDtypeStruct((nu, HEAD_DIM, HEAD_DIM), F32)],
        scratch_shapes=[pltpu.VMEM((nu, HEAD_DIM, HEAD_DIM), F32)],
        compiler_params=_params(("arbitrary",)),
        name="wkv_prompt",
    )(*(seqs(t) for t in (r, lw, k2, v, kk, b, gate, bonus)), ln_w, ln_b)
    return y.reshape(batch * seq, RWKV_WIDTH), s.reshape(batch, N_HEADS, HEAD_DIM, HEAD_DIM)


def _wkv_step_kernel(s_ref, lw_ref, kk_ref, b_ref, k_ref, r_ref, v_ref, y_ref, so_ref):
    s = s_ref[...]
    eye = _eye(HEAD_DIM)
    v_col = jnp.sum(eye * v_ref[...], axis=-1, keepdims=True)
    sa = jnp.sum(s * (-kk_ref[...]), axis=-1, keepdims=True)
    s = s * jnp.exp(lw_ref[...]) + sa * b_ref[...] + v_col * k_ref[...]
    so_ref[...] = s
    y_col = jnp.sum(s * r_ref[...], axis=-1, keepdims=True)
    y_ref[...] = jnp.sum(eye * y_col, axis=-2, keepdims=True)


def _wkv_sample(state, lw, kk, b, k2, r, v, bb):
    nb = state.shape[0]
    rowv = lambda t: t.reshape(nb, N_HEADS, 1, HEAD_DIM)
    idx = lambda i: (i, 0, 0, 0)
    s_spec = pl.BlockSpec((bb, N_HEADS, HEAD_DIM, HEAD_DIM), idx)
    r_spec = pl.BlockSpec((bb, N_HEADS, 1, HEAD_DIM), idx)
    y, s_new = pl.pallas_call(
        _wkv_step_kernel,
        grid=(nb // bb,),
        in_specs=[s_spec] + [r_spec] * 6,
        out_specs=[r_spec, s_spec],
        out_shape=[jax.ShapeDtypeStruct((nb, N_HEADS, 1, HEAD_DIM), F32),
                   jax.ShapeDtypeStruct(state.shape, F32)],
        compiler_params=_params(("arbitrary",)),
        name="wkv_sample",
    )(state, rowv(lw), rowv(kk), rowv(b), rowv(k2), rowv(r), rowv(v))
    return y.reshape(nb, RWKV_WIDTH), s_new


def _bias_tables():
    i = np.arange(Q_BLOCK)[:, None]
    c = np.arange(2 * Q_BLOCK)[None, :]
    n = i + Q_BLOCK - c
    ok = (n >= 0) & (n <= Q_BLOCK)
    prompt = np.stack([np.where(ok, _t5_bucket(d * np.clip(n, 0, Q_BLOCK)), -1) for d in DILATIONS])
    dist = WIN - np.arange(WIN)
    sample = np.stack([np.where((dist % d == 0) & (dist // d <= Q_BLOCK), _t5_bucket(dist), -1)
                       for d in DILATIONS])
    sample = np.broadcast_to(sample[:, None, :], (len(DILATIONS), N_HEADS, WIN))
    return prompt.astype(np.int32), sample.astype(np.int32)


def _bias_kernel(rb_ref, rbc_ref, bp_ref, bs_ref, op_ref, os_ref):
    bp = bp_ref[0]
    bs = bs_ref[0]
    acc_s = jnp.full(bs.shape, NEG, F32)
    for bk in range(N_BUCKETS):
        acc_s = jnp.where(bs == bk, rbc_ref[bk], acc_s)
    os_ref[0] = acc_s
    for h in range(N_HEADS):
        acc = jnp.full(bp.shape, NEG, F32)
        for bk in range(N_BUCKETS):
            acc = jnp.where(bp == bk, rb_ref[bk, h], acc)
        op_ref[0, h] = acc * LOG2E


def _bias_build(rel_bias):
    bp, bs = _bias_tables()
    rbc = rel_bias.reshape(N_BUCKETS, N_HEADS, 1)
    nb = len(DILATIONS)
    return pl.pallas_call(
        _bias_kernel,
        grid=(nb,),
        in_specs=[pl.BlockSpec(memory_space=pltpu.SMEM),
                  pl.BlockSpec((N_BUCKETS, N_HEADS, 1), lambda i: (0, 0, 0)),
                  pl.BlockSpec((1, Q_BLOCK, 2 * Q_BLOCK), lambda i: (i, 0, 0)),
                  pl.BlockSpec((1, N_HEADS, WIN), lambda i: (i, 0, 0))],
        out_specs=[pl.BlockSpec((1, N_HEADS, Q_BLOCK, 2 * Q_BLOCK), lambda i: (i, 0, 0, 0)),
                   pl.BlockSpec((1, N_HEADS, WIN), lambda i: (i, 0, 0))],
        out_shape=[jax.ShapeDtypeStruct((nb, N_HEADS, Q_BLOCK, 2 * Q_BLOCK), F32),
                   jax.ShapeDtypeStruct((nb, N_HEADS, WIN), F32)],
        compiler_params=_params(("arbitrary",)),
        name="bias_build",
    )(rel_bias, rbc, jnp.asarray(bp), jnp.asarray(bs)), rbc[0]


ATTN_GROUP = 4


def _attn_units():
    units = []
    for br, d in enumerate(DILATIONS):
        nblk = SUPER // (d * Q_BLOCK)
        for r in range(d):
            rows = lambda j, r=r, d=d: (pl.ds(r + d * Q_BLOCK * j, Q_BLOCK, stride=d) if d > 1
                                        else pl.ds(Q_BLOCK * j, Q_BLOCK))
            for blk in range(nblk):
                units.append((br, rows(blk), rows(blk - 1 if blk else nblk - 1), blk == 0))
    return units


def _attn_kernel(q_ref, kc_ref, kp_ref, vc_ref, vp_ref, bias_ref, o_ref, acc_ref, m_ref):
    has_prev = pl.program_id(2) > 0
    is_a = lax.broadcasted_iota(jnp.int32, (Q_BLOCK, 128), 1) < HEAD_DIM
    is_a2 = lax.broadcasted_iota(jnp.int32, (2 * Q_BLOCK, 128), 1) < HEAD_DIM
    live = jnp.logical_or(has_prev,
                          lax.broadcasted_iota(jnp.int32, (2 * Q_BLOCK, 2 * Q_BLOCK), 1) >= Q_BLOCK)
    units = _attn_units()
    for g0 in range(0, len(units), ATTN_GROUP):
        group = units[g0:g0 + ATTN_GROUP]
        ops = []
        for br, cur, prv, from_prev in group:
            qs = q_ref[cur, :] * (ATT_SCALE * LOG2E)
            q2 =jnp.concatenate([jnp.where(is_a, qs, 0.0), jnp.where(is_a, 0.0, qs)], axis=0)
            k2 = jnp.concatenate([(kp_ref if from_prev else kc_ref)[prv, :], kc_ref[cur, :]], axis=0)
            v2 = jnp.concatenate([(vp_ref if from_prev else vc_ref)[prv, :], vc_ref[cur, :]], axis=0)
            ops.append((br, cur, from_prev, q2.astype(BF16), k2.astype(BF16),
                        jnp.where(is_a2, v2, 1.0).astype(BF16), jnp.where(is_a2, 1.0, v2).astype(BF16)))
        logits = []
        for br, cur, from_prev, q2, k2, va, vb in ops:
            lg = lax.dot_general(q2, k2, NT, preferred_element_type=F32) + bias_ref[br]
            logits.append(jnp.where(live, lg, NEG) if from_prev else lg)
        ms = [jnp.max(lg, axis=-1, keepdims=True) for lg in logits]
        ps = [jnp.exp2(lg - m).astype(BF16) for lg, m in zip(logits, ms)]
        for (br, cur, _, _, _, va, vb), p, m in zip(ops, ps, ms):
            acc_ref[br, 0, cur, :] = jnp.dot(p[:Q_BLOCK], va, preferred_element_type=F32)
            acc_ref[br, 1, cur, :] = jnp.dot(p[Q_BLOCK:], vb, preferred_element_type=F32)
            m_ref[br, 0, cur, :] = jnp.broadcast_to(m[:Q_BLOCK], (Q_BLOCK, 128))
            m_ref[br, 1, cur, :] = jnp.broadcast_to(m[Q_BLOCK:], (Q_BLOCK, 128))

    for i in range(SUPER // Q_BLOCK):
        rs = pl.ds(i * Q_BLOCK, Q_BLOCK)
        outs = []
        for hd in range(2):
            ms = [m_ref[br, hd, rs, :] for br in range(len(DILATIONS))]
            m = functools.reduce(jnp.maximum, ms)
            tot = 0.0
            for br in range(len(DILATIONS)):
                tot = tot + jnp.exp2(ms[br] - m) * acc_ref[br, hd, rs, :]
            outs.append(tot / pltpu.roll(tot, HEAD_DIM, 1))
        o_ref[rs, :] = jnp.where(is_a, outs[0], outs[1]).astype(o_ref.dtype)


def _attn_prompt(qkv, bias_p, batch, seq):
    ns = seq // SUPER
    npair = ATT_WIDTH // 128
    blk = lambda col0, prev: pl.BlockSpec(
        (SUPER, 128),
        (lambda b, hp, sb: (b * ns + jnp.maximum(sb - 1, 0), col0 + hp)) if prev
        else (lambda b, hp, sb: (b * ns + sb, col0 + hp)))
    nb = len(DILATIONS)
    return pl.pallas_call(
        _attn_kernel,
        grid=(batch, npair, ns),
        in_specs=[blk(0, False), blk(npair, False), blk(npair, True), blk(2 * npair, False),
                  blk(2 * npair, True),
                  pl.BlockSpec((nb, None, 2 * Q_BLOCK, 2 * Q_BLOCK), lambda b, hp, sb: (0, hp, 0, 0))],
        out_specs=pl.BlockSpec((SUPER, 128), lambda b, hp, sb: (b * ns + sb, hp)),
        out_shape=jax.ShapeDtypeStruct((batch * seq, ATT_WIDTH), BF16),
        scratch_shapes=[pltpu.VMEM((nb, 2, SUPER, 128), F32)] * 2,
        compiler_params=_params(("arbitrary", "arbitrary", "arbitrary")),
        name="attn_prompt",
    )(qkv, qkv, qkv, qkv, qkv, bias_p.reshape(nb, npair, 2 * Q_BLOCK, 2 * Q_BLOCK))


WINDOW_HEADS = 4


def _window_body(q_ref, kn_ref, vn_ref, ck_ref, cv_ref, bias_ref, rb0_ref, o_ref, ok_ref, ov_ref, wt_ref):
    nh = WINDOW_HEADS
    lane = lax.broadcasted_iota(jnp.int32, (HEAD_DIM, WIN), 1)
    last = lane == WIN - 1
    eye = _eye(HEAD_DIM)
    col = lambda ref, h: jnp.sum(eye * ref[h:h + 1, :], axis=-1, keepdims=True)
    l_new = jnp.sum(kn_ref[...] * (q_ref[...] * ATT_SCALE), axis=-1, keepdims=True) + rb0_ref[...]
    rows = []
    for h in range(nh):
        kh = ck_ref[h]
        rows.append(jnp.sum(kh * (col(q_ref, h) * ATT_SCALE), axis=0, keepdims=True))
        ok_ref[h] = jnp.where(last, col(kn_ref, h), pltpu.roll(kh, WIN - 1, 1))
    logits = jnp.concatenate(rows, axis=0)
    parts = []
    for br in range(len(DILATIONS)):
        lg = logits + bias_ref[br]
        m = jnp.maximum(jnp.max(lg, axis=-1, keepdims=True), l_new)
        p = jnp.exp(lg - m)
        p_new = jnp.exp(l_new - m)
        parts.append((p, p_new, m, jnp.sum(p, axis=-1, keepdims=True) + p_new))
    m = functools.reduce(jnp.maximum, [pt[2] for pt in parts])
    wt = 0.0
    w_new = 0.0
    den = 0.0
    for p, p_new, m_g, s_g in parts:
        w = jnp.exp(m_g - m)
        wt = wt + w * p
        w_new = w_new + w * p_new
        den = den + w * s_g
    wt_ref[0:nh, :] = wt / den
    w_new = w_new / den
    for h in range(nh):
        vh = cv_ref[h]
        v_new = col(vn_ref, h)
        o_col = jnp.sum(vh * wt_ref[h:h + 1, :], axis=1, keepdims=True) + v_new * w_new[h:h + 1, :]
        o_ref[h:h + 1, :] = jnp.sum(eye * o_col, axis=0, keepdims=True)
        ov_ref[h] = jnp.where(last, v_new, pltpu.roll(vh, WIN - 1, 1))


N_WINDOW_IN = 7


def _window_operands(q, k_new, v_new, cache_k, cache_v, bias_s, rb0, step_of):
    nb = q.shape[0]
    nh = WINDOW_HEADS
    groups = N_HEADS // nh
    seq_grp = lambda *g: (step_of(*g) // groups, step_of(*g) % groups)
    rows = lambda t: t.reshape(nb, groups, nh, HEAD_DIM)
    wins = lambda t: t.reshape(nb, groups, nh, HEAD_DIM, WIN)
    row_spec = pl.BlockSpec((None, None, nh, HEAD_DIM), lambda *g: seq_grp(*g) + (0, 0))
    win_spec = pl.BlockSpec((None, None, nh, HEAD_DIM, WIN), lambda *g: seq_grp(*g) + (0, 0, 0))
    nbr = len(DILATIONS)
    bias_spec = pl.BlockSpec((nbr, None, nh, WIN), lambda *g: (0, seq_grp(*g)[1], 0, 0))
    rb0_spec = pl.BlockSpec((None, nh, 1), lambda *g: (seq_grp(*g)[1], 0, 0))
    args = (rows(q), rows(k_new), rows(v_new), wins(cache_k), wins(cache_v),
            bias_s.reshape(nbr, groups, nh, WIN), rb0.reshape(groups, nh, 1))
    in_specs = [row_spec, row_spec, row_spec, win_spec, win_spec, bias_spec, rb0_spec]
    assert len(args) == len(in_specs) == N_WINDOW_IN
    out_specs = [row_spec, win_spec, win_spec]
    out_shape = [jax.ShapeDtypeStruct((nb, groups, nh, HEAD_DIM), F32),
                 jax.ShapeDtypeStruct((nb, groups, nh, HEAD_DIM, WIN), F32),
                 jax.ShapeDtypeStruct((nb, groups, nh, HEAD_DIM, WIN), F32)]
    return args, in_specs, out_specs, out_shape


def _outproj_kernel(x_ref, attn_ref, y_ref, gate_ref, bonus_ref, lnw_ref, lnb_ref, wo_ref, gffn_ref,
                    ones_ref, h_ref, xn_ref):
    ones_bd = ones_ref[...]
    y = y_ref[...]
    mu = _segsum(y, ones_bd) * (1.0 / HEAD_DIM)
    dy = y - mu
    var = _segsum(dy * dy, ones_bd) * (1.0 / HEAD_DIM)
    yn = dy * lax.rsqrt(var + GN_EPS) * lnw_ref[...] + lnb_ref[...]
    rw = (yn + bonus_ref[...]) * gate_ref[...]
    mixed = (jnp.dot(attn_ref[...].astype(BF16), wo_ref[0:ATT_WIDTH, :], preferred_element_type=F32)
             + jnp.dot(rw.astype(BF16), wo_ref[ATT_WIDTH:D_MODEL, :], preferred_element_type=F32))
    h = x_ref[...] + mixed
    h_ref[...] = h
    xn_ref[...] = _rmsnorm(h, gffn_ref[...]).astype(BF16)


def _outproj(x2d, attn, y, gate, bonus, ln_w, ln_b, w_o, g_ffn, ones_bd, tm):
    m = x2d.shape[0]
    row = lambda i: (i, 0)
    fixed = lambda i: (0, 0)
    half = pl.BlockSpec((tm, RWKV_WIDTH), row)
    full = pl.BlockSpec((tm, D_MODEL), row)
    return pl.pallas_call(
        _outproj_kernel,
        grid=(m // tm,),
        in_specs=[full, half, half, half, half,
                  pl.BlockSpec((1, RWKV_WIDTH), fixed), pl.BlockSpec((1, RWKV_WIDTH), fixed),
                  pl.BlockSpec((D_MODEL, D_MODEL), fixed), pl.BlockSpec((1, D_MODEL), fixed),
                  pl.BlockSpec((RWKV_WIDTH, RWKV_WIDTH), fixed)],
        out_specs=[full, full],
        out_shape=[jax.ShapeDtypeStruct((m, D_MODEL), F32), jax.ShapeDtypeStruct((m, D_MODEL), BF16)],
        compiler_params=_params(("arbitrary",)),
        name="outproj",
    )(x2d, attn, y, gate, bonus, ln_w, ln_b, w_o, g_ffn, ones_bd)


def _ffn_tail(xn, h, gp_m1, gp_m2, gp_of, wi_ref, cw_ref, cb_ref, wout_ref, gfin_ref, o_ref):
    acc = None
    for c in range(D_FF // FF_CHUNK):
        cs = slice(c * FF_CHUNK, (c + 1) * FF_CHUNK)
        gp = gp_of(c)
        up = jnp.dot(xn, wi_ref[:, D_FF + c * FF_CHUNK:D_FF + (c + 1) * FF_CHUNK], preferred_element_type=F32)
        conv = (cb_ref[:, cs] + cw_ref[0:1, cs] * gp_m2(c, gp) + cw_ref[1:2, cs] * gp_m1(c, gp)
                + cw_ref[2:3, cs] * gp)
        act = conv * _sigmoid(conv) * up
        part = jnp.dot(act.astype(BF16), wout_ref[cs, :], preferred_element_type=F32)
        acc = part if acc is None else acc + part
    o_ref[...] = _rmsnorm(h + acc, gfin_ref[...])


def _ffn_prompt_kernel(x_ref, attn_ref, rw_ref, wo_ref, gffn_ref, wi_ref, cw_ref, cb_ref, wout_ref,
                       gfin_ref, *refs):
    n = N_WINDOW_IN
    win_in, (o_ref, conv_ref), win_out, (carry_ref, wt_ref) = refs[:n], refs[n:n + 2], refs[n + 2:n + 5], refs[n + 5:]

    @pl.when(pl.program_id(1) == 0)
    def _():
        carry_ref[...] = jnp.zeros_like(carry_ref)

    _window_body(*win_in, *win_out, wt_ref)

    h = (x_ref[...]
         + jnp.dot(attn_ref[...], wo_ref[0:ATT_WIDTH, :], preferred_element_type=F32)
         + jnp.dot(rw_ref[...], wo_ref[ATT_WIDTH:D_MODEL, :], preferred_element_type=F32))
    xn = _rmsnorm(h, gffn_ref[...]).astype(BF16)
    tm = xn.shape[0]
    rowi = lax.broadcasted_iota(jnp.int32, (tm, FF_CHUNK), 0)
    gps = {}

    def gp_of(c):
        cs = slice(c * FF_CHUNK, (c + 1) * FF_CHUNK)
        gps[c] = jnp.dot(xn, wi_ref[:, cs], preferred_element_type=F32)
        return gps[c]

    def gp_m1(c, gp):
        cs = slice(c * FF_CHUNK, (c + 1) * FF_CHUNK)
        return jnp.where(rowi == 0, carry_ref[1:2, cs], pltpu.roll(gp, 1, 0))

    def gp_m2(c, gp):
        cs = slice(c * FF_CHUNK, (c + 1) * FF_CHUNK)
        return jnp.where(rowi == 0, carry_ref[0:1, cs],
                         jnp.where(rowi == 1, carry_ref[1:2, cs], pltpu.roll(gp, 2, 0)))

    _ffn_tail(xn, h, gp_m1, gp_m2, gp_of, wi_ref, cw_ref, cb_ref, wout_ref, gfin_ref, o_ref)
    for c, gp in gps.items():
        cs = slice(c * FF_CHUNK, (c + 1) * FF_CHUNK)
        carry_ref[:, cs] = gp[tm - 2:tm, :]
        conv_ref[0, :, cs] = gp[tm - 2:tm, :]


def _ffn_sample_kernel(xn_ref, h_ref, p1_ref, p2_ref, wi_ref, cw_ref, cb_ref, wout_ref, gfin_ref,
                       o_ref, gp_ref):
    xn = xn_ref[...]

    def gp_of(c):
        cs = slice(c * FF_CHUNK, (c + 1) * FF_CHUNK)
        gp = jnp.dot(xn, wi_ref[:, cs], preferred_element_type=F32)
        gp_ref[:, cs] = gp
        return gp

    gp_m1 = lambda c, gp: p1_ref[:, c * FF_CHUNK:(c + 1) * FF_CHUNK]
    gp_m2 = lambda c, gp: p2_ref[:, c * FF_CHUNK:(c + 1) * FF_CHUNK]
    _ffn_tail(xn, h_ref[...], gp_m1, gp_m2, gp_of, wi_ref, cw_ref, cb_ref, wout_ref, gfin_ref, o_ref)


def _ffn_weight_specs(fixed):
    once = pl.Buffered(1)
    return [pl.BlockSpec((D_MODEL, 2 * D_FF), fixed, pipeline_mode=once),
            pl.BlockSpec((CONV_W, D_FF), fixed), pl.BlockSpec((1, D_FF), fixed),
            pl.BlockSpec((D_FF, D_MODEL), fixed, pipeline_mode=once), pl.BlockSpec((1, D_MODEL), fixed)]


def _ffn_prompt(x2d, attn, rw, w_o, g_ffn, fw, batch, seq, window):
    nt = window[0].shape[0] * (N_HEADS // WINDOW_HEADS) // batch
    tm = seq // nt
    assert tm * nt == seq and tm % 16 == 0
    row = lambda b, j: (b * nt + j, 0)
    fixed = lambda b, j: (0, 0)
    full = pl.BlockSpec((tm, D_MODEL), row)
    half = pl.BlockSpec((tm, ATT_WIDTH), row)
    mix_specs = [full, half, half, pl.BlockSpec((D_MODEL, D_MODEL), fixed), pl.BlockSpec((1, D_MODEL), fixed)]
    w_args, w_in_specs, w_out_specs, w_out_shape = _window_operands(*window, lambda b, j: b * nt + j)
    y, conv, attn_s, win_k, win_v = pl.pallas_call(
        _ffn_prompt_kernel,
        grid=(batch, nt),
        in_specs=mix_specs + _ffn_weight_specs(fixed) + w_in_specs,
        out_specs=[full, pl.BlockSpec((1, CONV_W - 1, D_FF), lambda b, j: (b, 0, 0))] + w_out_specs,
        out_shape=[jax.ShapeDtypeStruct((batch * seq, D_MODEL), F32),
                   jax.ShapeDtypeStruct((batch, CONV_W - 1, D_FF), F32)] + w_out_shape,
        scratch_shapes=[pltpu.VMEM((CONV_W - 1, D_FF), F32), pltpu.VMEM((8, WIN), F32)],
        compiler_params=_params(("arbitrary", "arbitrary")),
        name="ffn_prompt",
    )(x2d, attn, rw, w_o, g_ffn, *fw, *w_args)
    nb = window[0].shape[0]
    return (y, conv, attn_s.reshape(nb, ATT_WIDTH), win_k.reshape(nb, N_HEADS, HEAD_DIM, WIN),
            win_v.reshape(nb, N_HEADS, HEAD_DIM, WIN))


def _ffn_sample(xn, h, prev1, prev2, fw):
    m = xn.shape[0]
    fixed = lambda i: (0, 0)
    full = pl.BlockSpec((m, D_MODEL), fixed)
    ffs = pl.BlockSpec((m, D_FF), fixed)
    return pl.pallas_call(
        _ffn_sample_kernel,
        grid=(1,),
        in_specs=[full, full, ffs, ffs] + _ffn_weight_specs(fixed),
        out_specs=[full, ffs],
        out_shape=[jax.ShapeDtypeStruct((m, D_MODEL), F32), jax.ShapeDtypeStruct((m, D_FF), F32)],
        compiler_params=_params(("arbitrary",)),
        name="ffn_sample",
    )(xn, h, prev1, prev2, *fw)


def _layer_weights(g_mix, w_in, tok_mu, w0, w_decay_up, a0, w_iclr_up, w_gate_up, k_k, k_a, r_k,
                   ln_x_w, ln_x_b, w_o, g_ffn, w_ffn_in, conv_w, conv_b, w_ffn_out, g_final):
    vec = lambda t: t.reshape(1, -1)
    pad = LORA_PAD - LORA_W
    w_cat = w_in
    mu_p = vec(tok_mu[:RKV_W])
    mu_l = vec(jnp.pad(tok_mu[RKV_W:], (0, pad)))
    lora_up = jnp.zeros((LORA_PAD, RKV_W), F32)
    lora_up = lora_up.at[0:LORA_DECAY, 0:RWKV_WIDTH].set(w_decay_up)
    lora_up = lora_up.at[LORA_DECAY:LORA_DECAY + LORA_ICLR, RWKV_WIDTH:2 * RWKV_WIDTH].set(w_iclr_up)
    lora_up = lora_up.at[LORA_DECAY + LORA_ICLR:LORA_W, 2 * RWKV_WIDTH:].set(w_gate_up)
    seg = np.arange(RWKV_WIDTH) // HEAD_DIM
    ones_bd = jnp.asarray(seg[:, None] == seg[None, :], BF16)
    prep = (mu_p, mu_l, lora_up.astype(BF16), vec(w0), vec(a0), vec(k_k), vec(k_a), vec(r_k), ones_bd)
    ffn = (w_ffn_in.astype(BF16), conv_w, vec(conv_b),
           w_ffn_out.astype(BF16), vec(g_final))
    return dict(g_mix=vec(g_mix), w_cat=w_cat, prep=prep, ones_bd=ones_bd, ln_w=vec(ln_x_w), ln_b=vec(ln_x_b),
                w_o=w_o.astype(BF16), g_ffn=vec(g_ffn), ffn=ffn)


def kernel(x_prompt, x_sample, cache_win_k, cache_win_v, state_shift, state_wkv, state_ffn_conv, g_mix, w_in, rel_bias, tok_mu, w0, w_decay_up, a0, w_iclr_up, w_gate_up, k_k, k_a, r_k, ln_x_w, ln_x_b, w_o, g_ffn, w_ffn_in, conv_w, conv_b, w_ffn_out, g_final):
    batch, seq, _ = x_prompt.shape
    nb = x_sample.shape[0]
    lw = _layer_weights(g_mix[0], w_in[0], tok_mu[0], w0[0], w_decay_up[0], a0[0], w_iclr_up[0],
                        w_gate_up[0], k_k[0], k_a[0], r_k[0].reshape(-1), ln_x_w[0], ln_x_b[0], w_o[0],
                        g_ffn[0], w_ffn_in[0], conv_w[0], conv_b[0], w_ffn_out[0], g_final)
    (bias_p, bias_s), rb0 = _bias_build(rel_bias)

    xp = x_prompt.reshape(batch * seq, D_MODEL)
    qkv, r, lgw, k2, v, kk, b, gate, bonus, sh_p, sh_l = _inproj_prep_prompt(
        xp, lw['g_mix'], lw['w_cat'], lw['prep'], batch, seq, 512)
    rw, wkv_p = _wkv_prompt(r, lgw, k2, v, kk, b, gate, bonus, lw['ln_w'], lw['ln_b'], batch, seq)
    attn = _attn_prompt(qkv, bias_p, batch, seq)
    qkv3 = qkv.reshape(batch, seq, QKV_W)
    win_k_p = qkv3[:, seq - WIN:, ATT_WIDTH:2 * ATT_WIDTH].reshape(1, batch, WIN, N_HEADS, HEAD_DIM)
    win_v_p = qkv3[:, seq - WIN:, 2 * ATT_WIDTH:].reshape(1, batch, WIN, N_HEADS, HEAD_DIM)
    shift_p = jnp.concatenate([sh_p[:, 0], sh_l[:, 0, :LORA_W]], axis=-1)[None]

    xs = x_sample.reshape(nb, D_MODEL)
    qkv_s, rkv_s, lora_s = _inproj(xs, lw['g_mix'], lw['w_cat'], nb)
    sh = state_shift[0]
    r, lgw, k2, v, kk, b, gate_s, bonus_s = _prep_sample(
        rkv_s, sh[:, :RKV_W], lora_s, jnp.pad(sh[:, RKV_W:], ((0, 0), (0, LORA_PAD - LORA_W))), lw['prep'])
    y_s, wkv_s = _wkv_sample(state_wkv[0], lgw, kk, b, k2, r, v, 8)
    q_s = qkv_s[:, :ATT_WIDTH]
    k_s = qkv_s[:, ATT_WIDTH:2 * ATT_WIDTH]
    v_s = qkv_s[:, 2 * ATT_WIDTH:]
    ck = jnp.transpose(cache_win_k[0], (0, 2, 3, 1))
    cv = jnp.transpose(cache_win_v[0], (0, 2, 3, 1))

    y_p, conv_p, attn_s, win_k_s, win_v_s = _ffn_prompt(xp, attn, rw, lw['w_o'], lw['g_ffn'], lw['ffn'],
                                                        batch, seq, (q_s, k_s, v_s, ck, cv, bias_s, rb0))

    win_k_s = jnp.transpose(win_k_s, (0, 3, 1, 2))
    win_v_s = jnp.transpose(win_v_s, (0, 3, 1, 2))
    h1_s, xn2_s = _outproj(xs, attn_s, y_s, gate_s, bonus_s, lw['ln_w'], lw['ln_b'], lw['w_o'], lw['g_ffn'],
                           lw['ones_bd'], nb)
    conv_state = state_ffn_conv[0]
    y_smp, gp_s = _ffn_sample(xn2_s, h1_s, conv_state[:, 1], conv_state[:, 0], lw['ffn'])
    conv_s = jnp.stack([conv_state[:, 1], gp_s], axis=1)
    shift_s = jnp.concatenate([rkv_s, lora_s[:, :LORA_W]], axis=-1)[None]

    return (y_p.reshape(batch, seq, D_MODEL), y_smp.reshape(nb, 1, D_MODEL),
            win_k_p, win_v_p, shift_p, wkv_p[None], conv_p[None],
            win_k_s[None], win_v_s[None], shift_s, wkv_s[None], conv_s[None])
```

```python
import functools

import numpy as np
import jax
import jax.numpy as jnp
from jax import lax
from jax.experimental import pallas as pl
from jax.experimental.pallas import tpu as pltpu

F32 = jnp.float32
BF16 = jnp.bfloat16

D_MODEL = 1024
HEAD_DIM = 64
ATT_WIDTH = 512
RWKV_WIDTH = 512
N_HEADS = 8
QKV_W = 3 * ATT_WIDTH
RKV_W = 3 * RWKV_WIDTH
LORA_DECAY, LORA_ICLR, LORA_GATE = 32, 32, 96
LORA_W = LORA_DECAY + LORA_ICLR + LORA_GATE
LORA_PAD = 256
RWKV_PROJ = RKV_W + LORA_W
PROJ_PAD = QKV_W + RKV_W + LORA_PAD
D_FF = 2816
FF_CHUNK = 2816
CONV_W = 3
WIN = 2048
DILATIONS = (1, 4, 16)
Q_BLOCK = 128
SUPER = 2048
N_BUCKETS = 32
MAX_DISTANCE = 2048
NORM_EPS = 1e-6
GN_EPS = 64e-5
NEG = -1e30
ATT_SCALE = HEAD_DIM ** -0.5
LOG2E = float(np.log2(np.e))
CHUNK = 64
WKV_CHUNKS_PER_STEP = 2
VMEM_LIMIT = 56 * 1024 * 1024

NT = (((1,), (1,)), ((), ()))
TN = (((0,), (0,)), ((), ()))


def _dot(a, b):
    return jnp.dot(a.astype(BF16), b.astype(BF16), preferred_element_type=F32)


def _dot_nt(a, b):
    return lax.dot_general(a.astype(BF16), b.astype(BF16), NT, preferred_element_type=F32)


def _dot_tn(a, b):
    return lax.dot_general(a.astype(BF16), b.astype(BF16), TN, preferred_element_type=F32)


def _split2(x):
    hi = x.astype(BF16)
    lo = (x - hi.astype(F32)).astype(BF16)
    return hi, lo


def _split3(x):
    hi = x.astype(BF16)
    r1 = x - hi.astype(F32)
    mid = r1.astype(BF16)
    lo = (r1 - mid.astype(F32)).astype(BF16)
    return hi, mid, lo


def _segsum(x, ones_bd):
    hi, lo = _split2(x)
    return (jnp.dot(hi, ones_bd, preferred_element_type=F32)
            + jnp.dot(lo, ones_bd, preferred_element_type=F32))


def _sigmoid(x):
    return 1.0 / (1.0 + jnp.exp(-x))


def _eye(n):
    return jnp.where(lax.broadcasted_iota(jnp.int32, (n, n), 0) == lax.broadcasted_iota(jnp.int32, (n, n), 1),
                     1.0, 0.0)


def _rmsnorm(x, g):
    return x * lax.rsqrt(jnp.mean(x * x, axis=-1, keepdims=True) + NORM_EPS) * g


def _params(sem):
    return pltpu.CompilerParams(dimension_semantics=sem, vmem_limit_bytes=VMEM_LIMIT)


def _t5_bucket(dist):
    dist = np.asarray(dist, dtype=np.int64)
    exact = N_BUCKETS // 2
    scaled = np.log(np.maximum(dist, 1) / exact) / np.log(MAX_DISTANCE / exact)
    large = np.minimum(exact + (scaled * (N_BUCKETS - exact)).astype(np.int64), N_BUCKETS - 1)
    return np.where(dist < exact, dist, large).astype(np.int32)


PROJ = QKV_W + RWKV_PROJ


def _inproj_kernel(x_ref, g_ref, w_ref, qkv_ref, rkv_ref, lora_ref):
    xb = _rmsnorm(x_ref[...], g_ref[...]).astype(BF16)
    proj = lambda a, b: lax.dot_general(xb, w_ref[a:b, :].astype(BF16), NT, preferred_element_type=F32)
    qkv_ref[...] = proj(0, QKV_W)
    rkv_ref[...] = proj(QKV_W, QKV_W + RKV_W)
    lora_ref[:, 0:LORA_W] = proj(QKV_W + RKV_W, PROJ)
    lora_ref[:, LORA_W:LORA_PAD] = jnp.zeros((x_ref.shape[0], LORA_PAD - LORA_W), F32)


def _inproj(x2d, g, w, tm):
    m = x2d.shape[0]
    row = lambda i: (i, 0)
    fixed = lambda i: (0, 0)
    return pl.pallas_call(
        _inproj_kernel,
        grid=(m // tm,),
        in_specs=[pl.BlockSpec((tm, D_MODEL), row),
                  pl.BlockSpec((1, D_MODEL), fixed),
                  pl.BlockSpec((PROJ, D_MODEL), fixed)],
        out_specs=[pl.BlockSpec((tm, QKV_W), row),
                   pl.BlockSpec((tm, RKV_W), row),
                   pl.BlockSpec((tm, LORA_PAD), row)],
        out_shape=[jax.ShapeDtypeStruct((m, QKV_W), F32),
                   jax.ShapeDtypeStruct((m, RKV_W), F32),
                   jax.ShapeDtypeStruct((m, LORA_PAD), F32)],
        compiler_params=_params(("arbitrary",)),
        name="inproj",
    )(x2d, g, w)


def _prep_math(p, p_prev, l, l_prev, mu_p, mu_l, wcat, w0, a0, k_k, k_a, r_k, ones_bd):
    pm = p + mu_p * (p_prev - p)
    lm = l + mu_l * (l_prev - l)
    r = pm[:, 0:RWKV_WIDTH]
    kr = pm[:, RWKV_WIDTH:2 * RWKV_WIDTH]
    vr = pm[:, 2 * RWKV_WIDTH:3 * RWKV_WIDTH]
    lane = lax.broadcasted_iota(jnp.int32, lm.shape, 1)
    feat = jnp.where(lane < LORA_DECAY, jnp.tanh(lm),
                     jnp.where(lane < LORA_DECAY + LORA_ICLR, lm, _sigmoid(lm)))
    z = jnp.dot(feat.astype(BF16), wcat, preferred_element_type=F32)
    zd = -(w0 + z[:, 0:RWKV_WIDTH])
    softplus = jnp.maximum(zd, 0.0) + jnp.log(1.0 + jnp.exp(-jnp.abs(zd)))
    lw = -jnp.exp(-softplus - 0.5)
    a = _sigmoid(a0 + z[:, RWKV_WIDTH:2 * RWKV_WIDTH])
    gate = z[:, 2 * RWKV_WIDTH:3 * RWKV_WIDTH]
    kk = kr * k_k
    head_sum = lambda t: jnp.dot(t.astype(BF16), ones_bd, preferred_element_type=F32)
    kk = kk / jnp.maximum(jnp.sqrt(head_sum(kk * kk)), 1e-12)
    k2 = kr * (1.0 + (a - 1.0) * k_a)
    bonus = head_sum(r * k2 * r_k) * vr
    return r, lw, k2, vr, kk, kk * a, gate, bonus


def _inproj_prep_kernel(x_ref, g_ref, w_ref, mu_p_ref, mu_l_ref, wcat_ref, w0_ref, a0_ref, kk_ref, ka_ref,
                        rk_ref, ones_ref, qkv_ref, *refs):
    outs, (sp_ref, sl_ref, kw_ref, vw_ref), (cp_ref, cl_ref, wb_ref) = refs[:8], refs[8:12], refs[12:]

    @pl.when(jnp.logical_and(pl.program_id(0) == 0, pl.program_id(1) == 0))
    def _():
        for c0 in range(0, PROJ, RWKV_WIDTH):
            c1 = min(c0 + RWKV_WIDTH, PROJ)
            wb_ref[c0:c1, :] = w_ref[c0:c1, :].astype(BF16)
        wb_ref[PROJ:PROJ_PAD, :] = jnp.zeros((PROJ_PAD - PROJ, D_MODEL), BF16)

    @pl.when(pl.program_id(1) == 0)
    def _():
        cp_ref[...] = jnp.zeros_like(cp_ref)
        cl_ref[...] = jnp.zeros_like(cl_ref)

    xb = _rmsnorm(x_ref[...], g_ref[...]).astype(BF16)
    proj = lambda a, b: lax.dot_general(xb, wb_ref[a:b, :], NT, preferred_element_type=F32)
    p = proj(QKV_W, QKV_W + RKV_W)
    l = proj(QKV_W + RKV_W, PROJ_PAD)
    tm = p.shape[0]
    first_p = lax.broadcasted_iota(jnp.int32, p.shape, 0) == 0
    first_l = lax.broadcasted_iota(jnp.int32, l.shape, 0) == 0
    p_prev = jnp.where(first_p, cp_ref[...], pltpu.roll(p, 1, 0))
    l_prev = jnp.where(first_l, cl_ref[...], pltpu.roll(l, 1, 0))
    cp_ref[...] = p[tm - 1:tm, :]
    cl_ref[...] = l[tm - 1:tm, :]
    sp_ref[0] = p[tm - 1:tm, :]
    sl_ref[0] = l[tm - 1:tm, :]
    res = _prep_math(p, p_prev, l, l_prev, mu_p_ref[...], mu_l_ref[...], wcat_ref[...], w0_ref[...],
                     a0_ref[...], kk_ref[...], ka_ref[...], rk_ref[...], ones_ref[...])
    for o, v in zip(outs, res):
        o[...] = v
    qkv = proj(0, QKV_W)
    qkv_ref[...] = qkv
    kw_ref[...] = qkv[:, ATT_WIDTH:2 * ATT_WIDTH]
    vw_ref[...] = qkv[:, 2 * ATT_WIDTH:QKV_W]


def _prep_sample_kernel(p_ref, pp_ref, l_ref, lp_ref, mu_p_ref, mu_l_ref, wcat_ref, w0_ref, a0_ref,
                        kk_ref, ka_ref, rk_ref, ones_ref, *outs):
    res = _prep_math(p_ref[...], pp_ref[...], l_ref[...], lp_ref[...], mu_p_ref[...], mu_l_ref[...],
                     wcat_ref[...], w0_ref[...], a0_ref[...], kk_ref[...], ka_ref[...], rk_ref[...],
                     ones_ref[...])
    for o, v in zip(outs, res):
        o[...] = v


def _inproj_prep_prompt(x2d, g, w, pw, batch, seq, tm):
    nt = seq // tm
    row = lambda b, j: (b * nt + j, 0)
    fixed = lambda b, j: (0, 0)
    vec = lambda n: pl.BlockSpec((1, n), fixed)
    once = pl.Buffered(1)
    m = x2d.shape[0]
    last = lambda n: pl.BlockSpec((1, 1, n), lambda b, j: (b, 0, 0))
    nw = WIN // tm
    assert nw * tm == WIN and nw <= nt
    win = pl.BlockSpec((tm, ATT_WIDTH), lambda b, j: (b * nw + jnp.maximum(j - (nt - nw), 0), 0))
    return pl.pallas_call(
        _inproj_prep_kernel,
        grid=(batch, nt),
        in_specs=[pl.BlockSpec((tm, D_MODEL), row), vec(D_MODEL),
                  pl.BlockSpec((PROJ, D_MODEL), fixed, pipeline_mode=once),
                  vec(RKV_W), vec(LORA_PAD), pl.BlockSpec((LORA_PAD, RKV_W), fixed),
                  vec(RWKV_WIDTH), vec(RWKV_WIDTH), vec(RWKV_WIDTH), vec(RWKV_WIDTH), vec(RWKV_WIDTH),
                  pl.BlockSpec((RWKV_WIDTH, RWKV_WIDTH), fixed)],
        out_specs=[pl.BlockSpec((tm, QKV_W), row)] + [pl.BlockSpec((tm, RWKV_WIDTH), row)] * 8
                  + [last(RKV_W), last(LORA_PAD), win, win],
        out_shape=[jax.ShapeDtypeStruct((m, QKV_W), F32)] + [jax.ShapeDtypeStruct((m, RWKV_WIDTH), F32)] * 8
                  + [jax.ShapeDtypeStruct((batch, 1, RKV_W), F32), jax.ShapeDtypeStruct((batch, 1, LORA_PAD), F32)]
                  + [jax.ShapeDtypeStruct((batch * WIN, ATT_WIDTH), F32)] * 2,
        scratch_shapes=[pltpu.VMEM((1, RKV_W), F32), pltpu.VMEM((1, LORA_PAD), F32),
                        pltpu.VMEM((PROJ_PAD, D_MODEL), BF16)],
        compiler_params=_params(("arbitrary", "arbitrary")),
        name="inproj_prep_prompt",
    )(x2d, g, w, *pw)


def _prep_sample(rkv, rkv_prev, lora, lora_prev, pw):
    m = rkv.shape[0]
    full = lambda a: pl.BlockSpec(a.shape, lambda i: (0,) * a.ndim)
    args = (rkv, rkv_prev, lora, lora_prev) + tuple(pw)
    return pl.pallas_call(
        _prep_sample_kernel,
        grid=(1,),
        in_specs=[full(a) for a in args],
        out_specs=[pl.BlockSpec((m, RWKV_WIDTH), lambda i: (0, 0))] * 8,
        out_shape=[jax.ShapeDtypeStruct((m, RWKV_WIDTH), F32)] * 8,
        compiler_params=_params(("arbitrary",)),
        name="rwkv_prep_sample",
    )(*args)


def _cumsum_rows(x, tri):
    tri_b = tri.astype(BF16)
    out = None
    for piece in _split3(x):
        t = jnp.dot(tri_b, piece, preferred_element_type=F32)
        out = t if out is None else out + t
    return out


def _wkv_kernel(r_ref, lw_ref, k_ref, v_ref, kk_ref, b_ref, gate_ref, bonus_ref, lnw_ref, lnb_ref,
                y_ref, s_ref, st_ref):
    c = pl.program_id(0)

    @pl.when(c == 0)
    def _():
        st_ref[...] = jnp.zeros_like(st_ref)

    nb = r_ref.shape[0]
    n = CHUNK
    row = lax.broadcasted_iota(jnp.int32, (n, n), 0)
    col = lax.broadcasted_iota(jnp.int32, (n, n), 1)
    eye = jnp.where(row == col, 1.0, 0.0)
    tri = jnp.where(row >= col, 1.0, 0.0)
    row2 = lax.broadcasted_iota(jnp.int32, (2 * n, 2 * n), 0)
    col2 = lax.broadcasted_iota(jnp.int32, (2 * n, 2 * n), 1) % n
    gram_mask = jnp.logical_or(row2 % n > col2, jnp.logical_and(row2 >= n, row2 % n == col2))
    zeros = jnp.zeros((n, HEAD_DIM), BF16)

    units = range(nb * N_HEADS)
    stack = lambda xs, ys: [jnp.concatenate([x, y], axis=0) for x, y in zip(xs, ys)]

    def prelude(rows):
        ah, bh, kh, rh, beh, keh, vh, ge = ([] for _ in range(8))
        for bi in range(nb):
            lw = lw_ref[bi, rows, :]
            lg = _cumsum_rows(lw, tri)
            g_inv = jnp.exp(-lg)
            lg_end = lg[n - 1:n, :]
            g_to_end = jnp.exp(lg_end - lg)
            g_end = jnp.exp(lg_end)
            kk = kk_ref[bi, rows, :]
            b = b_ref[bi, rows, :]
            k = k_ref[bi, rows, :]
            full = (-kk * jnp.exp(lg - lw), b * g_inv, k * g_inv, r_ref[bi, rows, :] * jnp.exp(lg),
                    b * g_to_end, k * g_to_end, v_ref[bi, rows, :], g_end)
            for h in range(N_HEADS):
                sl = slice(h * HEAD_DIM, (h + 1) * HEAD_DIM)
                for dst, t in zip((ah, bh, kh, rh, beh, keh, vh), full[:7]):
                    dst.append(t[:, sl].astype(BF16))
                ge.append(g_end[:, sl])

        gram = [jnp.where(gram_mask, _dot_nt(ar, bk), 0.0) for ar, bk in zip(stack(ah, rh), stack(bh, kh))]
        top = [g[:n].astype(BF16) for g in gram]
        bot = [g[n:].astype(BF16) for g in gram]
        lv = [_dot(top[u], jnp.concatenate([zeros, vh[u]], axis=0)) for u in units]
        tm_ = [eye + gram[u][:n, :n] for u in units]
        lp = [_dot(t[:, :n], t[:, :n]).astype(BF16) for t in top]
        span = 2
        while 2 * span < n:
            prod = [_dot(x, p) for x, p in zip(stack(lp, [t.astype(BF16) for t in tm_]), lp)]
            tm_ = [tm_[u] + prod[u][n:] for u in units]
            lp = [p[:n].astype(BF16) for p in prod]
            span *= 2
        tm_ = [(tm_[u] + _dot(tm_[u], lp[u])).astype(BF16) for u in units]
        a_hat = [_dot(tm_[u], ah[u]).astype(BF16) for u in units]
        u0 = [_dot(tm_[u], lv[u]) for u in units]
        return rows, stack(a_hat, rh), u0, vh, bot, stack(beh, keh), ge

    def finish(rows, a_hat_r, u0, vh, bot, be_ke, ge):
        s0 = [st_ref[u] for u in units]
        on_s0 = [_dot_nt(x, s.astype(BF16)) for x, s in zip(a_hat_r, s0)]
        uv = stack([(on_s0[u][:n] + u0[u]).astype(BF16) for u in units], vh)
        y = [on_s0[u][n:] + _dot(bot[u], uv[u]) for u in units]
        for u in units:
            st_ref[u] = s0[u] * ge[u] + _dot_tn(uv[u], be_ke[u])
        for bi in range(nb):
            normed = []
            for yu in y[bi * N_HEADS:(bi + 1) * N_HEADS]:
                dy = yu - jnp.mean(yu, axis=-1, keepdims=True)
                normed.append(dy * lax.rsqrt(jnp.mean(dy * dy, axis=-1, keepdims=True) + GN_EPS))
            yn = jnp.concatenate(normed, axis=1)
            y_ref[bi, rows, :] = ((yn * lnw_ref[...] + lnb_ref[...] + bonus_ref[bi, rows, :])
                                  * gate_ref[bi, rows, :]).astype(BF16)

    for j in range(r_ref.shape[1] // n):
        finish(*prelude(pl.ds(j * n, n)))

    @pl.when(c == pl.num_programs(0) - 1)
    def _():
        s_ref[...] = st_ref[...]


def _wkv_prompt(r, lw, k2, v, kk, b, gate, bonus, ln_w, ln_b, batch, seq):
    rows = WKV_CHUNKS_PER_STEP * CHUNK
    nc = seq // rows
    assert nc * rows == seq
    seqs = lambda t: t.reshape(batch, seq, RWKV_WIDTH)
    spec = pl.BlockSpec((batch, rows, RWKV_WIDTH), lambda c: (0, c, 0))
    vec = pl.BlockSpec((1, RWKV_WIDTH), lambda c: (0, 0))
    nu = batch * N_HEADS
    y, s = pl.pallas_call(
        _wkv_kernel,
        grid=(nc,),
        in_specs=[spec] * 8 + [vec, vec],
        out_specs=[spec, pl.BlockSpec((nu, HEAD_DIM, HEAD_DIM), lambda c: (0, 0, 0))],
        out_shape=[jax.ShapeDtypeStruct((batch, seq, RWKV_WIDTH), BF16),
                   jax.ShapeDtypeStruct((nu, HEAD_DIM, HEAD_DIM), F32)],
        scratch_shapes=[pltpu.VMEM((nu, HEAD_DIM, HEAD_DIM), F32)],
        compiler_params=_params(("arbitrary",)),
        name="wkv_prompt",
    )(*(seqs(t) for t in (r, lw, k2, v, kk, b, gate, bonus)), ln_w, ln_b)
    return y.reshape(batch * seq, RWKV_WIDTH), s.reshape(batch, N_HEADS, HEAD_DIM, HEAD_DIM)


def _wkv_step_kernel(s_ref, lw_ref, kk_ref, b_ref, k_ref, r_ref, v_ref, y_ref, so_ref):
    s = s_ref[...]
    eye = _eye(HEAD_DIM)
    v_col = jnp.sum(eye * v_ref[...], axis=-1, keepdims=True)
    sa = jnp.sum(s * (-kk_ref[...]), axis=-1, keepdims=True)
    s = s * jnp.exp(lw_ref[...]) + sa * b_ref[...] + v_col * k_ref[...]
    so_ref[...] = s
    y_col = jnp.sum(s * r_ref[...], axis=-1, keepdims=True)
    y_ref[...] = jnp.sum(eye * y_col, axis=-2, keepdims=True)


def _wkv_sample(state, lw, kk, b, k2, r, v, bb):
    nb = state.shape[0]
    rowv = lambda t: t.reshape(nb, N_HEADS, 1, HEAD_DIM)
    idx = lambda i: (i, 0, 0, 0)
    s_spec = pl.BlockSpec((bb, N_HEADS, HEAD_DIM, HEAD_DIM), idx)
    r_spec = pl.BlockSpec((bb, N_HEADS, 1, HEAD_DIM), idx)
    y, s_new = pl.pallas_call(
        _wkv_step_kernel,
        grid=(nb // bb,),
        in_specs=[s_spec] + [r_spec] * 6,
        out_specs=[r_spec, s_spec],
        out_shape=[jax.ShapeDtypeStruct((nb, N_HEADS, 1, HEAD_DIM), F32),
                   jax.ShapeDtypeStruct(state.shape, F32)],
        compiler_params=_params(("arbitrary",)),
        name="wkv_sample",
    )(state, rowv(lw), rowv(kk), rowv(b), rowv(k2), rowv(r), rowv(v))
    return y.reshape(nb, RWKV_WIDTH), s_new


def _bias_tables():
    i = np.arange(Q_BLOCK)[:, None]
    c = np.arange(2 * Q_BLOCK)[None, :]
    n = i + Q_BLOCK - c
    ok = (n >= 0) & (n <= Q_BLOCK)
    prompt = np.stack([np.where(ok, _t5_bucket(d * np.clip(n, 0, Q_BLOCK)), -1) for d in DILATIONS])
    dist = WIN - np.arange(WIN)
    sample = np.stack([np.where((dist % d == 0) & (dist // d <= Q_BLOCK), _t5_bucket(dist), -1)
                       for d in DILATIONS])
    sample = np.broadcast_to(sample[:, None, :], (len(DILATIONS), N_HEADS, WIN))
    return prompt.astype(np.int32), sample.astype(np.int32)


def _bias_kernel(rb_ref, rbc_ref, bp_ref, bs_ref, op_ref, os_ref):
    bp = bp_ref[0]
    bs = bs_ref[0]
    acc_s = jnp.full(bs.shape, NEG, F32)
    for bk in range(N_BUCKETS):
        acc_s = jnp.where(bs == bk, rbc_ref[bk], acc_s)
    os_ref[0] = acc_s
    for h in range(N_HEADS):
        acc = jnp.full(bp.shape, NEG, F32)
        for bk in range(N_BUCKETS):
            acc = jnp.where(bp == bk, rb_ref[bk, h], acc)
        op_ref[0, h] = acc * LOG2E


def _bias_build(rel_bias):
    bp, bs = _bias_tables()
    rbc = rel_bias.reshape(N_BUCKETS, N_HEADS, 1)
    nb = len(DILATIONS)
    return pl.pallas_call(
        _bias_kernel,
        grid=(nb,),
        in_specs=[pl.BlockSpec(memory_space=pltpu.SMEM),
                  pl.BlockSpec((N_BUCKETS, N_HEADS, 1), lambda i: (0, 0, 0)),
                  pl.BlockSpec((1, Q_BLOCK, 2 * Q_BLOCK), lambda i: (i, 0, 0)),
                  pl.BlockSpec((1, N_HEADS, WIN), lambda i: (i, 0, 0))],
        out_specs=[pl.BlockSpec((1, N_HEADS, Q_BLOCK, 2 * Q_BLOCK), lambda i: (i, 0, 0, 0)),
                   pl.BlockSpec((1, N_HEADS, WIN), lambda i: (i, 0, 0))],
        out_shape=[jax.ShapeDtypeStruct((nb, N_HEADS, Q_BLOCK, 2 * Q_BLOCK), F32),
                   jax.ShapeDtypeStruct((nb, N_HEADS, WIN), F32)],
        compiler_params=_params(("arbitrary",)),
        name="bias_build",
    )(rel_bias, rbc, jnp.asarray(bp), jnp.asarray(bs)), rbc[0]


ATTN_GROUP = 4


def _attn_units():
    units = []
    for br, d in enumerate(DILATIONS):
        nblk = SUPER // (d * Q_BLOCK)
        for r in range(d):
            rows = lambda j, r=r, d=d: (pl.ds(r + d * Q_BLOCK * j, Q_BLOCK, stride=d) if d > 1
                                        else pl.ds(Q_BLOCK * j, Q_BLOCK))
            for blk in range(nblk):
                units.append((br, rows(blk), rows(blk - 1 if blk else nblk - 1), blk == 0))
    return units


def _attn_kernel(q_ref, kc_ref, kp_ref, vc_ref, vp_ref, bias_ref, o_ref, acc_ref, m_ref):
    has_prev = pl.program_id(2) > 0
    is_a = lax.broadcasted_iota(jnp.int32, (Q_BLOCK, 128), 1) < HEAD_DIM
    is_a2 = lax.broadcasted_iota(jnp.int32, (2 * Q_BLOCK, 128), 1) < HEAD_DIM
    live = jnp.logical_or(has_prev,
                          lax.broadcasted_iota(jnp.int32, (2 * Q_BLOCK, 2 * Q_BLOCK), 1) >= Q_BLOCK)
    units = _attn_units()
    for g0 in range(0, len(units), ATTN_GROUP):
        group = units[g0:g0 + ATTN_GROUP]
        ops = []
        for br, cur, prv, from_prev in group:
            qs = q_ref[cur, :] * (ATT_SCALE * LOG2E)
            q2 =jnp.concatenate([jnp.where(is_a, qs, 0.0), jnp.where(is_a, 0.0, qs)], axis=0)
            k2 = jnp.concatenate([(kp_ref if from_prev else kc_ref)[prv, :], kc_ref[cur, :]], axis=0)
            v2 = jnp.concatenate([(vp_ref if from_prev else vc_ref)[prv, :], vc_ref[cur, :]], axis=0)
            ops.append((br, cur, from_prev, q2.astype(BF16), k2.astype(BF16),
                        jnp.where(is_a2, v2, 1.0).astype(BF16), jnp.where(is_a2, 1.0, v2).astype(BF16)))
        logits = []
        for br, cur, from_prev, q2, k2, va, vb in ops:
            lg = lax.dot_general(q2, k2, NT, preferred_element_type=F32) + bias_ref[br]
            logits.append(jnp.where(live, lg, NEG) if from_prev else lg)
        ms = [jnp.max(lg, axis=-1, keepdims=True) for lg in logits]
        ps = [jnp.exp2(lg - m).astype(BF16) for lg, m in zip(logits, ms)]
        for (br, cur, _, _, _, va, vb), p, m in zip(ops, ps, ms):
            acc_ref[br, 0, cur, :] = jnp.dot(p[:Q_BLOCK], va, preferred_element_type=F32)
            acc_ref[br, 1, cur, :] = jnp.dot(p[Q_BLOCK:], vb, preferred_element_type=F32)
            m_ref[br, 0, cur, :] = jnp.broadcast_to(m[:Q_BLOCK], (Q_BLOCK, 128))
            m_ref[br, 1, cur, :] = jnp.broadcast_to(m[Q_BLOCK:], (Q_BLOCK, 128))

    for i in range(SUPER // Q_BLOCK):
        rs = pl.ds(i * Q_BLOCK, Q_BLOCK)
        outs = []
        for hd in range(2):
            ms = [m_ref[br, hd, rs, :] for br in range(len(DILATIONS))]
            m = functools.reduce(jnp.maximum, ms)
            tot = 0.0
            for br in range(len(DILATIONS)):
                tot = tot + jnp.exp2(ms[br] - m) * acc_ref[br, hd, rs, :]
            outs.append(tot / pltpu.roll(tot, HEAD_DIM, 1))
        o_ref[rs, :] = jnp.where(is_a, outs[0], outs[1]).astype(o_ref.dtype)


def _attn_prompt(qkv, bias_p, batch, seq):
    ns = seq // SUPER
    npair = ATT_WIDTH // 128
    blk = lambda col0, prev: pl.BlockSpec(
        (SUPER, 128),
        (lambda b, hp, sb: (b * ns + jnp.maximum(sb - 1, 0), col0 + hp)) if prev
        else (lambda b, hp, sb: (b * ns + sb, col0 + hp)))
    nb = len(DILATIONS)
    return pl.pallas_call(
        _attn_kernel,
        grid=(batch, npair, ns),
        in_specs=[blk(0, False), blk(npair, False), blk(npair, True), blk(2 * npair, False),
                  blk(2 * npair, True),
                  pl.BlockSpec((nb, None, 2 * Q_BLOCK, 2 * Q_BLOCK), lambda b, hp, sb: (0, hp, 0, 0))],
        out_specs=pl.BlockSpec((SUPER, 128), lambda b, hp, sb: (b * ns + sb, hp)),
        out_shape=jax.ShapeDtypeStruct((batch * seq, ATT_WIDTH), BF16),
        scratch_shapes=[pltpu.VMEM((nb, 2, SUPER, 128), F32)] * 2,
        compiler_params=_params(("arbitrary", "arbitrary", "arbitrary")),
        name="attn_prompt",
    )(qkv, qkv, qkv, qkv, qkv, bias_p.reshape(nb, npair, 2 * Q_BLOCK, 2 * Q_BLOCK))


WINDOW_HEADS = 4


def _window_body(q_ref, kn_ref, vn_ref, ck_ref, cv_ref, bias_ref, rb0_ref, o_ref, ok_ref, ov_ref, wt_ref):
    nh = WINDOW_HEADS
    lane = lax.broadcasted_iota(jnp.int32, (HEAD_DIM, WIN), 1)
    last = lane == WIN - 1
    eye = _eye(HEAD_DIM)
    col = lambda ref, h: jnp.sum(eye * ref[h:h + 1, :], axis=-1, keepdims=True)
    l_new = jnp.sum(kn_ref[...] * (q_ref[...] * ATT_SCALE), axis=-1, keepdims=True) + rb0_ref[...]
    rows = []
    for h in range(nh):
        kh = ck_ref[h]
        rows.append(jnp.sum(kh * (col(q_ref, h) * ATT_SCALE), axis=0, keepdims=True))
        ok_ref[h] = jnp.where(last, col(kn_ref, h), pltpu.roll(kh, WIN - 1, 1))
    logits = jnp.concatenate(rows, axis=0)
    parts = []
    for br in range(len(DILATIONS)):
        lg = logits + bias_ref[br]
        m = jnp.maximum(jnp.max(lg, axis=-1, keepdims=True), l_new)
        p = jnp.exp(lg - m)
        p_new = jnp.exp(l_new - m)
        parts.append((p, p_new, m, jnp.sum(p, axis=-1, keepdims=True) + p_new))
    m = functools.reduce(jnp.maximum, [pt[2] for pt in parts])
    wt = 0.0
    w_new = 0.0
    den = 0.0
    for p, p_new, m_g, s_g in parts:
        w = jnp.exp(m_g - m)
        wt = wt + w * p
        w_new = w_new + w * p_new
        den = den + w * s_g
    wt_ref[0:nh, :] = wt / den
    w_new = w_new / den
    for h in range(nh):
        vh = cv_ref[h]
        v_new = col(vn_ref, h)
        o_col = jnp.sum(vh * wt_ref[h:h + 1, :], axis=1, keepdims=True) + v_new * w_new[h:h + 1, :]
        o_ref[h:h + 1, :] = jnp.sum(eye * o_col, axis=0, keepdims=True)
        ov_ref[h] = jnp.where(last, v_new, pltpu.roll(vh, WIN - 1, 1))


N_WINDOW_IN = 7


def _window_operands(q, k_new, v_new, cache_k, cache_v, bias_s, rb0, step_of):
    nb = q.shape[0]
    nh = WINDOW_HEADS
    groups = N_HEADS // nh
    seq_grp = lambda *g: (step_of(*g) // groups, step_of(*g) % groups)
    rows = lambda t: t.reshape(nb, groups, nh, HEAD_DIM)
    wins = lambda t: t.reshape(nb, groups, nh, HEAD_DIM, WIN)
    row_spec = pl.BlockSpec((None, None, nh, HEAD_DIM), lambda *g: seq_grp(*g) + (0, 0))
    win_spec = pl.BlockSpec((None, None, nh, HEAD_DIM, WIN), lambda *g: seq_grp(*g) + (0, 0, 0))
    nbr = len(DILATIONS)
    bias_spec = pl.BlockSpec((nbr, None, nh, WIN), lambda *g: (0, seq_grp(*g)[1], 0, 0))
    rb0_spec = pl.BlockSpec((None, nh, 1), lambda *g: (seq_grp(*g)[1], 0, 0))
    args = (rows(q), rows(k_new), rows(v_new), wins(cache_k), wins(cache_v),
            bias_s.reshape(nbr, groups, nh, WIN), rb0.reshape(groups, nh, 1))
    in_specs = [row_spec, row_spec, row_spec, win_spec, win_spec, bias_spec, rb0_spec]
    assert len(args) == len(in_specs) == N_WINDOW_IN
    out_specs = [row_spec, win_spec, win_spec]
    out_shape = [jax.ShapeDtypeStruct((nb, groups, nh, HEAD_DIM), F32),
                 jax.ShapeDtypeStruct((nb, groups, nh, HEAD_DIM, WIN), F32),
                 jax.ShapeDtypeStruct((nb, groups, nh, HEAD_DIM, WIN), F32)]
    return args, in_specs, out_specs, out_shape


def _outproj_kernel(x_ref, attn_ref, y_ref, gate_ref, bonus_ref, lnw_ref, lnb_ref, wo_ref, gffn_ref,
                    ones_ref, h_ref, xn_ref):
    ones_bd = ones_ref[...]
    y = y_ref[...]
    mu = _segsum(y, ones_bd) * (1.0 / HEAD_DIM)
    dy = y - mu
    var = _segsum(dy * dy, ones_bd) * (1.0 / HEAD_DIM)
    yn = dy * lax.rsqrt(var + GN_EPS) * lnw_ref[...] + lnb_ref[...]
    rw = (yn + bonus_ref[...]) * gate_ref[...]
    mixed = (jnp.dot(attn_ref[...].astype(BF16), wo_ref[0:ATT_WIDTH, :], preferred_element_type=F32)
             + jnp.dot(rw.astype(BF16), wo_ref[ATT_WIDTH:D_MODEL, :], preferred_element_type=F32))
    h = x_ref[...] + mixed
    h_ref[...] = h
    xn_ref[...] = _rmsnorm(h, gffn_ref[...]).astype(BF16)


def _outproj(x2d, attn, y, gate, bonus, ln_w, ln_b, w_o, g_ffn, ones_bd, tm):
    m = x2d.shape[0]
    row = lambda i: (i, 0)
    fixed = lambda i: (0, 0)
    half = pl.BlockSpec((tm, RWKV_WIDTH), row)
    full = pl.BlockSpec((tm, D_MODEL), row)
    return pl.pallas_call(
        _outproj_kernel,
        grid=(m // tm,),
        in_specs=[full, half, half, half, half,
                  pl.BlockSpec((1, RWKV_WIDTH), fixed), pl.BlockSpec((1, RWKV_WIDTH), fixed),
                  pl.BlockSpec((D_MODEL, D_MODEL), fixed), pl.BlockSpec((1, D_MODEL), fixed),
                  pl.BlockSpec((RWKV_WIDTH, RWKV_WIDTH), fixed)],
        out_specs=[full, full],
        out_shape=[jax.ShapeDtypeStruct((m, D_MODEL), F32), jax.ShapeDtypeStruct((m, D_MODEL), BF16)],
        compiler_params=_params(("arbitrary",)),
        name="outproj",
    )(x2d, attn, y, gate, bonus, ln_w, ln_b, w_o, g_ffn, ones_bd)


def _ffn_tail(xn, h, gp_m1, gp_m2, gp_of, wi_ref, cw_ref, cb_ref, wout_ref, gfin_ref, o_ref):
    acc = None
    for c in range(D_FF // FF_CHUNK):
        cs = slice(c * FF_CHUNK, (c + 1) * FF_CHUNK)
        gp = gp_of(c)
        up = jnp.dot(xn, wi_ref[:, D_FF + c * FF_CHUNK:D_FF + (c + 1) * FF_CHUNK], preferred_element_type=F32)
        conv = (cb_ref[:, cs] + cw_ref[0:1, cs] * gp_m2(c, gp) + cw_ref[1:2, cs] * gp_m1(c, gp)
                + cw_ref[2:3, cs] * gp)
        act = conv * _sigmoid(conv) * up
        part = jnp.dot(act.astype(BF16), wout_ref[cs, :], preferred_element_type=F32)
        acc = part if acc is None else acc + part
    o_ref[...] = _rmsnorm(h + acc, gfin_ref[...])


def _ffn_prompt_kernel(x_ref, attn_ref, rw_ref, wo_ref, gffn_ref, wi_ref, cw_ref, cb_ref, wout_ref,
                       gfin_ref, *refs):
    n = N_WINDOW_IN
    win_in, (o_ref, conv_ref), win_out, (carry_ref, wt_ref) = refs[:n], refs[n:n + 2], refs[n + 2:n + 5], refs[n + 5:]

    @pl.when(pl.program_id(1) == 0)
    def _():
        carry_ref[...] = jnp.zeros_like(carry_ref)

    _window_body(*win_in, *win_out, wt_ref)

    h = (x_ref[...]
         + jnp.dot(attn_ref[...], wo_ref[0:ATT_WIDTH, :], preferred_element_type=F32)
         + jnp.dot(rw_ref[...], wo_ref[ATT_WIDTH:D_MODEL, :], preferred_element_type=F32))
    xn = _rmsnorm(h, gffn_ref[...]).astype(BF16)
    tm = xn.shape[0]
    rowi = lax.broadcasted_iota(jnp.int32, (tm, FF_CHUNK), 0)
    gps = {}

    def gp_of(c):
        cs = slice(c * FF_CHUNK, (c + 1) * FF_CHUNK)
        gps[c] = jnp.dot(xn, wi_ref[:, cs], preferred_element_type=F32)
        return gps[c]

    def gp_m1(c, gp):
        cs = slice(c * FF_CHUNK, (c + 1) * FF_CHUNK)
        return jnp.where(rowi == 0, carry_ref[1:2, cs], pltpu.roll(gp, 1, 0))

    def gp_m2(c, gp):
        cs = slice(c * FF_CHUNK, (c + 1) * FF_CHUNK)
        return jnp.where(rowi == 0, carry_ref[0:1, cs],
                         jnp.where(rowi == 1, carry_ref[1:2, cs], pltpu.roll(gp, 2, 0)))

    _ffn_tail(xn, h, gp_m1, gp_m2, gp_of, wi_ref, cw_ref, cb_ref, wout_ref, gfin_ref, o_ref)
    for c, gp in gps.items():
        cs = slice(c * FF_CHUNK, (c + 1) * FF_CHUNK)
        carry_ref[:, cs] = gp[tm - 2:tm, :]
        conv_ref[0, :, cs] = gp[tm - 2:tm, :]


def _ffn_sample_kernel(xn_ref, h_ref, p1_ref, p2_ref, wi_ref, cw_ref, cb_ref, wout_ref, gfin_ref,
                       o_ref, gp_ref):
    xn = xn_ref[...]

    def gp_of(c):
        cs = slice(c * FF_CHUNK, (c + 1) * FF_CHUNK)
        gp = jnp.dot(xn, wi_ref[:, cs], preferred_element_type=F32)
        gp_ref[:, cs] = gp
        return gp

    gp_m1 = lambda c, gp: p1_ref[:, c * FF_CHUNK:(c + 1) * FF_CHUNK]
    gp_m2 = lambda c, gp: p2_ref[:, c * FF_CHUNK:(c + 1) * FF_CHUNK]
    _ffn_tail(xn, h_ref[...], gp_m1, gp_m2, gp_of, wi_ref, cw_ref, cb_ref, wout_ref, gfin_ref, o_ref)


def _ffn_weight_specs(fixed):
    once = pl.Buffered(1)
    return [pl.BlockSpec((D_MODEL, 2 * D_FF), fixed, pipeline_mode=once),
            pl.BlockSpec((CONV_W, D_FF), fixed), pl.BlockSpec((1, D_FF), fixed),
            pl.BlockSpec((D_FF, D_MODEL), fixed, pipeline_mode=once), pl.BlockSpec((1, D_MODEL), fixed)]


def _ffn_prompt(x2d, attn, rw, w_o, g_ffn, fw, batch, seq, window):
    nt = window[0].shape[0] * (N_HEADS // WINDOW_HEADS) // batch
    tm = seq // nt
    assert tm * nt == seq and tm % 16 == 0
    row = lambda b, j: (b * nt + j, 0)
    fixed = lambda b, j: (0, 0)
    full = pl.BlockSpec((tm, D_MODEL), row)
    half = pl.BlockSpec((tm, ATT_WIDTH), row)
    mix_specs = [full, half, half, pl.BlockSpec((D_MODEL, D_MODEL), fixed), pl.BlockSpec((1, D_MODEL), fixed)]
    w_args, w_in_specs, w_out_specs, w_out_shape = _window_operands(*window, lambda b, j: b * nt + j)
    y, conv, attn_s, win_k, win_v = pl.pallas_call(
        _ffn_prompt_kernel,
        grid=(batch, nt),
        in_specs=mix_specs + _ffn_weight_specs(fixed) + w_in_specs,
        out_specs=[full, pl.BlockSpec((1, CONV_W - 1, D_FF), lambda b, j: (b, 0, 0))] + w_out_specs,
        out_shape=[jax.ShapeDtypeStruct((batch * seq, D_MODEL), F32),
                   jax.ShapeDtypeStruct((batch, CONV_W - 1, D_FF), F32)] + w_out_shape,
        scratch_shapes=[pltpu.VMEM((CONV_W - 1, D_FF), F32), pltpu.VMEM((8, WIN), F32)],
        compiler_params=_params(("arbitrary", "arbitrary")),
        name="ffn_prompt",
    )(x2d, attn, rw, w_o, g_ffn, *fw, *w_args)
    nb = window[0].shape[0]
    return (y, conv, attn_s.reshape(nb, ATT_WIDTH), win_k.reshape(nb, N_HEADS, HEAD_DIM, WIN),
            win_v.reshape(nb, N_HEADS, HEAD_DIM, WIN))


def _ffn_sample(xn, h, prev1, prev2, fw):
    m = xn.shape[0]
    fixed = lambda i: (0, 0)
    full = pl.BlockSpec((m, D_MODEL), fixed)
    ffs = pl.BlockSpec((m, D_FF), fixed)
    return pl.pallas_call(
        _ffn_sample_kernel,
        grid=(1,),
        in_specs=[full, full, ffs, ffs] + _ffn_weight_specs(fixed),
        out_specs=[full, ffs],
        out_shape=[jax.ShapeDtypeStruct((m, D_MODEL), F32), jax.ShapeDtypeStruct((m, D_FF), F32)],
        compiler_params=_params(("arbitrary",)),
        name="ffn_sample",
    )(xn, h, prev1, prev2, *fw)


def _layer_weights(g_mix, w_in, tok_mu, w0, w_decay_up, a0, w_iclr_up, w_gate_up, k_k, k_a, r_k,
                   ln_x_w, ln_x_b, w_o, g_ffn, w_ffn_in, conv_w, conv_b, w_ffn_out, g_final):
    vec = lambda t: t.reshape(1, -1)
    pad = LORA_PAD - LORA_W
    w_cat = jnp.transpose(w_in)
    mu_p = vec(tok_mu[:RKV_W])
    mu_l = vec(jnp.pad(tok_mu[RKV_W:], (0, pad)))
    lora_up = jnp.zeros((LORA_PAD, RKV_W), F32)
    lora_up = lora_up.at[0:LORA_DECAY, 0:RWKV_WIDTH].set(w_decay_up)
    lora_up = lora_up.at[LORA_DECAY:LORA_DECAY + LORA_ICLR, RWKV_WIDTH:2 * RWKV_WIDTH].set(w_iclr_up)
    lora_up = lora_up.at[LORA_DECAY + LORA_ICLR:LORA_W, 2 * RWKV_WIDTH:].set(w_gate_up)
    seg = np.arange(RWKV_WIDTH) // HEAD_DIM
    ones_bd = jnp.asarray(seg[:, None] == seg[None, :], BF16)
    prep = (mu_p, mu_l, lora_up.astype(BF16), vec(w0), vec(a0), vec(k_k), vec(k_a), vec(r_k), ones_bd)
    ffn = (w_ffn_in.astype(BF16), conv_w, vec(conv_b),
           w_ffn_out.astype(BF16), vec(g_final))
    return dict(g_mix=vec(g_mix), w_cat=w_cat, prep=prep, ones_bd=ones_bd, ln_w=vec(ln_x_w), ln_b=vec(ln_x_b),
                w_o=w_o.astype(BF16), g_ffn=vec(g_ffn), ffn=ffn)


def kernel(x_prompt, x_sample, cache_win_k, cache_win_v, state_shift, state_wkv, state_ffn_conv, g_mix, w_in, rel_bias, tok_mu, w0, w_decay_up, a0, w_iclr_up, w_gate_up, k_k, k_a, r_k, ln_x_w, ln_x_b, w_o, g_ffn, w_ffn_in, conv_w, conv_b, w_ffn_out, g_final):
    batch, seq, _ = x_prompt.shape
    nb = x_sample.shape[0]
    lw = _layer_weights(g_mix[0], w_in[0], tok_mu[0], w0[0], w_decay_up[0], a0[0], w_iclr_up[0],
                        w_gate_up[0], k_k[0], k_a[0], r_k[0].reshape(-1), ln_x_w[0], ln_x_b[0], w_o[0],
                        g_ffn[0], w_ffn_in[0], conv_w[0], conv_b[0], w_ffn_out[0], g_final)
    (bias_p, bias_s), rb0 = _bias_build(rel_bias)

    xp = x_prompt.reshape(batch * seq, D_MODEL)
    qkv, r, lgw, k2, v, kk, b, gate, bonus, sh_p, sh_l, k_win, v_win = _inproj_prep_prompt(
        xp, lw['g_mix'], lw['w_cat'], lw['prep'], batch, seq, 512)
    rw, wkv_p = _wkv_prompt(r, lgw, k2, v, kk, b, gate, bonus, lw['ln_w'], lw['ln_b'], batch, seq)
    attn = _attn_prompt(qkv, bias_p, batch, seq)
    win_k_p = k_win.reshape(1, batch, WIN, N_HEADS, HEAD_DIM)
    win_v_p = v_win.reshape(1, batch, WIN, N_HEADS, HEAD_DIM)
    shift_p = jnp.concatenate([sh_p[:, 0], sh_l[:, 0, :LORA_W]], axis=-1)[None]

    xs = x_sample.reshape(nb, D_MODEL)
    qkv_s, rkv_s, lora_s = _inproj(xs, lw['g_mix'], lw['w_cat'], nb)
    sh = state_shift[0]
    r, lgw, k2, v, kk, b, gate_s, bonus_s = _prep_sample(
        rkv_s, sh[:, :RKV_W], lora_s, jnp.pad(sh[:, RKV_W:], ((0, 0), (0, LORA_PAD - LORA_W))), lw['prep'])
    y_s, wkv_s = _wkv_sample(state_wkv[0], lgw, kk, b, k2, r, v, 8)
    q_s = qkv_s[:, :ATT_WIDTH]
    k_s = qkv_s[:, ATT_WIDTH:2 * ATT_WIDTH]
    v_s = qkv_s[:, 2 * ATT_WIDTH:]
    ck = jnp.transpose(cache_win_k[0], (0, 2, 3, 1))
    cv = jnp.transpose(cache_win_v[0], (0, 2, 3, 1))

    y_p, conv_p, attn_s, win_k_s, win_v_s = _ffn_prompt(xp, attn, rw, lw['w_o'], lw['g_ffn'], lw['ffn'],
                                                        batch, seq, (q_s, k_s, v_s, ck, cv, bias_s, rb0))

    win_k_s = jnp.transpose(win_k_s, (0, 3, 1, 2))
    win_v_s = jnp.transpose(win_v_s, (0, 3, 1, 2))
    h1_s, xn2_s = _outproj(xs, attn_s, y_s, gate_s, bonus_s, lw['ln_w'], lw['ln_b'], lw['w_o'], lw['g_ffn'],
                           lw['ones_bd'], nb)
    conv_state = state_ffn_conv[0]
    y_smp, gp_s = _ffn_sample(xn2_s, h1_s, conv_state[:, 1], conv_state[:, 0], lw['ffn'])
    conv_s = jnp.stack([conv_state[:, 1], gp_s], axis=1)
    shift_s = jnp.concatenate([rkv_s, lora_s[:, :LORA_W]], axis=-1)[None]

    return (y_p.reshape(batch, seq, D_MODEL), y_smp.reshape(nb, 1, D_MODEL),
            win_k_p, win_v_p, shift_p, wkv_p[None], conv_p[None],
            win_k_s[None], win_v_s[None], shift_s, wkv_s[None], conv_s[None])
```

```python
import functools

import numpy as np
import jax
import jax.numpy as jnp
from jax import lax
from jax.experimental import pallas as pl
from jax.experimental.pallas import tpu as pltpu

F32 = jnp.float32
BF16 = jnp.bfloat16

D_MODEL = 1024
HEAD_DIM = 64
ATT_WIDTH = 512
RWKV_WIDTH = 512
N_HEADS = 8
QKV_W = 3 * ATT_WIDTH
RKV_W = 3 * RWKV_WIDTH
LORA_DECAY, LORA_ICLR, LORA_GATE = 32, 32, 96
LORA_W = LORA_DECAY + LORA_ICLR + LORA_GATE
LORA_PAD = 256
RWKV_PROJ = RKV_W + LORA_W
PROJ_PAD = QKV_W + RKV_W + LORA_PAD
D_FF = 2816
FF_CHUNK = 2816
CONV_W = 3
WIN = 2048
DILATIONS = (1, 4, 16)
Q_BLOCK = 128
SUPER = 2048
N_BUCKETS = 32
MAX_DISTANCE = 2048
NORM_EPS = 1e-6
GN_EPS = 64e-5
NEG = -1e30
ATT_SCALE = HEAD_DIM ** -0.5
LOG2E = float(np.log2(np.e))
CHUNK = 64
WKV_CHUNKS_PER_STEP = 2
VMEM_LIMIT = 56 * 1024 * 1024

NT = (((1,), (1,)), ((), ()))
TN = (((0,), (0,)), ((), ()))


def _dot(a, b):
    return jnp.dot(a.astype(BF16), b.astype(BF16), preferred_element_type=F32)


def _dot_nt(a, b):
    return lax.dot_general(a.astype(BF16), b.astype(BF16), NT, preferred_element_type=F32)


def _dot_tn(a, b):
    return lax.dot_general(a.astype(BF16), b.astype(BF16), TN, preferred_element_type=F32)


def _split2(x):
    hi = x.astype(BF16)
    lo = (x - hi.astype(F32)).astype(BF16)
    return hi, lo


def _split3(x):
    hi = x.astype(BF16)
    r1 = x - hi.astype(F32)
    mid = r1.astype(BF16)
    lo = (r1 - mid.astype(F32)).astype(BF16)
    return hi, mid, lo


def _segsum(x, ones_bd):
    hi, lo = _split2(x)
    return (jnp.dot(hi, ones_bd, preferred_element_type=F32)
            + jnp.dot(lo, ones_bd, preferred_element_type=F32))


def _sigmoid(x):
    return 1.0 / (1.0 + jnp.exp(-x))


def _eye(n):
    return jnp.where(lax.broadcasted_iota(jnp.int32, (n, n), 0) == lax.broadcasted_iota(jnp.int32, (n, n), 1),
                     1.0, 0.0)


def _rmsnorm(x, g):
    return x * lax.rsqrt(jnp.mean(x * x, axis=-1, keepdims=True) + NORM_EPS) * g


def _params(sem):
    return pltpu.CompilerParams(dimension_semantics=sem, vmem_limit_bytes=VMEM_LIMIT)


def _t5_bucket(dist):
    dist = np.asarray(dist, dtype=np.int64)
    exact = N_BUCKETS // 2
    scaled = np.log(np.maximum(dist, 1) / exact) / np.log(MAX_DISTANCE / exact)
    large = np.minimum(exact + (scaled * (N_BUCKETS - exact)).astype(np.int64), N_BUCKETS - 1)
    return np.where(dist < exact, dist, large).astype(np.int32)


PROJ = QKV_W + RWKV_PROJ


def _inproj_kernel(x_ref, g_ref, w_ref, qkv_ref, rkv_ref, lora_ref):
    xb = _rmsnorm(x_ref[...], g_ref[...]).astype(BF16)
    proj = lambda a, b: lax.dot_general(xb, w_ref[a:b, :].astype(BF16), NT, preferred_element_type=F32)
    qkv_ref[...] = proj(0, QKV_W)
    rkv_ref[...] = proj(QKV_W, QKV_W + RKV_W)
    lora_ref[:, 0:LORA_W] = proj(QKV_W + RKV_W, PROJ)
    lora_ref[:, LORA_W:LORA_PAD] = jnp.zeros((x_ref.shape[0], LORA_PAD - LORA_W), F32)


def _inproj(x2d, g, w, tm):
    m = x2d.shape[0]
    row = lambda i: (i, 0)
    fixed = lambda i: (0, 0)
    return pl.pallas_call(
        _inproj_kernel,
        grid=(m // tm,),
        in_specs=[pl.BlockSpec((tm, D_MODEL), row),
                  pl.BlockSpec((1, D_MODEL), fixed),
                  pl.BlockSpec((PROJ, D_MODEL), fixed)],
        out_specs=[pl.BlockSpec((tm, QKV_W), row),
                   pl.BlockSpec((tm, RKV_W), row),
                   pl.BlockSpec((tm, LORA_PAD), row)],
        out_shape=[jax.ShapeDtypeStruct((m, QKV_W), F32),
                   jax.ShapeDtypeStruct((m, RKV_W), F32),
                   jax.ShapeDtypeStruct((m, LORA_PAD), F32)],
        compiler_params=_params(("arbitrary",)),
        name="inproj",
    )(x2d, g, w)


def _prep_math(p, p_prev, l, l_prev, mu_p, mu_l, wcat, w0, a0, k_k, k_a, r_k, ones_bd):
    pm = p + mu_p * (p_prev - p)
    lm = l + mu_l * (l_prev - l)
    r = pm[:, 0:RWKV_WIDTH]
    kr = pm[:, RWKV_WIDTH:2 * RWKV_WIDTH]
    vr = pm[:, 2 * RWKV_WIDTH:3 * RWKV_WIDTH]
    lane = lax.broadcasted_iota(jnp.int32, lm.shape, 1)
    feat = jnp.where(lane < LORA_DECAY, jnp.tanh(lm),
                     jnp.where(lane < LORA_DECAY + LORA_ICLR, lm, _sigmoid(lm)))
    z = jnp.dot(feat.astype(BF16), wcat, preferred_element_type=F32)
    zd = -(w0 + z[:, 0:RWKV_WIDTH])
    softplus = jnp.maximum(zd, 0.0) + jnp.log(1.0 + jnp.exp(-jnp.abs(zd)))
    lw = -jnp.exp(-softplus - 0.5)
    a = _sigmoid(a0 + z[:, RWKV_WIDTH:2 * RWKV_WIDTH])
    gate = z[:, 2 * RWKV_WIDTH:3 * RWKV_WIDTH]
    kk = kr * k_k
    head_sum = lambda t: jnp.dot(t.astype(BF16), ones_bd, preferred_element_type=F32)
    kk = kk / jnp.maximum(jnp.sqrt(head_sum(kk * kk)), 1e-12)
    k2 = kr * (1.0 + (a - 1.0) * k_a)
    bonus = head_sum(r * k2 * r_k) * vr
    return r, lw, k2, vr, kk, kk * a, gate, bonus


def _inproj_prep_kernel(x_ref, g_ref, w_ref, mu_p_ref, mu_l_ref, wcat_ref, w0_ref, a0_ref, kk_ref, ka_ref,
                        rk_ref, ones_ref, qkv_ref, *refs):
    outs, (sp_ref, sl_ref, kw_ref, vw_ref), (cp_ref, cl_ref, wb_ref) = refs[:8], refs[8:12], refs[12:]

    @pl.when(jnp.logical_and(pl.program_id(0) == 0, pl.program_id(1) == 0))
    def _():
        for c0 in range(0, PROJ, RWKV_WIDTH):
            c1 = min(c0 + RWKV_WIDTH, PROJ)
            wb_ref[c0:c1, :] = w_ref[c0:c1, :].astype(BF16)
        wb_ref[PROJ:PROJ_PAD, :] = jnp.zeros((PROJ_PAD - PROJ, D_MODEL), BF16)

    @pl.when(pl.program_id(1) == 0)
    def _():
        cp_ref[...] = jnp.zeros_like(cp_ref)
        cl_ref[...] = jnp.zeros_like(cl_ref)

    xb = _rmsnorm(x_ref[...], g_ref[...]).astype(BF16)
    proj = lambda a, b: lax.dot_general(xb, wb_ref[a:b, :], NT, preferred_element_type=F32)
    p = proj(QKV_W, QKV_W + RKV_W)
    l = proj(QKV_W + RKV_W, PROJ_PAD)
    tm = p.shape[0]
    first_p = lax.broadcasted_iota(jnp.int32, p.shape, 0) == 0
    first_l = lax.broadcasted_iota(jnp.int32, l.shape, 0) == 0
    p_prev = jnp.where(first_p, cp_ref[...], pltpu.roll(p, 1, 0))
    l_prev = jnp.where(first_l, cl_ref[...], pltpu.roll(l, 1, 0))
    cp_ref[...] = p[tm - 1:tm, :]
    cl_ref[...] = l[tm - 1:tm, :]
    sp_ref[0] = p[tm - 1:tm, :]
    sl_ref[0] = l[tm - 1:tm, :]
    res = _prep_math(p, p_prev, l, l_prev, mu_p_ref[...], mu_l_ref[...], wcat_ref[...], w0_ref[...],
                     a0_ref[...], kk_ref[...], ka_ref[...], rk_ref[...], ones_ref[...])
    for o, v in zip(outs, res):
        o[...] = v
    qkv = proj(0, QKV_W)
    qkv_ref[...] = qkv
    kw_ref[0] = qkv[:, ATT_WIDTH:2 * ATT_WIDTH].T
    vw_ref[0] = qkv[:, 2 * ATT_WIDTH:QKV_W].T


def _prep_sample_kernel(p_ref, pp_ref, l_ref, lp_ref, mu_p_ref, mu_l_ref, wcat_ref, w0_ref, a0_ref,
                        kk_ref, ka_ref, rk_ref, ones_ref, *outs):
    res = _prep_math(p_ref[...], pp_ref[...], l_ref[...], lp_ref[...], mu_p_ref[...], mu_l_ref[...],
                     wcat_ref[...], w0_ref[...], a0_ref[...], kk_ref[...], ka_ref[...], rk_ref[...],
                     ones_ref[...])
    for o, v in zip(outs, res):
        o[...] = v


def _inproj_prep_prompt(x2d, g, w, pw, batch, seq, tm):
    nt = seq // tm
    row = lambda b, j: (b * nt + j, 0)
    fixed = lambda b, j: (0, 0)
    vec = lambda n: pl.BlockSpec((1, n), fixed)
    once = pl.Buffered(1)
    m = x2d.shape[0]
    last = lambda n: pl.BlockSpec((1, 1, n), lambda b, j: (b, 0, 0))
    nw = WIN // tm
    assert nw * tm == WIN and nw <= nt
    win = pl.BlockSpec((1, ATT_WIDTH, tm), lambda b, j: (b, 0, jnp.maximum(j - (nt - nw), 0)))
    return pl.pallas_call(
        _inproj_prep_kernel,
        grid=(batch, nt),
        in_specs=[pl.BlockSpec((tm, D_MODEL), row), vec(D_MODEL),
                  pl.BlockSpec((PROJ, D_MODEL), fixed, pipeline_mode=once),
                  vec(RKV_W), vec(LORA_PAD), pl.BlockSpec((LORA_PAD, RKV_W), fixed),
                  vec(RWKV_WIDTH), vec(RWKV_WIDTH), vec(RWKV_WIDTH), vec(RWKV_WIDTH), vec(RWKV_WIDTH),
                  pl.BlockSpec((RWKV_WIDTH, RWKV_WIDTH), fixed)],
        out_specs=[pl.BlockSpec((tm, QKV_W), row)] + [pl.BlockSpec((tm, RWKV_WIDTH), row)] * 8
                  + [last(RKV_W), last(LORA_PAD), win, win],
        out_shape=[jax.ShapeDtypeStruct((m, QKV_W), F32)] + [jax.ShapeDtypeStruct((m, RWKV_WIDTH), F32)] * 8
                  + [jax.ShapeDtypeStruct((batch, 1, RKV_W), F32), jax.ShapeDtypeStruct((batch, 1, LORA_PAD), F32)]
                  + [jax.ShapeDtypeStruct((batch, ATT_WIDTH, WIN), F32)] * 2,
        scratch_shapes=[pltpu.VMEM((1, RKV_W), F32), pltpu.VMEM((1, LORA_PAD), F32),
                        pltpu.VMEM((PROJ_PAD, D_MODEL), BF16)],
        compiler_params=_params(("arbitrary", "arbitrary")),
        name="inproj_prep_prompt",
    )(x2d, g, w, *pw)


def _prep_sample(rkv, rkv_prev, lora, lora_prev, pw):
    m = rkv.shape[0]
    full = lambda a: pl.BlockSpec(a.shape, lambda i: (0,) * a.ndim)
    args = (rkv, rkv_prev, lora, lora_prev) + tuple(pw)
    return pl.pallas_call(
        _prep_sample_kernel,
        grid=(1,),
        in_specs=[full(a) for a in args],
        out_specs=[pl.BlockSpec((m, RWKV_WIDTH), lambda i: (0, 0))] * 8,
        out_shape=[jax.ShapeDtypeStruct((m, RWKV_WIDTH), F32)] * 8,
        compiler_params=_params(("arbitrary",)),
        name="rwkv_prep_sample",
    )(*args)


def _cumsum_rows(x, tri):
    tri_b = tri.astype(BF16)
    out = None
    for piece in _split3(x):
        t = jnp.dot(tri_b, piece, preferred_element_type=F32)
        out = t if out is None else out + t
    return out


def _wkv_kernel(r_ref, lw_ref, k_ref, v_ref, kk_ref, b_ref, gate_ref, bonus_ref, lnw_ref, lnb_ref,
                y_ref, s_ref, st_ref):
    c = pl.program_id(0)

    @pl.when(c == 0)
    def _():
        st_ref[...] = jnp.zeros_like(st_ref)

    nb = r_ref.shape[0]
    n = CHUNK
    row = lax.broadcasted_iota(jnp.int32, (n, n), 0)
    col = lax.broadcasted_iota(jnp.int32, (n, n), 1)
    eye = jnp.where(row == col, 1.0, 0.0)
    tri = jnp.where(row >= col, 1.0, 0.0)
    row2 = lax.broadcasted_iota(jnp.int32, (2 * n, 2 * n), 0)
    col2 = lax.broadcasted_iota(jnp.int32, (2 * n, 2 * n), 1) % n
    gram_mask = jnp.logical_or(row2 % n > col2, jnp.logical_and(row2 >= n, row2 % n == col2))
    zeros = jnp.zeros((n, HEAD_DIM), BF16)

    units = range(nb * N_HEADS)
    stack = lambda xs, ys: [jnp.concatenate([x, y], axis=0) for x, y in zip(xs, ys)]

    def prelude(rows):
        ah, bh, kh, rh, beh, keh, vh, ge = ([] for _ in range(8))
        for bi in range(nb):
            lw = lw_ref[bi, rows, :]
            lg = _cumsum_rows(lw, tri)
            g_inv = jnp.exp(-lg)
            lg_end = lg[n - 1:n, :]
            g_to_end = jnp.exp(lg_end - lg)
            g_end = jnp.exp(lg_end)
            kk = kk_ref[bi, rows, :]
            b = b_ref[bi, rows, :]
            k = k_ref[bi, rows, :]
            full = (-kk * jnp.exp(lg - lw), b * g_inv, k * g_inv, r_ref[bi, rows, :] * jnp.exp(lg),
                    b * g_to_end, k * g_to_end, v_ref[bi, rows, :], g_end)
            for h in range(N_HEADS):
                sl = slice(h * HEAD_DIM, (h + 1) * HEAD_DIM)
                for dst, t in zip((ah, bh, kh, rh, beh, keh, vh), full[:7]):
                    dst.append(t[:, sl].astype(BF16))
                ge.append(g_end[:, sl])

        gram = [jnp.where(gram_mask, _dot_nt(ar, bk), 0.0) for ar, bk in zip(stack(ah, rh), stack(bh, kh))]
        top = [g[:n].astype(BF16) for g in gram]
        bot = [g[n:].astype(BF16) for g in gram]
        lv = [_dot(top[u], jnp.concatenate([zeros, vh[u]], axis=0)) for u in units]
        tm_ = [eye + gram[u][:n, :n] for u in units]
        lp = [_dot(t[:, :n], t[:, :n]).astype(BF16) for t in top]
        span = 2
        while 2 * span < n:
            prod = [_dot(x, p) for x, p in zip(stack(lp, [t.astype(BF16) for t in tm_]), lp)]
            tm_ = [tm_[u] + prod[u][n:] for u in units]
            lp = [p[:n].astype(BF16) for p in prod]
            span *= 2
        tm_ = [(tm_[u] + _dot(tm_[u], lp[u])).astype(BF16) for u in units]
        a_hat = [_dot(tm_[u], ah[u]).astype(BF16) for u in units]
        u0 = [_dot(tm_[u], lv[u]) for u in units]
        return rows, stack(a_hat, rh), u0, vh, bot, stack(beh, keh), ge

    def finish(rows, a_hat_r, u0, vh, bot, be_ke, ge):
        s0 = [st_ref[u] for u in units]
        on_s0 = [_dot_nt(x, s.astype(BF16)) for x, s in zip(a_hat_r, s0)]
        uv = stack([(on_s0[u][:n] + u0[u]).astype(BF16) for u in units], vh)
        y = [on_s0[u][n:] + _dot(bot[u], uv[u]) for u in units]
        for u in units:
            st_ref[u] = s0[u] * ge[u] + _dot_tn(uv[u], be_ke[u])
        for bi in range(nb):
            normed = []
            for yu in y[bi * N_HEADS:(bi + 1) * N_HEADS]:
                dy = yu - jnp.mean(yu, axis=-1, keepdims=True)
                normed.append(dy * lax.rsqrt(jnp.mean(dy * dy, axis=-1, keepdims=True) + GN_EPS))
            yn = jnp.concatenate(normed, axis=1)
            y_ref[bi, rows, :] = ((yn * lnw_ref[...] + lnb_ref[...] + bonus_ref[bi, rows, :])
                                  * gate_ref[bi, rows, :]).astype(BF16)

    for j in range(r_ref.shape[1] // n):
        finish(*prelude(pl.ds(j * n, n)))

    @pl.when(c == pl.num_programs(0) - 1)
    def _():
        s_ref[...] = st_ref[...]


def _wkv_prompt(r, lw, k2, v, kk, b, gate, bonus, ln_w, ln_b, batch, seq):
    rows = WKV_CHUNKS_PER_STEP * CHUNK
    nc = seq // rows
    assert nc * rows == seq
    seqs = lambda t: t.reshape(batch, seq, RWKV_WIDTH)
    spec = pl.BlockSpec((batch, rows, RWKV_WIDTH), lambda c: (0, c, 0))
    vec = pl.BlockSpec((1, RWKV_WIDTH), lambda c: (0, 0))
    nu = batch * N_HEADS
    y, s = pl.pallas_call(
        _wkv_kernel,
        grid=(nc,),
        in_specs=[spec] * 8 + [vec, vec],
        out_specs=[spec, pl.BlockSpec((nu, HEAD_DIM, HEAD_DIM), lambda c: (0, 0, 0))],
        out_shape=[jax.ShapeDtypeStruct((batch, seq, RWKV_WIDTH), BF16),
                   jax.ShapeDtypeStruct((nu, HEAD_DIM, HEAD_DIM), F32)],
        scratch_shapes=[pltpu.VMEM((nu, HEAD_DIM, HEAD_DIM), F32)],
        compiler_params=_params(("arbitrary",)),
        name="wkv_prompt",
    )(*(seqs(t) for t in (r, lw, k2, v, kk, b, gate, bonus)), ln_w, ln_b)
    return y.reshape(batch * seq, RWKV_WIDTH), s.reshape(batch, N_HEADS, HEAD_DIM, HEAD_DIM)


def _wkv_step_kernel(s_ref, lw_ref, kk_ref, b_ref, k_ref, r_ref, v_ref, y_ref, so_ref):
    s = s_ref[...]
    eye = _eye(HEAD_DIM)
    v_col = jnp.sum(eye * v_ref[...], axis=-1, keepdims=True)
    sa = jnp.sum(s * (-kk_ref[...]), axis=-1, keepdims=True)
    s = s * jnp.exp(lw_ref[...]) + sa * b_ref[...] + v_col * k_ref[...]
    so_ref[...] = s
    y_col = jnp.sum(s * r_ref[...], axis=-1, keepdims=True)
    y_ref[...] = jnp.sum(eye * y_col, axis=-2, keepdims=True)


def _wkv_sample(state, lw, kk, b, k2, r, v, bb):
    nb = state.shape[0]
    rowv = lambda t: t.reshape(nb, N_HEADS, 1, HEAD_DIM)
    idx = lambda i: (i, 0, 0, 0)
    s_spec = pl.BlockSpec((bb, N_HEADS, HEAD_DIM, HEAD_DIM), idx)
    r_spec = pl.BlockSpec((bb, N_HEADS, 1, HEAD_DIM), idx)
    y, s_new = pl.pallas_call(
        _wkv_step_kernel,
        grid=(nb // bb,),
        in_specs=[s_spec] + [r_spec] * 6,
        out_specs=[r_spec, s_spec],
        out_shape=[jax.ShapeDtypeStruct((nb, N_HEADS, 1, HEAD_DIM), F32),
                   jax.ShapeDtypeStruct(state.shape, F32)],
        compiler_params=_params(("arbitrary",)),
        name="wkv_sample",
    )(state, rowv(lw), rowv(kk), rowv(b), rowv(k2), rowv(r), rowv(v))
    return y.reshape(nb, RWKV_WIDTH), s_new


def _bias_tables():
    i = np.arange(Q_BLOCK)[:, None]
    c = np.arange(2 * Q_BLOCK)[None, :]
    n = i + Q_BLOCK - c
    ok = (n >= 0) & (n <= Q_BLOCK)
    prompt = np.stack([np.where(ok, _t5_bucket(d * np.clip(n, 0, Q_BLOCK)), -1) for d in DILATIONS])
    dist = WIN - np.arange(WIN)
    sample = np.stack([np.where((dist % d == 0) & (dist // d <= Q_BLOCK), _t5_bucket(dist), -1)
                       for d in DILATIONS])
    sample = np.broadcast_to(sample[:, None, :], (len(DILATIONS), N_HEADS, WIN))
    return prompt.astype(np.int32), sample.astype(np.int32)


def _bias_kernel(rb_ref, rbc_ref, bp_ref, bs_ref, op_ref, os_ref):
    bp = bp_ref[0]
    bs = bs_ref[0]
    acc_s = jnp.full(bs.shape, NEG, F32)
    for bk in range(N_BUCKETS):
        acc_s = jnp.where(bs == bk, rbc_ref[bk], acc_s)
    os_ref[0] = acc_s
    for h in range(N_HEADS):
        acc = jnp.full(bp.shape, NEG, F32)
        for bk in range(N_BUCKETS):
            acc = jnp.where(bp == bk, rb_ref[bk, h], acc)
        op_ref[0, h] = acc * LOG2E


def _bias_build(rel_bias):
    bp, bs = _bias_tables()
    rbc = rel_bias.reshape(N_BUCKETS, N_HEADS, 1)
    nb = len(DILATIONS)
    return pl.pallas_call(
        _bias_kernel,
        grid=(nb,),
        in_specs=[pl.BlockSpec(memory_space=pltpu.SMEM),
                  pl.BlockSpec((N_BUCKETS, N_HEADS, 1), lambda i: (0, 0, 0)),
                  pl.BlockSpec((1, Q_BLOCK, 2 * Q_BLOCK), lambda i: (i, 0, 0)),
                  pl.BlockSpec((1, N_HEADS, WIN), lambda i: (i, 0, 0))],
        out_specs=[pl.BlockSpec((1, N_HEADS, Q_BLOCK, 2 * Q_BLOCK), lambda i: (i, 0, 0, 0)),
                   pl.BlockSpec((1, N_HEADS, WIN), lambda i: (i, 0, 0))],
        out_shape=[jax.ShapeDtypeStruct((nb, N_HEADS, Q_BLOCK, 2 * Q_BLOCK), F32),
                   jax.ShapeDtypeStruct((nb, N_HEADS, WIN), F32)],
        compiler_params=_params(("arbitrary",)),
        name="bias_build",
    )(rel_bias, rbc, jnp.asarray(bp), jnp.asarray(bs)), rbc[0]


ATTN_GROUP = 4


def _attn_units():
    units = []
    for br, d in enumerate(DILATIONS):
        nblk = SUPER // (d * Q_BLOCK)
        for r in range(d):
            rows = lambda j, r=r, d=d: (pl.ds(r + d * Q_BLOCK * j, Q_BLOCK, stride=d) if d > 1
                                        else pl.ds(Q_BLOCK * j, Q_BLOCK))
            for blk in range(nblk):
                units.append((br, rows(blk), rows(blk - 1 if blk else nblk - 1), blk == 0))
    return units


def _attn_kernel(q_ref, kc_ref, kp_ref, vc_ref, vp_ref, bias_ref, o_ref, acc_ref, m_ref):
    has_prev = pl.program_id(2) > 0
    is_a = lax.broadcasted_iota(jnp.int32, (Q_BLOCK, 128), 1) < HEAD_DIM
    is_a2 = lax.broadcasted_iota(jnp.int32, (2 * Q_BLOCK, 128), 1) < HEAD_DIM
    live = jnp.logical_or(has_prev,
                          lax.broadcasted_iota(jnp.int32, (2 * Q_BLOCK, 2 * Q_BLOCK), 1) >= Q_BLOCK)
    units = _attn_units()
    for g0 in range(0, len(units), ATTN_GROUP):
        group = units[g0:g0 + ATTN_GROUP]
        ops = []
        for br, cur, prv, from_prev in group:
            qs = q_ref[cur, :] * (ATT_SCALE * LOG2E)
            q2 =jnp.concatenate([jnp.where(is_a, qs, 0.0), jnp.where(is_a, 0.0, qs)], axis=0)
            k2 = jnp.concatenate([(kp_ref if from_prev else kc_ref)[prv, :], kc_ref[cur, :]], axis=0)
            v2 = jnp.concatenate([(vp_ref if from_prev else vc_ref)[prv, :], vc_ref[cur, :]], axis=0)
            ops.append((br, cur, from_prev, q2.astype(BF16), k2.astype(BF16),
                        jnp.where(is_a2, v2, 1.0).astype(BF16), jnp.where(is_a2, 1.0, v2).astype(BF16)))
        logits = []
        for br, cur, from_prev, q2, k2, va, vb in ops:
            lg = lax.dot_general(q2, k2, NT, preferred_element_type=F32) + bias_ref[br]
            logits.append(jnp.where(live, lg, NEG) if from_prev else lg)
        ms = [jnp.max(lg, axis=-1, keepdims=True) for lg in logits]
        ps = [jnp.exp2(lg - m).astype(BF16) for lg, m in zip(logits, ms)]
        for (br, cur, _, _, _, va, vb), p, m in zip(ops, ps, ms):
            acc_ref[br, 0, cur, :] = jnp.dot(p[:Q_BLOCK], va, preferred_element_type=F32)
            acc_ref[br, 1, cur, :] = jnp.dot(p[Q_BLOCK:], vb, preferred_element_type=F32)
            m_ref[br, 0, cur, :] = jnp.broadcast_to(m[:Q_BLOCK], (Q_BLOCK, 128))
            m_ref[br, 1, cur, :] = jnp.broadcast_to(m[Q_BLOCK:], (Q_BLOCK, 128))

    for i in range(SUPER // Q_BLOCK):
        rs = pl.ds(i * Q_BLOCK, Q_BLOCK)
        outs = []
        for hd in range(2):
            ms = [m_ref[br, hd, rs, :] for br in range(len(DILATIONS))]
            m = functools.reduce(jnp.maximum, ms)
            tot = 0.0
            for br in range(len(DILATIONS)):
                tot = tot + jnp.exp2(ms[br] - m) * acc_ref[br, hd, rs, :]
            outs.append(tot / pltpu.roll(tot, HEAD_DIM, 1))
        o_ref[rs, :] = jnp.where(is_a, outs[0], outs[1]).astype(o_ref.dtype)


def _attn_prompt(qkv, bias_p, batch, seq):
    ns = seq // SUPER
    npair = ATT_WIDTH // 128
    blk = lambda col0, prev: pl.BlockSpec(
        (SUPER, 128),
        (lambda b, hp, sb: (b * ns + jnp.maximum(sb - 1, 0), col0 + hp)) if prev
        else (lambda b, hp, sb: (b * ns + sb, col0 + hp)))
    nb = len(DILATIONS)
    return pl.pallas_call(
        _attn_kernel,
        grid=(batch, npair, ns),
        in_specs=[blk(0, False), blk(npair, False), blk(npair, True), blk(2 * npair, False),
                  blk(2 * npair, True),
                  pl.BlockSpec((nb, None, 2 * Q_BLOCK, 2 * Q_BLOCK), lambda b, hp, sb: (0, hp, 0, 0))],
        out_specs=pl.BlockSpec((SUPER, 128), lambda b, hp, sb: (b * ns + sb, hp)),
        out_shape=jax.ShapeDtypeStruct((batch * seq, ATT_WIDTH), BF16),
        scratch_shapes=[pltpu.VMEM((nb, 2, SUPER, 128), F32)] * 2,
        compiler_params=_params(("arbitrary", "arbitrary", "arbitrary")),
        name="attn_prompt",
    )(qkv, qkv, qkv, qkv, qkv, bias_p.reshape(nb, npair, 2 * Q_BLOCK, 2 * Q_BLOCK))


WINDOW_HEADS = 4


def _window_body(q_ref, kn_ref, vn_ref, ck_ref, cv_ref, bias_ref, rb0_ref, o_ref, ok_ref, ov_ref, wt_ref):
    nh = WINDOW_HEADS
    lane = lax.broadcasted_iota(jnp.int32, (HEAD_DIM, WIN), 1)
    last = lane == WIN - 1
    eye = _eye(HEAD_DIM)
    col = lambda ref, h: jnp.sum(eye * ref[h:h + 1, :], axis=-1, keepdims=True)
    l_new = jnp.sum(kn_ref[...] * (q_ref[...] * ATT_SCALE), axis=-1, keepdims=True) + rb0_ref[...]
    rows = []
    for h in range(nh):
        kh = ck_ref[h]
        rows.append(jnp.sum(kh * (col(q_ref, h) * ATT_SCALE), axis=0, keepdims=True))
        ok_ref[h] = jnp.where(last, col(kn_ref, h), pltpu.roll(kh, WIN - 1, 1))
    logits = jnp.concatenate(rows, axis=0)
    parts = []
    for br in range(len(DILATIONS)):
        lg = logits + bias_ref[br]
        m = jnp.maximum(jnp.max(lg, axis=-1, keepdims=True), l_new)
        p = jnp.exp(lg - m)
        p_new = jnp.exp(l_new - m)
        parts.append((p, p_new, m, jnp.sum(p, axis=-1, keepdims=True) + p_new))
    m = functools.reduce(jnp.maximum, [pt[2] for pt in parts])
    wt = 0.0
    w_new = 0.0
    den = 0.0
    for p, p_new, m_g, s_g in parts:
        w = jnp.exp(m_g - m)
        wt = wt + w * p
        w_new = w_new + w * p_new
        den = den + w * s_g
    wt_ref[0:nh, :] = wt / den
    w_new = w_new / den
    for h in range(nh):
        vh = cv_ref[h]
        v_new = col(vn_ref, h)
        o_col = jnp.sum(vh * wt_ref[h:h + 1, :], axis=1, keepdims=True) + v_new * w_new[h:h + 1, :]
        o_ref[h:h + 1, :] = jnp.sum(eye * o_col, axis=0, keepdims=True)
        ov_ref[h] = jnp.where(last, v_new, pltpu.roll(vh, WIN - 1, 1))


N_WINDOW_IN = 7


def _window_operands(q, k_new, v_new, cache_k, cache_v, bias_s, rb0, step_of):
    nb = q.shape[0]
    nh = WINDOW_HEADS
    groups = N_HEADS // nh
    seq_grp = lambda *g: (step_of(*g) // groups, step_of(*g) % groups)
    rows = lambda t: t.reshape(nb, groups, nh, HEAD_DIM)
    wins = lambda t: t.reshape(nb, groups, nh, HEAD_DIM, WIN)
    row_spec = pl.BlockSpec((None, None, nh, HEAD_DIM), lambda *g: seq_grp(*g) + (0, 0))
    win_spec = pl.BlockSpec((None, None, nh, HEAD_DIM, WIN), lambda *g: seq_grp(*g) + (0, 0, 0))
    nbr = len(DILATIONS)
    bias_spec = pl.BlockSpec((nbr, None, nh, WIN), lambda *g: (0, seq_grp(*g)[1], 0, 0))
    rb0_spec = pl.BlockSpec((None, nh, 1), lambda *g: (seq_grp(*g)[1], 0, 0))
    args = (rows(q), rows(k_new), rows(v_new), wins(cache_k), wins(cache_v),
            bias_s.reshape(nbr, groups, nh, WIN), rb0.reshape(groups, nh, 1))
    in_specs = [row_spec, row_spec, row_spec, win_spec, win_spec, bias_spec, rb0_spec]
    assert len(args) == len(in_specs) == N_WINDOW_IN
    out_specs = [row_spec, win_spec, win_spec]
    out_shape = [jax.ShapeDtypeStruct((nb, groups, nh, HEAD_DIM), F32),
                 jax.ShapeDtypeStruct((nb, groups, nh, HEAD_DIM, WIN), F32),
                 jax.ShapeDtypeStruct((nb, groups, nh, HEAD_DIM, WIN), F32)]
    return args, in_specs, out_specs, out_shape


def _outproj_kernel(x_ref, attn_ref, y_ref, gate_ref, bonus_ref, lnw_ref, lnb_ref, wo_ref, gffn_ref,
                    ones_ref, h_ref, xn_ref):
    ones_bd = ones_ref[...]
    y = y_ref[...]
    mu = _segsum(y, ones_bd) * (1.0 / HEAD_DIM)
    dy = y - mu
    var = _segsum(dy * dy, ones_bd) * (1.0 / HEAD_DIM)
    yn = dy * lax.rsqrt(var + GN_EPS) * lnw_ref[...] + lnb_ref[...]
    rw = (yn + bonus_ref[...]) * gate_ref[...]
    mixed = (jnp.dot(attn_ref[...].astype(BF16), wo_ref[0:ATT_WIDTH, :], preferred_element_type=F32)
             + jnp.dot(rw.astype(BF16), wo_ref[ATT_WIDTH:D_MODEL, :], preferred_element_type=F32))
    h = x_ref[...] + mixed
    h_ref[...] = h
    xn_ref[...] = _rmsnorm(h, gffn_ref[...]).astype(BF16)


def _outproj(x2d, attn, y, gate, bonus, ln_w, ln_b, w_o, g_ffn, ones_bd, tm):
    m = x2d.shape[0]
    row = lambda i: (i, 0)
    fixed = lambda i: (0, 0)
    half = pl.BlockSpec((tm, RWKV_WIDTH), row)
    full = pl.BlockSpec((tm, D_MODEL), row)
    return pl.pallas_call(
        _outproj_kernel,
        grid=(m // tm,),
        in_specs=[full, half, half, half, half,
                  pl.BlockSpec((1, RWKV_WIDTH), fixed), pl.BlockSpec((1, RWKV_WIDTH), fixed),
                  pl.BlockSpec((D_MODEL, D_MODEL), fixed), pl.BlockSpec((1, D_MODEL), fixed),
                  pl.BlockSpec((RWKV_WIDTH, RWKV_WIDTH), fixed)],
        out_specs=[full, full],
        out_shape=[jax.ShapeDtypeStruct((m, D_MODEL), F32), jax.ShapeDtypeStruct((m, D_MODEL), BF16)],
        compiler_params=_params(("arbitrary",)),
        name="outproj",
    )(x2d, attn, y, gate, bonus, ln_w, ln_b, w_o, g_ffn, ones_bd)


def _ffn_tail(xn, h, gp_m1, gp_m2, gp_of, wi_ref, cw_ref, cb_ref, wout_ref, gfin_ref, o_ref):
    acc = None
    for c in range(D_FF // FF_CHUNK):
        cs = slice(c * FF_CHUNK, (c + 1) * FF_CHUNK)
        gp = gp_of(c)
        up = jnp.dot(xn, wi_ref[:, D_FF + c * FF_CHUNK:D_FF + (c + 1) * FF_CHUNK], preferred_element_type=F32)
        conv = (cb_ref[:, cs] + cw_ref[0:1, cs] * gp_m2(c, gp) + cw_ref[1:2, cs] * gp_m1(c, gp)
                + cw_ref[2:3, cs] * gp)
        act = conv * _sigmoid(conv) * up
        part = jnp.dot(act.astype(BF16), wout_ref[cs, :], preferred_element_type=F32)
        acc = part if acc is None else acc + part
    o_ref[...] = _rmsnorm(h + acc, gfin_ref[...])


def _ffn_prompt_kernel(x_ref, attn_ref, rw_ref, wo_ref, gffn_ref, wi_ref, cw_ref, cb_ref, wout_ref,
                       gfin_ref, *refs):
    n = N_WINDOW_IN
    win_in, (o_ref, conv_ref), win_out, (carry_ref, wt_ref) = refs[:n], refs[n:n + 2], refs[n + 2:n + 5], refs[n + 5:]

    @pl.when(pl.program_id(1) == 0)
    def _():
        carry_ref[...] = jnp.zeros_like(carry_ref)

    _window_body(*win_in, *win_out, wt_ref)

    h = (x_ref[...]
         + jnp.dot(attn_ref[...], wo_ref[0:ATT_WIDTH, :], preferred_element_type=F32)
         + jnp.dot(rw_ref[...], wo_ref[ATT_WIDTH:D_MODEL, :], preferred_element_type=F32))
    xn = _rmsnorm(h, gffn_ref[...]).astype(BF16)
    tm = xn.shape[0]
    rowi = lax.broadcasted_iota(jnp.int32, (tm, FF_CHUNK), 0)
    gps = {}

    def gp_of(c):
        cs = slice(c * FF_CHUNK, (c + 1) * FF_CHUNK)
        gps[c] = jnp.dot(xn, wi_ref[:, cs], preferred_element_type=F32)
        return gps[c]

    def gp_m1(c, gp):
        cs = slice(c * FF_CHUNK, (c + 1) * FF_CHUNK)
        return jnp.where(rowi == 0, carry_ref[1:2, cs], pltpu.roll(gp, 1, 0))

    def gp_m2(c, gp):
        cs = slice(c * FF_CHUNK, (c + 1) * FF_CHUNK)
        return jnp.where(rowi == 0, carry_ref[0:1, cs],
                         jnp.where(rowi == 1, carry_ref[1:2, cs], pltpu.roll(gp, 2, 0)))

    _ffn_tail(xn, h, gp_m1, gp_m2, gp_of, wi_ref, cw_ref, cb_ref, wout_ref, gfin_ref, o_ref)
    for c, gp in gps.items():
        cs = slice(c * FF_CHUNK, (c + 1) * FF_CHUNK)
        carry_ref[:, cs] = gp[tm - 2:tm, :]
        conv_ref[0, :, cs] = gp[tm - 2:tm, :]


def _ffn_sample_kernel(xn_ref, h_ref, p1_ref, p2_ref, wi_ref, cw_ref, cb_ref, wout_ref, gfin_ref,
                       o_ref, gp_ref):
    xn = xn_ref[...]

    def gp_of(c):
        cs = slice(c * FF_CHUNK, (c + 1) * FF_CHUNK)
        gp = jnp.dot(xn, wi_ref[:, cs], preferred_element_type=F32)
        gp_ref[:, cs] = gp
        return gp

    gp_m1 = lambda c, gp: p1_ref[:, c * FF_CHUNK:(c + 1) * FF_CHUNK]
    gp_m2 = lambda c, gp: p2_ref[:, c * FF_CHUNK:(c + 1) * FF_CHUNK]
    _ffn_tail(xn, h_ref[...], gp_m1, gp_m2, gp_of, wi_ref, cw_ref, cb_ref, wout_ref, gfin_ref, o_ref)


def _ffn_weight_specs(fixed):
    once = pl.Buffered(1)
    return [pl.BlockSpec((D_MODEL, 2 * D_FF), fixed, pipeline_mode=once),
            pl.BlockSpec((CONV_W, D_FF), fixed), pl.BlockSpec((1, D_FF), fixed),
            pl.BlockSpec((D_FF, D_MODEL), fixed, pipeline_mode=once), pl.BlockSpec((1, D_MODEL), fixed)]


def _ffn_prompt(x2d, attn, rw, w_o, g_ffn, fw, batch, seq, window):
    nt = window[0].shape[0] * (N_HEADS // WINDOW_HEADS) // batch
    tm = seq // nt
    assert tm * nt == seq and tm % 16 == 0
    row = lambda b, j: (b * nt + j, 0)
    fixed = lambda b, j: (0, 0)
    full = pl.BlockSpec((tm, D_MODEL), row)
    half = pl.BlockSpec((tm, ATT_WIDTH), row)
    mix_specs = [full, half, half, pl.BlockSpec((D_MODEL, D_MODEL), fixed), pl.BlockSpec((1, D_MODEL), fixed)]
    w_args, w_in_specs, w_out_specs, w_out_shape = _window_operands(*window, lambda b, j: b * nt + j)
    y, conv, attn_s, win_k, win_v = pl.pallas_call(
        _ffn_prompt_kernel,
        grid=(batch, nt),
        in_specs=mix_specs + _ffn_weight_specs(fixed) + w_in_specs,
        out_specs=[full, pl.BlockSpec((1, CONV_W - 1, D_FF), lambda b, j: (b, 0, 0))] + w_out_specs,
        out_shape=[jax.ShapeDtypeStruct((batch * seq, D_MODEL), F32),
                   jax.ShapeDtypeStruct((batch, CONV_W - 1, D_FF), F32)] + w_out_shape,
        scratch_shapes=[pltpu.VMEM((CONV_W - 1, D_FF), F32), pltpu.VMEM((8, WIN), F32)],
        compiler_params=_params(("arbitrary", "arbitrary")),
        name="ffn_prompt",
    )(x2d, attn, rw, w_o, g_ffn, *fw, *w_args)
    nb = window[0].shape[0]
    return (y, conv, attn_s.reshape(nb, ATT_WIDTH), win_k.reshape(nb, N_HEADS, HEAD_DIM, WIN),
            win_v.reshape(nb, N_HEADS, HEAD_DIM, WIN))


def _ffn_sample(xn, h, prev1, prev2, fw):
    m = xn.shape[0]
    fixed = lambda i: (0, 0)
    full = pl.BlockSpec((m, D_MODEL), fixed)
    ffs = pl.BlockSpec((m, D_FF), fixed)
    return pl.pallas_call(
        _ffn_sample_kernel,
        grid=(1,),
        in_specs=[full, full, ffs, ffs] + _ffn_weight_specs(fixed),
        out_specs=[full, ffs],
        out_shape=[jax.ShapeDtypeStruct((m, D_MODEL), F32), jax.ShapeDtypeStruct((m, D_FF), F32)],
        compiler_params=_params(("arbitrary",)),
        name="ffn_sample",
    )(xn, h, prev1, prev2, *fw)


def _layer_weights(g_mix, w_in, tok_mu, w0, w_decay_up, a0, w_iclr_up, w_gate_up, k_k, k_a, r_k,
                   ln_x_w, ln_x_b, w_o, g_ffn, w_ffn_in, conv_w, conv_b, w_ffn_out, g_final):
    vec = lambda t: t.reshape(1, -1)
    pad = LORA_PAD - LORA_W
    w_cat = jnp.transpose(w_in)
    mu_p = vec(tok_mu[:RKV_W])
    mu_l = vec(jnp.pad(tok_mu[RKV_W:], (0, pad)))
    lora_up = jnp.zeros((LORA_PAD, RKV_W), F32)
    lora_up = lora_up.at[0:LORA_DECAY, 0:RWKV_WIDTH].set(w_decay_up)
    lora_up = lora_up.at[LORA_DECAY:LORA_DECAY + LORA_ICLR, RWKV_WIDTH:2 * RWKV_WIDTH].set(w_iclr_up)
    lora_up = lora_up.at[LORA_DECAY + LORA_ICLR:LORA_W, 2 * RWKV_WIDTH:].set(w_gate_up)
    seg = np.arange(RWKV_WIDTH) // HEAD_DIM
    ones_bd = jnp.asarray(seg[:, None] == seg[None, :], BF16)
    prep = (mu_p, mu_l, lora_up.astype(BF16), vec(w0), vec(a0), vec(k_k), vec(k_a), vec(r_k), ones_bd)
    ffn = (w_ffn_in.astype(BF16), conv_w, vec(conv_b),
           w_ffn_out.astype(BF16), vec(g_final))
    return dict(g_mix=vec(g_mix), w_cat=w_cat, prep=prep, ones_bd=ones_bd, ln_w=vec(ln_x_w), ln_b=vec(ln_x_b),
                w_o=w_o.astype(BF16), g_ffn=vec(g_ffn), ffn=ffn)


def kernel(x_prompt, x_sample, cache_win_k, cache_win_v, state_shift, state_wkv, state_ffn_conv, g_mix, w_in, rel_bias, tok_mu, w0, w_decay_up, a0, w_iclr_up, w_gate_up, k_k, k_a, r_k, ln_x_w, ln_x_b, w_o, g_ffn, w_ffn_in, conv_w, conv_b, w_ffn_out, g_final):
    batch, seq, _ = x_prompt.shape
    nb = x_sample.shape[0]
    lw = _layer_weights(g_mix[0], w_in[0], tok_mu[0], w0[0], w_decay_up[0], a0[0], w_iclr_up[0],
                        w_gate_up[0], k_k[0], k_a[0], r_k[0].reshape(-1), ln_x_w[0], ln_x_b[0], w_o[0],
                        g_ffn[0], w_ffn_in[0], conv_w[0], conv_b[0], w_ffn_out[0], g_final)
    (bias_p, bias_s), rb0 = _bias_build(rel_bias)

    xp = x_prompt.reshape(batch * seq, D_MODEL)
    qkv, r, lgw, k2, v, kk, b, gate, bonus, sh_p, sh_l, k_win, v_win = _inproj_prep_prompt(
        xp, lw['g_mix'], lw['w_cat'], lw['prep'], batch, seq, 512)
    rw, wkv_p = _wkv_prompt(r, lgw, k2, v, kk, b, gate, bonus, lw['ln_w'], lw['ln_b'], batch, seq)
    attn = _attn_prompt(qkv, bias_p, batch, seq)
    to_window = lambda t: jnp.transpose(t.reshape(batch, N_HEADS, HEAD_DIM, WIN), (0, 3, 1, 2))[None]
    win_k_p = to_window(k_win)
    win_v_p = to_window(v_win)
    shift_p = jnp.concatenate([sh_p[:, 0], sh_l[:, 0, :LORA_W]], axis=-1)[None]

    xs = x_sample.reshape(nb, D_MODEL)
    qkv_s, rkv_s, lora_s = _inproj(xs, lw['g_mix'], lw['w_cat'], nb)
    sh = state_shift[0]
    r, lgw, k2, v, kk, b, gate_s, bonus_s = _prep_sample(
        rkv_s, sh[:, :RKV_W], lora_s, jnp.pad(sh[:, RKV_W:], ((0, 0), (0, LORA_PAD - LORA_W))), lw['prep'])
    y_s, wkv_s = _wkv_sample(state_wkv[0], lgw, kk, b, k2, r, v, 8)
    q_s = qkv_s[:, :ATT_WIDTH]
    k_s = qkv_s[:, ATT_WIDTH:2 * ATT_WIDTH]
    v_s = qkv_s[:, 2 * ATT_WIDTH:]
    ck = jnp.transpose(cache_win_k[0], (0, 2, 3, 1))
    cv = jnp.transpose(cache_win_v[0], (0, 2, 3, 1))

    y_p, conv_p, attn_s, win_k_s, win_v_s = _ffn_prompt(xp, attn, rw, lw['w_o'], lw['g_ffn'], lw['ffn'],
                                                        batch, seq, (q_s, k_s, v_s, ck, cv, bias_s, rb0))

    win_k_s = jnp.transpose(win_k_s, (0, 3, 1, 2))
    win_v_s = jnp.transpose(win_v_s, (0, 3, 1, 2))
    h1_s, xn2_s = _outproj(xs, attn_s, y_s, gate_s, bonus_s, lw['ln_w'], lw['ln_b'], lw['w_o'], lw['g_ffn'],
                           lw['ones_bd'], nb)
    conv_state = state_ffn_conv[0]
    y_smp, gp_s = _ffn_sample(xn2_s, h1_s, conv_state[:, 1], conv_state[:, 0], lw['ffn'])
    conv_s = jnp.stack([conv_state[:, 1], gp_s], axis=1)
    shift_s = jnp.concatenate([rkv_s, lora_s[:, :LORA_W]], axis=-1)[None]

    return (y_p.reshape(batch, seq, D_MODEL), y_smp.reshape(nb, 1, D_MODEL),
            win_k_p, win_v_p, shift_p, wkv_p[None], conv_p[None],
            win_k_s[None], win_v_s[None], shift_s, wkv_s[None], conv_s[None])
```

```python
import functools

import numpy as np
import jax
import jax.numpy as jnp
from jax import lax
from jax.experimental import pallas as pl
from jax.experimental.pallas import tpu as pltpu

F32 = jnp.float32
BF16 = jnp.bfloat16

D_MODEL = 1024
HEAD_DIM = 64
ATT_WIDTH = 512
RWKV_WIDTH = 512
N_HEADS = 8
QKV_W = 3 * ATT_WIDTH
RKV_W = 3 * RWKV_WIDTH
LORA_DECAY, LORA_ICLR, LORA_GATE = 32, 32, 96
LORA_W = LORA_DECAY + LORA_ICLR + LORA_GATE
LORA_PAD = 256
RWKV_PROJ = RKV_W + LORA_W
PROJ = QKV_W + RWKV_PROJ
PROJ_PAD = QKV_W + RKV_W + LORA_PAD
ROW_TILE = 512
STATE_TILE = 8
D_FF = 2816
FF_CHUNK = 2816
CONV_W = 3
WIN = 2048
DILATIONS = (1, 4, 16)
Q_BLOCK = 128
SUPER = 2048
N_BUCKETS = 32
MAX_DISTANCE = 2048
NORM_EPS = 1e-6
GN_EPS = 64e-5
NEG = -1e30
ATT_SCALE = HEAD_DIM ** -0.5
LOG2E = float(np.log2(np.e))
CHUNK = 64
WKV_CHUNKS_PER_STEP = 2
VMEM_LIMIT = 56 * 1024 * 1024

NT = (((1,), (1,)), ((), ()))
TN = (((0,), (0,)), ((), ()))


def _dot(a, b):
    return jnp.dot(a.astype(BF16), b.astype(BF16), preferred_element_type=F32)


def _dot_nt(a, b):
    return lax.dot_general(a.astype(BF16), b.astype(BF16), NT, preferred_element_type=F32)


def _dot_tn(a, b):
    return lax.dot_general(a.astype(BF16), b.astype(BF16), TN, preferred_element_type=F32)


def _split2(x):
    hi = x.astype(BF16)
    lo = (x - hi.astype(F32)).astype(BF16)
    return hi, lo


def _split3(x):
    hi = x.astype(BF16)
    r1 = x - hi.astype(F32)
    mid = r1.astype(BF16)
    lo = (r1 - mid.astype(F32)).astype(BF16)
    return hi, mid, lo


def _segsum(x, ones_bd):
    hi, lo = _split2(x)
    return (jnp.dot(hi, ones_bd, preferred_element_type=F32)
            + jnp.dot(lo, ones_bd, preferred_element_type=F32))


def _sigmoid(x):
    return 1.0 / (1.0 + jnp.exp(-x))


def _eye(n):
    return jnp.where(lax.broadcasted_iota(jnp.int32, (n, n), 0) == lax.broadcasted_iota(jnp.int32, (n, n), 1),
                     1.0, 0.0)


def _rmsnorm(x, g):
    return x * lax.rsqrt(jnp.mean(x * x, axis=-1, keepdims=True) + NORM_EPS) * g


def _params(sem):
    return pltpu.CompilerParams(dimension_semantics=sem, vmem_limit_bytes=VMEM_LIMIT)


def _t5_bucket(dist):
    dist = np.asarray(dist, dtype=np.int64)
    exact = N_BUCKETS // 2
    scaled = np.log(np.maximum(dist, 1) / exact) / np.log(MAX_DISTANCE / exact)
    large = np.minimum(exact + (scaled * (N_BUCKETS - exact)).astype(np.int64), N_BUCKETS - 1)
    return np.where(dist < exact, dist, large).astype(np.int32)


def _inproj_kernel(x_ref, g_ref, w_ref, qkv_ref, rkv_ref, lora_ref):
    xb = _rmsnorm(x_ref[...], g_ref[...]).astype(BF16)
    proj = lambda a, b: lax.dot_general(xb, w_ref[a:b, :].astype(BF16), NT, preferred_element_type=F32)
    qkv_ref[...] = proj(0, QKV_W)
    rkv_ref[...] = proj(QKV_W, QKV_W + RKV_W)
    lora_ref[:, 0:LORA_W] = proj(QKV_W + RKV_W, PROJ)
    lora_ref[:, LORA_W:LORA_PAD] = jnp.zeros((x_ref.shape[0], LORA_PAD - LORA_W), F32)


def _inproj(x2d, g, w, tm):
    m = x2d.shape[0]
    row = lambda i: (i, 0)
    fixed = lambda i: (0, 0)
    return pl.pallas_call(
        _inproj_kernel,
        grid=(m // tm,),
        in_specs=[pl.BlockSpec((tm, D_MODEL), row),
                  pl.BlockSpec((1, D_MODEL), fixed),
                  pl.BlockSpec((PROJ, D_MODEL), fixed)],
        out_specs=[pl.BlockSpec((tm, QKV_W), row),
                   pl.BlockSpec((tm, RKV_W), row),
                   pl.BlockSpec((tm, LORA_PAD), row)],
        out_shape=[jax.ShapeDtypeStruct((m, QKV_W), F32),
                   jax.ShapeDtypeStruct((m, RKV_W), F32),
                   jax.ShapeDtypeStruct((m, LORA_PAD), F32)],
        compiler_params=_params(("arbitrary",)),
        name="inproj",
    )(x2d, g, w)


def _prep_math(p, p_prev, l, l_prev, mu_p, mu_l, wcat, w0, a0, k_k, k_a, r_k, ones_bd):
    pm = p + mu_p * (p_prev - p)
    lm = l + mu_l * (l_prev - l)
    r = pm[:, 0:RWKV_WIDTH]
    kr = pm[:, RWKV_WIDTH:2 * RWKV_WIDTH]
    vr = pm[:, 2 * RWKV_WIDTH:3 * RWKV_WIDTH]
    lane = lax.broadcasted_iota(jnp.int32, lm.shape, 1)
    feat = jnp.where(lane < LORA_DECAY, jnp.tanh(lm),
                     jnp.where(lane < LORA_DECAY + LORA_ICLR, lm, _sigmoid(lm)))
    z = jnp.dot(feat.astype(BF16), wcat, preferred_element_type=F32)
    zd = -(w0 + z[:, 0:RWKV_WIDTH])
    softplus = jnp.maximum(zd, 0.0) + jnp.log(1.0 + jnp.exp(-jnp.abs(zd)))
    lw = -jnp.exp(-softplus - 0.5)
    a = _sigmoid(a0 + z[:, RWKV_WIDTH:2 * RWKV_WIDTH])
    gate = z[:, 2 * RWKV_WIDTH:3 * RWKV_WIDTH]
    kk = kr * k_k
    head_sum = lambda t: jnp.dot(t.astype(BF16), ones_bd, preferred_element_type=F32)
    kk = kk / jnp.maximum(jnp.sqrt(head_sum(kk * kk)), 1e-12)
    k2 = kr * (1.0 + (a - 1.0) * k_a)
    bonus = head_sum(r * k2 * r_k) * vr
    return r, lw, k2, vr, kk, kk * a, gate, bonus


def _inproj_prep_kernel(x_ref, g_ref, w_ref, mu_p_ref, mu_l_ref, wcat_ref, w0_ref, a0_ref, kk_ref, ka_ref,
                        rk_ref, ones_ref, qkv_ref, *refs):
    outs, (sp_ref, sl_ref, kw_ref, vw_ref), (cp_ref, cl_ref, wb_ref) = refs[:8], refs[8:12], refs[12:]

    @pl.when(jnp.logical_and(pl.program_id(0) == 0, pl.program_id(1) == 0))
    def _():
        for c0 in range(0, PROJ, RWKV_WIDTH):
            c1 = min(c0 + RWKV_WIDTH, PROJ)
            wb_ref[c0:c1, :] = w_ref[c0:c1, :].astype(BF16)
        wb_ref[PROJ:PROJ_PAD, :] = jnp.zeros((PROJ_PAD - PROJ, D_MODEL), BF16)

    @pl.when(pl.program_id(1) == 0)
    def _():
        cp_ref[...] = jnp.zeros_like(cp_ref)
        cl_ref[...] = jnp.zeros_like(cl_ref)

    xb = _rmsnorm(x_ref[...], g_ref[...]).astype(BF16)
    proj = lambda a, b: lax.dot_general(xb, wb_ref[a:b, :], NT, preferred_element_type=F32)
    p = proj(QKV_W, QKV_W + RKV_W)
    l = proj(QKV_W + RKV_W, PROJ_PAD)
    tm = p.shape[0]
    first_p = lax.broadcasted_iota(jnp.int32, p.shape, 0) == 0
    first_l = lax.broadcasted_iota(jnp.int32, l.shape, 0) == 0
    p_prev = jnp.where(first_p, cp_ref[...], pltpu.roll(p, 1, 0))
    l_prev = jnp.where(first_l, cl_ref[...], pltpu.roll(l, 1, 0))
    cp_ref[...] = p[tm - 1:tm, :]
    cl_ref[...] = l[tm - 1:tm, :]
    sp_ref[0] = p[tm - 1:tm, :]
    sl_ref[0] = l[tm - 1:tm, :]
    res = _prep_math(p, p_prev, l, l_prev, mu_p_ref[...], mu_l_ref[...], wcat_ref[...], w0_ref[...],
                     a0_ref[...], kk_ref[...], ka_ref[...], rk_ref[...], ones_ref[...])
    for o, v in zip(outs, res):
        o[...] = v
    qkv = proj(0, QKV_W)
    qkv_ref[...] = qkv
    kw_ref[0] = qkv[:, ATT_WIDTH:2 * ATT_WIDTH].T
    vw_ref[0] = qkv[:, 2 * ATT_WIDTH:QKV_W].T


def _prep_sample_kernel(p_ref, pp_ref, l_ref, lp_ref, mu_p_ref, mu_l_ref, wcat_ref, w0_ref, a0_ref,
                        kk_ref, ka_ref, rk_ref, ones_ref, *outs):
    res = _prep_math(p_ref[...], pp_ref[...], l_ref[...], lp_ref[...], mu_p_ref[...], mu_l_ref[...],
                     wcat_ref[...], w0_ref[...], a0_ref[...], kk_ref[...], ka_ref[...], rk_ref[...],
                     ones_ref[...])
    for o, v in zip(outs, res):
        o[...] = v


def _inproj_prep_prompt(x2d, g, w, pw, batch, seq, tm):
    nt = seq // tm
    row = lambda b, j: (b * nt + j, 0)
    fixed = lambda b, j: (0, 0)
    vec = lambda n: pl.BlockSpec((1, n), fixed)
    once = pl.Buffered(1)
    m = x2d.shape[0]
    last = lambda n: pl.BlockSpec((1, 1, n), lambda b, j: (b, 0, 0))
    nw = WIN // tm
    assert nw * tm == WIN and nw <= nt
    win = pl.BlockSpec((1, ATT_WIDTH, tm), lambda b, j: (b, 0, jnp.maximum(j - (nt - nw), 0)))
    return pl.pallas_call(
        _inproj_prep_kernel,
        grid=(batch, nt),
        in_specs=[pl.BlockSpec((tm, D_MODEL), row), vec(D_MODEL),
                  pl.BlockSpec((PROJ, D_MODEL), fixed, pipeline_mode=once),
                  vec(RKV_W), vec(LORA_PAD), pl.BlockSpec((LORA_PAD, RKV_W), fixed),
                  vec(RWKV_WIDTH), vec(RWKV_WIDTH), vec(RWKV_WIDTH), vec(RWKV_WIDTH), vec(RWKV_WIDTH),
                  pl.BlockSpec((RWKV_WIDTH, RWKV_WIDTH), fixed)],
        out_specs=[pl.BlockSpec((tm, QKV_W), row)] + [pl.BlockSpec((tm, RWKV_WIDTH), row)] * 8
                  + [last(RKV_W), last(LORA_PAD), win, win],
        out_shape=[jax.ShapeDtypeStruct((m, QKV_W), F32)] + [jax.ShapeDtypeStruct((m, RWKV_WIDTH), F32)] * 8
                  + [jax.ShapeDtypeStruct((batch, 1, RKV_W), F32), jax.ShapeDtypeStruct((batch, 1, LORA_PAD), F32)]
                  + [jax.ShapeDtypeStruct((batch, ATT_WIDTH, WIN), F32)] * 2,
        scratch_shapes=[pltpu.VMEM((1, RKV_W), F32), pltpu.VMEM((1, LORA_PAD), F32),
                        pltpu.VMEM((PROJ_PAD, D_MODEL), BF16)],
        compiler_params=_params(("arbitrary", "arbitrary")),
        name="inproj_prep_prompt",
    )(x2d, g, w, *pw)


def _prep_sample(rkv, rkv_prev, lora, lora_prev, pw):
    m = rkv.shape[0]
    full = lambda a: pl.BlockSpec(a.shape, lambda i: (0,) * a.ndim)
    args = (rkv, rkv_prev, lora, lora_prev) + tuple(pw)
    return pl.pallas_call(
        _prep_sample_kernel,
        grid=(1,),
        in_specs=[full(a) for a in args],
        out_specs=[pl.BlockSpec((m, RWKV_WIDTH), lambda i: (0, 0))] * 8,
        out_shape=[jax.ShapeDtypeStruct((m, RWKV_WIDTH), F32)] * 8,
        compiler_params=_params(("arbitrary",)),
        name="rwkv_prep_sample",
    )(*args)


def _cumsum_rows(x, tri):
    tri_b = tri.astype(BF16)
    out = None
    for piece in _split3(x):
        t = jnp.dot(tri_b, piece, preferred_element_type=F32)
        out = t if out is None else out + t
    return out


def _wkv_kernel(r_ref, lw_ref, k_ref, v_ref, kk_ref, b_ref, gate_ref, bonus_ref, lnw_ref, lnb_ref,
                y_ref, s_ref, st_ref):
    c = pl.program_id(0)

    @pl.when(c == 0)
    def _():
        st_ref[...] = jnp.zeros_like(st_ref)

    nb = r_ref.shape[0]
    n = CHUNK
    row = lax.broadcasted_iota(jnp.int32, (n, n), 0)
    col = lax.broadcasted_iota(jnp.int32, (n, n), 1)
    eye = jnp.where(row == col, 1.0, 0.0)
    tri = jnp.where(row >= col, 1.0, 0.0)
    row2 = lax.broadcasted_iota(jnp.int32, (2 * n, 2 * n), 0)
    col2 = lax.broadcasted_iota(jnp.int32, (2 * n, 2 * n), 1) % n
    gram_mask = jnp.logical_or(row2 % n > col2, jnp.logical_and(row2 >= n, row2 % n == col2))
    zeros = jnp.zeros((n, HEAD_DIM), BF16)

    units = range(nb * N_HEADS)
    stack = lambda xs, ys: [jnp.concatenate([x, y], axis=0) for x, y in zip(xs, ys)]

    def prelude(rows):
        ah, bh, kh, rh, beh, keh, vh, ge = ([] for _ in range(8))
        for bi in range(nb):
            lw = lw_ref[bi, rows, :]
            lg = _cumsum_rows(lw, tri)
            g_inv = jnp.exp(-lg)
            lg_end = lg[n - 1:n, :]
            g_to_end = jnp.exp(lg_end - lg)
            g_end = jnp.exp(lg_end)
            kk = kk_ref[bi, rows, :]
            b = b_ref[bi, rows, :]
            k = k_ref[bi, rows, :]
            full = (-kk * jnp.exp(lg - lw), b * g_inv, k * g_inv, r_ref[bi, rows, :] * jnp.exp(lg),
                    b * g_to_end, k * g_to_end, v_ref[bi, rows, :], g_end)
            for h in range(N_HEADS):
                sl = slice(h * HEAD_DIM, (h + 1) * HEAD_DIM)
                for dst, t in zip((ah, bh, kh, rh, beh, keh, vh), full[:7]):
                    dst.append(t[:, sl].astype(BF16))
                ge.append(g_end[:, sl])

        gram = [jnp.where(gram_mask, _dot_nt(ar, bk), 0.0) for ar, bk in zip(stack(ah, rh), stack(bh, kh))]
        top = [g[:n].astype(BF16) for g in gram]
        bot = [g[n:].astype(BF16) for g in gram]
        lv = [_dot(top[u], jnp.concatenate([zeros, vh[u]], axis=0)) for u in units]
        tm_ = [eye + gram[u][:n, :n] for u in units]
        lp = [_dot(t[:, :n], t[:, :n]).astype(BF16) for t in top]
        span = 2
        while 2 * span < n:
            prod = [_dot(x, p) for x, p in zip(stack(lp, [t.astype(BF16) for t in tm_]), lp)]
            tm_ = [tm_[u] + prod[u][n:] for u in units]
            lp = [p[:n].astype(BF16) for p in prod]
            span *= 2
        tm_ = [(tm_[u] + _dot(tm_[u], lp[u])).astype(BF16) for u in units]
        a_hat = [_dot(tm_[u], ah[u]).astype(BF16) for u in units]
        u0 = [_dot(tm_[u], lv[u]) for u in units]
        return rows, stack(a_hat, rh), u0, vh, bot, stack(beh, keh), ge

    def finish(rows, a_hat_r, u0, vh, bot, be_ke, ge):
        s0 = [st_ref[u] for u in units]
        on_s0 = [_dot_nt(x, s.astype(BF16)) for x, s in zip(a_hat_r, s0)]
        uv = stack([(on_s0[u][:n] + u0[u]).astype(BF16) for u in units], vh)
        y = [on_s0[u][n:] + _dot(bot[u], uv[u]) for u in units]
        for u in units:
            st_ref[u] = s0[u] * ge[u] + _dot_tn(uv[u], be_ke[u])
        for bi in range(nb):
            normed = []
            for yu in y[bi * N_HEADS:(bi + 1) * N_HEADS]:
                dy = yu - jnp.mean(yu, axis=-1, keepdims=True)
                normed.append(dy * lax.rsqrt(jnp.mean(dy * dy, axis=-1, keepdims=True) + GN_EPS))
            yn = jnp.concatenate(normed, axis=1)
            y_ref[bi, rows, :] = ((yn * lnw_ref[...] + lnb_ref[...] + bonus_ref[bi, rows, :])
                                  * gate_ref[bi, rows, :]).astype(BF16)

    for j in range(r_ref.shape[1] // n):
        finish(*prelude(pl.ds(j * n, n)))

    @pl.when(c == pl.num_programs(0) - 1)
    def _():
        s_ref[...] = st_ref[...]


def _wkv_prompt(r, lw, k2, v, kk, b, gate, bonus, ln_w, ln_b, batch, seq):
    rows = WKV_CHUNKS_PER_STEP * CHUNK
    nc = seq // rows
    assert nc * rows == seq
    seqs = lambda t: t.reshape(batch, seq, RWKV_WIDTH)
    spec = pl.BlockSpec((batch, rows, RWKV_WIDTH), lambda c: (0, c, 0))
    vec = pl.BlockSpec((1, RWKV_WIDTH), lambda c: (0, 0))
    nu = batch * N_HEADS
    y, s = pl.pallas_call(
        _wkv_kernel,
        grid=(nc,),
        in_specs=[spec] * 8 + [vec, vec],
        out_specs=[spec, pl.BlockSpec((nu, HEAD_DIM, HEAD_DIM), lambda c: (0, 0, 0))],
        out_shape=[jax.ShapeDtypeStruct((batch, seq, RWKV_WIDTH), BF16),
                   jax.ShapeDtypeStruct((nu, HEAD_DIM, HEAD_DIM), F32)],
        scratch_shapes=[pltpu.VMEM((nu, HEAD_DIM, HEAD_DIM), F32)],
        compiler_params=_params(("arbitrary",)),
        name="wkv_prompt",
    )(*(seqs(t) for t in (r, lw, k2, v, kk, b, gate, bonus)), ln_w, ln_b)
    return y.reshape(batch * seq, RWKV_WIDTH), s.reshape(batch, N_HEADS, HEAD_DIM, HEAD_DIM)


def _wkv_step_kernel(s_ref, lw_ref, kk_ref, b_ref, k_ref, r_ref, v_ref, y_ref, so_ref):
    s = s_ref[...]
    eye = _eye(HEAD_DIM)
    v_col = jnp.sum(eye * v_ref[...], axis=-1, keepdims=True)
    sa = jnp.sum(s * (-kk_ref[...]), axis=-1, keepdims=True)
    s = s * jnp.exp(lw_ref[...]) + sa * b_ref[...] + v_col * k_ref[...]
    so_ref[...] = s
    y_col = jnp.sum(s * r_ref[...], axis=-1, keepdims=True)
    y_ref[...] = jnp.sum(eye * y_col, axis=-2, keepdims=True)


def _wkv_sample(state, lw, kk, b, k2, r, v, bb):
    nb = state.shape[0]
    rowv = lambda t: t.reshape(nb, N_HEADS, 1, HEAD_DIM)
    idx = lambda i: (i, 0, 0, 0)
    s_spec = pl.BlockSpec((bb, N_HEADS, HEAD_DIM, HEAD_DIM), idx)
    r_spec = pl.BlockSpec((bb, N_HEADS, 1, HEAD_DIM), idx)
    y, s_new = pl.pallas_call(
        _wkv_step_kernel,
        grid=(nb // bb,),
        in_specs=[s_spec] + [r_spec] * 6,
        out_specs=[r_spec, s_spec],
        out_shape=[jax.ShapeDtypeStruct((nb, N_HEADS, 1, HEAD_DIM), F32),
                   jax.ShapeDtypeStruct(state.shape, F32)],
        compiler_params=_params(("arbitrary",)),
        name="wkv_sample",
    )(state, rowv(lw), rowv(kk), rowv(b), rowv(k2), rowv(r), rowv(v))
    return y.reshape(nb, RWKV_WIDTH), s_new


def _bias_tables():
    i = np.arange(Q_BLOCK)[:, None]
    c = np.arange(2 * Q_BLOCK)[None, :]
    n = i + Q_BLOCK - c
    ok = (n >= 0) & (n <= Q_BLOCK)
    prompt = np.stack([np.where(ok, _t5_bucket(d * np.clip(n, 0, Q_BLOCK)), -1) for d in DILATIONS])
    dist = WIN - np.arange(WIN)
    sample = np.stack([np.where((dist % d == 0) & (dist // d <= Q_BLOCK), _t5_bucket(dist), -1)
                       for d in DILATIONS])
    sample = np.broadcast_to(sample[:, None, :], (len(DILATIONS), N_HEADS, WIN))
    return prompt.astype(np.int32), sample.astype(np.int32)


def _bias_kernel(rb_ref, rbc_ref, bp_ref, bs_ref, op_ref, os_ref):
    bp = bp_ref[0]
    bs = bs_ref[0]
    acc_s = jnp.full(bs.shape, NEG, F32)
    for bk in range(N_BUCKETS):
        acc_s = jnp.where(bs == bk, rbc_ref[bk], acc_s)
    os_ref[0] = acc_s
    for h in range(N_HEADS):
        acc = jnp.full(bp.shape, NEG, F32)
        for bk in range(N_BUCKETS):
            acc = jnp.where(bp == bk, rb_ref[bk, h], acc)
        op_ref[0, h] = acc * LOG2E


def _bias_build(rel_bias):
    bp, bs = _bias_tables()
    rbc = rel_bias.reshape(N_BUCKETS, N_HEADS, 1)
    nb = len(DILATIONS)
    return pl.pallas_call(
        _bias_kernel,
        grid=(nb,),
        in_specs=[pl.BlockSpec(memory_space=pltpu.SMEM),
                  pl.BlockSpec((N_BUCKETS, N_HEADS, 1), lambda i: (0, 0, 0)),
                  pl.BlockSpec((1, Q_BLOCK, 2 * Q_BLOCK), lambda i: (i, 0, 0)),
                  pl.BlockSpec((1, N_HEADS, WIN), lambda i: (i, 0, 0))],
        out_specs=[pl.BlockSpec((1, N_HEADS, Q_BLOCK, 2 * Q_BLOCK), lambda i: (i, 0, 0, 0)),
                   pl.BlockSpec((1, N_HEADS, WIN), lambda i: (i, 0, 0))],
        out_shape=[jax.ShapeDtypeStruct((nb, N_HEADS, Q_BLOCK, 2 * Q_BLOCK), F32),
                   jax.ShapeDtypeStruct((nb, N_HEADS, WIN), F32)],
        compiler_params=_params(("arbitrary",)),
        name="bias_build",
    )(rel_bias, rbc, jnp.asarray(bp), jnp.asarray(bs)), rbc[0]


ATTN_GROUP = 4


def _attn_units():
    units = []
    for br, d in enumerate(DILATIONS):
        nblk = SUPER // (d * Q_BLOCK)
        for r in range(d):
            rows = lambda j, r=r, d=d: (pl.ds(r + d * Q_BLOCK * j, Q_BLOCK, stride=d) if d > 1
                                        else pl.ds(Q_BLOCK * j, Q_BLOCK))
            for blk in range(nblk):
                units.append((br, rows(blk), rows(blk - 1 if blk else nblk - 1), blk == 0))
    return units


def _attn_kernel(q_ref, kc_ref, kp_ref, vc_ref, vp_ref, bias_ref, o_ref, acc_ref, m_ref, edge_ref):
    has_prev = pl.program_id(2) > 0
    is_a = lax.broadcasted_iota(jnp.int32, (Q_BLOCK, 128), 1) < HEAD_DIM
    is_a2 = lax.broadcasted_iota(jnp.int32, (2 * Q_BLOCK, 128), 1) < HEAD_DIM
    live = jnp.logical_or(has_prev,
                          lax.broadcasted_iota(jnp.int32, (2 * Q_BLOCK, 2 * Q_BLOCK), 1) >= Q_BLOCK)
    for br in range(len(DILATIONS)):
        edge_ref[br] = jnp.where(live, bias_ref[br], NEG)
    units = _attn_units()
    for g0 in range(0, len(units), ATTN_GROUP):
        group = units[g0:g0 + ATTN_GROUP]
        ops = []
        for br, cur, prv, from_prev in group:
            qs = q_ref[cur, :] * (ATT_SCALE * LOG2E)
            q2 = jnp.concatenate([jnp.where(is_a, qs, 0.0), jnp.where(is_a, 0.0, qs)], axis=0)
            k2 = jnp.concatenate([(kp_ref if from_prev else kc_ref)[prv, :], kc_ref[cur, :]], axis=0)
            v2 = jnp.concatenate([(vp_ref if from_prev else vc_ref)[prv, :], vc_ref[cur, :]], axis=0)
            ops.append((br, cur, from_prev, q2.astype(BF16), k2.astype(BF16),
                        jnp.where(is_a2, v2, 1.0).astype(BF16), jnp.where(is_a2, 1.0, v2).astype(BF16)))
        logits = []
        for br, cur, from_prev, q2, k2, va, vb in ops:
            logits.append(lax.dot_general(q2, k2, NT, preferred_element_type=F32)
                          + (edge_ref if from_prev else bias_ref)[br])
        ms = [jnp.max(lg, axis=-1, keepdims=True) for lg in logits]
        ps = [jnp.exp2(lg - m).astype(BF16) for lg, m in zip(logits, ms)]
        for (br, cur, _, _, _, va, vb), p, m in zip(ops, ps, ms):
            acc_ref[br, 0, cur, :] = jnp.dot(p[:Q_BLOCK], va, preferred_element_type=F32)
            acc_ref[br, 1, cur, :] = jnp.dot(p[Q_BLOCK:], vb, preferred_element_type=F32)
            m_ref[br, 0, cur, :] = jnp.broadcast_to(m[:Q_BLOCK], (Q_BLOCK, 128))
            m_ref[br, 1, cur, :] = jnp.broadcast_to(m[Q_BLOCK:], (Q_BLOCK, 128))

    for i in range(SUPER // Q_BLOCK):
        rs = pl.ds(i * Q_BLOCK, Q_BLOCK)
        outs = []
        for hd in range(2):
            ms = [m_ref[br, hd, rs, :] for br in range(len(DILATIONS))]
            m = functools.reduce(jnp.maximum, ms)
            tot = 0.0
            for br in range(len(DILATIONS)):
                tot = tot + jnp.exp2(ms[br] - m) * acc_ref[br, hd, rs, :]
            outs.append(tot / pltpu.roll(tot, HEAD_DIM, 1))
        o_ref[rs, :] = jnp.where(is_a, outs[0], outs[1]).astype(o_ref.dtype)


def _attn_prompt(qkv, bias_p, batch, seq):
    ns = seq // SUPER
    npair = ATT_WIDTH // 128
    blk = lambda col0, prev: pl.BlockSpec(
        (SUPER, 128),
        (lambda b, hp, sb: (b * ns + jnp.maximum(sb - 1, 0), col0 + hp)) if prev
        else (lambda b, hp, sb: (b * ns + sb, col0 + hp)))
    nb = len(DILATIONS)
    return pl.pallas_call(
        _attn_kernel,
        grid=(batch, npair, ns),
        in_specs=[blk(0, False), blk(npair, False), blk(npair, True), blk(2 * npair, False),
                  blk(2 * npair, True),
                  pl.BlockSpec((nb, None, 2 * Q_BLOCK, 2 * Q_BLOCK), lambda b, hp, sb: (0, hp, 0, 0))],
        out_specs=pl.BlockSpec((SUPER, 128), lambda b, hp, sb: (b * ns + sb, hp)),
        out_shape=jax.ShapeDtypeStruct((batch * seq, ATT_WIDTH), BF16),
        scratch_shapes=[pltpu.VMEM((nb, 2, SUPER, 128), F32)] * 2
                       + [pltpu.VMEM((nb, 2 * Q_BLOCK, 2 * Q_BLOCK), F32)],
        compiler_params=_params(("arbitrary", "arbitrary", "arbitrary")),
        name="attn_prompt",
    )(qkv, qkv, qkv, qkv, qkv, bias_p.reshape(nb, npair, 2 * Q_BLOCK, 2 * Q_BLOCK))


WINDOW_HEADS = 4


def _window_body(q_ref, kn_ref, vn_ref, ck_ref, cv_ref, bias_ref, rb0_ref, o_ref, ok_ref, ov_ref, wt_ref):
    nh = WINDOW_HEADS
    lane = lax.broadcasted_iota(jnp.int32, (HEAD_DIM, WIN), 1)
    last = lane == WIN - 1
    eye = _eye(HEAD_DIM)
    col = lambda ref, h: jnp.sum(eye * ref[h:h + 1, :], axis=-1, keepdims=True)
    l_new = jnp.sum(kn_ref[...] * (q_ref[...] * ATT_SCALE), axis=-1, keepdims=True) + rb0_ref[...]
    rows = []
    for h in range(nh):
        kh = ck_ref[h]
        rows.append(jnp.sum(kh * (col(q_ref, h) * ATT_SCALE), axis=0, keepdims=True))
        ok_ref[h] = jnp.where(last, col(kn_ref, h), pltpu.roll(kh, WIN - 1, 1))
    logits = jnp.concatenate(rows, axis=0)
    parts = []
    for br in range(len(DILATIONS)):
        lg = logits + bias_ref[br]
        m = jnp.maximum(jnp.max(lg, axis=-1, keepdims=True), l_new)
        p = jnp.exp(lg - m)
        p_new = jnp.exp(l_new - m)
        parts.append((p, p_new, m, jnp.sum(p, axis=-1, keepdims=True) + p_new))
    m = functools.reduce(jnp.maximum, [pt[2] for pt in parts])
    wt = 0.0
    w_new = 0.0
    den = 0.0
    for p, p_new, m_g, s_g in parts:
        w = jnp.exp(m_g - m)
        wt = wt + w * p
        w_new = w_new + w * p_new
        den = den + w * s_g
    wt_ref[0:nh, :] = wt / den
    w_new = w_new / den
    for h in range(nh):
        vh = cv_ref[h]
        v_new = col(vn_ref, h)
        o_col = jnp.sum(vh * wt_ref[h:h + 1, :], axis=1, keepdims=True) + v_new * w_new[h:h + 1, :]
        o_ref[h:h + 1, :] = jnp.sum(eye * o_col, axis=0, keepdims=True)
        ov_ref[h] = jnp.where(last, v_new, pltpu.roll(vh, WIN - 1, 1))


N_WINDOW_IN = 7


def _window_operands(q, k_new, v_new, cache_k, cache_v, bias_s, rb0, step_of):
    nb = q.shape[0]
    nh = WINDOW_HEADS
    groups = N_HEADS // nh
    seq_grp = lambda *g: (step_of(*g) // groups, step_of(*g) % groups)
    rows = lambda t: t.reshape(nb, groups, nh, HEAD_DIM)
    wins = lambda t: t.reshape(nb, groups, nh, HEAD_DIM, WIN)
    row_spec = pl.BlockSpec((None, None, nh, HEAD_DIM), lambda *g: seq_grp(*g) + (0, 0))
    win_spec = pl.BlockSpec((None, None, nh, HEAD_DIM, WIN), lambda *g: seq_grp(*g) + (0, 0, 0))
    nbr = len(DILATIONS)
    bias_spec = pl.BlockSpec((nbr, None, nh, WIN), lambda *g: (0, seq_grp(*g)[1], 0, 0))
    rb0_spec = pl.BlockSpec((None, nh, 1), lambda *g: (seq_grp(*g)[1], 0, 0))
    args = (rows(q), rows(k_new), rows(v_new), wins(cache_k), wins(cache_v),
            bias_s.reshape(nbr, groups, nh, WIN), rb0.reshape(groups, nh, 1))
    in_specs = [row_spec, row_spec, row_spec, win_spec, win_spec, bias_spec, rb0_spec]
    assert len(args) == len(in_specs) == N_WINDOW_IN
    out_specs = [row_spec, win_spec, win_spec]
    out_shape = [jax.ShapeDtypeStruct((nb, groups, nh, HEAD_DIM), F32),
                 jax.ShapeDtypeStruct((nb, groups, nh, HEAD_DIM, WIN), F32),
                 jax.ShapeDtypeStruct((nb, groups, nh, HEAD_DIM, WIN), F32)]
    return args, in_specs, out_specs, out_shape


def _outproj_kernel(x_ref, attn_ref, y_ref, gate_ref, bonus_ref, lnw_ref, lnb_ref, wo_ref, gffn_ref,
                    ones_ref, h_ref, xn_ref):
    ones_bd = ones_ref[...]
    y = y_ref[...]
    mu = _segsum(y, ones_bd) * (1.0 / HEAD_DIM)
    dy = y - mu
    var = _segsum(dy * dy, ones_bd) * (1.0 / HEAD_DIM)
    yn = dy * lax.rsqrt(var + GN_EPS) * lnw_ref[...] + lnb_ref[...]
    rw = (yn + bonus_ref[...]) * gate_ref[...]
    mixed = (jnp.dot(attn_ref[...].astype(BF16), wo_ref[0:ATT_WIDTH, :], preferred_element_type=F32)
             + jnp.dot(rw.astype(BF16), wo_ref[ATT_WIDTH:D_MODEL, :], preferred_element_type=F32))
    h = x_ref[...] + mixed
    h_ref[...] = h
    xn_ref[...] = _rmsnorm(h, gffn_ref[...]).astype(BF16)


def _outproj(x2d, attn, y, gate, bonus, ln_w, ln_b, w_o, g_ffn, ones_bd, tm):
    m = x2d.shape[0]
    row = lambda i: (i, 0)
    fixed = lambda i: (0, 0)
    half = pl.BlockSpec((tm, RWKV_WIDTH), row)
    full = pl.BlockSpec((tm, D_MODEL), row)
    return pl.pallas_call(
        _outproj_kernel,
        grid=(m // tm,),
        in_specs=[full, half, half, half, half,
                  pl.BlockSpec((1, RWKV_WIDTH), fixed), pl.BlockSpec((1, RWKV_WIDTH), fixed),
                  pl.BlockSpec((D_MODEL, D_MODEL), fixed), pl.BlockSpec((1, D_MODEL), fixed),
                  pl.BlockSpec((RWKV_WIDTH, RWKV_WIDTH), fixed)],
        out_specs=[full, full],
        out_shape=[jax.ShapeDtypeStruct((m, D_MODEL), F32), jax.ShapeDtypeStruct((m, D_MODEL), BF16)],
        compiler_params=_params(("arbitrary",)),
        name="outproj",
    )(x2d, attn, y, gate, bonus, ln_w, ln_b, w_o, g_ffn, ones_bd)


def _ffn_tail(xn, h, gp_m1, gp_m2, gp_of, wi_ref, cw_ref, cb_ref, wout_ref, gfin_ref, o_ref):
    acc = None
    for c in range(D_FF // FF_CHUNK):
        cs = slice(c * FF_CHUNK, (c + 1) * FF_CHUNK)
        gp = gp_of(c)
        up = jnp.dot(xn, wi_ref[:, D_FF + c * FF_CHUNK:D_FF + (c + 1) * FF_CHUNK], preferred_element_type=F32)
        conv = (cb_ref[:, cs] + cw_ref[0:1, cs] * gp_m2(c, gp) + cw_ref[1:2, cs] * gp_m1(c, gp)
                + cw_ref[2:3, cs] * gp)
        act = conv * _sigmoid(conv) * up
        part = jnp.dot(act.astype(BF16), wout_ref[cs, :], preferred_element_type=F32)
        acc = part if acc is None else acc + part
    o_ref[...] = _rmsnorm(h + acc, gfin_ref[...])


def _ffn_prompt_kernel(x_ref, attn_ref, rw_ref, wo_ref, gffn_ref, wi_ref, cw_ref, cb_ref, wout_ref,
                       gfin_ref, *refs):
    n = N_WINDOW_IN
    win_in, (o_ref, conv_ref), win_out, (carry_ref, wt_ref) = refs[:n], refs[n:n + 2], refs[n + 2:n + 5], refs[n + 5:]

    @pl.when(pl.program_id(1) == 0)
    def _():
        carry_ref[...] = jnp.zeros_like(carry_ref)

    _window_body(*win_in, *win_out, wt_ref)

    h = (x_ref[...]
         + jnp.dot(attn_ref[...], wo_ref[0:ATT_WIDTH, :], preferred_element_type=F32)
         + jnp.dot(rw_ref[...], wo_ref[ATT_WIDTH:D_MODEL, :], preferred_element_type=F32))
    xn = _rmsnorm(h, gffn_ref[...]).astype(BF16)
    tm = xn.shape[0]
    rowi = lax.broadcasted_iota(jnp.int32, (tm, FF_CHUNK), 0)
    gps = {}

    def gp_of(c):
        cs = slice(c * FF_CHUNK, (c + 1) * FF_CHUNK)
        gps[c] = jnp.dot(xn, wi_ref[:, cs], preferred_element_type=F32)
        return gps[c]

    def gp_m1(c, gp):
        cs = slice(c * FF_CHUNK, (c + 1) * FF_CHUNK)
        return jnp.where(rowi == 0, carry_ref[1:2, cs], pltpu.roll(gp, 1, 0))

    def gp_m2(c, gp):
        cs = slice(c * FF_CHUNK, (c + 1) * FF_CHUNK)
        return jnp.where(rowi == 0, carry_ref[0:1, cs],
                         jnp.where(rowi == 1, carry_ref[1:2, cs], pltpu.roll(gp, 2, 0)))

    _ffn_tail(xn, h, gp_m1, gp_m2, gp_of, wi_ref, cw_ref, cb_ref, wout_ref, gfin_ref, o_ref)
    for c, gp in gps.items():
        cs = slice(c * FF_CHUNK, (c + 1) * FF_CHUNK)
        carry_ref[:, cs] = gp[tm - 2:tm, :]
        conv_ref[0, :, cs] = gp[tm - 2:tm, :]


def _ffn_sample_kernel(xn_ref, h_ref, p1_ref, p2_ref, wi_ref, cw_ref, cb_ref, wout_ref, gfin_ref,
                       o_ref, gp_ref):
    xn = xn_ref[...]

    def gp_of(c):
        cs = slice(c * FF_CHUNK, (c + 1) * FF_CHUNK)
        gp = jnp.dot(xn, wi_ref[:, cs], preferred_element_type=F32)
        gp_ref[:, cs] = gp
        return gp

    gp_m1 = lambda c, gp: p1_ref[:, c * FF_CHUNK:(c + 1) * FF_CHUNK]
    gp_m2 = lambda c, gp: p2_ref[:, c * FF_CHUNK:(c + 1) * FF_CHUNK]
    _ffn_tail(xn, h_ref[...], gp_m1, gp_m2, gp_of, wi_ref, cw_ref, cb_ref, wout_ref, gfin_ref, o_ref)


def _ffn_weight_specs(fixed):
    once = pl.Buffered(1)
    return [pl.BlockSpec((D_MODEL, 2 * D_FF), fixed, pipeline_mode=once),
            pl.BlockSpec((CONV_W, D_FF), fixed), pl.BlockSpec((1, D_FF), fixed),
            pl.BlockSpec((D_FF, D_MODEL), fixed, pipeline_mode=once), pl.BlockSpec((1, D_MODEL), fixed)]


def _ffn_prompt(x2d, attn, rw, w_o, g_ffn, fw, batch, seq, window):
    nt = window[0].shape[0] * (N_HEADS // WINDOW_HEADS) // batch
    tm = seq // nt
    assert tm * nt == seq and tm % 16 == 0
    row = lambda b, j: (b * nt + j, 0)
    fixed = lambda b, j: (0, 0)
    full = pl.BlockSpec((tm, D_MODEL), row)
    half = pl.BlockSpec((tm, ATT_WIDTH), row)
    mix_specs = [full, half, half, pl.BlockSpec((D_MODEL, D_MODEL), fixed), pl.BlockSpec((1, D_MODEL), fixed)]
    w_args, w_in_specs, w_out_specs, w_out_shape = _window_operands(*window, lambda b, j: b * nt + j)
    y, conv, attn_s, win_k, win_v = pl.pallas_call(
        _ffn_prompt_kernel,
        grid=(batch, nt),
        in_specs=mix_specs + _ffn_weight_specs(fixed) + w_in_specs,
        out_specs=[full, pl.BlockSpec((1, CONV_W - 1, D_FF), lambda b, j: (b, 0, 0))] + w_out_specs,
        out_shape=[jax.ShapeDtypeStruct((batch * seq, D_MODEL), F32),
                   jax.ShapeDtypeStruct((batch, CONV_W - 1, D_FF), F32)] + w_out_shape,
        scratch_shapes=[pltpu.VMEM((CONV_W - 1, D_FF), F32), pltpu.VMEM((8, WIN), F32)],
        compiler_params=_params(("arbitrary", "arbitrary")),
        name="ffn_prompt",
    )(x2d, attn, rw, w_o, g_ffn, *fw, *w_args)
    nb = window[0].shape[0]
    return (y, conv, attn_s.reshape(nb, ATT_WIDTH), win_k.reshape(nb, N_HEADS, HEAD_DIM, WIN),
            win_v.reshape(nb, N_HEADS, HEAD_DIM, WIN))


def _ffn_sample(xn, h, prev1, prev2, fw):
    m = xn.shape[0]
    fixed = lambda i: (0, 0)
    full = pl.BlockSpec((m, D_MODEL), fixed)
    ffs = pl.BlockSpec((m, D_FF), fixed)
    return pl.pallas_call(
        _ffn_sample_kernel,
        grid=(1,),
        in_specs=[full, full, ffs, ffs] + _ffn_weight_specs(fixed),
        out_specs=[full, ffs],
        out_shape=[jax.ShapeDtypeStruct((m, D_MODEL), F32), jax.ShapeDtypeStruct((m, D_FF), F32)],
        compiler_params=_params(("arbitrary",)),
        name="ffn_sample",
    )(xn, h, prev1, prev2, *fw)


def _layer_weights(g_mix, w_in, tok_mu, w0, w_decay_up, a0, w_iclr_up, w_gate_up, k_k, k_a, r_k,
                   ln_x_w, ln_x_b, w_o, g_ffn, w_ffn_in, conv_w, conv_b, w_ffn_out, g_final):
    vec = lambda t: t.reshape(1, -1)
    pad = LORA_PAD - LORA_W
    w_in_t = jnp.transpose(w_in)
    mu_p = vec(tok_mu[:RKV_W])
    mu_l = vec(jnp.pad(tok_mu[RKV_W:], (0, pad)))
    lora_up = jnp.zeros((LORA_PAD, RKV_W), F32)
    lora_up = lora_up.at[0:LORA_DECAY, 0:RWKV_WIDTH].set(w_decay_up)
    lora_up = lora_up.at[LORA_DECAY:LORA_DECAY + LORA_ICLR, RWKV_WIDTH:2 * RWKV_WIDTH].set(w_iclr_up)
    lora_up = lora_up.at[LORA_DECAY + LORA_ICLR:LORA_W, 2 * RWKV_WIDTH:].set(w_gate_up)
    seg = np.arange(RWKV_WIDTH) // HEAD_DIM
    ones_bd = jnp.asarray(seg[:, None] == seg[None, :], BF16)
    prep = (mu_p, mu_l, lora_up.astype(BF16), vec(w0), vec(a0), vec(k_k), vec(k_a), vec(r_k), ones_bd)
    ffn = (w_ffn_in.astype(BF16), conv_w, vec(conv_b),
           w_ffn_out.astype(BF16), vec(g_final))
    return dict(g_mix=vec(g_mix), w_in_t=w_in_t, prep=prep, ones_bd=ones_bd, ln_w=vec(ln_x_w), ln_b=vec(ln_x_b),
                w_o=w_o.astype(BF16), g_ffn=vec(g_ffn), ffn=ffn)


def kernel(x_prompt, x_sample, cache_win_k, cache_win_v, state_shift, state_wkv, state_ffn_conv, g_mix, w_in, rel_bias, tok_mu, w0, w_decay_up, a0, w_iclr_up, w_gate_up, k_k, k_a, r_k, ln_x_w, ln_x_b, w_o, g_ffn, w_ffn_in, conv_w, conv_b, w_ffn_out, g_final):
    batch, seq, _ = x_prompt.shape
    nb = x_sample.shape[0]
    lw = _layer_weights(g_mix[0], w_in[0], tok_mu[0], w0[0], w_decay_up[0], a0[0], w_iclr_up[0],
                        w_gate_up[0], k_k[0], k_a[0], r_k[0].reshape(-1), ln_x_w[0], ln_x_b[0], w_o[0],
                        g_ffn[0], w_ffn_in[0], conv_w[0], conv_b[0], w_ffn_out[0], g_final)
    (bias_p, bias_s), rb0 = _bias_build(rel_bias)

    xp = x_prompt.reshape(batch * seq, D_MODEL)
    qkv, r, lgw, k2, v, kk, b, gate, bonus, sh_p, sh_l, k_win, v_win = _inproj_prep_prompt(
        xp, lw['g_mix'], lw['w_in_t'], lw['prep'], batch, seq, ROW_TILE)
    rw, wkv_p = _wkv_prompt(r, lgw, k2, v, kk, b, gate, bonus, lw['ln_w'], lw['ln_b'], batch, seq)
    attn = _attn_prompt(qkv, bias_p, batch, seq)
    to_window = lambda t: jnp.transpose(t.reshape(batch, N_HEADS, HEAD_DIM, WIN), (0, 3, 1, 2))[None]
    win_k_p = to_window(k_win)
    win_v_p = to_window(v_win)
    shift_p = jnp.concatenate([sh_p[:, 0], sh_l[:, 0, :LORA_W]], axis=-1)[None]

    xs = x_sample.reshape(nb, D_MODEL)
    qkv_s, rkv_s, lora_s = _inproj(xs, lw['g_mix'], lw['w_in_t'], nb)
    sh = state_shift[0]
    r, lgw, k2, v, kk, b, gate_s, bonus_s = _prep_sample(
        rkv_s, sh[:, :RKV_W], lora_s, jnp.pad(sh[:, RKV_W:], ((0, 0), (0, LORA_PAD - LORA_W))), lw['prep'])
    y_s, wkv_s = _wkv_sample(state_wkv[0], lgw, kk, b, k2, r, v, STATE_TILE)
    q_s = qkv_s[:, :ATT_WIDTH]
    k_s = qkv_s[:, ATT_WIDTH:2 * ATT_WIDTH]
    v_s = qkv_s[:, 2 * ATT_WIDTH:]
    ck = jnp.transpose(cache_win_k[0], (0, 2, 3, 1))
    cv = jnp.transpose(cache_win_v[0], (0, 2, 3, 1))

    y_p, conv_p, attn_s, win_k_s, win_v_s = _ffn_prompt(xp, attn, rw, lw['w_o'], lw['g_ffn'], lw['ffn'],
                                                        batch, seq, (q_s, k_s, v_s, ck, cv, bias_s, rb0))

    win_k_s = jnp.transpose(win_k_s, (0, 3, 1, 2))
    win_v_s = jnp.transpose(win_v_s, (0, 3, 1, 2))
    h1_s, xn2_s = _outproj(xs, attn_s, y_s, gate_s, bonus_s, lw['ln_w'], lw['ln_b'], lw['w_o'], lw['g_ffn'],
                           lw['ones_bd'], nb)
    conv_state = state_ffn_conv[0]
    y_smp, gp_s = _ffn_sample(xn2_s, h1_s, conv_state[:, 1], conv_state[:, 0], lw['ffn'])
    conv_s = jnp.stack([conv_state[:, 1], gp_s], axis=1)
    shift_s = jnp.concatenate([rkv_s, lora_s[:, :LORA_W]], axis=-1)[None]

    return (y_p.reshape(batch, seq, D_MODEL), y_smp.reshape(nb, 1, D_MODEL),
            win_k_p, win_v_p, shift_p, wkv_p[None], conv_p[None],
            win_k_s[None], win_v_s[None], shift_s, wkv_s[None], conv_s[None])
```

```python
import functools

import numpy as np
import jax
import jax.numpy as jnp
from jax import lax
from jax.experimental import pallas as pl
from jax.experimental.pallas import tpu as pltpu

F32 = jnp.float32
BF16 = jnp.bfloat16

D_MODEL = 1024
HEAD_DIM = 64
ATT_WIDTH = 512
RWKV_WIDTH = 512
N_HEADS = 8
QKV_W = 3 * ATT_WIDTH
RKV_W = 3 * RWKV_WIDTH
LORA_DECAY, LORA_ICLR, LORA_GATE = 32, 32, 96
LORA_W = LORA_DECAY + LORA_ICLR + LORA_GATE
LORA_PAD = 256
RWKV_PROJ = RKV_W + LORA_W
PROJ = QKV_W + RWKV_PROJ
PROJ_PAD = QKV_W + RKV_W + LORA_PAD
ROW_TILE = 512
STATE_TILE = 8
D_FF = 2816
FF_CHUNK = 2816
CONV_W = 3
WIN = 2048
DILATIONS = (1, 4, 16)
Q_BLOCK = 128
SUPER = 2048
N_BUCKETS = 32
MAX_DISTANCE = 2048
NORM_EPS = 1e-6
GN_EPS = 64e-5
NEG = -1e30
ATT_SCALE = HEAD_DIM ** -0.5
LOG2E = float(np.log2(np.e))
CHUNK = 64
WKV_CHUNKS_PER_STEP = 2
VMEM_LIMIT = 56 * 1024 * 1024

NT = (((1,), (1,)), ((), ()))
TN = (((0,), (0,)), ((), ()))


def _dot(a, b):
    return jnp.dot(a.astype(BF16), b.astype(BF16), preferred_element_type=F32)


def _dot_nt(a, b):
    return lax.dot_general(a.astype(BF16), b.astype(BF16), NT, preferred_element_type=F32)


def _dot_tn(a, b):
    return lax.dot_general(a.astype(BF16), b.astype(BF16), TN, preferred_element_type=F32)


def _split2(x):
    hi = x.astype(BF16)
    lo = (x - hi.astype(F32)).astype(BF16)
    return hi, lo


def _split3(x):
    hi = x.astype(BF16)
    r1 = x - hi.astype(F32)
    mid = r1.astype(BF16)
    lo = (r1 - mid.astype(F32)).astype(BF16)
    return hi, mid, lo


def _segsum(x, ones_bd):
    hi, lo = _split2(x)
    return (jnp.dot(hi, ones_bd, preferred_element_type=F32)
            + jnp.dot(lo, ones_bd, preferred_element_type=F32))


def _sigmoid(x):
    return 1.0 / (1.0 + jnp.exp(-x))


def _eye(n):
    return jnp.where(lax.broadcasted_iota(jnp.int32, (n, n), 0) == lax.broadcasted_iota(jnp.int32, (n, n), 1),
                     1.0, 0.0)


def _rmsnorm(x, g):
    return x * lax.rsqrt(jnp.mean(x * x, axis=-1, keepdims=True) + NORM_EPS) * g


def _params(sem):
    return pltpu.CompilerParams(dimension_semantics=sem, vmem_limit_bytes=VMEM_LIMIT)


def _t5_bucket(dist):
    dist = np.asarray(dist, dtype=np.int64)
    exact = N_BUCKETS // 2
    scaled = np.log(np.maximum(dist, 1) / exact) / np.log(MAX_DISTANCE / exact)
    large = np.minimum(exact + (scaled * (N_BUCKETS - exact)).astype(np.int64), N_BUCKETS - 1)
    return np.where(dist < exact, dist, large).astype(np.int32)


def _inproj_kernel(x_ref, g_ref, w_ref, qkv_ref, rkv_ref, lora_ref):
    xb = _rmsnorm(x_ref[...], g_ref[...]).astype(BF16)
    proj = lambda a, b: lax.dot_general(xb, w_ref[a:b, :].astype(BF16), NT, preferred_element_type=F32)
    qkv_ref[...] = proj(0, QKV_W)
    rkv_ref[...] = proj(QKV_W, QKV_W + RKV_W)
    lora_ref[:, 0:LORA_W] = proj(QKV_W + RKV_W, PROJ)
    lora_ref[:, LORA_W:LORA_PAD] = jnp.zeros((x_ref.shape[0], LORA_PAD - LORA_W), F32)


def _inproj(x2d, g, w, tm):
    m = x2d.shape[0]
    row = lambda i: (i, 0)
    fixed = lambda i: (0, 0)
    return pl.pallas_call(
        _inproj_kernel,
        grid=(m // tm,),
        in_specs=[pl.BlockSpec((tm, D_MODEL), row),
                  pl.BlockSpec((1, D_MODEL), fixed),
                  pl.BlockSpec((PROJ, D_MODEL), fixed)],
        out_specs=[pl.BlockSpec((tm, QKV_W), row),
                   pl.BlockSpec((tm, RKV_W), row),
                   pl.BlockSpec((tm, LORA_PAD), row)],
        out_shape=[jax.ShapeDtypeStruct((m, QKV_W), F32),
                   jax.ShapeDtypeStruct((m, RKV_W), F32),
                   jax.ShapeDtypeStruct((m, LORA_PAD), F32)],
        compiler_params=_params(("arbitrary",)),
        name="inproj",
    )(x2d, g, w)


def _prep_math(p, p_prev, l, l_prev, mu_p, mu_l, wcat, w0, a0, k_k, k_a, r_k, ones_bd):
    pm = p + mu_p * (p_prev - p)
    lm = l + mu_l * (l_prev - l)
    r = pm[:, 0:RWKV_WIDTH]
    kr = pm[:, RWKV_WIDTH:2 * RWKV_WIDTH]
    vr = pm[:, 2 * RWKV_WIDTH:3 * RWKV_WIDTH]
    lane = lax.broadcasted_iota(jnp.int32, lm.shape, 1)
    feat = jnp.where(lane < LORA_DECAY, jnp.tanh(lm),
                     jnp.where(lane < LORA_DECAY + LORA_ICLR, lm, _sigmoid(lm)))
    z = jnp.dot(feat.astype(BF16), wcat, preferred_element_type=F32)
    zd = -(w0 + z[:, 0:RWKV_WIDTH])
    softplus = jnp.maximum(zd, 0.0) + jnp.log(1.0 + jnp.exp(-jnp.abs(zd)))
    lw = -jnp.exp(-softplus - 0.5)
    a = _sigmoid(a0 + z[:, RWKV_WIDTH:2 * RWKV_WIDTH])
    gate = z[:, 2 * RWKV_WIDTH:3 * RWKV_WIDTH]
    kk = kr * k_k
    head_sum = lambda t: jnp.dot(t.astype(BF16), ones_bd, preferred_element_type=F32)
    kk = kk / jnp.maximum(jnp.sqrt(head_sum(kk * kk)), 1e-12)
    k2 = kr * (1.0 + (a - 1.0) * k_a)
    bonus = head_sum(r * k2 * r_k) * vr
    return r, lw, k2, vr, kk, kk * a, gate, bonus


def _inproj_prep_kernel(x_ref, g_ref, w_ref, mu_p_ref, mu_l_ref, wcat_ref, w0_ref, a0_ref, kk_ref, ka_ref,
                        rk_ref, ones_ref, qkv_ref, *refs):
    outs, (sp_ref, sl_ref, kw_ref, vw_ref), (cp_ref, cl_ref, wb_ref) = refs[:8], refs[8:12], refs[12:]

    @pl.when(jnp.logical_and(pl.program_id(0) == 0, pl.program_id(1) == 0))
    def _():
        for c0 in range(0, PROJ, RWKV_WIDTH):
            c1 = min(c0 + RWKV_WIDTH, PROJ)
            wb_ref[c0:c1, :] = w_ref[c0:c1, :].astype(BF16)
        wb_ref[PROJ:PROJ_PAD, :] = jnp.zeros((PROJ_PAD - PROJ, D_MODEL), BF16)

    @pl.when(pl.program_id(1) == 0)
    def _():
        cp_ref[...] = jnp.zeros_like(cp_ref)
        cl_ref[...] = jnp.zeros_like(cl_ref)

    xb = _rmsnorm(x_ref[...], g_ref[...]).astype(BF16)
    proj = lambda a, b: lax.dot_general(xb, wb_ref[a:b, :], NT, preferred_element_type=F32)
    p = proj(QKV_W, QKV_W + RKV_W)
    l = proj(QKV_W + RKV_W, PROJ_PAD)
    tm = p.shape[0]
    first_p = lax.broadcasted_iota(jnp.int32, p.shape, 0) == 0
    first_l = lax.broadcasted_iota(jnp.int32, l.shape, 0) == 0
    p_prev = jnp.where(first_p, cp_ref[...], pltpu.roll(p, 1, 0))
    l_prev = jnp.where(first_l, cl_ref[...], pltpu.roll(l, 1, 0))
    cp_ref[...] = p[tm - 1:tm, :]
    cl_ref[...] = l[tm - 1:tm, :]
    sp_ref[0] = p[tm - 1:tm, :]
    sl_ref[0] = l[tm - 1:tm, :]
    res = _prep_math(p, p_prev, l, l_prev, mu_p_ref[...], mu_l_ref[...], wcat_ref[...], w0_ref[...],
                     a0_ref[...], kk_ref[...], ka_ref[...], rk_ref[...], ones_ref[...])
    for o, v in zip(outs, res):
        o[...] = v
    qkv = proj(0, QKV_W)
    qkv_ref[...] = qkv
    kw_ref[0] = qkv[:, ATT_WIDTH:2 * ATT_WIDTH].T
    vw_ref[0] = qkv[:, 2 * ATT_WIDTH:QKV_W].T


def _prep_sample_kernel(p_ref, pp_ref, l_ref, lp_ref, mu_p_ref, mu_l_ref, wcat_ref, w0_ref, a0_ref,
                        kk_ref, ka_ref, rk_ref, ones_ref, *outs):
    res = _prep_math(p_ref[...], pp_ref[...], l_ref[...], lp_ref[...], mu_p_ref[...], mu_l_ref[...],
                     wcat_ref[...], w0_ref[...], a0_ref[...], kk_ref[...], ka_ref[...], rk_ref[...],
                     ones_ref[...])
    for o, v in zip(outs, res):
        o[...] = v


def _inproj_prep_prompt(x2d, g, w, pw, batch, seq, tm):
    nt = seq // tm
    row = lambda b, j: (b * nt + j, 0)
    fixed = lambda b, j: (0, 0)
    vec = lambda n: pl.BlockSpec((1, n), fixed)
    once = pl.Buffered(1)
    m = x2d.shape[0]
    last = lambda n: pl.BlockSpec((1, 1, n), lambda b, j: (b, 0, 0))
    nw = WIN // tm
    assert nw * tm == WIN and nw <= nt
    win = pl.BlockSpec((1, ATT_WIDTH, tm), lambda b, j: (b, 0, jnp.maximum(j - (nt - nw), 0)))
    return pl.pallas_call(
        _inproj_prep_kernel,
        grid=(batch, nt),
        in_specs=[pl.BlockSpec((tm, D_MODEL), row), vec(D_MODEL),
                  pl.BlockSpec((PROJ, D_MODEL), fixed, pipeline_mode=once),
                  vec(RKV_W), vec(LORA_PAD), pl.BlockSpec((LORA_PAD, RKV_W), fixed),
                  vec(RWKV_WIDTH), vec(RWKV_WIDTH), vec(RWKV_WIDTH), vec(RWKV_WIDTH), vec(RWKV_WIDTH),
                  pl.BlockSpec((RWKV_WIDTH, RWKV_WIDTH), fixed)],
        out_specs=[pl.BlockSpec((tm, QKV_W), row)] + [pl.BlockSpec((tm, RWKV_WIDTH), row)] * 8
                  + [last(RKV_W), last(LORA_PAD), win, win],
        out_shape=[jax.ShapeDtypeStruct((m, QKV_W), F32)] + [jax.ShapeDtypeStruct((m, RWKV_WIDTH), F32)] * 8
                  + [jax.ShapeDtypeStruct((batch, 1, RKV_W), F32), jax.ShapeDtypeStruct((batch, 1, LORA_PAD), F32)]
                  + [jax.ShapeDtypeStruct((batch, ATT_WIDTH, WIN), F32)] * 2,
        scratch_shapes=[pltpu.VMEM((1, RKV_W), F32), pltpu.VMEM((1, LORA_PAD), F32),
                        pltpu.VMEM((PROJ_PAD, D_MODEL), BF16)],
        compiler_params=_params(("arbitrary", "arbitrary")),
        name="inproj_prep_prompt",
    )(x2d, g, w, *pw)


def _prep_sample(rkv, rkv_prev, lora, lora_prev, pw):
    m = rkv.shape[0]
    full = lambda a: pl.BlockSpec(a.shape, lambda i: (0,) * a.ndim)
    args = (rkv, rkv_prev, lora, lora_prev) + tuple(pw)
    return pl.pallas_call(
        _prep_sample_kernel,
        grid=(1,),
        in_specs=[full(a) for a in args],
        out_specs=[pl.BlockSpec((m, RWKV_WIDTH), lambda i: (0, 0))] * 8,
        out_shape=[jax.ShapeDtypeStruct((m, RWKV_WIDTH), F32)] * 8,
        compiler_params=_params(("arbitrary",)),
        name="rwkv_prep_sample",
    )(*args)


def _cumsum_rows(x, tri):
    tri_b = tri.astype(BF16)
    out = None
    for piece in _split3(x):
        t = jnp.dot(tri_b, piece, preferred_element_type=F32)
        out = t if out is None else out + t
    return out


def _wkv_kernel(r_ref, lw_ref, k_ref, v_ref, kk_ref, b_ref, gate_ref, bonus_ref, lnw_ref, lnb_ref,
                y_ref, s_ref, st_ref):
    c = pl.program_id(0)

    @pl.when(c == 0)
    def _():
        st_ref[...] = jnp.zeros_like(st_ref)

    nb = r_ref.shape[0]
    n = CHUNK
    row = lax.broadcasted_iota(jnp.int32, (n, n), 0)
    col = lax.broadcasted_iota(jnp.int32, (n, n), 1)
    eye = jnp.where(row == col, 1.0, 0.0)
    tri = jnp.where(row >= col, 1.0, 0.0)
    row2 = lax.broadcasted_iota(jnp.int32, (2 * n, 2 * n), 0)
    col2 = lax.broadcasted_iota(jnp.int32, (2 * n, 2 * n), 1) % n
    gram_mask = jnp.logical_or(row2 % n > col2, jnp.logical_and(row2 >= n, row2 % n == col2))
    zeros = jnp.zeros((n, HEAD_DIM), BF16)

    units = range(nb * N_HEADS)
    stack = lambda xs, ys: [jnp.concatenate([x, y], axis=0) for x, y in zip(xs, ys)]

    def prelude(rows):
        ah, bh, kh, rh, beh, keh, vh, ge = ([] for _ in range(8))
        for bi in range(nb):
            lw = lw_ref[bi, rows, :]
            lg = _cumsum_rows(lw, tri)
            g_inv = jnp.exp(-lg)
            lg_end = lg[n - 1:n, :]
            g_to_end = jnp.exp(lg_end - lg)
            g_end = jnp.exp(lg_end)
            kk = kk_ref[bi, rows, :]
            b = b_ref[bi, rows, :]
            k = k_ref[bi, rows, :]
            full = (-kk * jnp.exp(lg - lw), b * g_inv, k * g_inv, r_ref[bi, rows, :] * jnp.exp(lg),
                    b * g_to_end, k * g_to_end, v_ref[bi, rows, :], g_end)
            for h in range(N_HEADS):
                sl = slice(h * HEAD_DIM, (h + 1) * HEAD_DIM)
                for dst, t in zip((ah, bh, kh, rh, beh, keh, vh), full[:7]):
                    dst.append(t[:, sl].astype(BF16))
                ge.append(g_end[:, sl])

        gram = [jnp.where(gram_mask, _dot_nt(ar, bk), 0.0) for ar, bk in zip(stack(ah, rh), stack(bh, kh))]
        top = [g[:n].astype(BF16) for g in gram]
        bot = [g[n:].astype(BF16) for g in gram]
        lv = [_dot(top[u], jnp.concatenate([zeros, vh[u]], axis=0)) for u in units]
        tm_ = [eye + gram[u][:n, :n] for u in units]
        lp = [_dot(t[:, :n], t[:, :n]).astype(BF16) for t in top]
        span = 2
        while 2 * span < n:
            prod = [_dot(x, p) for x, p in zip(stack(lp, [t.astype(BF16) for t in tm_]), lp)]
            tm_ = [tm_[u] + prod[u][n:] for u in units]
            lp = [p[:n].astype(BF16) for p in prod]
            span *= 2
        tm_ = [(tm_[u] + _dot(tm_[u], lp[u])).astype(BF16) for u in units]
        a_hat = [_dot(tm_[u], ah[u]).astype(BF16) for u in units]
        u0 = [_dot(tm_[u], lv[u]) for u in units]
        return rows, stack(a_hat, rh), u0, vh, bot, stack(beh, keh), ge

    def finish(rows, a_hat_r, u0, vh, bot, be_ke, ge):
        s0 = [st_ref[u] for u in units]
        on_s0 = [_dot_nt(x, s.astype(BF16)) for x, s in zip(a_hat_r, s0)]
        uv = stack([(on_s0[u][:n] + u0[u]).astype(BF16) for u in units], vh)
        y = [on_s0[u][n:] + _dot(bot[u], uv[u]) for u in units]
        for u in units:
            st_ref[u] = s0[u] * ge[u] + _dot_tn(uv[u], be_ke[u])
        for bi in range(nb):
            normed = []
            for yu in y[bi * N_HEADS:(bi + 1) * N_HEADS]:
                dy = yu - jnp.mean(yu, axis=-1, keepdims=True)
                normed.append(dy * lax.rsqrt(jnp.mean(dy * dy, axis=-1, keepdims=True) + GN_EPS))
            yn = jnp.concatenate(normed, axis=1)
            y_ref[bi, rows, :] = ((yn * lnw_ref[...] + lnb_ref[...] + bonus_ref[bi, rows, :])
                                  * gate_ref[bi, rows, :]).astype(BF16)

    for j in range(r_ref.shape[1] // n):
        finish(*prelude(pl.ds(j * n, n)))

    @pl.when(c == pl.num_programs(0) - 1)
    def _():
        s_ref[...] = st_ref[...]


def _wkv_prompt(r, lw, k2, v, kk, b, gate, bonus, ln_w, ln_b, batch, seq):
    rows = WKV_CHUNKS_PER_STEP * CHUNK
    nc = seq // rows
    assert nc * rows == seq
    seqs = lambda t: t.reshape(batch, seq, RWKV_WIDTH)
    spec = pl.BlockSpec((batch, rows, RWKV_WIDTH), lambda c: (0, c, 0))
    vec = pl.BlockSpec((1, RWKV_WIDTH), lambda c: (0, 0))
    nu = batch * N_HEADS
    y, s = pl.pallas_call(
        _wkv_kernel,
        grid=(nc,),
        in_specs=[spec] * 8 + [vec, vec],
        out_specs=[spec, pl.BlockSpec((nu, HEAD_DIM, HEAD_DIM), lambda c: (0, 0, 0))],
        out_shape=[jax.ShapeDtypeStruct((batch, seq, RWKV_WIDTH), BF16),
                   jax.ShapeDtypeStruct((nu, HEAD_DIM, HEAD_DIM), F32)],
        scratch_shapes=[pltpu.VMEM((nu, HEAD_DIM, HEAD_DIM), F32)],
        compiler_params=_params(("arbitrary",)),
        name="wkv_prompt",
    )(*(seqs(t) for t in (r, lw, k2, v, kk, b, gate, bonus)), ln_w, ln_b)
    return y.reshape(batch * seq, RWKV_WIDTH), s.reshape(batch, N_HEADS, HEAD_DIM, HEAD_DIM)


def _wkv_step_kernel(s_ref, lw_ref, kk_ref, b_ref, k_ref, r_ref, v_ref, y_ref, so_ref):
    s = s_ref[...]
    eye = _eye(HEAD_DIM)
    v_col = jnp.sum(eye * v_ref[...], axis=-1, keepdims=True)
    sa = jnp.sum(s * (-kk_ref[...]), axis=-1, keepdims=True)
    s = s * jnp.exp(lw_ref[...]) + sa * b_ref[...] + v_col * k_ref[...]
    so_ref[...] = s
    y_col = jnp.sum(s * r_ref[...], axis=-1, keepdims=True)
    y_ref[...] = jnp.sum(eye * y_col, axis=-2, keepdims=True)


def _wkv_sample(state, lw, kk, b, k2, r, v, bb):
    nb = state.shape[0]
    rowv = lambda t: t.reshape(nb, N_HEADS, 1, HEAD_DIM)
    idx = lambda i: (i, 0, 0, 0)
    s_spec = pl.BlockSpec((bb, N_HEADS, HEAD_DIM, HEAD_DIM), idx)
    r_spec = pl.BlockSpec((bb, N_HEADS, 1, HEAD_DIM), idx)
    y, s_new = pl.pallas_call(
        _wkv_step_kernel,
        grid=(nb // bb,),
        in_specs=[s_spec] + [r_spec] * 6,
        out_specs=[r_spec, s_spec],
        out_shape=[jax.ShapeDtypeStruct((nb, N_HEADS, 1, HEAD_DIM), F32),
                   jax.ShapeDtypeStruct(state.shape, F32)],
        compiler_params=_params(("arbitrary",)),
        name="wkv_sample",
    )(state, rowv(lw), rowv(kk), rowv(b), rowv(k2), rowv(r), rowv(v))
    return y.reshape(nb, RWKV_WIDTH), s_new


def _bias_tables():
    i = np.arange(Q_BLOCK)[:, None]
    c = np.arange(2 * Q_BLOCK)[None, :]
    n = i + Q_BLOCK - c
    ok = (n >= 0) & (n <= Q_BLOCK)
    prompt = np.stack([np.where(ok, _t5_bucket(d * np.clip(n, 0, Q_BLOCK)), -1) for d in DILATIONS])
    dist = WIN - np.arange(WIN)
    sample = np.stack([np.where((dist % d == 0) & (dist // d <= Q_BLOCK), _t5_bucket(dist), -1)
                       for d in DILATIONS])
    sample = np.broadcast_to(sample[:, None, :], (len(DILATIONS), N_HEADS, WIN))
    return prompt.astype(np.int32), sample.astype(np.int32)


def _bias_kernel(rb_ref, rbc_ref, bp_ref, bs_ref, op_ref, os_ref):
    bp = bp_ref[0]
    bs = bs_ref[0]
    acc_s = jnp.full(bs.shape, NEG, F32)
    for bk in range(N_BUCKETS):
        acc_s = jnp.where(bs == bk, rbc_ref[bk], acc_s)
    os_ref[0] = acc_s
    for h in range(N_HEADS):
        acc = jnp.full(bp.shape, NEG, F32)
        for bk in range(N_BUCKETS):
            acc = jnp.where(bp == bk, rb_ref[bk, h], acc)
        op_ref[0, h] = acc * LOG2E


def _bias_build(rel_bias):
    bp, bs = _bias_tables()
    rbc = rel_bias.reshape(N_BUCKETS, N_HEADS, 1)
    nb = len(DILATIONS)
    return pl.pallas_call(
        _bias_kernel,
        grid=(nb,),
        in_specs=[pl.BlockSpec(memory_space=pltpu.SMEM),
                  pl.BlockSpec((N_BUCKETS, N_HEADS, 1), lambda i: (0, 0, 0)),
                  pl.BlockSpec((1, Q_BLOCK, 2 * Q_BLOCK), lambda i: (i, 0, 0)),
                  pl.BlockSpec((1, N_HEADS, WIN), lambda i: (i, 0, 0))],
        out_specs=[pl.BlockSpec((1, N_HEADS, Q_BLOCK, 2 * Q_BLOCK), lambda i: (i, 0, 0, 0)),
                   pl.BlockSpec((1, N_HEADS, WIN), lambda i: (i, 0, 0))],
        out_shape=[jax.ShapeDtypeStruct((nb, N_HEADS, Q_BLOCK, 2 * Q_BLOCK), F32),
                   jax.ShapeDtypeStruct((nb, N_HEADS, WIN), F32)],
        compiler_params=_params(("arbitrary",)),
        name="bias_build",
    )(rel_bias, rbc, jnp.asarray(bp), jnp.asarray(bs)), rbc[0]


ATTN_GROUP = 4


def _attn_units():
    units = []
    for br, d in enumerate(DILATIONS):
        nblk = SUPER // (d * Q_BLOCK)
        for r in range(d):
            rows = lambda j, r=r, d=d: (pl.ds(r + d * Q_BLOCK * j, Q_BLOCK, stride=d) if d > 1
                                        else pl.ds(Q_BLOCK * j, Q_BLOCK))
            for blk in range(nblk):
                units.append((br, rows(blk), rows(blk - 1 if blk else nblk - 1), blk == 0))
    return units


def _attn_kernel(q_ref, kc_ref, kp_ref, vc_ref, vp_ref, bias_ref, o_ref, num_ref, den_ref, m_ref, edge_ref):
    has_prev = pl.program_id(2) > 0
    is_a = lax.broadcasted_iota(jnp.int32, (Q_BLOCK, 128), 1) < HEAD_DIM
    ones = jnp.ones((2 * Q_BLOCK, 128), BF16)
    live = jnp.logical_or(has_prev,
                          lax.broadcasted_iota(jnp.int32, (2 * Q_BLOCK, 2 * Q_BLOCK), 1) >= Q_BLOCK)
    for br in range(len(DILATIONS)):
        edge_ref[br] = jnp.where(live, bias_ref[br], NEG)
    units = _attn_units()
    for g0 in range(0, len(units), ATTN_GROUP):
        group = units[g0:g0 + ATTN_GROUP]
        ops = []
        for br, cur, prv, from_prev in group:
            qs = q_ref[cur, :] * (ATT_SCALE * LOG2E)
            q2 = jnp.concatenate([jnp.where(is_a, qs, 0.0), jnp.where(is_a, 0.0, qs)], axis=0)
            k2 = jnp.concatenate([(kp_ref if from_prev else kc_ref)[prv, :], kc_ref[cur, :]], axis=0)
            v2 = jnp.concatenate([(vp_ref if from_prev else vc_ref)[prv, :], vc_ref[cur, :]], axis=0)
            ops.append((br, cur, from_prev, q2.astype(BF16), k2.astype(BF16),
                        jnp.concatenate([v2.astype(BF16), ones], axis=1)))
        logits = []
        for br, cur, from_prev, q2, k2, vx in ops:
            logits.append(lax.dot_general(q2, k2, NT, preferred_element_type=F32)
                          + (edge_ref if from_prev else bias_ref)[br])
        ms = [jnp.max(lg, axis=-1, keepdims=True) for lg in logits]
        ps = [jnp.exp2(lg - m).astype(BF16) for lg, m in zip(logits, ms)]
        for (br, cur, _, _, _, vx), p, m in zip(ops, ps, ms):
            res = jnp.dot(p, vx, preferred_element_type=F32)
            num_ref[br, cur, :] = jnp.where(is_a, res[:Q_BLOCK, :128], res[Q_BLOCK:, :128])
            den_ref[br, cur, :] = jnp.where(is_a, res[:Q_BLOCK, 128:], res[Q_BLOCK:, 128:])
            m_ref[br, cur, :] = jnp.where(is_a, m[:Q_BLOCK], m[Q_BLOCK:])

    for i in range(SUPER // Q_BLOCK):
        rs = pl.ds(i * Q_BLOCK, Q_BLOCK)
        ms = [m_ref[br, rs, :] for br in range(len(DILATIONS))]
        m = functools.reduce(jnp.maximum, ms)
        num = 0.0
        den = 0.0
        for br in range(len(DILATIONS)):
            w = jnp.exp2(ms[br] - m)
            num = num + w * num_ref[br, rs, :]
            den = den + w * den_ref[br, rs, :]
        o_ref[rs, :] = (num / den).astype(o_ref.dtype)


def _attn_prompt(qkv, bias_p, batch, seq):
    ns = seq // SUPER
    npair = ATT_WIDTH // 128
    blk = lambda col0, prev: pl.BlockSpec(
        (SUPER, 128),
        (lambda b, hp, sb: (b * ns + jnp.maximum(sb - 1, 0), col0 + hp)) if prev
        else (lambda b, hp, sb: (b * ns + sb, col0 + hp)))
    nb = len(DILATIONS)
    return pl.pallas_call(
        _attn_kernel,
        grid=(batch, npair, ns),
        in_specs=[blk(0, False), blk(npair, False), blk(npair, True), blk(2 * npair, False),
                  blk(2 * npair, True),
                  pl.BlockSpec((nb, None, 2 * Q_BLOCK, 2 * Q_BLOCK), lambda b, hp, sb: (0, hp, 0, 0))],
        out_specs=pl.BlockSpec((SUPER, 128), lambda b, hp, sb: (b * ns + sb, hp)),
        out_shape=jax.ShapeDtypeStruct((batch * seq, ATT_WIDTH), BF16),
        scratch_shapes=[pltpu.VMEM((nb, SUPER, 128), F32)] * 3
                       + [pltpu.VMEM((nb, 2 * Q_BLOCK, 2 * Q_BLOCK), F32)],
        compiler_params=_params(("arbitrary", "arbitrary", "arbitrary")),
        name="attn_prompt",
    )(qkv, qkv, qkv, qkv, qkv, bias_p.reshape(nb, npair, 2 * Q_BLOCK, 2 * Q_BLOCK))


WINDOW_HEADS = 4


def _window_body(q_ref, kn_ref, vn_ref, ck_ref, cv_ref, bias_ref, rb0_ref, o_ref, ok_ref, ov_ref, wt_ref):
    nh = WINDOW_HEADS
    lane = lax.broadcasted_iota(jnp.int32, (HEAD_DIM, WIN), 1)
    last = lane == WIN - 1
    eye = _eye(HEAD_DIM)
    col = lambda ref, h: jnp.sum(eye * ref[h:h + 1, :], axis=-1, keepdims=True)
    l_new = jnp.sum(kn_ref[...] * (q_ref[...] * ATT_SCALE), axis=-1, keepdims=True) + rb0_ref[...]
    rows = []
    for h in range(nh):
        kh = ck_ref[h]
        rows.append(jnp.sum(kh * (col(q_ref, h) * ATT_SCALE), axis=0, keepdims=True))
        ok_ref[h] = jnp.where(last, col(kn_ref, h), pltpu.roll(kh, WIN - 1, 1))
    logits = jnp.concatenate(rows, axis=0)
    parts = []
    for br in range(len(DILATIONS)):
        lg = logits + bias_ref[br]
        m = jnp.maximum(jnp.max(lg, axis=-1, keepdims=True), l_new)
        p = jnp.exp(lg - m)
        p_new = jnp.exp(l_new - m)
        parts.append((p, p_new, m, jnp.sum(p, axis=-1, keepdims=True) + p_new))
    m = functools.reduce(jnp.maximum, [pt[2] for pt in parts])
    wt = 0.0
    w_new = 0.0
    den = 0.0
    for p, p_new, m_g, s_g in parts:
        w = jnp.exp(m_g - m)
        wt = wt + w * p
        w_new = w_new + w * p_new
        den = den + w * s_g
    wt_ref[0:nh, :] = wt / den
    w_new = w_new / den
    for h in range(nh):
        vh = cv_ref[h]
        v_new = col(vn_ref, h)
        o_col = jnp.sum(vh * wt_ref[h:h + 1, :], axis=1, keepdims=True) + v_new * w_new[h:h + 1, :]
        o_ref[h:h + 1, :] = jnp.sum(eye * o_col, axis=0, keepdims=True)
        ov_ref[h] = jnp.where(last, v_new, pltpu.roll(vh, WIN - 1, 1))


N_WINDOW_IN = 7


def _window_operands(q, k_new, v_new, cache_k, cache_v, bias_s, rb0, step_of):
    nb = q.shape[0]
    nh = WINDOW_HEADS
    groups = N_HEADS // nh
    seq_grp = lambda *g: (step_of(*g) // groups, step_of(*g) % groups)
    rows = lambda t: t.reshape(nb, groups, nh, HEAD_DIM)
    wins = lambda t: t.reshape(nb, groups, nh, HEAD_DIM, WIN)
    row_spec = pl.BlockSpec((None, None, nh, HEAD_DIM), lambda *g: seq_grp(*g) + (0, 0))
    win_spec = pl.BlockSpec((None, None, nh, HEAD_DIM, WIN), lambda *g: seq_grp(*g) + (0, 0, 0))
    nbr = len(DILATIONS)
    bias_spec = pl.BlockSpec((nbr, None, nh, WIN), lambda *g: (0, seq_grp(*g)[1], 0, 0))
    rb0_spec = pl.BlockSpec((None, nh, 1), lambda *g: (seq_grp(*g)[1], 0, 0))
    args = (rows(q), rows(k_new), rows(v_new), wins(cache_k), wins(cache_v),
            bias_s.reshape(nbr, groups, nh, WIN), rb0.reshape(groups, nh, 1))
    in_specs = [row_spec, row_spec, row_spec, win_spec, win_spec, bias_spec, rb0_spec]
    assert len(args) == len(in_specs) == N_WINDOW_IN
    out_specs = [row_spec, win_spec, win_spec]
    out_shape = [jax.ShapeDtypeStruct((nb, groups, nh, HEAD_DIM), F32),
                 jax.ShapeDtypeStruct((nb, groups, nh, HEAD_DIM, WIN), F32),
                 jax.ShapeDtypeStruct((nb, groups, nh, HEAD_DIM, WIN), F32)]
    return args, in_specs, out_specs, out_shape


def _outproj_kernel(x_ref, attn_ref, y_ref, gate_ref, bonus_ref, lnw_ref, lnb_ref, wo_ref, gffn_ref,
                    ones_ref, h_ref, xn_ref):
    ones_bd = ones_ref[...]
    y = y_ref[...]
    mu = _segsum(y, ones_bd) * (1.0 / HEAD_DIM)
    dy = y - mu
    var = _segsum(dy * dy, ones_bd) * (1.0 / HEAD_DIM)
    yn = dy * lax.rsqrt(var + GN_EPS) * lnw_ref[...] + lnb_ref[...]
    rw = (yn + bonus_ref[...]) * gate_ref[...]
    mixed = (jnp.dot(attn_ref[...].astype(BF16), wo_ref[0:ATT_WIDTH, :], preferred_element_type=F32)
             + jnp.dot(rw.astype(BF16), wo_ref[ATT_WIDTH:D_MODEL, :], preferred_element_type=F32))
    h = x_ref[...] + mixed
    h_ref[...] = h
    xn_ref[...] = _rmsnorm(h, gffn_ref[...]).astype(BF16)


def _outproj(x2d, attn, y, gate, bonus, ln_w, ln_b, w_o, g_ffn, ones_bd, tm):
    m = x2d.shape[0]
    row = lambda i: (i, 0)
    fixed = lambda i: (0, 0)
    half = pl.BlockSpec((tm, RWKV_WIDTH), row)
    full = pl.BlockSpec((tm, D_MODEL), row)
    return pl.pallas_call(
        _outproj_kernel,
        grid=(m // tm,),
        in_specs=[full, half, half, half, half,
                  pl.BlockSpec((1, RWKV_WIDTH), fixed), pl.BlockSpec((1, RWKV_WIDTH), fixed),
                  pl.BlockSpec((D_MODEL, D_MODEL), fixed), pl.BlockSpec((1, D_MODEL), fixed),
                  pl.BlockSpec((RWKV_WIDTH, RWKV_WIDTH), fixed)],
        out_specs=[full, full],
        out_shape=[jax.ShapeDtypeStruct((m, D_MODEL), F32), jax.ShapeDtypeStruct((m, D_MODEL), BF16)],
        compiler_params=_params(("arbitrary",)),
        name="outproj",
    )(x2d, attn, y, gate, bonus, ln_w, ln_b, w_o, g_ffn, ones_bd)


def _ffn_tail(xn, h, gp_m1, gp_m2, gp_of, wi_ref, cw_ref, cb_ref, wout_ref, gfin_ref, o_ref):
    acc = None
    for c in range(D_FF // FF_CHUNK):
        cs = slice(c * FF_CHUNK, (c + 1) * FF_CHUNK)
        gp = gp_of(c)
        up = jnp.dot(xn, wi_ref[:, D_FF + c * FF_CHUNK:D_FF + (c + 1) * FF_CHUNK], preferred_element_type=F32)
        conv = (cb_ref[:, cs] + cw_ref[0:1, cs] * gp_m2(c, gp) + cw_ref[1:2, cs] * gp_m1(c, gp)
                + cw_ref[2:3, cs] * gp)
        act = conv * _sigmoid(conv) * up
        part = jnp.dot(act.astype(BF16), wout_ref[cs, :], preferred_element_type=F32)
        acc = part if acc is None else acc + part
    o_ref[...] = _rmsnorm(h + acc, gfin_ref[...])


def _ffn_prompt_kernel(x_ref, attn_ref, rw_ref, wo_ref, gffn_ref, wi_ref, cw_ref, cb_ref, wout_ref,
                       gfin_ref, *refs):
    n = N_WINDOW_IN
    win_in, (o_ref, conv_ref), win_out, (carry_ref, wt_ref) = refs[:n], refs[n:n + 2], refs[n + 2:n + 5], refs[n + 5:]

    @pl.when(pl.program_id(1) == 0)
    def _():
        carry_ref[...] = jnp.zeros_like(carry_ref)

    _window_body(*win_in, *win_out, wt_ref)

    h = (x_ref[...]
         + jnp.dot(attn_ref[...], wo_ref[0:ATT_WIDTH, :], preferred_element_type=F32)
         + jnp.dot(rw_ref[...], wo_ref[ATT_WIDTH:D_MODEL, :], preferred_element_type=F32))
    xn = _rmsnorm(h, gffn_ref[...]).astype(BF16)
    tm = xn.shape[0]
    rowi = lax.broadcasted_iota(jnp.int32, (tm, FF_CHUNK), 0)
    gps = {}

    def gp_of(c):
        cs = slice(c * FF_CHUNK, (c + 1) * FF_CHUNK)
        gps[c] = jnp.dot(xn, wi_ref[:, cs], preferred_element_type=F32)
        return gps[c]

    def gp_m1(c, gp):
        cs = slice(c * FF_CHUNK, (c + 1) * FF_CHUNK)
        return jnp.where(rowi == 0, carry_ref[1:2, cs], pltpu.roll(gp, 1, 0))

    def gp_m2(c, gp):
        cs = slice(c * FF_CHUNK, (c + 1) * FF_CHUNK)
        return jnp.where(rowi == 0, carry_ref[0:1, cs],
                         jnp.where(rowi == 1, carry_ref[1:2, cs], pltpu.roll(gp, 2, 0)))

    _ffn_tail(xn, h, gp_m1, gp_m2, gp_of, wi_ref, cw_ref, cb_ref, wout_ref, gfin_ref, o_ref)
    for c, gp in gps.items():
        cs = slice(c * FF_CHUNK, (c + 1) * FF_CHUNK)
        carry_ref[:, cs] = gp[tm - 2:tm, :]
        conv_ref[0, :, cs] = gp[tm - 2:tm, :]


def _ffn_sample_kernel(xn_ref, h_ref, p1_ref, p2_ref, wi_ref, cw_ref, cb_ref, wout_ref, gfin_ref,
                       o_ref, gp_ref):
    xn = xn_ref[...]

    def gp_of(c):
        cs = slice(c * FF_CHUNK, (c + 1) * FF_CHUNK)
        gp = jnp.dot(xn, wi_ref[:, cs], preferred_element_type=F32)
        gp_ref[:, cs] = gp
        return gp

    gp_m1 = lambda c, gp: p1_ref[:, c * FF_CHUNK:(c + 1) * FF_CHUNK]
    gp_m2 = lambda c, gp: p2_ref[:, c * FF_CHUNK:(c + 1) * FF_CHUNK]
    _ffn_tail(xn, h_ref[...], gp_m1, gp_m2, gp_of, wi_ref, cw_ref, cb_ref, wout_ref, gfin_ref, o_ref)


def _ffn_weight_specs(fixed):
    once = pl.Buffered(1)
    return [pl.BlockSpec((D_MODEL, 2 * D_FF), fixed, pipeline_mode=once),
            pl.BlockSpec((CONV_W, D_FF), fixed), pl.BlockSpec((1, D_FF), fixed),
            pl.BlockSpec((D_FF, D_MODEL), fixed, pipeline_mode=once), pl.BlockSpec((1, D_MODEL), fixed)]


def _ffn_prompt(x2d, attn, rw, w_o, g_ffn, fw, batch, seq, window):
    nt = window[0].shape[0] * (N_HEADS // WINDOW_HEADS) // batch
    tm = seq // nt
    assert tm * nt == seq and tm % 16 == 0
    row = lambda b, j: (b * nt + j, 0)
    fixed = lambda b, j: (0, 0)
    full = pl.BlockSpec((tm, D_MODEL), row)
    half = pl.BlockSpec((tm, ATT_WIDTH), row)
    mix_specs = [full, half, half, pl.BlockSpec((D_MODEL, D_MODEL), fixed), pl.BlockSpec((1, D_MODEL), fixed)]
    w_args, w_in_specs, w_out_specs, w_out_shape = _window_operands(*window, lambda b, j: b * nt + j)
    y, conv, attn_s, win_k, win_v = pl.pallas_call(
        _ffn_prompt_kernel,
        grid=(batch, nt),
        in_specs=mix_specs + _ffn_weight_specs(fixed) + w_in_specs,
        out_specs=[full, pl.BlockSpec((1, CONV_W - 1, D_FF), lambda b, j: (b, 0, 0))] + w_out_specs,
        out_shape=[jax.ShapeDtypeStruct((batch * seq, D_MODEL), F32),
                   jax.ShapeDtypeStruct((batch, CONV_W - 1, D_FF), F32)] + w_out_shape,
        scratch_shapes=[pltpu.VMEM((CONV_W - 1, D_FF), F32), pltpu.VMEM((8, WIN), F32)],
        compiler_params=_params(("arbitrary", "arbitrary")),
        name="ffn_prompt",
    )(x2d, attn, rw, w_o, g_ffn, *fw, *w_args)
    nb = window[0].shape[0]
    return (y, conv, attn_s.reshape(nb, ATT_WIDTH), win_k.reshape(nb, N_HEADS, HEAD_DIM, WIN),
            win_v.reshape(nb, N_HEADS, HEAD_DIM, WIN))


def _ffn_sample(xn, h, prev1, prev2, fw):
    m = xn.shape[0]
    fixed = lambda i: (0, 0)
    full = pl.BlockSpec((m, D_MODEL), fixed)
    ffs = pl.BlockSpec((m, D_FF), fixed)
    return pl.pallas_call(
        _ffn_sample_kernel,
        grid=(1,),
        in_specs=[full, full, ffs, ffs] + _ffn_weight_specs(fixed),
        out_specs=[full, ffs],
        out_shape=[jax.ShapeDtypeStruct((m, D_MODEL), F32), jax.ShapeDtypeStruct((m, D_FF), F32)],
        compiler_params=_params(("arbitrary",)),
        name="ffn_sample",
    )(xn, h, prev1, prev2, *fw)


def _layer_weights(g_mix, w_in, tok_mu, w0, w_decay_up, a0, w_iclr_up, w_gate_up, k_k, k_a, r_k,
                   ln_x_w, ln_x_b, w_o, g_ffn, w_ffn_in, conv_w, conv_b, w_ffn_out, g_final):
    vec = lambda t: t.reshape(1, -1)
    pad = LORA_PAD - LORA_W
    w_in_t = jnp.transpose(w_in)
    mu_p = vec(tok_mu[:RKV_W])
    mu_l = vec(jnp.pad(tok_mu[RKV_W:], (0, pad)))
    lora_up = jnp.zeros((LORA_PAD, RKV_W), F32)
    lora_up = lora_up.at[0:LORA_DECAY, 0:RWKV_WIDTH].set(w_decay_up)
    lora_up = lora_up.at[LORA_DECAY:LORA_DECAY + LORA_ICLR, RWKV_WIDTH:2 * RWKV_WIDTH].set(w_iclr_up)
    lora_up = lora_up.at[LORA_DECAY + LORA_ICLR:LORA_W, 2 * RWKV_WIDTH:].set(w_gate_up)
    seg = np.arange(RWKV_WIDTH) // HEAD_DIM
    ones_bd = jnp.asarray(seg[:, None] == seg[None, :], BF16)
    prep = (mu_p, mu_l, lora_up.astype(BF16), vec(w0), vec(a0), vec(k_k), vec(k_a), vec(r_k), ones_bd)
    ffn = (w_ffn_in.astype(BF16), conv_w, vec(conv_b),
           w_ffn_out.astype(BF16), vec(g_final))
    return dict(g_mix=vec(g_mix), w_in_t=w_in_t, prep=prep, ones_bd=ones_bd, ln_w=vec(ln_x_w), ln_b=vec(ln_x_b),
                w_o=w_o.astype(BF16), g_ffn=vec(g_ffn), ffn=ffn)


def kernel(x_prompt, x_sample, cache_win_k, cache_win_v, state_shift, state_wkv, state_ffn_conv, g_mix, w_in, rel_bias, tok_mu, w0, w_decay_up, a0, w_iclr_up, w_gate_up, k_k, k_a, r_k, ln_x_w, ln_x_b, w_o, g_ffn, w_ffn_in, conv_w, conv_b, w_ffn_out, g_final):
    batch, seq, _ = x_prompt.shape
    nb = x_sample.shape[0]
    lw = _layer_weights(g_mix[0], w_in[0], tok_mu[0], w0[0], w_decay_up[0], a0[0], w_iclr_up[0],
                        w_gate_up[0], k_k[0], k_a[0], r_k[0].reshape(-1), ln_x_w[0], ln_x_b[0], w_o[0],
                        g_ffn[0], w_ffn_in[0], conv_w[0], conv_b[0], w_ffn_out[0], g_final)
    (bias_p, bias_s), rb0 = _bias_build(rel_bias)

    xp = x_prompt.reshape(batch * seq, D_MODEL)
    qkv, r, lgw, k2, v, kk, b, gate, bonus, sh_p, sh_l, k_win, v_win = _inproj_prep_prompt(
        xp, lw['g_mix'], lw['w_in_t'], lw['prep'], batch, seq, ROW_TILE)
    rw, wkv_p = _wkv_prompt(r, lgw, k2, v, kk, b, gate, bonus, lw['ln_w'], lw['ln_b'], batch, seq)
    attn = _attn_prompt(qkv, bias_p, batch, seq)
    to_window = lambda t: jnp.transpose(t.reshape(batch, N_HEADS, HEAD_DIM, WIN), (0, 3, 1, 2))[None]
    win_k_p = to_window(k_win)
    win_v_p = to_window(v_win)
    shift_p = jnp.concatenate([sh_p[:, 0], sh_l[:, 0, :LORA_W]], axis=-1)[None]

    xs = x_sample.reshape(nb, D_MODEL)
    qkv_s, rkv_s, lora_s = _inproj(xs, lw['g_mix'], lw['w_in_t'], nb)
    sh = state_shift[0]
    r, lgw, k2, v, kk, b, gate_s, bonus_s = _prep_sample(
        rkv_s, sh[:, :RKV_W], lora_s, jnp.pad(sh[:, RKV_W:], ((0, 0), (0, LORA_PAD - LORA_W))), lw['prep'])
    y_s, wkv_s = _wkv_sample(state_wkv[0], lgw, kk, b, k2, r, v, STATE_TILE)
    q_s = qkv_s[:, :ATT_WIDTH]
    k_s = qkv_s[:, ATT_WIDTH:2 * ATT_WIDTH]
    v_s = qkv_s[:, 2 * ATT_WIDTH:]
    ck = jnp.transpose(cache_win_k[0], (0, 2, 3, 1))
    cv = jnp.transpose(cache_win_v[0], (0, 2, 3, 1))

    y_p, conv_p, attn_s, win_k_s, win_v_s = _ffn_prompt(xp, attn, rw, lw['w_o'], lw['g_ffn'], lw['ffn'],
                                                        batch, seq, (q_s, k_s, v_s, ck, cv, bias_s, rb0))

    win_k_s = jnp.transpose(win_k_s, (0, 3, 1, 2))
    win_v_s = jnp.transpose(win_v_s, (0, 3, 1, 2))
    h1_s, xn2_s = _outproj(xs, attn_s, y_s, gate_s, bonus_s, lw['ln_w'], lw['ln_b'], lw['w_o'], lw['g_ffn'],
                           lw['ones_bd'], nb)
    conv_state = state_ffn_conv[0]
    y_smp, gp_s = _ffn_sample(xn2_s, h1_s, conv_state[:, 1], conv_state[:, 0], lw['ffn'])
    conv_s = jnp.stack([conv_state[:, 1], gp_s], axis=1)
    shift_s = jnp.concatenate([rkv_s, lora_s[:, :LORA_W]], axis=-1)[None]

    return (y_p.reshape(batch, seq, D_MODEL), y_smp.reshape(nb, 1, D_MODEL),
            win_k_p, win_v_p, shift_p, wkv_p[None], conv_p[None],
            win_k_s[None], win_v_s[None], shift_s, wkv_s[None], conv_s[None])
```
